```python
import jax, jax.numpy as jnp
from jax import lax
import numpy as np

D_MODEL = 1024
BATCH = 4
SEQ = 8192
DEPTH = 1

CHUNK = 64
MIX_WIDTH = D_MODEL
HEAD_DIM = 64
CONV_WIDTH = MIX_WIDTH // 2
RWKV_WIDTH = MIX_WIDTH - CONV_WIDTH
CONV_GROUPS = CONV_WIDTH // HEAD_DIM
RWKV_HEADS = RWKV_WIDTH // HEAD_DIM
CONV_K = 3
DECAY_LORA = 64
AAA_LORA = 64
GATE_LORA = 128
N_EXPERTS = 32
TOP_K = 4
D_FF = D_MODEL
SWIGLU_LIMIT = 7.0
SWIGLU_ALPHA = 1.702
MOE_BLOCK = 128
NORM_EPS = 1e-5
GN_EPS = HEAD_DIM * 1e-5

CONV_COLS = 3 * CONV_WIDTH
RWKV_COLS = 3 * RWKV_WIDTH + DECAY_LORA + AAA_LORA + GATE_LORA
IN_COLS = CONV_COLS + RWKV_COLS

kernel_name = "hymba_conv_rwkv7_moe_encoder"


def rms_norm(x, g):
    xf = x.astype(jnp.float32)
    y = xf * lax.rsqrt(jnp.mean(xf * xf, axis=-1, keepdims=True) + NORM_EPS)
    return (y * g.astype(jnp.float32)).astype(x.dtype)


def group_rms_norm(y, g, groups):
    lead = y.shape[:-1]
    yf = y.astype(jnp.float32).reshape(*lead, groups, -1)
    yf = yf * lax.rsqrt(jnp.mean(yf * yf, axis=-1, keepdims=True) + NORM_EPS)
    return (yf.reshape(*lead, -1) * g.astype(jnp.float32)).astype(y.dtype)


def token_shift_lerp(p, mu):
    prev = jnp.pad(p, ((0, 0), (1, 0), (0, 0)))[:, :-1]
    return p + (prev - p) * mu


def causal_depthwise_conv(z, w):
    C = z.shape[-1]
    return lax.conv_general_dilated(
        z, w.reshape(CONV_K, 1, C).astype(z.dtype), window_strides=(1,),
        padding=[(CONV_K - 1, 0)], dimension_numbers=("NWC", "WIO", "NWC"),
        feature_group_count=C)


def short_conv_mixer(p, conv_w, conv_norm_g):
    u, b_gate, c_gate = jnp.split(p, 3, axis=-1)
    y = b_gate * causal_depthwise_conv(c_gate * u, conv_w)
    return group_rms_norm(y, conv_norm_g, CONV_GROUPS)


def wkv7_scan(r, w, k, v, kk, a):
    B_, S_, H, N = r.shape
    n_chunks = S_ // CHUNK

    def to_chunks(t):
        return t.reshape(B_, n_chunks, CHUNK, H, N).transpose(1, 2, 0, 3, 4)

    def frame_step(state, inp):
        r_t, w_t, k_t, v_t, kk_t, a_t = inp
        sa = jnp.einsum("bhvk,bhk->bhv", state, -kk_t)
        state = (state * w_t[:, :, None, :]
                 + sa[..., None] * (kk_t * a_t)[:, :, None, :]
                 + v_t[..., None] * k_t[:, :, None, :])
        return state, jnp.einsum("bhvk,bhk->bhv", state, r_t)

    def chunk_step(state, chunk_inp):
        return lax.scan(frame_step, state, chunk_inp)

    state0 = jnp.zeros((B_, H, N, N), jnp.float32)
    _, out = lax.scan(chunk_step, state0, tuple(to_chunks(t) for t in (r, w, k, v, kk, a)))
    return out.transpose(2, 0, 1, 3, 4).reshape(B_, S_, H, N)


def rwkv7_mixer(p, mu, w0, w_up, a0, a_up, g_up, k_k, k_a, r_k, gn_w, gn_b):
    B_, S_, _ = p.shape
    f32 = jnp.float32
    q = token_shift_lerp(p, mu)
    i1 = RWKV_WIDTH
    i2 = 2 * RWKV_WIDTH
    i3 = 3 * RWKV_WIDTH
    i4 = i3 + DECAY_LORA
    i5 = i4 + AAA_LORA
    r, k, v, dw, da, dg = jnp.split(q, [i1, i2, i3, i4, i5], axis=-1)
    w_log = -jax.nn.softplus(-(w0 + jnp.tanh(dw) @ w_up).astype(f32)) - 0.5
    decay = jnp.exp(-jnp.exp(w_log))
    a = jax.nn.sigmoid((a0 + da @ a_up).astype(f32))
    g = jax.nn.sigmoid(dg) @ g_up

    def heads(t):
        return t.astype(f32).reshape(B_, S_, RWKV_HEADS, HEAD_DIM)

    kk = heads(k * k_k)
    kk = kk / jnp.maximum(jnp.linalg.norm(kk, axis=-1, keepdims=True), 1e-12)
    k_mod = k.astype(f32) * (1.0 + (a - 1.0) * k_a.astype(f32))
    r_h, k_h, v_h = heads(r), heads(k_mod), heads(v)
    a_h, w_h = heads(a), heads(decay)
    o = wkv7_scan(r_h, w_h, k_h, v_h, kk, a_h)
    mean = jnp.mean(o, axis=-1, keepdims=True)
    var = jnp.mean(jnp.square(o - mean), axis=-1, keepdims=True)
    o = ((o - mean) * lax.rsqrt(var + GN_EPS)).reshape(B_, S_, RWKV_WIDTH)
    o = o * gn_w.astype(f32) + gn_b.astype(f32)
    bonus = jnp.sum(r_h * k_h * r_k.astype(f32), axis=-1, keepdims=True) * v_h
    o = o + bonus.reshape(B_, S_, RWKV_WIDTH)
    return (o * g.astype(f32)).astype(p.dtype)


def moe_ffn(h, w_router, b_router, w_gu, b_gu, w_down, b_down):
    B_, S_, D = h.shape
    n_tok = B_ * S_
    t = h.reshape(n_tok, D)
    logits = (t @ w_router + b_router).astype(jnp.float32)
    top_logit, top_e = lax.top_k(logits, TOP_K)
    gates = jax.nn.softmax(top_logit, axis=-1)
    n_assign = n_tok * TOP_K
    flat_e = top_e.reshape(n_assign).astype(jnp.int32)
    flat_tok = jnp.arange(n_assign, dtype=jnp.int32) // TOP_K
    order = jnp.argsort(flat_e)
    sorted_e = flat_e[order]
    counts = jnp.bincount(flat_e, length=N_EXPERTS).astype(jnp.int32)
    padded = (counts + MOE_BLOCK - 1) // MOE_BLOCK * MOE_BLOCK
    pad_end = jnp.cumsum(padded)
    pad_start = pad_end - padded
    start = jnp.cumsum(counts) - counts
    rank = jnp.arange(n_assign, dtype=jnp.int32) - start[sorted_e]
    dest = jnp.zeros((n_assign,), jnp.int32).at[order].set(pad_start[sorted_e] + rank)
    n_slots = n_assign + N_EXPERTS * MOE_BLOCK
    n_blocks = n_slots // MOE_BLOCK
    slot_tok = jnp.full((n_slots,), n_tok, jnp.int32).at[dest].set(flat_tok)
    block_e = jnp.minimum(
        jnp.searchsorted(pad_end, jnp.arange(n_blocks, dtype=jnp.int32) * MOE_BLOCK, side="right"),
        N_EXPERTS - 1).astype(jnp.int32)
    t_pad = jnp.concatenate([t, jnp.zeros((1, D), t.dtype)], axis=0)
    xin = t_pad[slot_tok].reshape(n_blocks, MOE_BLOCK, D)

    def expert_block(args):
        xb, e = args
        gu = xb @ w_gu[e] + b_gu[e]
        gate, up = gu[:, :D_FF], gu[:, D_FF:]
        gate = jnp.minimum(gate, SWIGLU_LIMIT)
        up = jnp.clip(up, -SWIGLU_LIMIT, SWIGLU_LIMIT)
        act = (up + 1.0) * (gate * jax.nn.sigmoid(SWIGLU_ALPHA * gate))
        return act @ w_down[e] + b_down[e]

    y_slots = lax.map(expert_block, (xin, block_e)).reshape(n_slots, D)
    y = y_slots[dest].reshape(n_tok, TOP_K, D)
    out = jnp.einsum("tkd,tk->td", y, gates.astype(y.dtype))
    return out.reshape(B_, S_, D)


def setup_inputs(seed: int = 0) -> dict:
    key = jax.random.key(seed)
    ks = jax.random.split(key, 26)
    f32 = jnp.float32
    L = DEPTH

    def nrm(k, shape, scale):
        return jax.random.normal(k, shape, f32) * scale

    def gain(k, shape):
        return 1.0 + 0.02 * jax.random.normal(k, shape, f32)

    return {
        "x": nrm(ks[0], (BATCH, SEQ, D_MODEL), 1.0),
        "w_in": nrm(ks[1], (L, D_MODEL, IN_COLS), D_MODEL ** -0.5),
        "conv_w": nrm(ks[2], (L, CONV_K, CONV_WIDTH), CONV_K ** -0.5),
        "conv_norm_g": gain(ks[3], (L, CONV_WIDTH)),
        "rwkv_mu": jax.random.uniform(ks[4], (L, RWKV_COLS), f32, 0.0, 1.0),
        "w0": jax.random.uniform(ks[5], (L, RWKV_WIDTH), f32, -6.0, 0.0),
        "w_up": nrm(ks[6], (L, DECAY_LORA, RWKV_WIDTH), 0.5 * DECAY_LORA ** -0.5),
        "a0": nrm(ks[7], (L, RWKV_WIDTH), 0.5),
        "a_up": nrm(ks[8], (L, AAA_LORA, RWKV_WIDTH), 0.5 * AAA_LORA ** -0.5),
        "g_up": nrm(ks[9], (L, GATE_LORA, RWKV_WIDTH), GATE_LORA ** -0.5),
        "k_k": 0.85 + 0.05 * jax.random.normal(ks[10], (L, RWKV_WIDTH), f32),
        "k_a": 1.0 + 0.05 * jax.random.normal(ks[11], (L, RWKV_WIDTH), f32),
        "r_k": nrm(ks[12], (L, RWKV_HEADS, HEAD_DIM), 0.1),
        "gn_w": gain(ks[13], (L, RWKV_WIDTH)),
        "gn_b": nrm(ks[14], (L, RWKV_WIDTH), 0.01),
        "w_out": nrm(ks[15], (L, MIX_WIDTH, D_MODEL), MIX_WIDTH ** -0.5),
        "norm_mix_g": gain(ks[16], (L, D_MODEL)),
        "norm_ffn_g": gain(ks[17], (L, D_MODEL)),
        "w_router": nrm(ks[18], (L, D_MODEL, N_EXPERTS), D_MODEL ** -0.5),
        "b_router": nrm(ks[19], (L, N_EXPERTS), 0.01),
        "w_gu": nrm(ks[20], (L, N_EXPERTS, D_MODEL, 2 * D_FF), D_MODEL ** -0.5),
        "b_gu": nrm(ks[21], (L, N_EXPERTS, 2 * D_FF), 0.01),
        "w_down": nrm(ks[22], (L, N_EXPERTS, D_FF, D_MODEL), D_FF ** -0.5),
        "b_down": nrm(ks[23], (L, N_EXPERTS, D_MODEL), 0.01),
        "norm_final_g": gain(ks[24], (D_MODEL,)),
    }


def reference(x, w_in, conv_w, conv_norm_g, rwkv_mu, w0, w_up, a0, a_up, g_up, k_k, k_a, r_k,
              gn_w, gn_b, w_out, norm_mix_g, norm_ffn_g, w_router, b_router, w_gu, b_gu,
              w_down, b_down, norm_final_g):
    for l in range(DEPTH):
        h = rms_norm(x, norm_mix_g[l])
        p = h @ w_in[l]
        y_conv = short_conv_mixer(p[..., :CONV_COLS], conv_w[l], conv_norm_g[l])
        y_rwkv = rwkv7_mixer(p[..., CONV_COLS:], rwkv_mu[l], w0[l], w_up[l], a0[l], a_up[l],
                             g_up[l], k_k[l], k_a[l], r_k[l], gn_w[l], gn_b[l])
        y = jnp.concatenate([y_conv, y_rwkv], axis=-1)
        x = x + y @ w_out[l]
        h = rms_norm(x, norm_ffn_g[l])
        x = x + moe_ffn(h, w_router[l], b_router[l], w_gu[l], b_gu[l], w_down[l], b_down[l])
    return rms_norm(x, norm_final_g)
```

```python
import functools

import jax
import jax.numpy as jnp
from jax import lax
from jax.experimental import pallas as pl
from jax.experimental.pallas import tpu as pltpu

F32 = jnp.float32
BF16 = jnp.bfloat16
I32 = jnp.int32

HEAD_DIM = 64
TOP_K = 4
NORM_EPS = 1e-5
GN_EPS = HEAD_DIM * 1e-5
SWIGLU_LIMIT = 7.0
SWIGLU_ALPHA = 1.702

V7X_LANES = 128
V7X_SUBLANES = 8
V7X_VMEM_LIMIT_BYTES = 56 * 1024 * 1024

PREP_ROWS = 256
SCAN_FRAMES = 64
ROUTE_TOKENS = 1024
MOVE_TOKENS = 256
EXPERT_ROWS = 256


def _params(*semantics):
    return pltpu.CompilerParams(dimension_semantics=semantics,
                                vmem_limit_bytes=V7X_VMEM_LIMIT_BYTES)


def _split_bf16(x):
    hi = x.astype(BF16)
    lo = (x - hi.astype(F32)).astype(BF16)
    return hi, lo


def _seg_sum(x, seg2):
    hi, lo = _split_bf16(x)
    return jnp.dot(jnp.concatenate([hi, lo], axis=1), seg2, preferred_element_type=F32)


def _rms_rows(x, g):
    return x * lax.rsqrt(jnp.mean(x * x, axis=-1, keepdims=True) + NORM_EPS) * g


def _sigmoid(x):
    return 1.0 / (1.0 + jnp.exp(-x))


def _mix_prep_kernel(blocks_per_seq, cw, rw,
                     xprev_ref, x_ref, g_ref, w_in_ref, conv_w_ref, conv_g_ref, mu_ref,
                     w0_ref, w_up_ref, a0_ref, a_up_ref, g_up_ref, k_k_ref, k_a_ref, r_k_ref,
                     seg2_ref,
                     yconv_ref, r_ref, w_ref, k_ref, v_ref, kk_ref, b_ref, gate_ref, bonus_ref,
                     p_scr, z_scr):
    tm = x_ref.shape[0]
    halo = xprev_ref.shape[0]
    first = (pl.program_id(0) % blocks_per_seq) == 0
    xp = xprev_ref[...] * jnp.where(first, 0.0, 1.0)
    xa = jnp.concatenate([xp, x_ref[...]], axis=0)
    h = _rms_rows(xa, g_ref[...])
    p_scr[...] = jnp.dot(h.astype(BF16), w_in_ref[...], preferred_element_type=F32)
    seg2 = seg2_ref[...]

    z_scr[...] = p_scr[:, 2 * cw:3 * cw] * p_scr[:, 0:cw]
    conv = (conv_w_ref[0:1, :] * z_scr[halo - 2:halo - 2 + tm, :]
            + conv_w_ref[1:2, :] * z_scr[halo - 1:halo - 1 + tm, :]
            + conv_w_ref[2:3, :] * z_scr[halo:halo + tm, :])
    y = p_scr[halo:halo + tm, cw:2 * cw] * conv
    ms = _seg_sum(y * y, seg2) * (1.0 / HEAD_DIM)
    yconv_ref[...] = (y * lax.rsqrt(ms + NORM_EPS) * conv_g_ref[...]).astype(yconv_ref.dtype)

    c0 = 3 * cw
    cur = p_scr[halo:halo + tm, c0:]
    prev = p_scr[halo - 1:halo - 1 + tm, c0:]
    q = cur + (prev - cur) * mu_ref[...]
    r = q[:, 0:rw]
    k = q[:, rw:2 * rw]
    v = q[:, 2 * rw:3 * rw]
    lora_wa = q[:, 3 * rw:3 * rw + V7X_LANES]
    lora_g = q[:, 3 * rw + V7X_LANES:]
    w_lin = w0_ref[...] + jnp.dot(jnp.tanh(lora_wa).astype(BF16), w_up_ref[...],
                                  preferred_element_type=F32)
    neg = -w_lin
    softplus = jnp.maximum(neg, 0.0) + jnp.log(1.0 + jnp.exp(-jnp.abs(neg)))
    decay = jnp.exp(-jnp.exp(-softplus - 0.5))
    a = _sigmoid(a0_ref[...] + jnp.dot(lora_wa.astype(BF16), a_up_ref[...],
                                       preferred_element_type=F32))
    gate = jnp.dot(_sigmoid(lora_g).astype(BF16), g_up_ref[...], preferred_element_type=F32)
    kk = k * k_k_ref[...]
    kk = kk / jnp.maximum(jnp.sqrt(_seg_sum(kk * kk, seg2)), 1e-12)
    k_mod = k * (1.0 + (a - 1.0) * k_a_ref[...])
    bonus = _seg_sum(r * k_mod * r_k_ref[...], seg2) * v
    r_ref[...] = r
    w_ref[...] = decay
    k_ref[...] = k_mod
    v_ref[...] = v
    kk_ref[...] = kk
    b_ref[...] = kk * a
    gate_ref[...] = gate
    bonus_ref[...] = bonus


def _mix_prep(x2d, seq, norm_g, w_in, conv_w, conv_g, mu, w0, w_up, a0, a_up, g_up, k_k, k_a,
              r_k, seg2):
    n, d = x2d.shape
    tm = PREP_ROWS
    halo = V7X_SUBLANES
    cw = conv_w.shape[1]
    rw = w0.shape[1]
    in_cols = w_in.shape[1]
    assert seq % tm == 0 and n % tm == 0
    full = lambda a: pl.BlockSpec(a.shape, lambda i: (0,) * a.ndim)
    row_spec = lambda c: pl.BlockSpec((tm, c), lambda i: (i, 0))
    consts = (norm_g, w_in, conv_w, conv_g, mu, w0, w_up, a0, a_up, g_up, k_k, k_a, r_k, seg2)
    outs = [jax.ShapeDtypeStruct((n, cw), BF16)] + [jax.ShapeDtypeStruct((n, rw), F32)] * 8
    return pl.pallas_call(
        functools.partial(_mix_prep_kernel, seq // tm, cw, rw),
        grid=(n // tm,),
        in_specs=[pl.BlockSpec((halo, d), lambda i: (jnp.maximum(i * (tm // halo) - 1, 0), 0)),
                  row_spec(d)] + [full(c) for c in consts],
        out_specs=[row_spec(cw)] + [row_spec(rw)] * 8,
        out_shape=outs,
        scratch_shapes=[pltpu.VMEM((halo + tm, in_cols), F32), pltpu.VMEM((halo + tm, cw), F32)],
        compiler_params=_params("parallel"),
        name="mix_prep",
    )(x2d, x2d, *consts)


def _wkv_scan_kernel(n_batch, n_pairs,
                     r_ref, w_ref, k_ref, v_ref, kk_ref, b_ref, seg2_ref, o_ref,
                     state_ref, acc_ref):
    frames = r_ref.shape[1]
    lanes = V7X_LANES

    @pl.when(pl.program_id(0) == 0)
    def _():
        state_ref[...] = jnp.zeros_like(state_ref)

    acc_ref[...] = jnp.zeros_like(acc_ref)
    seg2 = seg2_ref[...]
    row_id = lax.broadcasted_iota(I32, (HEAD_DIM, lanes), 0)
    lane_in_head = lax.broadcasted_iota(I32, (HEAD_DIM, lanes), 1) % HEAD_DIM
    diag = lane_in_head == row_id

    def frame_group(gi, carry):
        t0 = pl.multiple_of(gi * V7X_SUBLANES, V7X_SUBLANES)
        for j in range(V7X_SUBLANES):
            hit = lane_in_head == t0 + j
            for bi in range(n_batch):
                for pi in range(n_pairs):
                    idx = bi * n_pairs + pi
                    sl = (bi, pl.ds(t0, V7X_SUBLANES), pl.ds(pi * lanes, lanes))
                    row = lambda ref: ref[sl][j:j + 1, :]
                    s = state_ref[idx]
                    sa = _seg_sum(s * row(kk_ref), seg2)
                    vcol = _seg_sum(jnp.where(diag, row(v_ref), 0.0), seg2)
                    s = s * row(w_ref) - sa * row(b_ref) + vcol * row(k_ref)
                    state_ref[idx] = s
                    o = _seg_sum(s * row(r_ref), seg2)
                    acc_ref[idx] = jnp.where(hit, o, acc_ref[idx])
        return carry

    lax.fori_loop(0, frames // V7X_SUBLANES, frame_group, 0)

    for bi in range(n_batch):
        for qi in range(n_pairs // 2):
            i0 = bi * n_pairs + 2 * qi
            tile = jnp.concatenate([acc_ref[i0], acc_ref[i0 + 1]], axis=0).T
            for pp in range(2):
                for hh in range(2):
                    c0 = (2 * qi + pp) * lanes + hh * HEAD_DIM
                    o_ref[bi, :, c0:c0 + HEAD_DIM] = tile[hh * HEAD_DIM:(hh + 1) * HEAD_DIM,
                                                          pp * HEAD_DIM:(pp + 1) * HEAD_DIM]


def _wkv_scan(r, w, k, v, kk, b, seg2_pair):
    n_batch, seq, rw = r.shape
    n_pairs = rw // V7X_LANES
    frames = SCAN_FRAMES
    assert frames == HEAD_DIM and seq % frames == 0 and n_pairs % 2 == 0
    spec = pl.BlockSpec((n_batch, frames, rw), lambda c: (0, c, 0))
    return pl.pallas_call(
        functools.partial(_wkv_scan_kernel, n_batch, n_pairs),
        grid=(seq // frames,),
        in_specs=[spec] * 6 + [pl.BlockSpec(seg2_pair.shape, lambda c: (0, 0))],
        out_specs=spec,
        out_shape=jax.ShapeDtypeStruct((n_batch, seq, rw), F32),
        scratch_shapes=[pltpu.VMEM((n_batch * n_pairs, HEAD_DIM, V7X_LANES), F32),
                        pltpu.VMEM((n_batch * n_pairs, HEAD_DIM, V7X_LANES), F32)],
        compiler_params=_params("arbitrary"),
        name="wkv_scan",
    )(r, w, k, v, kk, b, seg2_pair)


def _post_mix_kernel(n_experts,
                     o_ref, bonus_ref, gate_ref, yconv_ref, x_ref, gn_w_ref, gn_b_ref,
                     w_out_c_ref, w_out_r_ref, ffn_g_ref, wr_hi_ref, wr_lo_ref, b_router_ref,
                     seg2_ref,
                     x1_ref, h2_ref, top_e_ref, gates_ref):
    seg2 = seg2_ref[...]
    o = o_ref[...]
    mean = _seg_sum(o, seg2) * (1.0 / HEAD_DIM)
    cen = o - mean
    var = _seg_sum(cen * cen, seg2) * (1.0 / HEAD_DIM)
    o = cen * lax.rsqrt(var + GN_EPS) * gn_w_ref[...] + gn_b_ref[...]
    y_rwkv = (o + bonus_ref[...]) * gate_ref[...]
    x1 = (x_ref[...]
          + jnp.dot(yconv_ref[...], w_out_c_ref[...], preferred_element_type=F32)
          + jnp.dot(y_rwkv.astype(BF16), w_out_r_ref[...], preferred_element_type=F32))
    x1_ref[...] = x1
    h2 = _rms_rows(x1, ffn_g_ref[...])
    h2_ref[...] = h2

    h_hi, h_lo = _split_bf16(h2)
    nt = (((1,), (1,)), ((), ()))
    logits = (lax.dot_general(wr_hi_ref[...], h_hi, nt, preferred_element_type=F32)
              + lax.dot_general(wr_hi_ref[...], h_lo, nt, preferred_element_type=F32)
              + lax.dot_general(wr_lo_ref[...], h_hi, nt, preferred_element_type=F32)
              + b_router_ref[...])
    e_id = lax.broadcasted_iota(I32, logits.shape, 0).astype(F32)
    work = logits
    tops, ids = [], []
    for _ in range(TOP_K):
        m = jnp.max(work, axis=0, keepdims=True)
        sel = jnp.min(jnp.where(work == m, e_id, float(n_experts)), axis=0, keepdims=True)
        tops.append(m)
        ids.append(sel)
        work = jnp.where(e_id == sel, -jnp.inf, work)
    ex = [jnp.exp(t - tops[0]) for t in tops]
    denom = ex[0] + ex[1] + ex[2] + ex[3]
    top_e_ref[...] = jnp.concatenate(ids, axis=0).astype(I32)
    gates_ref[...] = jnp.concatenate([e / denom for e in ex], axis=0)


def _post_mix(o, bonus, gate, yconv, x2d, gn_w, gn_b, w_out_c, w_out_r, ffn_g, wr_hi, wr_lo,
              b_router, seg2):
    n, d = x2d.shape
    rw = o.shape[1]
    cw = yconv.shape[1]
    n_experts = wr_hi.shape[0]
    tm = PREP_ROWS
    full = lambda a: pl.BlockSpec(a.shape, lambda i: (0,) * a.ndim)
    row_spec = lambda c: pl.BlockSpec((tm, c), lambda i: (i, 0))
    col_spec = pl.BlockSpec((TOP_K, tm), lambda i: (0, i))
    consts = (gn_w, gn_b, w_out_c, w_out_r, ffn_g, wr_hi, wr_lo, b_router, seg2)
    return pl.pallas_call(
        functools.partial(_post_mix_kernel, n_experts),
        grid=(n // tm,),
        in_specs=[row_spec(rw), row_spec(rw), row_spec(rw), row_spec(cw), row_spec(d)]
                 + [full(c) for c in consts],
        out_specs=[row_spec(d), row_spec(d), col_spec, col_spec],
        out_shape=[jax.ShapeDtypeStruct((n, d), F32), jax.ShapeDtypeStruct((n, d), F32),
                   jax.ShapeDtypeStruct((TOP_K, n), I32), jax.ShapeDtypeStruct((TOP_K, n), F32)],
        compiler_params=_params("parallel"),
        name="post_mix",
    )(o, bonus, gate, yconv, x2d, *consts)


def _route_kernel(n_experts, block_rows,
                  top_e_ref, tri_ref, dest_ref, meta_ref,
                  count_ref, start_ref, carry_ref):
    phase = pl.program_id(0)
    j = pl.program_id(1)
    tb = top_e_ref.shape[1]
    e_id = lax.broadcasted_iota(I32, (n_experts, tb), 0)
    top_e = top_e_ref[...]
    onehot = jnp.zeros((n_experts, tb), F32)
    for c in range(TOP_K):
        onehot = onehot + jnp.where(top_e[c:c + 1, :] == e_id, 1.0, 0.0)
    block_count = jnp.sum(onehot, axis=1, keepdims=True)

    @pl.when((phase == 0) & (j == 0))
    def _():
        count_ref[...] = jnp.zeros_like(count_ref)

    @pl.when(phase == 0)
    def _():
        count_ref[...] += block_count

    @pl.when((phase == 1) & (j == 0))
    def _():
        counts = count_ref[...]
        padded = jnp.ceil(counts * (1.0 / block_rows)) * block_rows
        sub = lax.broadcasted_iota(I32, (n_experts, n_experts), 0)
        lane = lax.broadcasted_iota(I32, (n_experts, n_experts), 1)
        padded_row = jnp.sum(jnp.where(sub == lane, padded, 0.0), axis=0, keepdims=True)
        start = jnp.sum(jnp.where(lane < sub, padded_row, 0.0), axis=1, keepdims=True)
        start_ref[...] = start
        carry_ref[...] = jnp.zeros_like(carry_ref)
        end = start + padded
        nb = meta_ref.shape[1]
        slot0 = (lax.broadcasted_iota(I32, (n_experts, nb), 1) * block_rows).astype(F32)
        block_e = jnp.sum(jnp.where(end <= slot0, 1.0, 0.0), axis=0, keepdims=True)
        block_e = jnp.minimum(block_e, n_experts - 1.0)
        used = jnp.max(end, axis=0, keepdims=True) * (1.0 / block_rows)
        row = lax.broadcasted_iota(I32, meta_ref.shape, 0)
        meta_ref[...] = jnp.where(row == 0, block_e, used).astype(I32)

    @pl.when(phase == 1)
    def _():
        incl = jnp.dot(onehot.astype(BF16), tri_ref[...], preferred_element_type=F32)
        base = incl - onehot + carry_ref[...] + start_ref[...]
        rows = [jnp.sum(jnp.where(top_e[c:c + 1, :] == e_id, base, 0.0), axis=0, keepdims=True)
                for c in range(TOP_K)]
        dest_ref[...] = jnp.concatenate(rows, axis=0).astype(I32)
        carry_ref[...] += block_count


def _route(top_e, n_experts, n_blocks_padded):
    n = top_e.shape[1]
    tb = ROUTE_TOKENS
    assert n % tb == 0
    tri = (lax.broadcasted_iota(I32, (tb, tb), 0) <= lax.broadcasted_iota(I32, (tb, tb), 1)
           ).astype(BF16)
    return pl.pallas_call(
        functools.partial(_route_kernel, n_experts, EXPERT_ROWS),
        grid=(2, n // tb),
        in_specs=[pl.BlockSpec((TOP_K, tb), lambda ph, j: (0, j)),
                  pl.BlockSpec((tb, tb), lambda ph, j: (0, 0))],
        out_specs=[pl.BlockSpec((TOP_K, tb), lambda ph, j: (0, j * ph)),
                   pl.BlockSpec((V7X_SUBLANES, n_blocks_padded), lambda ph, j: (0, 0))],
        out_shape=[jax.ShapeDtypeStruct((TOP_K, n), I32),
                   jax.ShapeDtypeStruct((V7X_SUBLANES, n_blocks_padded), I32)],
        scratch_shapes=[pltpu.VMEM((n_experts, 1), F32)] * 3,
        compiler_params=_params("arbitrary", "arbitrary"),
        name="route",
    )(top_e, tri)


def _dispatch_kernel(dest_ref, h_ref, slots_in_ref, slots_ref, sem):
    del slots_in_ref
    tb = dest_ref.shape[2]
    base = pl.program_id(0) * tb

    def row_copy(src_row, dst_row):
        return pltpu.make_async_copy(h_ref.at[pl.ds(src_row, 1)], slots_ref.at[pl.ds(dst_row, 1)],
                                     sem)

    def issue(t, carry):
        for c in range(TOP_K):
            row_copy(base + t, dest_ref[0, c, t]).start()
        return carry

    lax.fori_loop(0, tb, issue, 0)
    pltpu.make_async_copy(h_ref.at[pl.ds(0, TOP_K * tb)], slots_ref.at[pl.ds(0, TOP_K * tb)],
                          sem).wait()


def _dispatch(dest_blocks, h2, n_slots):
    n, d = h2.shape
    tb = dest_blocks.shape[2]
    zeros = jnp.zeros((n_slots, d), h2.dtype)
    return pl.pallas_call(
        _dispatch_kernel,
        grid=(n // tb,),
        in_specs=[pl.BlockSpec((1, TOP_K, tb), lambda i: (i, 0, 0), memory_space=pltpu.SMEM),
                  pl.BlockSpec(memory_space=pl.ANY),
                  pl.BlockSpec(memory_space=pl.ANY)],
        out_specs=pl.BlockSpec(memory_space=pl.ANY),
        out_shape=jax.ShapeDtypeStruct((n_slots, d), h2.dtype),
        scratch_shapes=[pltpu.SemaphoreType.DMA(())],
        input_output_aliases={2: 0},
        compiler_params=_params("arbitrary"),
        name="dispatch",
    )(dest_blocks, h2, zeros)


def _experts_kernel(d_ff, meta_ref, x_ref, w_gu_ref, b_gu_ref, w_down_ref, b_down_ref, y_ref):
    j = pl.program_id(0)
    used = meta_ref[1, 0]

    @pl.when(j < used)
    def _():
        gu = jnp.dot(x_ref[...].astype(BF16), w_gu_ref[0], preferred_element_type=F32) + b_gu_ref[0]
        gate = jnp.minimum(gu[:, :d_ff], SWIGLU_LIMIT)
        up = jnp.clip(gu[:, d_ff:], -SWIGLU_LIMIT, SWIGLU_LIMIT)
        act = (up + 1.0) * (gate * _sigmoid(SWIGLU_ALPHA * gate))
        y_ref[...] = (jnp.dot(act.astype(BF16), w_down_ref[0], preferred_element_type=F32)
                      + b_down_ref[0])

    @pl.when(j >= used)
    def _():
        y_ref[...] = jnp.zeros_like(y_ref)


def _experts(meta, slots, w_gu, b_gu, w_down, b_down):
    n_slots, d = slots.shape
    n_experts, _, two_ff = w_gu.shape
    d_ff = two_ff // 2
    bm = EXPERT_ROWS
    grid_spec = pltpu.PrefetchScalarGridSpec(
        num_scalar_prefetch=1,
        grid=(n_slots // bm,),
        in_specs=[pl.BlockSpec((bm, d), lambda j, m: (jnp.minimum(j, m[1, 0] - 1), 0)),
                  pl.BlockSpec((1, d, two_ff), lambda j, m: (m[0, j], 0, 0)),
                  pl.BlockSpec((1, 1, two_ff), lambda j, m: (m[0, j], 0, 0)),
                  pl.BlockSpec((1, d_ff, d), lambda j, m: (m[0, j], 0, 0)),
                  pl.BlockSpec((1, 1, d), lambda j, m: (m[0, j], 0, 0))],
        out_specs=pl.BlockSpec((bm, d), lambda j, m: (j, 0)),
    )
    return pl.pallas_call(
        functools.partial(_experts_kernel, d_ff),
        grid_spec=grid_spec,
        out_shape=jax.ShapeDtypeStruct((n_slots, d), F32),
        compiler_params=_params("arbitrary"),
        name="experts",
    )(meta, slots, w_gu, b_gu, w_down, b_down)


def _combine_kernel(dest_ref, y_ref, x1_ref, gates_ref, g_ref, out_ref, buf, sem):
    tb = x1_ref.shape[0]

    def row_copy(src_row, c, t):
        return pltpu.make_async_copy(y_ref.at[pl.ds(src_row, 1)], buf.at[c, pl.ds(t, 1)], sem)

    def issue(t, carry):
        for c in range(TOP_K):
            row_copy(dest_ref[0, c, t], c, t).start()
        return carry

    lax.fori_loop(0, tb, issue, 0)
    for c in range(TOP_K):
        pltpu.make_async_copy(y_ref.at[pl.ds(0, tb)], buf.at[c], sem).wait()
    gates = gates_ref[...]
    acc = x1_ref[...]
    for c in range(TOP_K):
        acc = acc + buf[c] * gates[:, c:c + 1]
    out_ref[...] = _rms_rows(acc, g_ref[...])


def _combine(dest_blocks, y_slots, x1, gates_t, final_g):
    n, d = x1.shape
    tb = dest_blocks.shape[2]
    return pl.pallas_call(
        _combine_kernel,
        grid=(n // tb,),
        in_specs=[pl.BlockSpec((1, TOP_K, tb), lambda i: (i, 0, 0), memory_space=pltpu.SMEM),
                  pl.BlockSpec(memory_space=pl.ANY),
                  pl.BlockSpec((tb, d), lambda i: (i, 0)),
                  pl.BlockSpec((tb, TOP_K), lambda i: (i, 0)),
                  pl.BlockSpec((1, d), lambda i: (0, 0))],
        out_specs=pl.BlockSpec((tb, d), lambda i: (i, 0)),
        out_shape=jax.ShapeDtypeStruct((n, d), F32),
        scratch_shapes=[pltpu.VMEM((TOP_K, tb, d), F32), pltpu.SemaphoreType.DMA(())],
        compiler_params=_params("arbitrary"),
        name="combine",
    )(dest_blocks, y_slots, x1, gates_t, final_g)


def _group_matrix(width):
    a = lax.broadcasted_iota(I32, (width, width), 0) // HEAD_DIM
    b = lax.broadcasted_iota(I32, (width, width), 1) // HEAD_DIM
    g = (a == b).astype(BF16)
    return jnp.concatenate([g, g], axis=0)


def _row(vec):
    return vec.reshape(1, -1).astype(F32)


def kernel(x, w_in, conv_w, conv_norm_g, rwkv_mu, w0, w_up, a0, a_up, g_up, k_k, k_a, r_k,
           gn_w, gn_b, w_out, norm_mix_g, norm_ffn_g, w_router, b_router, w_gu, b_gu, w_down,
           b_down, norm_final_g):
    n_batch, seq, d = x.shape
    n = n_batch * seq
    depth = w_in.shape[0]
    cw = conv_w.shape[2]
    rw = w0.shape[1]
    n_experts = w_router.shape[2]
    decay_lora = w_up.shape[1]
    aaa_lora = a_up.shape[1]
    assert decay_lora + aaa_lora == V7X_LANES
    assert depth == 1
    n_slots = n * TOP_K + n_experts * EXPERT_ROWS
    n_blocks = n_slots // EXPERT_ROWS
    n_blocks_padded = -(-n_blocks // V7X_LANES) * V7X_LANES
    seg2 = _group_matrix(rw)
    seg2_pair = _group_matrix(V7X_LANES)

    x2d = x.reshape(n, d)
    for l in range(depth):
        w_up_pad = jnp.concatenate([w_up[l], jnp.zeros((aaa_lora, rw), F32)], axis=0).astype(BF16)
        a_up_pad = jnp.concatenate([jnp.zeros((decay_lora, rw), F32), a_up[l]], axis=0).astype(BF16)
        (yconv, r, w, k, v, kk, b, gate, bonus) = _mix_prep(
            x2d, seq, _row(norm_mix_g[l]), w_in[l].astype(BF16), conv_w[l].astype(F32),
            _row(conv_norm_g[l]), _row(rwkv_mu[l]), _row(w0[l]), w_up_pad, _row(a0[l]), a_up_pad,
            g_up[l].astype(BF16), _row(k_k[l]), _row(k_a[l]), _row(r_k[l]), seg2)
        shape3 = (n_batch, seq, rw)
        o = _wkv_scan(r.reshape(shape3), w.reshape(shape3), k.reshape(shape3), v.reshape(shape3),
                      kk.reshape(shape3), b.reshape(shape3), seg2_pair).reshape(n, rw)
        w_out_b = w_out[l].astype(BF16)
        wr_t = w_router[l].T.astype(F32)
        wr_hi = wr_t.astype(BF16)
        wr_lo = (wr_t - wr_hi.astype(F32)).astype(BF16)
        x1, h2, top_e, gates = _post_mix(
            o, bonus, gate, yconv, x2d, _row(gn_w[l]), _row(gn_b[l]), w_out_b[:cw], w_out_b[cw:],
            _row(norm_ffn_g[l]), wr_hi, wr_lo, b_router[l].reshape(n_experts, 1).astype(F32), seg2)
        dest, meta = _route(top_e, n_experts, n_blocks_padded)
        tb = MOVE_TOKENS
        dest_blocks = dest.reshape(TOP_K, n // tb, tb).transpose(1, 0, 2)
        slots = _dispatch(dest_blocks, h2, n_slots)
        y_slots = _experts(meta, slots, w_gu[l].astype(BF16),
                           b_gu[l].reshape(n_experts, 1, -1).astype(F32),
                           w_down[l].astype(BF16),
                           b_down[l].reshape(n_experts, 1, -1).astype(F32))
        x2d = _combine(dest_blocks, y_slots, x1, gates.T, _row(norm_final_g))
    return x2d.reshape(n_batch, seq, d)
```

```python
import functools

import jax
import jax.numpy as jnp
from jax import lax
from jax.experimental import pallas as pl
from jax.experimental.pallas import tpu as pltpu

F32 = jnp.float32
BF16 = jnp.bfloat16
I32 = jnp.int32

HEAD_DIM = 64
TOP_K = 4
NORM_EPS = 1e-5
GN_EPS = HEAD_DIM * 1e-5
SWIGLU_LIMIT = 7.0
SWIGLU_ALPHA = 1.702

V7X_LANES = 128
V7X_SUBLANES = 8
V7X_VMEM_LIMIT_BYTES = 56 * 1024 * 1024

PREP_ROWS = 256
SCAN_FRAMES = 64
SCAN_BATCHES_PER_GROUP = 2
ROUTE_TOKENS = 1024
MOVE_TOKENS = 256
EXPERT_ROWS = 256


def _params(*semantics):
    return pltpu.CompilerParams(dimension_semantics=semantics,
                                vmem_limit_bytes=V7X_VMEM_LIMIT_BYTES)


def _split_bf16(x):
    hi = x.astype(BF16)
    lo = (x - hi.astype(F32)).astype(BF16)
    return hi, lo


def _seg_sum(x, seg2):
    hi, lo = _split_bf16(x)
    return jnp.dot(jnp.concatenate([hi, lo], axis=1), seg2, preferred_element_type=F32)


def _rms_rows(x, g):
    return x * lax.rsqrt(jnp.mean(x * x, axis=-1, keepdims=True) + NORM_EPS) * g


def _sigmoid(x):
    return 1.0 / (1.0 + jnp.exp(-x))


def _mix_prep_kernel(blocks_per_seq, cw, rw,
                     xprev_ref, x_ref, g_ref, w_in_ref, conv_w_ref, conv_g_ref, mu_ref,
                     w0_ref, w_up_ref, a0_ref, a_up_ref, g_up_ref, k_k_ref, k_a_ref, r_k_ref,
                     seg2_ref,
                     yconv_ref, r_ref, w_ref, k_ref, v_ref, kk_ref, b_ref, gate_ref, bonus_ref,
                     p_scr, z_scr):
    tm = x_ref.shape[0]
    halo = xprev_ref.shape[0]
    first = (pl.program_id(0) % blocks_per_seq) == 0
    xp = xprev_ref[...] * jnp.where(first, 0.0, 1.0)
    xa = jnp.concatenate([xp, x_ref[...]], axis=0)
    h = _rms_rows(xa, g_ref[...])
    p_scr[...] = jnp.dot(h.astype(BF16), w_in_ref[...], preferred_element_type=F32)
    seg2 = seg2_ref[...]

    z_scr[...] = p_scr[:, 2 * cw:3 * cw] * p_scr[:, 0:cw]
    conv = (conv_w_ref[0:1, :] * z_scr[halo - 2:halo - 2 + tm, :]
            + conv_w_ref[1:2, :] * z_scr[halo - 1:halo - 1 + tm, :]
            + conv_w_ref[2:3, :] * z_scr[halo:halo + tm, :])
    y = p_scr[halo:halo + tm, cw:2 * cw] * conv
    ms = _seg_sum(y * y, seg2) * (1.0 / HEAD_DIM)
    yconv_ref[...] = (y * lax.rsqrt(ms + NORM_EPS) * conv_g_ref[...]).astype(yconv_ref.dtype)

    c0 = 3 * cw
    cur = p_scr[halo:halo + tm, c0:]
    prev = p_scr[halo - 1:halo - 1 + tm, c0:]
    q = cur + (prev - cur) * mu_ref[...]
    r = q[:, 0:rw]
    k = q[:, rw:2 * rw]
    v = q[:, 2 * rw:3 * rw]
    lora_wa = q[:, 3 * rw:3 * rw + V7X_LANES]
    lora_g = q[:, 3 * rw + V7X_LANES:]
    w_lin = w0_ref[...] + jnp.dot(jnp.tanh(lora_wa).astype(BF16), w_up_ref[...],
                                  preferred_element_type=F32)
    neg = -w_lin
    softplus = jnp.maximum(neg, 0.0) + jnp.log(1.0 + jnp.exp(-jnp.abs(neg)))
    log_decay = -jnp.exp(-softplus - 0.5)
    a = _sigmoid(a0_ref[...] + jnp.dot(lora_wa.astype(BF16), a_up_ref[...],
                                       preferred_element_type=F32))
    gate = jnp.dot(_sigmoid(lora_g).astype(BF16), g_up_ref[...], preferred_element_type=F32)
    kk = k * k_k_ref[...]
    kk = kk / jnp.maximum(jnp.sqrt(_seg_sum(kk * kk, seg2)), 1e-12)
    k_mod = k * (1.0 + (a - 1.0) * k_a_ref[...])
    bonus = _seg_sum(r * k_mod * r_k_ref[...], seg2) * v
    r_ref[...] = r
    w_ref[...] = log_decay
    k_ref[...] = k_mod
    v_ref[...] = v
    kk_ref[...] = kk
    b_ref[...] = kk * a
    gate_ref[...] = gate
    bonus_ref[...] = bonus


def _mix_prep(x2d, seq, norm_g, w_in, conv_w, conv_g, mu, w0, w_up, a0, a_up, g_up, k_k, k_a,
              r_k, seg2):
    n, d = x2d.shape
    tm = PREP_ROWS
    halo = V7X_SUBLANES
    cw = conv_w.shape[1]
    rw = w0.shape[1]
    in_cols = w_in.shape[1]
    assert seq % tm == 0 and n % tm == 0
    full = lambda a: pl.BlockSpec(a.shape, lambda i: (0,) * a.ndim)
    row_spec = lambda c: pl.BlockSpec((tm, c), lambda i: (i, 0))
    consts = (norm_g, w_in, conv_w, conv_g, mu, w0, w_up, a0, a_up, g_up, k_k, k_a, r_k, seg2)
    outs = [jax.ShapeDtypeStruct((n, cw), BF16)] + [jax.ShapeDtypeStruct((n, rw), F32)] * 8
    return pl.pallas_call(
        functools.partial(_mix_prep_kernel, seq // tm, cw, rw),
        grid=(n // tm,),
        in_specs=[pl.BlockSpec((halo, d), lambda i: (jnp.maximum(i * (tm // halo) - 1, 0), 0)),
                  row_spec(d)] + [full(c) for c in consts],
        out_specs=[row_spec(cw)] + [row_spec(rw)] * 8,
        out_shape=outs,
        scratch_shapes=[pltpu.VMEM((halo + tm, in_cols), F32), pltpu.VMEM((halo + tm, cw), F32)],
        compiler_params=_params("parallel"),
        name="mix_prep",
    )(x2d, x2d, *consts)


def _bdot(a, b):
    return jnp.dot(a.astype(BF16), b.astype(BF16), preferred_element_type=F32)


def _bdot_nt(a, b):
    return lax.dot_general(a.astype(BF16), b.astype(BF16), (((1,), (1,)), ((), ())),
                           preferred_element_type=F32)


def _wkv_chunk_kernel(n_batch, n_pairs,
                      r_ref, lw_ref, k_ref, v_ref, kk_ref, b_ref, o_ref, state_ref):
    frames = r_ref.shape[1]
    lanes = V7X_LANES
    hd = HEAD_DIM
    assert frames == hd

    @pl.when(pl.program_id(0) == 0)
    def _():
        state_ref[...] = jnp.zeros_like(state_ref)

    row = lax.broadcasted_iota(I32, (lanes, lanes), 0)
    lane = lax.broadcasted_iota(I32, (lanes, lanes), 1)
    same_head = (row // hd) == (lane // hd)
    rt = row % hd
    ls = lane % hd
    strict_same = same_head & (rt > ls)
    strict_cross = jnp.logical_not(same_head) & (rt > ls)
    incl_same = same_head & (rt >= ls)
    incl_cross = jnp.logical_not(same_head) & (rt >= ls)
    eye = row == lane
    level_masks = []
    m = 1
    while m < hd:
        level_masks.append(same_head & ((rt // (2 * m)) == (ls // (2 * m)))
                           & (((rt // m) % 2) == 1) & (((ls // m) % 2) == 0))
        m *= 2
    left = lax.broadcasted_iota(I32, (frames, lanes), 1) < hd
    tri = (lax.broadcasted_iota(I32, (frames, frames), 0)
           >= lax.broadcasted_iota(I32, (frames, frames), 1)).astype(BF16)

    def diag_blocks(x):
        return jnp.concatenate([jnp.where(left, x, 0.0), jnp.where(left, 0.0, x)], axis=0)

    def cross_blocks(x):
        return jnp.concatenate([jnp.where(left, 0.0, x), jnp.where(left, x, 0.0)], axis=0)

    def batch_prep(bi):
        lw = lw_ref[bi]
        hi = lw.astype(BF16)
        rem = lw - hi.astype(F32)
        mid = rem.astype(BF16)
        lo = (rem - mid.astype(F32)).astype(BF16)
        cs = (jnp.dot(tri, hi, preferred_element_type=F32)
              + jnp.dot(tri, mid, preferred_element_type=F32)
              + jnp.dot(tri, lo, preferred_element_type=F32))
        cs_end = cs[frames - 1:frames, :]
        e_neg = jnp.exp(-cs)
        e_end = jnp.exp(cs_end - cs)
        b_in = b_ref[bi]
        k_in = k_ref[bi]
        return dict(a_t=-kk_ref[bi] * jnp.exp(cs - lw), b_t=b_in * e_neg, k_t=k_in * e_neg,
                    r_t=r_ref[bi] * jnp.exp(cs), b_h=b_in * e_end, k_h=k_in * e_end,
                    v=v_ref[bi], g_end=jnp.exp(cs_end))

    for b0 in range(0, n_batch, SCAN_BATCHES_PER_GROUP):
        group = []
        for bi in range(b0, min(b0 + SCAN_BATCHES_PER_GROUP, n_batch)):
            prep = batch_prep(bi)
            for pi in range(n_pairs):
                sl = slice(pi * lanes, (pi + 1) * lanes)
                group.append((bi, sl, bi * n_pairs + pi, {n: x[:, sl] for n, x in prep.items()}))
        a_bd = [diag_blocks(p["a_t"]) for _, _, _, p in group]
        r_bd = [diag_blocks(p["r_t"]) for _, _, _, p in group]
        d0 = [_bdot_nt(jnp.concatenate([a[:hd], r[:hd]], axis=0),
                       jnp.concatenate([p["b_t"], p["k_t"]], axis=0))
              for a, r, (_, _, _, p) in zip(a_bd, r_bd, group)]
        d1 = [_bdot_nt(jnp.concatenate([a[hd:], r[hd:]], axis=0),
                       jnp.concatenate([p["k_t"], p["b_t"]], axis=0))
              for a, r, (_, _, _, p) in zip(a_bd, r_bd, group)]
        a_rows = [jnp.concatenate([x[:hd], y[:hd]], axis=0) for x, y in zip(d0, d1)]
        m_rows = [jnp.concatenate([x[hd:], y[hd:]], axis=0) for x, y in zip(d0, d1)]
        a_ab = [jnp.where(strict_same, x, 0.0) for x in a_rows]
        a_ak = [jnp.where(strict_cross, x, 0.0) for x in a_rows]
        m_rb = [jnp.where(incl_same, x, 0.0) for x in m_rows]
        m_rk = [jnp.where(incl_cross, x, 0.0) for x in m_rows]
        t_inv = [jnp.where(eye, 1.0, jnp.where(level_masks[0], x, 0.0)) for x in a_ab]
        for mask in level_masks[1:]:
            half = [_bdot(t, jnp.where(mask, x, 0.0)) for t, x in zip(t_inv, a_ab)]
            t_inv = [t + _bdot(h, t) for t, h in zip(t_inv, half)]
        v_x = [cross_blocks(p["v"]) for _, _, _, p in group]
        akv = [_bdot(x, v) for x, v in zip(a_ak, v_x)]
        wu = [_bdot(t, jnp.concatenate([a, x], axis=1))
              for t, a, x in zip(t_inv, a_bd, akv)]
        bh_t = [diag_blocks(p["b_h"]).T for _, _, _, p in group]
        kh_t = [cross_blocks(p["k_h"]).T for _, _, _, p in group]
        pw_rw = [_bdot(jnp.concatenate([bt, m], axis=0), x[:, :lanes])
                 for bt, m, x in zip(bh_t, m_rb, wu)]
        q_o = [_bdot(jnp.concatenate([jnp.concatenate([bt, kt], axis=1),
                                      jnp.concatenate([mb, mk], axis=1)], axis=0),
                     jnp.concatenate([x[:, lanes:], v], axis=0))
               for bt, kt, mb, mk, x, v in zip(bh_t, kh_t, m_rb, m_rk, wu, v_x)]
        res = [_bdot(jnp.concatenate([r + pr[lanes:],
                                      pr[:lanes] + jnp.where(eye, p["g_end"], 0.0)], axis=0),
                     state_ref[idx])
               for r, pr, (_, _, idx, p) in zip(r_bd, pw_rw, group)]
        for x, q, (bi, sl, idx, _) in zip(res, q_o, group):
            o_bd = x[:lanes] + q[lanes:]
            state_ref[idx] = x[lanes:] + q[:lanes]
            o_ref[bi, :, sl] = jnp.where(left, o_bd[:hd], o_bd[hd:])


def _wkv_chunk(r, lw, k, v, kk, b):
    n_batch, seq, rw = r.shape
    n_pairs = rw // V7X_LANES
    frames = SCAN_FRAMES
    assert seq % frames == 0
    spec = pl.BlockSpec((n_batch, frames, rw), lambda c: (0, c, 0))
    return pl.pallas_call(
        functools.partial(_wkv_chunk_kernel, n_batch, n_pairs),
        grid=(seq // frames,),
        in_specs=[spec] * 6,
        out_specs=spec,
        out_shape=jax.ShapeDtypeStruct((n_batch, seq, rw), F32),
        scratch_shapes=[pltpu.VMEM((n_batch * n_pairs, V7X_LANES, V7X_LANES), F32)],
        compiler_params=_params("arbitrary"),
        name="wkv_chunk",
    )(r, lw, k, v, kk, b)


def _post_mix_kernel(n_experts,
                     o_ref, bonus_ref, gate_ref, yconv_ref, x_ref, gn_w_ref, gn_b_ref,
                     w_out_c_ref, w_out_r_ref, ffn_g_ref, wr_hi_ref, wr_lo_ref, b_router_ref,
                     seg2_ref,
                     x1_ref, h2_ref, top_e_ref, gates_ref):
    seg2 = seg2_ref[...]
    o = o_ref[...]
    mean = _seg_sum(o, seg2) * (1.0 / HEAD_DIM)
    cen = o - mean
    var = _seg_sum(cen * cen, seg2) * (1.0 / HEAD_DIM)
    o = cen * lax.rsqrt(var + GN_EPS) * gn_w_ref[...] + gn_b_ref[...]
    y_rwkv = (o + bonus_ref[...]) * gate_ref[...]
    x1 = (x_ref[...]
          + jnp.dot(yconv_ref[...], w_out_c_ref[...], preferred_element_type=F32)
          + jnp.dot(y_rwkv.astype(BF16), w_out_r_ref[...], preferred_element_type=F32))
    x1_ref[...] = x1
    h2 = _rms_rows(x1, ffn_g_ref[...])
    h2_ref[...] = h2

    h_hi, h_lo = _split_bf16(h2)
    nt = (((1,), (1,)), ((), ()))
    logits = (lax.dot_general(wr_hi_ref[...], h_hi, nt, preferred_element_type=F32)
              + lax.dot_general(wr_hi_ref[...], h_lo, nt, preferred_element_type=F32)
              + lax.dot_general(wr_lo_ref[...], h_hi, nt, preferred_element_type=F32)
              + b_router_ref[...])
    e_id = lax.broadcasted_iota(I32, logits.shape, 0).astype(F32)
    work = logits
    tops, ids = [], []
    for _ in range(TOP_K):
        m = jnp.max(work, axis=0, keepdims=True)
        sel = jnp.min(jnp.where(work == m, e_id, float(n_experts)), axis=0, keepdims=True)
        tops.append(m)
        ids.append(sel)
        work = jnp.where(e_id == sel, -jnp.inf, work)
    ex = [jnp.exp(t - tops[0]) for t in tops]
    denom = ex[0] + ex[1] + ex[2] + ex[3]
    top_e_ref[...] = jnp.concatenate(ids, axis=0).astype(I32)
    gates_ref[...] = jnp.concatenate([e / denom for e in ex], axis=0)


def _post_mix(o, bonus, gate, yconv, x2d, gn_w, gn_b, w_out_c, w_out_r, ffn_g, wr_hi, wr_lo,
              b_router, seg2):
    n, d = x2d.shape
    rw = o.shape[1]
    cw = yconv.shape[1]
    n_experts = wr_hi.shape[0]
    tm = PREP_ROWS
    full = lambda a: pl.BlockSpec(a.shape, lambda i: (0,) * a.ndim)
    row_spec = lambda c: pl.BlockSpec((tm, c), lambda i: (i, 0))
    col_spec = pl.BlockSpec((TOP_K, tm), lambda i: (0, i))
    consts = (gn_w, gn_b, w_out_c, w_out_r, ffn_g, wr_hi, wr_lo, b_router, seg2)
    return pl.pallas_call(
        functools.partial(_post_mix_kernel, n_experts),
        grid=(n // tm,),
        in_specs=[row_spec(rw), row_spec(rw), row_spec(rw), row_spec(cw), row_spec(d)]
                 + [full(c) for c in consts],
        out_specs=[row_spec(d), row_spec(d), col_spec, col_spec],
        out_shape=[jax.ShapeDtypeStruct((n, d), F32), jax.ShapeDtypeStruct((n, d), F32),
                   jax.ShapeDtypeStruct((TOP_K, n), I32), jax.ShapeDtypeStruct((TOP_K, n), F32)],
        compiler_params=_params("parallel"),
        name="post_mix",
    )(o, bonus, gate, yconv, x2d, *consts)


def _route_kernel(n_experts, block_rows,
                  top_e_ref, tri_ref, dest_ref, meta_ref,
                  count_ref, start_ref, carry_ref):
    phase = pl.program_id(0)
    j = pl.program_id(1)
    tb = top_e_ref.shape[1]
    e_id = lax.broadcasted_iota(I32, (n_experts, tb), 0)
    top_e = top_e_ref[...]
    onehot = jnp.zeros((n_experts, tb), F32)
    for c in range(TOP_K):
        onehot = onehot + jnp.where(top_e[c:c + 1, :] == e_id, 1.0, 0.0)
    block_count = jnp.sum(onehot, axis=1, keepdims=True)

    @pl.when((phase == 0) & (j == 0))
    def _():
        count_ref[...] = jnp.zeros_like(count_ref)

    @pl.when(phase == 0)
    def _():
        count_ref[...] += block_count

    @pl.when((phase == 1) & (j == 0))
    def _():
        counts = count_ref[...]
        padded = jnp.ceil(counts * (1.0 / block_rows)) * block_rows
        sub = lax.broadcasted_iota(I32, (n_experts, n_experts), 0)
        lane = lax.broadcasted_iota(I32, (n_experts, n_experts), 1)
        padded_row = jnp.sum(jnp.where(sub == lane, padded, 0.0), axis=0, keepdims=True)
        start = jnp.sum(jnp.where(lane < sub, padded_row, 0.0), axis=1, keepdims=True)
        start_ref[...] = start
        carry_ref[...] = jnp.zeros_like(carry_ref)
        end = start + padded
        nb = meta_ref.shape[1]
        slot0 = (lax.broadcasted_iota(I32, (n_experts, nb), 1) * block_rows).astype(F32)
        block_e = jnp.sum(jnp.where(end <= slot0, 1.0, 0.0), axis=0, keepdims=True)
        block_e = jnp.minimum(block_e, n_experts - 1.0)
        used = jnp.max(end, axis=0, keepdims=True) * (1.0 / block_rows)
        row = lax.broadcasted_iota(I32, meta_ref.shape, 0)
        meta_ref[...] = jnp.where(row == 0, block_e, used).astype(I32)

    @pl.when(phase == 1)
    def _():
        incl = jnp.dot(onehot.astype(BF16), tri_ref[...], preferred_element_type=F32)
        base = incl - onehot + carry_ref[...] + start_ref[...]
        rows = [jnp.sum(jnp.where(top_e[c:c + 1, :] == e_id, base, 0.0), axis=0, keepdims=True)
                for c in range(TOP_K)]
        dest_ref[...] = jnp.concatenate(rows, axis=0).astype(I32)
        carry_ref[...] += block_count


def _route(top_e, n_experts, n_blocks_padded):
    n = top_e.shape[1]
    tb = ROUTE_TOKENS
    assert n % tb == 0
    tri = (lax.broadcasted_iota(I32, (tb, tb), 0) <= lax.broadcasted_iota(I32, (tb, tb), 1)
           ).astype(BF16)
    return pl.pallas_call(
        functools.partial(_route_kernel, n_experts, EXPERT_ROWS),
        grid=(2, n // tb),
        in_specs=[pl.BlockSpec((TOP_K, tb), lambda ph, j: (0, j)),
                  pl.BlockSpec((tb, tb), lambda ph, j: (0, 0))],
        out_specs=[pl.BlockSpec((TOP_K, tb), lambda ph, j: (0, j * ph)),
                   pl.BlockSpec((V7X_SUBLANES, n_blocks_padded), lambda ph, j: (0, 0))],
        out_shape=[jax.ShapeDtypeStruct((TOP_K, n), I32),
                   jax.ShapeDtypeStruct((V7X_SUBLANES, n_blocks_padded), I32)],
        scratch_shapes=[pltpu.VMEM((n_experts, 1), F32)] * 3,
        compiler_params=_params("arbitrary", "arbitrary"),
        name="route",
    )(top_e, tri)


def _dispatch_kernel(dest_ref, h_ref, slots_in_ref, slots_ref, sem):
    del slots_in_ref
    tb = dest_ref.shape[2]

    def row_copy(src_row, dst_row):
        return pltpu.make_async_copy(h_ref.at[pl.ds(src_row, 1)], slots_ref.at[pl.ds(dst_row, 1)],
                                     sem)

    def issue(t, carry):
        for c in range(TOP_K):
            row_copy(t, dest_ref[0, c, t]).start()
        return carry

    lax.fori_loop(0, tb, issue, 0)
    for c in range(TOP_K):
        pltpu.make_async_copy(h_ref, slots_ref.at[pl.ds(0, tb)], sem).wait()


def _dispatch(dest_blocks, h2, n_slots):
    n, d = h2.shape
    tb = dest_blocks.shape[2]
    zeros = jnp.zeros((n_slots, d), h2.dtype)
    return pl.pallas_call(
        _dispatch_kernel,
        grid=(n // tb,),
        in_specs=[pl.BlockSpec((1, TOP_K, tb), lambda i: (i, 0, 0), memory_space=pltpu.SMEM),
                  pl.BlockSpec((tb, d), lambda i: (i, 0)),
                  pl.BlockSpec(memory_space=pl.ANY)],
        out_specs=pl.BlockSpec(memory_space=pl.ANY),
        out_shape=jax.ShapeDtypeStruct((n_slots, d), h2.dtype),
        scratch_shapes=[pltpu.SemaphoreType.DMA(())],
        input_output_aliases={2: 0},
        compiler_params=_params("arbitrary"),
        name="dispatch",
    )(dest_blocks, h2, zeros)


def _experts_kernel(d_ff, meta_ref, x_ref, w_gu_ref, b_gu_ref, w_down_ref, b_down_ref, y_ref):
    j = pl.program_id(0)
    used = meta_ref[1, 0]

    @pl.when(j < used)
    def _():
        gu = jnp.dot(x_ref[...].astype(BF16), w_gu_ref[0], preferred_element_type=F32) + b_gu_ref[0]
        gate = jnp.minimum(gu[:, :d_ff], SWIGLU_LIMIT)
        up = jnp.clip(gu[:, d_ff:], -SWIGLU_LIMIT, SWIGLU_LIMIT)
        act = (up + 1.0) * (gate * _sigmoid(SWIGLU_ALPHA * gate))
        y_ref[...] = (jnp.dot(act.astype(BF16), w_down_ref[0], preferred_element_type=F32)
                      + b_down_ref[0])

    @pl.when(j >= used)
    def _():
        y_ref[...] = jnp.zeros_like(y_ref)


def _experts(meta, slots, w_gu, b_gu, w_down, b_down):
    n_slots, d = slots.shape
    n_experts, _, two_ff = w_gu.shape
    d_ff = two_ff // 2
    bm = EXPERT_ROWS
    grid_spec = pltpu.PrefetchScalarGridSpec(
        num_scalar_prefetch=1,
        grid=(n_slots // bm,),
        in_specs=[pl.BlockSpec((bm, d), lambda j, m: (jnp.minimum(j, m[1, 0] - 1), 0)),
                  pl.BlockSpec((1, d, two_ff), lambda j, m: (m[0, j], 0, 0)),
                  pl.BlockSpec((1, 1, two_ff), lambda j, m: (m[0, j], 0, 0)),
                  pl.BlockSpec((1, d_ff, d), lambda j, m: (m[0, j], 0, 0)),
                  pl.BlockSpec((1, 1, d), lambda j, m: (m[0, j], 0, 0))],
        out_specs=pl.BlockSpec((bm, d), lambda j, m: (j, 0)),
    )
    return pl.pallas_call(
        functools.partial(_experts_kernel, d_ff),
        grid_spec=grid_spec,
        out_shape=jax.ShapeDtypeStruct((n_slots, d), F32),
        compiler_params=_params("arbitrary"),
        name="experts",
    )(meta, slots, w_gu, b_gu, w_down, b_down)


def _combine_kernel(dest_ref, y_ref, x1_ref, gates_ref, g_ref, out_ref, buf, sem):
    tb = x1_ref.shape[0]

    def row_copy(src_row, c, t):
        return pltpu.make_async_copy(y_ref.at[pl.ds(src_row, 1)], buf.at[c, pl.ds(t, 1)], sem)

    def issue(t, carry):
        for c in range(TOP_K):
            row_copy(dest_ref[0, c, t], c, t).start()
        return carry

    lax.fori_loop(0, tb, issue, 0)
    for c in range(TOP_K):
        pltpu.make_async_copy(y_ref.at[pl.ds(0, tb)], buf.at[c], sem).wait()
    gates = gates_ref[...]
    acc = x1_ref[...]
    for c in range(TOP_K):
        acc = acc + buf[c] * gates[:, c:c + 1]
    out_ref[...] = _rms_rows(acc, g_ref[...])


def _combine(dest_blocks, y_slots, x1, gates_t, final_g):
    n, d = x1.shape
    tb = dest_blocks.shape[2]
    return pl.pallas_call(
        _combine_kernel,
        grid=(n // tb,),
        in_specs=[pl.BlockSpec((1, TOP_K, tb), lambda i: (i, 0, 0), memory_space=pltpu.SMEM),
                  pl.BlockSpec(memory_space=pl.ANY),
                  pl.BlockSpec((tb, d), lambda i: (i, 0)),
                  pl.BlockSpec((tb, TOP_K), lambda i: (i, 0)),
                  pl.BlockSpec((1, d), lambda i: (0, 0))],
        out_specs=pl.BlockSpec((tb, d), lambda i: (i, 0)),
        out_shape=jax.ShapeDtypeStruct((n, d), F32),
        scratch_shapes=[pltpu.VMEM((TOP_K, tb, d), F32), pltpu.SemaphoreType.DMA(())],
        compiler_params=_params("arbitrary"),
        name="combine",
    )(dest_blocks, y_slots, x1, gates_t, final_g)


def _group_matrix(width):
    a = lax.broadcasted_iota(I32, (width, width), 0) // HEAD_DIM
    b = lax.broadcasted_iota(I32, (width, width), 1) // HEAD_DIM
    g = (a == b).astype(BF16)
    return jnp.concatenate([g, g], axis=0)


def _row(vec):
    return vec.reshape(1, -1).astype(F32)


def kernel(x, w_in, conv_w, conv_norm_g, rwkv_mu, w0, w_up, a0, a_up, g_up, k_k, k_a, r_k,
           gn_w, gn_b, w_out, norm_mix_g, norm_ffn_g, w_router, b_router, w_gu, b_gu, w_down,
           b_down, norm_final_g):
    n_batch, seq, d = x.shape
    n = n_batch * seq
    depth = w_in.shape[0]
    cw = conv_w.shape[2]
    rw = w0.shape[1]
    n_experts = w_router.shape[2]
    decay_lora = w_up.shape[1]
    aaa_lora = a_up.shape[1]
    assert decay_lora + aaa_lora == V7X_LANES
    assert depth == 1
    n_slots = n * TOP_K + n_experts * EXPERT_ROWS
    n_blocks = n_slots // EXPERT_ROWS
    n_blocks_padded = -(-n_blocks // V7X_LANES) * V7X_LANES
    seg2 = _group_matrix(rw)

    x2d = x.reshape(n, d)
    for l in range(depth):
        w_up_pad = jnp.concatenate([w_up[l], jnp.zeros((aaa_lora, rw), F32)], axis=0).astype(BF16)
        a_up_pad = jnp.concatenate([jnp.zeros((decay_lora, rw), F32), a_up[l]], axis=0).astype(BF16)
        (yconv, r, lw, k, v, kk, b, gate, bonus) = _mix_prep(
            x2d, seq, _row(norm_mix_g[l]), w_in[l].astype(BF16), conv_w[l].astype(F32),
            _row(conv_norm_g[l]), _row(rwkv_mu[l]), _row(w0[l]), w_up_pad, _row(a0[l]), a_up_pad,
            g_up[l].astype(BF16), _row(k_k[l]), _row(k_a[l]), _row(r_k[l]), seg2)
        shape3 = (n_batch, seq, rw)
        o = _wkv_chunk(r.reshape(shape3), lw.reshape(shape3), k.reshape(shape3),
                       v.reshape(shape3), kk.reshape(shape3), b.reshape(shape3)).reshape(n, rw)
        w_out_b = w_out[l].astype(BF16)
        wr_t = w_router[l].T.astype(F32)
        wr_hi = wr_t.astype(BF16)
        wr_lo = (wr_t - wr_hi.astype(F32)).astype(BF16)
        x1, h2, top_e, gates = _post_mix(
            o, bonus, gate, yconv, x2d, _row(gn_w[l]), _row(gn_b[l]), w_out_b[:cw], w_out_b[cw:],
            _row(norm_ffn_g[l]), wr_hi, wr_lo, b_router[l].reshape(n_experts, 1).astype(F32), seg2)
        dest, meta = _route(top_e, n_experts, n_blocks_padded)
        tb = MOVE_TOKENS
        dest_blocks = dest.reshape(TOP_K, n // tb, tb).transpose(1, 0, 2)
        slots = _dispatch(dest_blocks, h2, n_slots)
        y_slots = _experts(meta, slots, w_gu[l].astype(BF16),
                           b_gu[l].reshape(n_experts, 1, -1).astype(F32),
                           w_down[l].astype(BF16),
                           b_down[l].reshape(n_experts, 1, -1).astype(F32))
        x2d = _combine(dest_blocks, y_slots, x1, gates.T, _row(norm_final_g))
    return x2d.reshape(n_batch, seq, d)
```

```python
import functools

import jax
import jax.numpy as jnp
from jax import lax
from jax.experimental import pallas as pl
from jax.experimental.pallas import tpu as pltpu

F32 = jnp.float32
BF16 = jnp.bfloat16
I32 = jnp.int32

HEAD_DIM = 64
TOP_K = 4
NORM_EPS = 1e-5
GN_EPS = HEAD_DIM * 1e-5
SWIGLU_LIMIT = 7.0
SWIGLU_ALPHA = 1.702

V7X_LANES = 128
V7X_SUBLANES = 8
V7X_VMEM_LIMIT_BYTES = 56 * 1024 * 1024

PREP_ROWS = 256
SCAN_FRAMES = 64
SCAN_BATCHES_PER_GROUP = 2
ROUTE_TOKENS = 1024
MOVE_TOKENS = 256
TOKENS_PER_DEST_ROW = V7X_LANES // TOP_K
META_BLOCK_EXPERT = 0
META_USED_BLOCKS = 1
META_LAST_BLOCK = 2
EXPERT_ROWS = 512

def _params(*semantics):
    return pltpu.CompilerParams(dimension_semantics=semantics,
                                vmem_limit_bytes=V7X_VMEM_LIMIT_BYTES)


def _split_bf16(x):
    hi = x.astype(BF16)
    lo = (x - hi.astype(F32)).astype(BF16)
    return hi, lo


def _seg_sum(x, seg2):
    hi, lo = _split_bf16(x)
    return jnp.dot(jnp.concatenate([hi, lo], axis=1), seg2, preferred_element_type=F32)


def _rms_rows(x, g):
    return x * lax.rsqrt(jnp.mean(x * x, axis=-1, keepdims=True) + NORM_EPS) * g


def _sigmoid(x):
    return 1.0 / (1.0 + jnp.exp(-x))


def _mix_prep_kernel(blocks_per_seq, cw, rw,
                     xprev_ref, x_ref, g_ref, w_in_ref, conv_w_ref, conv_g_ref, mu_ref,
                     w0_ref, w_up_ref, a0_ref, a_up_ref, g_up_ref, k_k_ref, k_a_ref, r_k_ref,
                     seg2_ref,
                     yconv_ref, r_ref, w_ref, k_ref, v_ref, kk_ref, b_ref, gate_ref, bonus_ref,
                     p_scr, z_scr):
    tm = x_ref.shape[0]
    halo = xprev_ref.shape[0]
    first = (pl.program_id(0) % blocks_per_seq) == 0
    xp = xprev_ref[...] * jnp.where(first, 0.0, 1.0)
    xa = jnp.concatenate([xp, x_ref[...]], axis=0)
    h = _rms_rows(xa, g_ref[...])
    p_scr[...] = jnp.dot(h.astype(BF16), w_in_ref[...], preferred_element_type=F32)
    seg2 = seg2_ref[...]

    z_scr[...] = p_scr[:, 2 * cw:3 * cw] * p_scr[:, 0:cw]
    conv = (conv_w_ref[0:1, :] * z_scr[halo - 2:halo - 2 + tm, :]
            + conv_w_ref[1:2, :] * z_scr[halo - 1:halo - 1 + tm, :]
            + conv_w_ref[2:3, :] * z_scr[halo:halo + tm, :])
    y = p_scr[halo:halo + tm, cw:2 * cw] * conv
    ms = _seg_sum(y * y, seg2) * (1.0 / HEAD_DIM)
    yconv_ref[...] = (y * lax.rsqrt(ms + NORM_EPS) * conv_g_ref[...]).astype(yconv_ref.dtype)

    c0 = 3 * cw
    cur = p_scr[halo:halo + tm, c0:]
    prev = p_scr[halo - 1:halo - 1 + tm, c0:]
    q = cur + (prev - cur) * mu_ref[...]
    r = q[:, 0:rw]
    k = q[:, rw:2 * rw]
    v = q[:, 2 * rw:3 * rw]
    lora_wa = q[:, 3 * rw:3 * rw + V7X_LANES]
    lora_g = q[:, 3 * rw + V7X_LANES:]
    w_lin = w0_ref[...] + jnp.dot(jnp.tanh(lora_wa).astype(BF16), w_up_ref[...],
                                  preferred_element_type=F32)
    neg = -w_lin
    softplus = jnp.maximum(neg, 0.0) + jnp.log(1.0 + jnp.exp(-jnp.abs(neg)))
    log_decay = -jnp.exp(-softplus - 0.5)
    a = _sigmoid(a0_ref[...] + jnp.dot(lora_wa.astype(BF16), a_up_ref[...],
                                       preferred_element_type=F32))
    gate = jnp.dot(_sigmoid(lora_g).astype(BF16), g_up_ref[...], preferred_element_type=F32)
    kk = k * k_k_ref[...]
    kk = kk / jnp.maximum(jnp.sqrt(_seg_sum(kk * kk, seg2)), 1e-12)
    k_mod = k * (1.0 + (a - 1.0) * k_a_ref[...])
    bonus = _seg_sum(r * k_mod * r_k_ref[...], seg2) * v
    r_ref[...] = r
    w_ref[...] = log_decay
    k_ref[...] = k_mod
    v_ref[...] = v
    kk_ref[...] = kk
    b_ref[...] = kk * a
    gate_ref[...] = gate
    bonus_ref[...] = bonus


def _mix_prep(x2d, seq, norm_g, w_in, conv_w, conv_g, mu, w0, w_up, a0, a_up, g_up, k_k, k_a,
              r_k, seg2):
    n, d = x2d.shape
    tm = PREP_ROWS
    halo = V7X_SUBLANES
    cw = conv_w.shape[1]
    rw = w0.shape[1]
    in_cols = w_in.shape[1]
    assert seq % tm == 0 and n % tm == 0
    full = lambda a: pl.BlockSpec(a.shape, lambda i: (0,) * a.ndim)
    row_spec = lambda c: pl.BlockSpec((tm, c), lambda i: (i, 0))
    consts = (norm_g, w_in, conv_w, conv_g, mu, w0, w_up, a0, a_up, g_up, k_k, k_a, r_k, seg2)
    outs = [jax.ShapeDtypeStruct((n, cw), BF16)] + [jax.ShapeDtypeStruct((n, rw), F32)] * 8
    return pl.pallas_call(
        functools.partial(_mix_prep_kernel, seq // tm, cw, rw),
        grid=(n // tm,),
        in_specs=[pl.BlockSpec((halo, d), lambda i: (jnp.maximum(i * (tm // halo) - 1, 0), 0)),
                  row_spec(d)] + [full(c) for c in consts],
        out_specs=[row_spec(cw)] + [row_spec(rw)] * 8,
        out_shape=outs,
        scratch_shapes=[pltpu.VMEM((halo + tm, in_cols), F32), pltpu.VMEM((halo + tm, cw), F32)],
        compiler_params=_params("parallel"),
        name="mix_prep",
    )(x2d, x2d, *consts)


def _bdot(a, b):
    return jnp.dot(a.astype(BF16), b.astype(BF16), preferred_element_type=F32)


def _bdot_nt(a, b):
    return lax.dot_general(a.astype(BF16), b.astype(BF16), (((1,), (1,)), ((), ())),
                           preferred_element_type=F32)


def _wkv_chunk_kernel(n_batch, n_pairs,
                      r_ref, lw_ref, k_ref, v_ref, kk_ref, b_ref, o_ref, state_ref):
    frames = r_ref.shape[1]
    lanes = V7X_LANES
    hd = HEAD_DIM
    assert frames == hd

    @pl.when(pl.program_id(0) == 0)
    def _():
        state_ref[...] = jnp.zeros_like(state_ref)

    row = lax.broadcasted_iota(I32, (lanes, lanes), 0)
    lane = lax.broadcasted_iota(I32, (lanes, lanes), 1)
    same_head = (row // hd) == (lane // hd)
    rt = row % hd
    ls = lane % hd
    strict_same = same_head & (rt > ls)
    strict_cross = jnp.logical_not(same_head) & (rt > ls)
    incl_same = same_head & (rt >= ls)
    incl_cross = jnp.logical_not(same_head) & (rt >= ls)
    eye = row == lane
    level_masks = []
    m = 1
    while m < hd:
        level_masks.append(same_head & ((rt // (2 * m)) == (ls // (2 * m)))
                           & (((rt // m) % 2) == 1) & (((ls // m) % 2) == 0))
        m *= 2
    left = lax.broadcasted_iota(I32, (frames, lanes), 1) < hd
    tri = (lax.broadcasted_iota(I32, (frames, frames), 0)
           >= lax.broadcasted_iota(I32, (frames, frames), 1)).astype(BF16)

    def diag_blocks(x):
        return jnp.concatenate([jnp.where(left, x, 0.0), jnp.where(left, 0.0, x)], axis=0)

    def cross_blocks(x):
        return jnp.concatenate([jnp.where(left, 0.0, x), jnp.where(left, x, 0.0)], axis=0)

    def batch_prep(bi):
        lw = lw_ref[bi]
        hi = lw.astype(BF16)
        rem = lw - hi.astype(F32)
        mid = rem.astype(BF16)
        lo = (rem - mid.astype(F32)).astype(BF16)
        cs = (jnp.dot(tri, hi, preferred_element_type=F32)
              + jnp.dot(tri, mid, preferred_element_type=F32)
              + jnp.dot(tri, lo, preferred_element_type=F32))
        cs_end = cs[frames - 1:frames, :]
        e_neg = jnp.exp(-cs)
        e_end = jnp.exp(cs_end - cs)
        b_in = b_ref[bi]
        k_in = k_ref[bi]
        return dict(a_t=-kk_ref[bi] * jnp.exp(cs - lw), b_t=b_in * e_neg, k_t=k_in * e_neg,
                    r_t=r_ref[bi] * jnp.exp(cs), b_h=b_in * e_end, k_h=k_in * e_end,
                    v=v_ref[bi], g_end=jnp.exp(cs_end))

    for b0 in range(0, n_batch, SCAN_BATCHES_PER_GROUP):
        group = []
        for bi in range(b0, min(b0 + SCAN_BATCHES_PER_GROUP, n_batch)):
            prep = batch_prep(bi)
            for pi in range(n_pairs):
                sl = slice(pi * lanes, (pi + 1) * lanes)
                group.append((bi, sl, bi * n_pairs + pi, {n: x[:, sl] for n, x in prep.items()}))
        a_bd = [diag_blocks(p["a_t"]) for _, _, _, p in group]
        r_bd = [diag_blocks(p["r_t"]) for _, _, _, p in group]
        d0 = [_bdot_nt(jnp.concatenate([a[:hd], r[:hd]], axis=0),
                       jnp.concatenate([p["b_t"], p["k_t"]], axis=0))
              for a, r, (_, _, _, p) in zip(a_bd, r_bd, group)]
        d1 = [_bdot_nt(jnp.concatenate([a[hd:], r[hd:]], axis=0),
                       jnp.concatenate([p["k_t"], p["b_t"]], axis=0))
              for a, r, (_, _, _, p) in zip(a_bd, r_bd, group)]
        a_rows = [jnp.concatenate([x[:hd], y[:hd]], axis=0) for x, y in zip(d0, d1)]
        m_rows = [jnp.concatenate([x[hd:], y[hd:]], axis=0) for x, y in zip(d0, d1)]
        a_ab = [jnp.where(strict_same, x, 0.0) for x in a_rows]
        a_ak = [jnp.where(strict_cross, x, 0.0) for x in a_rows]
        m_rb = [jnp.where(incl_same, x, 0.0) for x in m_rows]
        m_rk = [jnp.where(incl_cross, x, 0.0) for x in m_rows]
        t_inv = [jnp.where(eye, 1.0, jnp.where(level_masks[0], x, 0.0)) for x in a_ab]
        for mask in level_masks[1:]:
            half = [_bdot(t, jnp.where(mask, x, 0.0)) for t, x in zip(t_inv, a_ab)]
            t_inv = [t + _bdot(h, t) for t, h in zip(t_inv, half)]
        v_x = [cross_blocks(p["v"]) for _, _, _, p in group]
        akv = [_bdot(x, v) for x, v in zip(a_ak, v_x)]
        wu = [_bdot(t, jnp.concatenate([a, x], axis=1))
              for t, a, x in zip(t_inv, a_bd, akv)]
        bh_t = [diag_blocks(p["b_h"]).T for _, _, _, p in group]
        kh_t = [cross_blocks(p["k_h"]).T for _, _, _, p in group]
        pw_rw = [_bdot(jnp.concatenate([bt, m], axis=0), x[:, :lanes])
                 for bt, m, x in zip(bh_t, m_rb, wu)]
        q_o = [_bdot(jnp.concatenate([jnp.concatenate([bt, kt], axis=1),
                                      jnp.concatenate([mb, mk], axis=1)], axis=0),
                     jnp.concatenate([x[:, lanes:], v], axis=0))
               for bt, kt, mb, mk, x, v in zip(bh_t, kh_t, m_rb, m_rk, wu, v_x)]
        res = [_bdot(jnp.concatenate([r + pr[lanes:],
                                      pr[:lanes] + jnp.where(eye, p["g_end"], 0.0)], axis=0),
                     state_ref[idx])
               for r, pr, (_, _, idx, p) in zip(r_bd, pw_rw, group)]
        for x, q, (bi, sl, idx, _) in zip(res, q_o, group):
            o_bd = x[:lanes] + q[lanes:]
            state_ref[idx] = x[lanes:] + q[:lanes]
            o_ref[bi, :, sl] = jnp.where(left, o_bd[:hd], o_bd[hd:])


def _wkv_chunk(r, lw, k, v, kk, b):
    n_batch, seq, rw = r.shape
    n_pairs = rw // V7X_LANES
    frames = SCAN_FRAMES
    assert seq % frames == 0
    spec = pl.BlockSpec((n_batch, frames, rw), lambda c: (0, c, 0))
    return pl.pallas_call(
        functools.partial(_wkv_chunk_kernel, n_batch, n_pairs),
        grid=(seq // frames,),
        in_specs=[spec] * 6,
        out_specs=spec,
        out_shape=jax.ShapeDtypeStruct((n_batch, seq, rw), F32),
        scratch_shapes=[pltpu.VMEM((n_batch * n_pairs, V7X_LANES, V7X_LANES), F32)],
        compiler_params=_params("arbitrary"),
        name="wkv_chunk",
    )(r, lw, k, v, kk, b)


def _post_mix_kernel(n_experts,
                     o_ref, bonus_ref, gate_ref, yconv_ref, x_ref, gn_w_ref, gn_b_ref,
                     w_out_c_ref, w_out_r_ref, ffn_g_ref, wr_hi_ref, wr_lo_ref, b_router_ref,
                     seg2_ref,
                     x1_ref, h2_ref, top_e_ref, gates_ref):
    seg2 = seg2_ref[...]
    o = o_ref[...]
    mean = _seg_sum(o, seg2) * (1.0 / HEAD_DIM)
    cen = o - mean
    var = _seg_sum(cen * cen, seg2) * (1.0 / HEAD_DIM)
    o = cen * lax.rsqrt(var + GN_EPS) * gn_w_ref[...] + gn_b_ref[...]
    y_rwkv = (o + bonus_ref[...]) * gate_ref[...]
    x1 = (x_ref[...]
          + jnp.dot(yconv_ref[...], w_out_c_ref[...], preferred_element_type=F32)
          + jnp.dot(y_rwkv.astype(BF16), w_out_r_ref[...], preferred_element_type=F32))
    x1_ref[...] = x1
    h2 = _rms_rows(x1, ffn_g_ref[...])
    h2_ref[...] = h2

    h_hi, h_lo = _split_bf16(h2)
    nt = (((1,), (1,)), ((), ()))
    logits = (lax.dot_general(wr_hi_ref[...], h_hi, nt, preferred_element_type=F32)
              + lax.dot_general(wr_hi_ref[...], h_lo, nt, preferred_element_type=F32)
              + lax.dot_general(wr_lo_ref[...], h_hi, nt, preferred_element_type=F32)
              + b_router_ref[...])
    e_id = lax.broadcasted_iota(I32, logits.shape, 0).astype(F32)
    work = logits
    tops, ids = [], []
    for _ in range(TOP_K):
        m = jnp.max(work, axis=0, keepdims=True)
        sel = jnp.min(jnp.where(work == m, e_id, float(n_experts)), axis=0, keepdims=True)
        tops.append(m)
        ids.append(sel)
        work = jnp.where(e_id == sel, -jnp.inf, work)
    ex = [jnp.exp(t - tops[0]) for t in tops]
    denom = ex[0] + ex[1] + ex[2] + ex[3]
    top_e_ref[...] = jnp.concatenate(ids, axis=0).astype(I32)
    gates_ref[...] = jnp.concatenate([e / denom for e in ex], axis=0)


def _post_mix(o, bonus, gate, yconv, x2d, gn_w, gn_b, w_out_c, w_out_r, ffn_g, wr_hi, wr_lo,
              b_router, seg2):
    n, d = x2d.shape
    rw = o.shape[1]
    cw = yconv.shape[1]
    n_experts = wr_hi.shape[0]
    tm = PREP_ROWS
    full = lambda a: pl.BlockSpec(a.shape, lambda i: (0,) * a.ndim)
    row_spec = lambda c: pl.BlockSpec((tm, c), lambda i: (i, 0))
    col_spec = pl.BlockSpec((TOP_K, tm), lambda i: (0, i))
    consts = (gn_w, gn_b, w_out_c, w_out_r, ffn_g, wr_hi, wr_lo, b_router, seg2)
    return pl.pallas_call(
        functools.partial(_post_mix_kernel, n_experts),
        grid=(n // tm,),
        in_specs=[row_spec(rw), row_spec(rw), row_spec(rw), row_spec(cw), row_spec(d)]
                 + [full(c) for c in consts],
        out_specs=[row_spec(d), row_spec(d), col_spec, col_spec],
        out_shape=[jax.ShapeDtypeStruct((n, d), F32), jax.ShapeDtypeStruct((n, d), F32),
                   jax.ShapeDtypeStruct((TOP_K, n), I32), jax.ShapeDtypeStruct((TOP_K, n), F32)],
        compiler_params=_params("parallel"),
        name="post_mix",
    )(o, bonus, gate, yconv, x2d, *consts)


def _route_kernel(n_experts, block_rows,
                  top_e_ref, tri_ref, dest_ref, meta_ref,
                  count_ref, start_ref, carry_ref):
    phase = pl.program_id(0)
    j = pl.program_id(1)
    tb = top_e_ref.shape[1]
    e_id = lax.broadcasted_iota(I32, (n_experts, tb), 0)
    top_e = top_e_ref[...]
    onehot = jnp.zeros((n_experts, tb), F32)
    for c in range(TOP_K):
        onehot = onehot + jnp.where(top_e[c:c + 1, :] == e_id, 1.0, 0.0)
    block_count = jnp.sum(onehot, axis=1, keepdims=True)

    @pl.when((phase == 0) & (j == 0))
    def _():
        count_ref[...] = jnp.zeros_like(count_ref)

    @pl.when(phase == 0)
    def _():
        count_ref[...] += block_count

    @pl.when((phase == 1) & (j == 0))
    def _():
        counts = count_ref[...]
        padded = jnp.ceil(counts * (1.0 / block_rows)) * block_rows
        sub = lax.broadcasted_iota(I32, (n_experts, n_experts), 0)
        lane = lax.broadcasted_iota(I32, (n_experts, n_experts), 1)
        padded_row = jnp.sum(jnp.where(sub == lane, padded, 0.0), axis=0, keepdims=True)
        start = jnp.sum(jnp.where(lane < sub, padded_row, 0.0), axis=1, keepdims=True)
        start_ref[...] = start
        carry_ref[...] = jnp.zeros_like(carry_ref)
        end = start + padded
        nb = meta_ref.shape[1]
        slot0 = (lax.broadcasted_iota(I32, (n_experts, nb), 1) * block_rows).astype(F32)
        block_e = jnp.sum(jnp.where(end <= slot0, 1.0, 0.0), axis=0, keepdims=True)
        block_e = jnp.minimum(block_e, n_experts - 1.0)
        used = jnp.max(end, axis=0, keepdims=True) * (1.0 / block_rows)
        e_sub = lax.broadcasted_iota(I32, (n_experts, nb), 0)
        e_lane = lax.broadcasted_iota(I32, (n_experts, nb), 1)
        end_row = jnp.sum(jnp.where(e_sub == e_lane, end, 0.0), axis=0, keepdims=True)
        last_block = jnp.maximum(end_row * (1.0 / block_rows) - 1.0, 0.0)
        row = lax.broadcasted_iota(I32, meta_ref.shape, 0)
        meta_ref[...] = jnp.where(row == META_BLOCK_EXPERT, block_e,
                                  jnp.where(row == META_LAST_BLOCK, last_block, used)).astype(I32)

    @pl.when(phase == 1)
    def _():
        incl = jnp.dot(onehot.astype(BF16), tri_ref[...], preferred_element_type=F32)
        base = incl - onehot + carry_ref[...] + start_ref[...]
        rows = [jnp.sum(jnp.where(top_e[c:c + 1, :] == e_id, base, 0.0), axis=0, keepdims=True)
                for c in range(TOP_K)]
        dest_ref[...] = jnp.concatenate(rows, axis=0).astype(I32)
        carry_ref[...] += block_count


def _route(top_e, n_experts, n_blocks_padded):
    n = top_e.shape[1]
    tb = ROUTE_TOKENS
    assert n % tb == 0
    tri = (lax.broadcasted_iota(I32, (tb, tb), 0) <= lax.broadcasted_iota(I32, (tb, tb), 1)
           ).astype(BF16)
    return pl.pallas_call(
        functools.partial(_route_kernel, n_experts, EXPERT_ROWS),
        grid=(2, n // tb),
        in_specs=[pl.BlockSpec((TOP_K, tb), lambda ph, j: (0, j)),
                  pl.BlockSpec((tb, tb), lambda ph, j: (0, 0))],
        out_specs=[pl.BlockSpec((TOP_K, tb), lambda ph, j: (0, j * ph)),
                   pl.BlockSpec((V7X_SUBLANES, n_blocks_padded), lambda ph, j: (0, 0))],
        out_shape=[jax.ShapeDtypeStruct((TOP_K, n), I32),
                   jax.ShapeDtypeStruct((V7X_SUBLANES, n_blocks_padded), I32)],
        scratch_shapes=[pltpu.VMEM((n_experts, 1), F32)] * 3,
        compiler_params=_params("arbitrary", "arbitrary"),
        name="route",
    )(top_e, tri)


def _dispatch_kernel(n_experts, block_rows,
                     meta_ref, dest_ref, h_ref, slots_ref, zero_ref, sem, zero_sem):
    sub = h_ref.shape[1]
    tb = h_ref.shape[0] * sub

    @pl.when(pl.program_id(0) == 0)
    def _():
        zero_ref[...] = jnp.zeros_like(zero_ref)

        def clear(block):
            row0 = pl.multiple_of(block * block_rows, block_rows)
            return pltpu.make_async_copy(zero_ref, slots_ref.at[pl.ds(row0, block_rows)], zero_sem)

        def start_clear(block, carry):
            clear(block).start()
            return carry

        def wait_clear(block, carry):
            clear(block).wait()
            return carry

        for e in range(n_experts):
            clear(meta_ref[META_LAST_BLOCK, e]).start()
        used = meta_ref[META_USED_BLOCKS, 0]
        n_blocks = slots_ref.shape[0] // block_rows
        lax.fori_loop(used, n_blocks, start_clear, 0)
        for e in range(n_experts):
            clear(meta_ref[META_LAST_BLOCK, e]).wait()
        lax.fori_loop(used, n_blocks, wait_clear, 0)

    def issue(g, carry):
        for j in range(TOKENS_PER_DEST_ROW):
            src = h_ref.at[g * (TOKENS_PER_DEST_ROW // sub) + j // sub, pl.ds(j % sub, 1)]
            for c in range(TOP_K):
                slot = dest_ref[0, g, c * TOKENS_PER_DEST_ROW + j]
                pltpu.make_async_copy(src, slots_ref.at[pl.ds(slot, 1)], sem).start()
        return carry

    lax.fori_loop(0, tb // TOKENS_PER_DEST_ROW, issue, 0)
    for c in range(TOP_K):
        pltpu.make_async_copy(slots_ref.at[pl.ds(0, tb)], slots_ref.at[pl.ds(0, tb)], sem).wait()


def _dispatch(meta, dest_blocks, h2, n_slots, n_experts):
    n, d = h2.shape
    tb = MOVE_TOKENS
    sub = V7X_SUBLANES
    grid_spec = pltpu.PrefetchScalarGridSpec(
        num_scalar_prefetch=1,
        grid=(n // tb,),
        in_specs=[pl.BlockSpec((1,) + dest_blocks.shape[1:], lambda i, m: (i, 0, 0),
                               memory_space=pltpu.SMEM),
                  pl.BlockSpec((tb // sub, sub, d), lambda i, m: (i, 0, 0))],
        out_specs=pl.BlockSpec(memory_space=pl.ANY),
        scratch_shapes=[pltpu.VMEM((EXPERT_ROWS, d), h2.dtype),
                        pltpu.SemaphoreType.DMA(()), pltpu.SemaphoreType.DMA(())],
    )
    return pl.pallas_call(
        functools.partial(_dispatch_kernel, n_experts, EXPERT_ROWS),
        grid_spec=grid_spec,
        out_shape=jax.ShapeDtypeStruct((n_slots, d), h2.dtype),
        compiler_params=_params("arbitrary"),
        name="dispatch",
    )(meta, dest_blocks, h2.reshape(n // sub, sub, d))


def _experts_kernel(d_ff, meta_ref, x_ref, w_gu_ref, b_gu_ref, w_down_ref, b_down_ref, y_ref):
    j = pl.program_id(0)
    used = meta_ref[META_USED_BLOCKS, 0]

    @pl.when(j < used)
    def _():
        gu = jnp.dot(x_ref[...].astype(BF16), w_gu_ref[0], preferred_element_type=F32) + b_gu_ref[0]
        gate = jnp.minimum(gu[:, :d_ff], SWIGLU_LIMIT)
        up = jnp.clip(gu[:, d_ff:], -SWIGLU_LIMIT, SWIGLU_LIMIT)
        act = (up + 1.0) * (gate * _sigmoid(SWIGLU_ALPHA * gate))
        y_ref[...] = (jnp.dot(act.astype(BF16), w_down_ref[0], preferred_element_type=F32)
                      + b_down_ref[0])

    @pl.when(j >= used)
    def _():
        y_ref[...] = jnp.zeros_like(y_ref)


def _experts(meta, slots, w_gu, b_gu, w_down, b_down):
    n_slots, d = slots.shape
    n_experts, _, two_ff = w_gu.shape
    d_ff = two_ff // 2
    bm = EXPERT_ROWS
    grid_spec = pltpu.PrefetchScalarGridSpec(
        num_scalar_prefetch=1,
        grid=(n_slots // bm,),
        in_specs=[pl.BlockSpec((bm, d),
                               lambda j, m: (jnp.minimum(j, m[META_USED_BLOCKS, 0] - 1), 0)),
                  pl.BlockSpec((1, d, two_ff), lambda j, m: (m[META_BLOCK_EXPERT, j], 0, 0)),
                  pl.BlockSpec((1, 1, two_ff), lambda j, m: (m[META_BLOCK_EXPERT, j], 0, 0)),
                  pl.BlockSpec((1, d_ff, d), lambda j, m: (m[META_BLOCK_EXPERT, j], 0, 0)),
                  pl.BlockSpec((1, 1, d), lambda j, m: (m[META_BLOCK_EXPERT, j], 0, 0))],
        out_specs=pl.BlockSpec((bm, d), lambda j, m: (j, 0)),
    )
    return pl.pallas_call(
        functools.partial(_experts_kernel, d_ff),
        grid_spec=grid_spec,
        out_shape=jax.ShapeDtypeStruct((n_slots, d), F32),
        compiler_params=_params("arbitrary"),
        name="experts",
    )(meta, slots, w_gu, b_gu, w_down, b_down)


def _combine_kernel(dest_ref, y_ref, x1_ref, gates_ref, g_ref, out_ref, buf, sem):
    tb, d = x1_ref.shape
    sub = buf.shape[2]

    def issue(g, carry):
        for j in range(TOKENS_PER_DEST_ROW):
            group = g * (TOKENS_PER_DEST_ROW // sub) + j // sub
            for c in range(TOP_K):
                slot = dest_ref[0, g, c * TOKENS_PER_DEST_ROW + j]
                pltpu.make_async_copy(y_ref.at[pl.ds(slot, 1)],
                                      buf.at[c, group, pl.ds(j % sub, 1)], sem).start()
        return carry

    lax.fori_loop(0, tb // TOKENS_PER_DEST_ROW, issue, 0)
    for c in range(TOP_K):
        pltpu.make_async_copy(y_ref.at[pl.ds(0, tb)], y_ref.at[pl.ds(0, tb)], sem).wait()
    gates = gates_ref[...]
    acc = x1_ref[...]
    for c in range(TOP_K):
        acc = acc + buf[c].reshape(tb, d) * gates[:, c:c + 1]
    out_ref[...] = _rms_rows(acc, g_ref[...])


def _combine(dest_blocks, y_slots, x1, gates_t, final_g):
    n, d = x1.shape
    tb = MOVE_TOKENS
    return pl.pallas_call(
        _combine_kernel,
        grid=(n // tb,),
        in_specs=[pl.BlockSpec((1,) + dest_blocks.shape[1:], lambda i: (i, 0, 0),
                               memory_space=pltpu.SMEM),
                  pl.BlockSpec(memory_space=pl.ANY),
                  pl.BlockSpec((tb, d), lambda i: (i, 0)),
                  pl.BlockSpec((tb, TOP_K), lambda i: (i, 0)),
                  pl.BlockSpec((1, d), lambda i: (0, 0))],
        out_specs=pl.BlockSpec((tb, d), lambda i: (i, 0)),
        out_shape=jax.ShapeDtypeStruct((n, d), F32),
        scratch_shapes=[pltpu.VMEM((TOP_K, tb // V7X_SUBLANES, V7X_SUBLANES, d), F32),
                        pltpu.SemaphoreType.DMA(())],
        compiler_params=_params("arbitrary"),
        name="combine",
    )(dest_blocks, y_slots, x1, gates_t, final_g)


def _group_matrix(width):
    a = lax.broadcasted_iota(I32, (width, width), 0) // HEAD_DIM
    b = lax.broadcasted_iota(I32, (width, width), 1) // HEAD_DIM
    g = (a == b).astype(BF16)
    return jnp.concatenate([g, g], axis=0)


def _row(vec):
    return vec.reshape(1, -1).astype(F32)


def kernel(x, w_in, conv_w, conv_norm_g, rwkv_mu, w0, w_up, a0, a_up, g_up, k_k, k_a, r_k,
           gn_w, gn_b, w_out, norm_mix_g, norm_ffn_g, w_router, b_router, w_gu, b_gu, w_down,
           b_down, norm_final_g):
    n_batch, seq, d = x.shape
    n = n_batch * seq
    depth = w_in.shape[0]
    cw = conv_w.shape[2]
    rw = w0.shape[1]
    n_experts = w_router.shape[2]
    decay_lora = w_up.shape[1]
    aaa_lora = a_up.shape[1]
    assert decay_lora + aaa_lora == V7X_LANES
    assert depth == 1
    n_slots = n * TOP_K + n_experts * EXPERT_ROWS
    n_blocks = n_slots // EXPERT_ROWS
    n_blocks_padded = -(-n_blocks // V7X_LANES) * V7X_LANES
    seg2 = _group_matrix(rw)

    x2d = x.reshape(n, d)
    for l in range(depth):
        w_up_pad = jnp.concatenate([w_up[l], jnp.zeros((aaa_lora, rw), F32)], axis=0).astype(BF16)
        a_up_pad = jnp.concatenate([jnp.zeros((decay_lora, rw), F32), a_up[l]], axis=0).astype(BF16)
        (yconv, r, lw, k, v, kk, b, gate, bonus) = _mix_prep(
            x2d, seq, _row(norm_mix_g[l]), w_in[l].astype(BF16), conv_w[l].astype(F32),
            _row(conv_norm_g[l]), _row(rwkv_mu[l]), _row(w0[l]), w_up_pad, _row(a0[l]), a_up_pad,
            g_up[l].astype(BF16), _row(k_k[l]), _row(k_a[l]), _row(r_k[l]), seg2)
        shape3 = (n_batch, seq, rw)
        o = _wkv_chunk(r.reshape(shape3), lw.reshape(shape3), k.reshape(shape3),
                       v.reshape(shape3), kk.reshape(shape3), b.reshape(shape3)).reshape(n, rw)
        w_out_b = w_out[l].astype(BF16)
        wr_t = w_router[l].T.astype(F32)
        wr_hi = wr_t.astype(BF16)
        wr_lo = (wr_t - wr_hi.astype(F32)).astype(BF16)
        x1, h2, top_e, gates = _post_mix(
            o, bonus, gate, yconv, x2d, _row(gn_w[l]), _row(gn_b[l]), w_out_b[:cw], w_out_b[cw:],
            _row(norm_ffn_g[l]), wr_hi, wr_lo, b_router[l].reshape(n_experts, 1).astype(F32), seg2)
        dest, meta = _route(top_e, n_experts, n_blocks_padded)
        tb = MOVE_TOKENS
        dest_blocks = (dest.reshape(TOP_K, n // tb, tb // TOKENS_PER_DEST_ROW, TOKENS_PER_DEST_ROW)
                       .transpose(1, 2, 0, 3).reshape(n // tb, tb // TOKENS_PER_DEST_ROW, V7X_LANES))
        slots = _dispatch(meta, dest_blocks, h2, n_slots, n_experts)
        y_slots = _experts(meta, slots, w_gu[l].astype(BF16),
                           b_gu[l].reshape(n_experts, 1, -1).astype(F32),
                           w_down[l].astype(BF16),
                           b_down[l].reshape(n_experts, 1, -1).astype(F32))
        x2d = _combine(dest_blocks, y_slots, x1, gates.T, _row(norm_final_g))
    return x2d.reshape(n_batch, seq, d)
```

```python
import functools

import jax
import jax.numpy as jnp
from jax import lax
from jax.experimental import pallas as pl
from jax.experimental.pallas import tpu as pltpu

F32 = jnp.float32
BF16 = jnp.bfloat16
I32 = jnp.int32

HEAD_DIM = 64
TOP_K = 4
NORM_EPS = 1e-5
GN_EPS = HEAD_DIM * 1e-5
SWIGLU_LIMIT = 7.0
SWIGLU_ALPHA = 1.702

V7X_LANES = 128
V7X_SUBLANES = 8
V7X_VMEM_LIMIT_BYTES = 56 * 1024 * 1024

PREP_ROWS = 256
SCAN_FRAMES = 64
SCAN_BATCHES_PER_GROUP = 2
ROUTE_TOKENS = 1024
MOVE_TOKENS = 256
TOKENS_PER_DEST_ROW = V7X_LANES // TOP_K
META_BLOCK_EXPERT = 0
META_USED_BLOCKS = 1
META_LAST_BLOCK = 2
EXPERT_ROWS = 512

def _params(*semantics):
    return pltpu.CompilerParams(dimension_semantics=semantics,
                                vmem_limit_bytes=V7X_VMEM_LIMIT_BYTES)


def _split_bf16(x):
    hi = x.astype(BF16)
    lo = (x - hi.astype(F32)).astype(BF16)
    return hi, lo


def _seg_sum(x, seg2):
    hi, lo = _split_bf16(x)
    return jnp.dot(jnp.concatenate([hi, lo], axis=1), seg2, preferred_element_type=F32)


def _rms_rows(x, g):
    return x * lax.rsqrt(jnp.mean(x * x, axis=-1, keepdims=True) + NORM_EPS) * g


def _sigmoid(x):
    return 1.0 / (1.0 + jnp.exp(-x))


def _mix_prep_kernel(blocks_per_seq, cw, rw,
                     xprev_ref, x_ref, g_ref, w_in_ref, conv_w_ref, conv_g_ref, mu_ref,
                     w0_ref, w_up_ref, a0_ref, a_up_ref, g_up_ref, k_k_ref, k_a_ref, r_k_ref,
                     seg2_ref,
                     yconv_ref, r_ref, w_ref, k_ref, v_ref, kk_ref, b_ref, gate_ref, bonus_ref,
                     p_scr, z_scr):
    tm = x_ref.shape[0]
    halo = xprev_ref.shape[0]
    first = (pl.program_id(0) % blocks_per_seq) == 0
    xp = xprev_ref[...] * jnp.where(first, 0.0, 1.0)
    xa = jnp.concatenate([xp, x_ref[...]], axis=0)
    h = _rms_rows(xa, g_ref[...])
    p_scr[...] = jnp.dot(h.astype(BF16), w_in_ref[...], preferred_element_type=F32)
    seg2 = seg2_ref[...]

    z_scr[...] = p_scr[:, 2 * cw:3 * cw] * p_scr[:, 0:cw]
    conv = (conv_w_ref[0:1, :] * z_scr[halo - 2:halo - 2 + tm, :]
            + conv_w_ref[1:2, :] * z_scr[halo - 1:halo - 1 + tm, :]
            + conv_w_ref[2:3, :] * z_scr[halo:halo + tm, :])
    y = p_scr[halo:halo + tm, cw:2 * cw] * conv
    ms = _seg_sum(y * y, seg2) * (1.0 / HEAD_DIM)
    yconv_ref[...] = (y * lax.rsqrt(ms + NORM_EPS) * conv_g_ref[...]).astype(yconv_ref.dtype)

    c0 = 3 * cw
    cur = p_scr[halo:halo + tm, c0:]
    prev = p_scr[halo - 1:halo - 1 + tm, c0:]
    q = cur + (prev - cur) * mu_ref[...]
    r = q[:, 0:rw]
    k = q[:, rw:2 * rw]
    v = q[:, 2 * rw:3 * rw]
    lora_wa = q[:, 3 * rw:3 * rw + V7X_LANES]
    lora_g = q[:, 3 * rw + V7X_LANES:]
    w_lin = w0_ref[...] + jnp.dot(jnp.tanh(lora_wa).astype(BF16), w_up_ref[...],
                                  preferred_element_type=F32)
    neg = -w_lin
    softplus = jnp.maximum(neg, 0.0) + jnp.log(1.0 + jnp.exp(-jnp.abs(neg)))
    log_decay = -jnp.exp(-softplus - 0.5)
    a = _sigmoid(a0_ref[...] + jnp.dot(lora_wa.astype(BF16), a_up_ref[...],
                                       preferred_element_type=F32))
    gate = jnp.dot(_sigmoid(lora_g).astype(BF16), g_up_ref[...], preferred_element_type=F32)
    kk = k * k_k_ref[...]
    kk = kk / jnp.maximum(jnp.sqrt(_seg_sum(kk * kk, seg2)), 1e-12)
    k_mod = k * (1.0 + (a - 1.0) * k_a_ref[...])
    bonus = _seg_sum(r * k_mod * r_k_ref[...], seg2) * v
    r_ref[...] = r
    w_ref[...] = log_decay
    k_ref[...] = k_mod
    v_ref[...] = v
    kk_ref[...] = kk
    b_ref[...] = kk * a
    gate_ref[...] = gate
    bonus_ref[...] = bonus


def _mix_prep(x2d, seq, norm_g, w_in, conv_w, conv_g, mu, w0, w_up, a0, a_up, g_up, k_k, k_a,
              r_k, seg2):
    n, d = x2d.shape
    tm = PREP_ROWS
    halo = V7X_SUBLANES
    cw = conv_w.shape[1]
    rw = w0.shape[1]
    in_cols = w_in.shape[1]
    assert seq % tm == 0 and n % tm == 0
    full = lambda a: pl.BlockSpec(a.shape, lambda i: (0,) * a.ndim)
    row_spec = lambda c: pl.BlockSpec((tm, c), lambda i: (i, 0))
    consts = (norm_g, w_in, conv_w, conv_g, mu, w0, w_up, a0, a_up, g_up, k_k, k_a, r_k, seg2)
    outs = [jax.ShapeDtypeStruct((n, cw), BF16)] + [jax.ShapeDtypeStruct((n, rw), F32)] * 8
    return pl.pallas_call(
        functools.partial(_mix_prep_kernel, seq // tm, cw, rw),
        grid=(n // tm,),
        in_specs=[pl.BlockSpec((halo, d), lambda i: (jnp.maximum(i * (tm // halo) - 1, 0), 0)),
                  row_spec(d)] + [full(c) for c in consts],
        out_specs=[row_spec(cw)] + [row_spec(rw)] * 8,
        out_shape=outs,
        scratch_shapes=[pltpu.VMEM((halo + tm, in_cols), F32), pltpu.VMEM((halo + tm, cw), F32)],
        compiler_params=_params("parallel"),
        name="mix_prep",
    )(x2d, x2d, *consts)


def _bdot(a, b):
    return jnp.dot(a.astype(BF16), b.astype(BF16), preferred_element_type=F32)


def _bdot_nt(a, b):
    return lax.dot_general(a.astype(BF16), b.astype(BF16), (((1,), (1,)), ((), ())),
                           preferred_element_type=F32)


def _wkv_chunk_kernel(n_batch, n_pairs,
                      r_ref, lw_ref, k_ref, v_ref, kk_ref, b_ref, o_ref, state_ref):
    frames = r_ref.shape[1]
    lanes = V7X_LANES
    hd = HEAD_DIM
    assert frames == hd

    @pl.when(pl.program_id(0) == 0)
    def _():
        state_ref[...] = jnp.zeros_like(state_ref)

    row = lax.broadcasted_iota(I32, (lanes, lanes), 0)
    lane = lax.broadcasted_iota(I32, (lanes, lanes), 1)
    same_head = (row // hd) == (lane // hd)
    rt = row % hd
    ls = lane % hd
    strict_same = same_head & (rt > ls)
    strict_cross = jnp.logical_not(same_head) & (rt > ls)
    incl_same = same_head & (rt >= ls)
    incl_cross = jnp.logical_not(same_head) & (rt >= ls)
    eye = row == lane
    level_masks = []
    m = 1
    while m < hd:
        level_masks.append(same_head & ((rt // (2 * m)) == (ls // (2 * m)))
                           & (((rt // m) % 2) == 1) & (((ls // m) % 2) == 0))
        m *= 2
    left = lax.broadcasted_iota(I32, (frames, lanes), 1) < hd
    tri = (lax.broadcasted_iota(I32, (frames, frames), 0)
           >= lax.broadcasted_iota(I32, (frames, frames), 1)).astype(BF16)

    def diag_blocks(x):
        return jnp.concatenate([jnp.where(left, x, 0.0), jnp.where(left, 0.0, x)], axis=0)

    def cross_blocks(x):
        return jnp.concatenate([jnp.where(left, 0.0, x), jnp.where(left, x, 0.0)], axis=0)

    def batch_prep(bi):
        lw = lw_ref[bi]
        hi = lw.astype(BF16)
        rem = lw - hi.astype(F32)
        mid = rem.astype(BF16)
        lo = (rem - mid.astype(F32)).astype(BF16)
        cs = (jnp.dot(tri, hi, preferred_element_type=F32)
              + jnp.dot(tri, mid, preferred_element_type=F32)
              + jnp.dot(tri, lo, preferred_element_type=F32))
        cs_end = cs[frames - 1:frames, :]
        e_neg = jnp.exp(-cs)
        e_end = jnp.exp(cs_end - cs)
        b_in = b_ref[bi]
        k_in = k_ref[bi]
        return dict(a_t=-kk_ref[bi] * jnp.exp(cs - lw), b_t=b_in * e_neg, k_t=k_in * e_neg,
                    r_t=r_ref[bi] * jnp.exp(cs), b_h=b_in * e_end, k_h=k_in * e_end,
                    v=v_ref[bi], g_end=jnp.exp(cs_end))

    for b0 in range(0, n_batch, SCAN_BATCHES_PER_GROUP):
        group = []
        for bi in range(b0, min(b0 + SCAN_BATCHES_PER_GROUP, n_batch)):
            prep = batch_prep(bi)
            for pi in range(n_pairs):
                sl = slice(pi * lanes, (pi + 1) * lanes)
                group.append((bi, sl, bi * n_pairs + pi, {n: x[:, sl] for n, x in prep.items()}))
        a_bd = [diag_blocks(p["a_t"]) for _, _, _, p in group]
        r_bd = [diag_blocks(p["r_t"]) for _, _, _, p in group]
        d0 = [_bdot_nt(jnp.concatenate([a[:hd], r[:hd]], axis=0),
                       jnp.concatenate([p["b_t"], p["k_t"]], axis=0))
              for a, r, (_, _, _, p) in zip(a_bd, r_bd, group)]
        d1 = [_bdot_nt(jnp.concatenate([a[hd:], r[hd:]], axis=0),
                       jnp.concatenate([p["k_t"], p["b_t"]], axis=0))
              for a, r, (_, _, _, p) in zip(a_bd, r_bd, group)]
        a_rows = [jnp.concatenate([x[:hd], y[:hd]], axis=0) for x, y in zip(d0, d1)]
        m_rows = [jnp.concatenate([x[hd:], y[hd:]], axis=0) for x, y in zip(d0, d1)]
        a_ab = [jnp.where(strict_same, x, 0.0) for x in a_rows]
        a_ak = [jnp.where(strict_cross, x, 0.0) for x in a_rows]
        m_rb = [jnp.where(incl_same, x, 0.0) for x in m_rows]
        m_rk = [jnp.where(incl_cross, x, 0.0) for x in m_rows]
        t_inv = [jnp.where(eye, 1.0, jnp.where(level_masks[0], x, 0.0)) for x in a_ab]
        for mask in level_masks[1:]:
            half = [_bdot(t, jnp.where(mask, x, 0.0)) for t, x in zip(t_inv, a_ab)]
            t_inv = [t + _bdot(h, t) for t, h in zip(t_inv, half)]
        v_x = [cross_blocks(p["v"]) for _, _, _, p in group]
        akv = [_bdot(x, v) for x, v in zip(a_ak, v_x)]
        wu = [_bdot(t, jnp.concatenate([a, x], axis=1))
              for t, a, x in zip(t_inv, a_bd, akv)]
        bh_t = [diag_blocks(p["b_h"]).T for _, _, _, p in group]
        kh_t = [cross_blocks(p["k_h"]).T for _, _, _, p in group]
        pw_rw = [_bdot(jnp.concatenate([bt, m], axis=0), x[:, :lanes])
                 for bt, m, x in zip(bh_t, m_rb, wu)]
        q_o = [_bdot(jnp.concatenate([jnp.concatenate([bt, kt], axis=1),
                                      jnp.concatenate([mb, mk], axis=1)], axis=0),
                     jnp.concatenate([x[:, lanes:], v], axis=0))
               for bt, kt, mb, mk, x, v in zip(bh_t, kh_t, m_rb, m_rk, wu, v_x)]
        res = [_bdot(jnp.concatenate([r + pr[lanes:],
                                      pr[:lanes] + jnp.where(eye, p["g_end"], 0.0)], axis=0),
                     state_ref[idx])
               for r, pr, (_, _, idx, p) in zip(r_bd, pw_rw, group)]
        for x, q, (bi, sl, idx, _) in zip(res, q_o, group):
            o_bd = x[:lanes] + q[lanes:]
            state_ref[idx] = x[lanes:] + q[:lanes]
            o_ref[bi, :, sl] = jnp.where(left, o_bd[:hd], o_bd[hd:])


def _wkv_chunk(r, lw, k, v, kk, b):
    n_batch, seq, rw = r.shape
    n_pairs = rw // V7X_LANES
    frames = SCAN_FRAMES
    assert seq % frames == 0
    spec = pl.BlockSpec((n_batch, frames, rw), lambda c: (0, c, 0))
    return pl.pallas_call(
        functools.partial(_wkv_chunk_kernel, n_batch, n_pairs),
        grid=(seq // frames,),
        in_specs=[spec] * 6,
        out_specs=spec,
        out_shape=jax.ShapeDtypeStruct((n_batch, seq, rw), F32),
        scratch_shapes=[pltpu.VMEM((n_batch * n_pairs, V7X_LANES, V7X_LANES), F32)],
        compiler_params=_params("arbitrary"),
        name="wkv_chunk",
    )(r, lw, k, v, kk, b)


def _post_mix_kernel(n_experts,
                     o_ref, bonus_ref, gate_ref, yconv_ref, x_ref, gn_w_ref, gn_b_ref,
                     w_out_c_ref, w_out_r_ref, ffn_g_ref, wr_hi_ref, wr_lo_ref, b_router_ref,
                     seg2_ref,
                     x1_ref, h2_ref, top_e_ref, gates_ref):
    seg2 = seg2_ref[...]
    o = o_ref[...]
    mean = _seg_sum(o, seg2) * (1.0 / HEAD_DIM)
    cen = o - mean
    var = _seg_sum(cen * cen, seg2) * (1.0 / HEAD_DIM)
    o = cen * lax.rsqrt(var + GN_EPS) * gn_w_ref[...] + gn_b_ref[...]
    y_rwkv = (o + bonus_ref[...]) * gate_ref[...]
    x1 = (x_ref[...]
          + jnp.dot(yconv_ref[...], w_out_c_ref[...], preferred_element_type=F32)
          + jnp.dot(y_rwkv.astype(BF16), w_out_r_ref[...], preferred_element_type=F32))
    x1_ref[...] = x1
    h2 = _rms_rows(x1, ffn_g_ref[...])
    h2_ref[...] = h2

    h_hi, h_lo = _split_bf16(h2)
    nt = (((1,), (1,)), ((), ()))
    logits = (lax.dot_general(wr_hi_ref[...], h_hi, nt, preferred_element_type=F32)
              + lax.dot_general(wr_hi_ref[...], h_lo, nt, preferred_element_type=F32)
              + lax.dot_general(wr_lo_ref[...], h_hi, nt, preferred_element_type=F32)
              + b_router_ref[...])
    e_id = lax.broadcasted_iota(I32, logits.shape, 0).astype(F32)
    work = logits
    tops, ids = [], []
    for _ in range(TOP_K):
        m = jnp.max(work, axis=0, keepdims=True)
        sel = jnp.min(jnp.where(work == m, e_id, float(n_experts)), axis=0, keepdims=True)
        tops.append(m)
        ids.append(sel)
        work = jnp.where(e_id == sel, -jnp.inf, work)
    ex = [jnp.exp(t - tops[0]) for t in tops]
    denom = ex[0] + ex[1] + ex[2] + ex[3]
    top_e_ref[...] = jnp.concatenate(ids, axis=0).astype(I32)
    gates_ref[...] = jnp.concatenate([e / denom for e in ex], axis=0)


def _post_mix(o, bonus, gate, yconv, x2d, gn_w, gn_b, w_out_c, w_out_r, ffn_g, wr_hi, wr_lo,
              b_router, seg2):
    n, d = x2d.shape
    rw = o.shape[1]
    cw = yconv.shape[1]
    n_experts = wr_hi.shape[0]
    tm = PREP_ROWS
    full = lambda a: pl.BlockSpec(a.shape, lambda i: (0,) * a.ndim)
    row_spec = lambda c: pl.BlockSpec((tm, c), lambda i: (i, 0))
    col_spec = pl.BlockSpec((TOP_K, tm), lambda i: (0, i))
    consts = (gn_w, gn_b, w_out_c, w_out_r, ffn_g, wr_hi, wr_lo, b_router, seg2)
    return pl.pallas_call(
        functools.partial(_post_mix_kernel, n_experts),
        grid=(n // tm,),
        in_specs=[row_spec(rw), row_spec(rw), row_spec(rw), row_spec(cw), row_spec(d)]
                 + [full(c) for c in consts],
        out_specs=[row_spec(d), row_spec(d), col_spec, col_spec],
        out_shape=[jax.ShapeDtypeStruct((n, d), F32), jax.ShapeDtypeStruct((n, d), F32),
                   jax.ShapeDtypeStruct((TOP_K, n), I32), jax.ShapeDtypeStruct((TOP_K, n), F32)],
        compiler_params=_params("parallel"),
        name="post_mix",
    )(o, bonus, gate, yconv, x2d, *consts)


def _route_kernel(n_experts, block_rows,
                  top_e_ref, tri_ref, dest_ref, meta_ref,
                  count_ref, start_ref, carry_ref):
    phase = pl.program_id(0)
    j = pl.program_id(1)
    tb = top_e_ref.shape[1]
    e_id = lax.broadcasted_iota(I32, (n_experts, tb), 0)
    top_e = top_e_ref[...]
    onehot = jnp.zeros((n_experts, tb), F32)
    for c in range(TOP_K):
        onehot = onehot + jnp.where(top_e[c:c + 1, :] == e_id, 1.0, 0.0)
    block_count = jnp.sum(onehot, axis=1, keepdims=True)

    @pl.when((phase == 0) & (j == 0))
    def _():
        count_ref[...] = jnp.zeros_like(count_ref)

    @pl.when(phase == 0)
    def _():
        count_ref[...] += block_count

    @pl.when((phase == 1) & (j == 0))
    def _():
        counts = count_ref[...]
        padded = jnp.ceil(counts * (1.0 / block_rows)) * block_rows
        sub = lax.broadcasted_iota(I32, (n_experts, n_experts), 0)
        lane = lax.broadcasted_iota(I32, (n_experts, n_experts), 1)
        padded_row = jnp.sum(jnp.where(sub == lane, padded, 0.0), axis=0, keepdims=True)
        start = jnp.sum(jnp.where(lane < sub, padded_row, 0.0), axis=1, keepdims=True)
        start_ref[...] = start
        carry_ref[...] = jnp.zeros_like(carry_ref)
        end = start + padded
        nb = meta_ref.shape[1]
        slot0 = (lax.broadcasted_iota(I32, (n_experts, nb), 1) * block_rows).astype(F32)
        block_e = jnp.sum(jnp.where(end <= slot0, 1.0, 0.0), axis=0, keepdims=True)
        block_e = jnp.minimum(block_e, n_experts - 1.0)
        used = jnp.max(end, axis=0, keepdims=True) * (1.0 / block_rows)
        e_sub = lax.broadcasted_iota(I32, (n_experts, nb), 0)
        e_lane = lax.broadcasted_iota(I32, (n_experts, nb), 1)
        end_row = jnp.sum(jnp.where(e_sub == e_lane, end, 0.0), axis=0, keepdims=True)
        last_block = jnp.maximum(end_row * (1.0 / block_rows) - 1.0, 0.0)
        row = lax.broadcasted_iota(I32, meta_ref.shape, 0)
        meta_ref[...] = jnp.where(row == META_BLOCK_EXPERT, block_e,
                                  jnp.where(row == META_LAST_BLOCK, last_block, used)).astype(I32)

    @pl.when(phase == 1)
    def _():
        incl = jnp.dot(onehot.astype(BF16), tri_ref[...], preferred_element_type=F32)
        base = incl - onehot + carry_ref[...] + start_ref[...]
        rows = [jnp.sum(jnp.where(top_e[c:c + 1, :] == e_id, base, 0.0), axis=0, keepdims=True)
                for c in range(TOP_K)]
        dest_ref[...] = jnp.concatenate(rows, axis=0).astype(I32)
        carry_ref[...] += block_count


def _route(top_e, n_experts, n_blocks_padded):
    n = top_e.shape[1]
    tb = ROUTE_TOKENS
    assert n % tb == 0
    tri = (lax.broadcasted_iota(I32, (tb, tb), 0) <= lax.broadcasted_iota(I32, (tb, tb), 1)
           ).astype(BF16)
    return pl.pallas_call(
        functools.partial(_route_kernel, n_experts, EXPERT_ROWS),
        grid=(2, n // tb),
        in_specs=[pl.BlockSpec((TOP_K, tb), lambda ph, j: (0, j)),
                  pl.BlockSpec((tb, tb), lambda ph, j: (0, 0))],
        out_specs=[pl.BlockSpec((TOP_K, tb), lambda ph, j: (0, j * ph)),
                   pl.BlockSpec((V7X_SUBLANES, n_blocks_padded), lambda ph, j: (0, 0))],
        out_shape=[jax.ShapeDtypeStruct((TOP_K, n), I32),
                   jax.ShapeDtypeStruct((V7X_SUBLANES, n_blocks_padded), I32)],
        scratch_shapes=[pltpu.VMEM((n_experts, 1), F32)] * 3,
        compiler_params=_params("arbitrary", "arbitrary"),
        name="route",
    )(top_e, tri)


def _dispatch_kernel(n_experts, block_rows,
                     meta_ref, dest_ref, h_ref, slots_ref, zero_ref, sem, zero_sem):
    sub = h_ref.shape[1]
    tb = h_ref.shape[0] * sub

    @pl.when(pl.program_id(0) == 0)
    def _():
        zero_ref[...] = jnp.zeros_like(zero_ref)

        def clear(block):
            row0 = pl.multiple_of(block * block_rows, block_rows)
            return pltpu.make_async_copy(zero_ref, slots_ref.at[pl.ds(row0, block_rows)], zero_sem)

        def start_clear(block, carry):
            clear(block).start()
            return carry

        def wait_clear(block, carry):
            clear(block).wait()
            return carry

        for e in range(n_experts):
            clear(meta_ref[META_LAST_BLOCK, e]).start()
        used = meta_ref[META_USED_BLOCKS, 0]
        n_blocks = slots_ref.shape[0] // block_rows
        lax.fori_loop(used, n_blocks, start_clear, 0)
        for e in range(n_experts):
            clear(meta_ref[META_LAST_BLOCK, e]).wait()
        lax.fori_loop(used, n_blocks, wait_clear, 0)

    def issue(g, carry):
        for j in range(TOKENS_PER_DEST_ROW):
            src = h_ref.at[g * (TOKENS_PER_DEST_ROW // sub) + j // sub, pl.ds(j % sub, 1)]
            for c in range(TOP_K):
                slot = dest_ref[0, g, c * TOKENS_PER_DEST_ROW + j]
                pltpu.make_async_copy(src, slots_ref.at[pl.ds(slot, 1)], sem).start(priority=c % 2)
        return carry

    lax.fori_loop(0, tb // TOKENS_PER_DEST_ROW, issue, 0)
    for c in range(TOP_K):
        pltpu.make_async_copy(slots_ref.at[pl.ds(0, tb)], slots_ref.at[pl.ds(0, tb)], sem).wait()


def _dispatch(meta, dest_blocks, h2, n_slots, n_experts):
    n, d = h2.shape
    tb = MOVE_TOKENS
    sub = V7X_SUBLANES
    grid_spec = pltpu.PrefetchScalarGridSpec(
        num_scalar_prefetch=1,
        grid=(n // tb,),
        in_specs=[pl.BlockSpec((1,) + dest_blocks.shape[1:], lambda i, m: (i, 0, 0),
                               memory_space=pltpu.SMEM),
                  pl.BlockSpec((tb // sub, sub, d), lambda i, m: (i, 0, 0))],
        out_specs=pl.BlockSpec(memory_space=pl.ANY),
        scratch_shapes=[pltpu.VMEM((EXPERT_ROWS, d), h2.dtype),
                        pltpu.SemaphoreType.DMA(()), pltpu.SemaphoreType.DMA(())],
    )
    return pl.pallas_call(
        functools.partial(_dispatch_kernel, n_experts, EXPERT_ROWS),
        grid_spec=grid_spec,
        out_shape=jax.ShapeDtypeStruct((n_slots, d), h2.dtype),
        compiler_params=_params("arbitrary"),
        name="dispatch",
    )(meta, dest_blocks, h2.reshape(n // sub, sub, d))


def _experts_kernel(d_ff, meta_ref, x_ref, w_gu_ref, b_gu_ref, w_down_ref, b_down_ref, y_ref,
                    w_gu_bf16, w_down_bf16):
    j = pl.program_id(0)
    used = meta_ref[META_USED_BLOCKS, 0]
    expert = meta_ref[META_BLOCK_EXPERT, j]
    prev_expert = meta_ref[META_BLOCK_EXPERT, jnp.maximum(j - 1, 0)]

    @pl.when((j < used) & ((j == 0) | (expert != prev_expert)))
    def _():
        w_gu_bf16[...] = w_gu_ref[0].astype(BF16)
        w_down_bf16[...] = w_down_ref[0].astype(BF16)

    @pl.when(j < used)
    def _():
        gu = (jnp.dot(x_ref[...].astype(BF16), w_gu_bf16[...], preferred_element_type=F32)
              + b_gu_ref[0])
        gate = jnp.minimum(gu[:, :d_ff], SWIGLU_LIMIT)
        up = jnp.clip(gu[:, d_ff:], -SWIGLU_LIMIT, SWIGLU_LIMIT)
        act = (up + 1.0) * (gate * _sigmoid(SWIGLU_ALPHA * gate))
        y_ref[...] = (jnp.dot(act.astype(BF16), w_down_bf16[...], preferred_element_type=F32)
                      + b_down_ref[0])

    @pl.when(j >= used)
    def _():
        y_ref[...] = jnp.zeros_like(y_ref)


def _experts(meta, slots, w_gu, b_gu, w_down, b_down):
    n_slots, d = slots.shape
    n_experts, _, two_ff = w_gu.shape
    d_ff = two_ff // 2
    bm = EXPERT_ROWS
    grid_spec = pltpu.PrefetchScalarGridSpec(
        num_scalar_prefetch=1,
        grid=(n_slots // bm,),
        in_specs=[pl.BlockSpec((bm, d),
                               lambda j, m: (jnp.minimum(j, m[META_USED_BLOCKS, 0] - 1), 0)),
                  pl.BlockSpec((1, d, two_ff), lambda j, m: (m[META_BLOCK_EXPERT, j], 0, 0)),
                  pl.BlockSpec((1, 1, two_ff), lambda j, m: (m[META_BLOCK_EXPERT, j], 0, 0)),
                  pl.BlockSpec((1, d_ff, d), lambda j, m: (m[META_BLOCK_EXPERT, j], 0, 0)),
                  pl.BlockSpec((1, 1, d), lambda j, m: (m[META_BLOCK_EXPERT, j], 0, 0))],
        out_specs=pl.BlockSpec((bm, d), lambda j, m: (j, 0)),
        scratch_shapes=[pltpu.VMEM((d, two_ff), BF16), pltpu.VMEM((d_ff, d), BF16)],
    )
    return pl.pallas_call(
        functools.partial(_experts_kernel, d_ff),
        grid_spec=grid_spec,
        out_shape=jax.ShapeDtypeStruct((n_slots, d), F32),
        compiler_params=_params("arbitrary"),
        name="experts",
    )(meta, slots, w_gu, b_gu, w_down, b_down)


def _combine_kernel(dest_ref, y_ref, x1_ref, gates_ref, g_ref, out_ref, buf, sem):
    tb, d = x1_ref.shape
    sub = buf.shape[2]

    def issue(g, carry):
        for j in range(TOKENS_PER_DEST_ROW):
            group = g * (TOKENS_PER_DEST_ROW // sub) + j // sub
            for c in range(TOP_K):
                slot = dest_ref[0, g, c * TOKENS_PER_DEST_ROW + j]
                pltpu.make_async_copy(y_ref.at[pl.ds(slot, 1)],
                                      buf.at[c, group, pl.ds(j % sub, 1)], sem
                                      ).start(priority=c % 2)
        return carry

    lax.fori_loop(0, tb // TOKENS_PER_DEST_ROW, issue, 0)
    for c in range(TOP_K):
        pltpu.make_async_copy(y_ref.at[pl.ds(0, tb)], y_ref.at[pl.ds(0, tb)], sem).wait()
    gates = gates_ref[...]
    acc = x1_ref[...]
    for c in range(TOP_K):
        acc = acc + buf[c].reshape(tb, d) * gates[:, c:c + 1]
    out_ref[...] = _rms_rows(acc, g_ref[...])


def _combine(dest_blocks, y_slots, x1, gates_t, final_g):
    n, d = x1.shape
    tb = MOVE_TOKENS
    return pl.pallas_call(
        _combine_kernel,
        grid=(n // tb,),
        in_specs=[pl.BlockSpec((1,) + dest_blocks.shape[1:], lambda i: (i, 0, 0),
                               memory_space=pltpu.SMEM),
                  pl.BlockSpec(memory_space=pl.ANY),
                  pl.BlockSpec((tb, d), lambda i: (i, 0)),
                  pl.BlockSpec((tb, TOP_K), lambda i: (i, 0)),
                  pl.BlockSpec((1, d), lambda i: (0, 0))],
        out_specs=pl.BlockSpec((tb, d), lambda i: (i, 0)),
        out_shape=jax.ShapeDtypeStruct((n, d), F32),
        scratch_shapes=[pltpu.VMEM((TOP_K, tb // V7X_SUBLANES, V7X_SUBLANES, d), F32),
                        pltpu.SemaphoreType.DMA(())],
        compiler_params=_params("arbitrary"),
        name="combine",
    )(dest_blocks, y_slots, x1, gates_t, final_g)


def _group_matrix(width):
    a = lax.broadcasted_iota(I32, (width, width), 0) // HEAD_DIM
    b = lax.broadcasted_iota(I32, (width, width), 1) // HEAD_DIM
    g = (a == b).astype(BF16)
    return jnp.concatenate([g, g], axis=0)


def _row(vec):
    return vec.reshape(1, -1).astype(F32)


def kernel(x, w_in, conv_w, conv_norm_g, rwkv_mu, w0, w_up, a0, a_up, g_up, k_k, k_a, r_k,
           gn_w, gn_b, w_out, norm_mix_g, norm_ffn_g, w_router, b_router, w_gu, b_gu, w_down,
           b_down, norm_final_g):
    n_batch, seq, d = x.shape
    n = n_batch * seq
    depth = w_in.shape[0]
    cw = conv_w.shape[2]
    rw = w0.shape[1]
    n_experts = w_router.shape[2]
    decay_lora = w_up.shape[1]
    aaa_lora = a_up.shape[1]
    assert decay_lora + aaa_lora == V7X_LANES
    assert depth == 1
    n_slots = n * TOP_K + n_experts * EXPERT_ROWS
    n_blocks = n_slots // EXPERT_ROWS
    n_blocks_padded = -(-n_blocks // V7X_LANES) * V7X_LANES
    seg2 = _group_matrix(rw)

    x2d = x.reshape(n, d)
    for l in range(depth):
        w_up_pad = jnp.concatenate([w_up[l], jnp.zeros((aaa_lora, rw), F32)], axis=0).astype(BF16)
        a_up_pad = jnp.concatenate([jnp.zeros((decay_lora, rw), F32), a_up[l]], axis=0).astype(BF16)
        (yconv, r, lw, k, v, kk, b, gate, bonus) = _mix_prep(
            x2d, seq, _row(norm_mix_g[l]), w_in[l].astype(BF16), conv_w[l].astype(F32),
            _row(conv_norm_g[l]), _row(rwkv_mu[l]), _row(w0[l]), w_up_pad, _row(a0[l]), a_up_pad,
            g_up[l].astype(BF16), _row(k_k[l]), _row(k_a[l]), _row(r_k[l]), seg2)
        shape3 = (n_batch, seq, rw)
        o = _wkv_chunk(r.reshape(shape3), lw.reshape(shape3), k.reshape(shape3),
                       v.reshape(shape3), kk.reshape(shape3), b.reshape(shape3)).reshape(n, rw)
        w_out_b = w_out[l].astype(BF16)
        wr_t = w_router[l].T.astype(F32)
        wr_hi = wr_t.astype(BF16)
        wr_lo = (wr_t - wr_hi.astype(F32)).astype(BF16)
        x1, h2, top_e, gates = _post_mix(
            o, bonus, gate, yconv, x2d, _row(gn_w[l]), _row(gn_b[l]), w_out_b[:cw], w_out_b[cw:],
            _row(norm_ffn_g[l]), wr_hi, wr_lo, b_router[l].reshape(n_experts, 1).astype(F32), seg2)
        dest, meta = _route(top_e, n_experts, n_blocks_padded)
        tb = MOVE_TOKENS
        dest_blocks = (dest.reshape(TOP_K, n // tb, tb // TOKENS_PER_DEST_ROW, TOKENS_PER_DEST_ROW)
                       .transpose(1, 2, 0, 3).reshape(n // tb, tb // TOKENS_PER_DEST_ROW, V7X_LANES))
        slots = _dispatch(meta, dest_blocks, h2, n_slots, n_experts)
        y_slots = _experts(meta, slots, w_gu[l].astype(F32),
                           b_gu[l].reshape(n_experts, 1, -1).astype(F32),
                           w_down[l].astype(F32),
                           b_down[l].reshape(n_experts, 1, -1).astype(F32))
        x2d = _combine(dest_blocks, y_slots, x1, gates.T, _row(norm_final_g))
    return x2d.reshape(n_batch, seq, d)
```

```python
import functools

import jax
import jax.numpy as jnp
from jax import lax
from jax.experimental import pallas as pl
from jax.experimental.pallas import tpu as pltpu

F32 = jnp.float32
BF16 = jnp.bfloat16
I32 = jnp.int32

HEAD_DIM = 64
TOP_K = 4
NORM_EPS = 1e-5
GN_EPS = HEAD_DIM * 1e-5
SWIGLU_LIMIT = 7.0
SWIGLU_ALPHA = 1.702

V7X_LANES = 128
V7X_SUBLANES = 8
V7X_VMEM_LIMIT_BYTES = 56 * 1024 * 1024

PREP_ROWS = 256
POST_ROWS = 512
SCAN_FRAMES = 64
SCAN_BATCHES_PER_GROUP = 4
ROUTE_TOKENS = 1024
MOVE_TOKENS = 256
DISPATCH_TOKENS = 512
TOKENS_PER_DEST_ROW = V7X_LANES // TOP_K
META_BLOCK_EXPERT = 0
META_USED_BLOCKS = 1
META_LAST_BLOCK = 2
EXPERT_ROWS = 512

def _params(*semantics):
    return pltpu.CompilerParams(dimension_semantics=semantics,
                                vmem_limit_bytes=V7X_VMEM_LIMIT_BYTES)


def _split_bf16(x):
    hi = x.astype(BF16)
    lo = (x - hi.astype(F32)).astype(BF16)
    return hi, lo


def _seg_sum(x, seg2):
    hi, lo = _split_bf16(x)
    return jnp.dot(jnp.concatenate([hi, lo], axis=1), seg2, preferred_element_type=F32)


def _rms_rows(x, g):
    return x * lax.rsqrt(jnp.mean(x * x, axis=-1, keepdims=True) + NORM_EPS) * g


def _sigmoid(x):
    return 1.0 / (1.0 + jnp.exp(-x))


def _mix_prep_kernel(blocks_per_seq, cw, rw,
                     xprev_ref, x_ref, g_ref, w_in_ref, conv_w_ref, conv_g_ref, mu_ref,
                     w0_ref, w_up_ref, a0_ref, a_up_ref, g_up_ref, k_k_ref, k_a_ref, r_k_ref,
                     seg2_ref,
                     yconv_ref, r_ref, w_ref, k_ref, v_ref, kk_ref, b_ref, gate_ref, bonus_ref,
                     p_scr, z_scr):
    tm = x_ref.shape[0]
    halo = xprev_ref.shape[0]
    first = (pl.program_id(0) % blocks_per_seq) == 0
    xp = xprev_ref[...] * jnp.where(first, 0.0, 1.0)
    xa = jnp.concatenate([xp, x_ref[...]], axis=0)
    h = _rms_rows(xa, g_ref[...])
    p_scr[...] = jnp.dot(h.astype(BF16), w_in_ref[...], preferred_element_type=F32)
    seg2 = seg2_ref[...]

    z_scr[...] = p_scr[:, 2 * cw:3 * cw] * p_scr[:, 0:cw]
    conv = (conv_w_ref[0:1, :] * z_scr[halo - 2:halo - 2 + tm, :]
            + conv_w_ref[1:2, :] * z_scr[halo - 1:halo - 1 + tm, :]
            + conv_w_ref[2:3, :] * z_scr[halo:halo + tm, :])
    y = p_scr[halo:halo + tm, cw:2 * cw] * conv
    ms = _seg_sum(y * y, seg2) * (1.0 / HEAD_DIM)
    yconv_ref[...] = (y * lax.rsqrt(ms + NORM_EPS) * conv_g_ref[...]).astype(yconv_ref.dtype)

    c0 = 3 * cw
    cur = p_scr[halo:halo + tm, c0:]
    prev = p_scr[halo - 1:halo - 1 + tm, c0:]
    q = cur + (prev - cur) * mu_ref[...]
    r = q[:, 0:rw]
    k = q[:, rw:2 * rw]
    v = q[:, 2 * rw:3 * rw]
    lora_wa = q[:, 3 * rw:3 * rw + V7X_LANES]
    lora_g = q[:, 3 * rw + V7X_LANES:]
    w_lin = w0_ref[...] + jnp.dot(jnp.tanh(lora_wa).astype(BF16), w_up_ref[...],
                                  preferred_element_type=F32)
    neg = -w_lin
    softplus = jnp.maximum(neg, 0.0) + jnp.log(1.0 + jnp.exp(-jnp.abs(neg)))
    log_decay = -jnp.exp(-softplus - 0.5)
    a = _sigmoid(a0_ref[...] + jnp.dot(lora_wa.astype(BF16), a_up_ref[...],
                                       preferred_element_type=F32))
    gate = jnp.dot(_sigmoid(lora_g).astype(BF16), g_up_ref[...], preferred_element_type=F32)
    kk = k * k_k_ref[...]
    kk = kk / jnp.maximum(jnp.sqrt(_seg_sum(kk * kk, seg2)), 1e-12)
    k_mod = k * (1.0 + (a - 1.0) * k_a_ref[...])
    bonus = _seg_sum(r * k_mod * r_k_ref[...], seg2) * v
    r_ref[...] = r
    w_ref[...] = log_decay
    k_ref[...] = k_mod
    v_ref[...] = v
    kk_ref[...] = kk
    b_ref[...] = kk * a
    gate_ref[...] = gate
    bonus_ref[...] = bonus


def _mix_prep(x2d, seq, norm_g, w_in, conv_w, conv_g, mu, w0, w_up, a0, a_up, g_up, k_k, k_a,
              r_k, seg2):
    n, d = x2d.shape
    tm = PREP_ROWS
    halo = V7X_SUBLANES
    cw = conv_w.shape[1]
    rw = w0.shape[1]
    in_cols = w_in.shape[1]
    assert seq % tm == 0 and n % tm == 0
    full = lambda a: pl.BlockSpec(a.shape, lambda i: (0,) * a.ndim)
    row_spec = lambda c: pl.BlockSpec((tm, c), lambda i: (i, 0))
    consts = (norm_g, w_in, conv_w, conv_g, mu, w0, w_up, a0, a_up, g_up, k_k, k_a, r_k, seg2)
    outs = [jax.ShapeDtypeStruct((n, cw), BF16)] + [jax.ShapeDtypeStruct((n, rw), F32)] * 8
    return pl.pallas_call(
        functools.partial(_mix_prep_kernel, seq // tm, cw, rw),
        grid=(n // tm,),
        in_specs=[pl.BlockSpec((halo, d), lambda i: (jnp.maximum(i * (tm // halo) - 1, 0), 0)),
                  row_spec(d)] + [full(c) for c in consts],
        out_specs=[row_spec(cw)] + [row_spec(rw)] * 8,
        out_shape=outs,
        scratch_shapes=[pltpu.VMEM((halo + tm, in_cols), F32), pltpu.VMEM((halo + tm, cw), F32)],
        compiler_params=_params("parallel"),
        name="mix_prep",
    )(x2d, x2d, *consts)


def _bdot(a, b):
    return jnp.dot(a.astype(BF16), b.astype(BF16), preferred_element_type=F32)


def _bdot_nt(a, b):
    return lax.dot_general(a.astype(BF16), b.astype(BF16), (((1,), (1,)), ((), ())),
                           preferred_element_type=F32)


def _wkv_chunk_kernel(n_batch, n_pairs,
                      r_ref, lw_ref, k_ref, v_ref, kk_ref, b_ref, o_ref, state_ref):
    frames = r_ref.shape[1]
    lanes = V7X_LANES
    hd = HEAD_DIM
    assert frames == hd

    @pl.when(pl.program_id(0) == 0)
    def _():
        state_ref[...] = jnp.zeros_like(state_ref)

    row = lax.broadcasted_iota(I32, (lanes, lanes), 0)
    lane = lax.broadcasted_iota(I32, (lanes, lanes), 1)
    same_head = (row // hd) == (lane // hd)
    rt = row % hd
    ls = lane % hd
    strict_same = same_head & (rt > ls)
    strict_cross = jnp.logical_not(same_head) & (rt > ls)
    incl_same = same_head & (rt >= ls)
    incl_cross = jnp.logical_not(same_head) & (rt >= ls)
    eye = row == lane
    level_masks = []
    m = 1
    while m < hd:
        level_masks.append(same_head & ((rt // (2 * m)) == (ls // (2 * m)))
                           & (((rt // m) % 2) == 1) & (((ls // m) % 2) == 0))
        m *= 2
    left = lax.broadcasted_iota(I32, (frames, lanes), 1) < hd
    tri = (lax.broadcasted_iota(I32, (frames, frames), 0)
           >= lax.broadcasted_iota(I32, (frames, frames), 1)).astype(BF16)

    def diag_blocks(x):
        return jnp.concatenate([jnp.where(left, x, 0.0), jnp.where(left, 0.0, x)], axis=0)

    def cross_blocks(x):
        return jnp.concatenate([jnp.where(left, 0.0, x), jnp.where(left, x, 0.0)], axis=0)

    def batch_prep(bi):
        lw = lw_ref[bi]
        hi = lw.astype(BF16)
        rem = lw - hi.astype(F32)
        mid = rem.astype(BF16)
        lo = (rem - mid.astype(F32)).astype(BF16)
        cs = (jnp.dot(tri, hi, preferred_element_type=F32)
              + jnp.dot(tri, mid, preferred_element_type=F32)
              + jnp.dot(tri, lo, preferred_element_type=F32))
        cs_end = cs[frames - 1:frames, :]
        e_neg = jnp.exp(-cs)
        e_end = jnp.exp(cs_end - cs)
        b_in = b_ref[bi]
        k_in = k_ref[bi]
        return dict(a_t=-kk_ref[bi] * jnp.exp(cs - lw), b_t=b_in * e_neg, k_t=k_in * e_neg,
                    r_t=r_ref[bi] * jnp.exp(cs), b_h=b_in * e_end, k_h=k_in * e_end,
                    v=v_ref[bi], g_end=jnp.exp(cs_end))

    for b0 in range(0, n_batch, SCAN_BATCHES_PER_GROUP):
        group = []
        for bi in range(b0, min(b0 + SCAN_BATCHES_PER_GROUP, n_batch)):
            prep = batch_prep(bi)
            for pi in range(n_pairs):
                sl = slice(pi * lanes, (pi + 1) * lanes)
                group.append((bi, sl, bi * n_pairs + pi, {n: x[:, sl] for n, x in prep.items()}))
        a_bd = [diag_blocks(p["a_t"]) for _, _, _, p in group]
        r_bd = [diag_blocks(p["r_t"]) for _, _, _, p in group]
        d0 = [_bdot_nt(jnp.concatenate([a[:hd], r[:hd]], axis=0),
                       jnp.concatenate([p["b_t"], p["k_t"]], axis=0))
              for a, r, (_, _, _, p) in zip(a_bd, r_bd, group)]
        d1 = [_bdot_nt(jnp.concatenate([a[hd:], r[hd:]], axis=0),
                       jnp.concatenate([p["k_t"], p["b_t"]], axis=0))
              for a, r, (_, _, _, p) in zip(a_bd, r_bd, group)]
        a_rows = [jnp.concatenate([x[:hd], y[:hd]], axis=0) for x, y in zip(d0, d1)]
        m_rows = [jnp.concatenate([x[hd:], y[hd:]], axis=0) for x, y in zip(d0, d1)]
        a_ab = [jnp.where(strict_same, x, 0.0) for x in a_rows]
        a_ak = [jnp.where(strict_cross, x, 0.0) for x in a_rows]
        m_rb = [jnp.where(incl_same, x, 0.0) for x in m_rows]
        m_rk = [jnp.where(incl_cross, x, 0.0) for x in m_rows]
        t_inv = [jnp.where(eye, 1.0, jnp.where(level_masks[0], x, 0.0)) for x in a_ab]
        for mask in level_masks[1:]:
            half = [_bdot(t, jnp.where(mask, x, 0.0)) for t, x in zip(t_inv, a_ab)]
            t_inv = [t + _bdot(h, t) for t, h in zip(t_inv, half)]
        v_x = [cross_blocks(p["v"]) for _, _, _, p in group]
        akv = [_bdot(x, v) for x, v in zip(a_ak, v_x)]
        wu = [_bdot(t, jnp.concatenate([a, x], axis=1))
              for t, a, x in zip(t_inv, a_bd, akv)]
        bh_t = [diag_blocks(p["b_h"]).T for _, _, _, p in group]
        kh_t = [cross_blocks(p["k_h"]).T for _, _, _, p in group]
        pw_rw = [_bdot(jnp.concatenate([bt, m], axis=0), x[:, :lanes])
                 for bt, m, x in zip(bh_t, m_rb, wu)]
        q_o = [_bdot(jnp.concatenate([jnp.concatenate([bt, kt], axis=1),
                                      jnp.concatenate([mb, mk], axis=1)], axis=0),
                     jnp.concatenate([x[:, lanes:], v], axis=0))
               for bt, kt, mb, mk, x, v in zip(bh_t, kh_t, m_rb, m_rk, wu, v_x)]
        res = [_bdot(jnp.concatenate([r + pr[lanes:],
                                      pr[:lanes] + jnp.where(eye, p["g_end"], 0.0)], axis=0),
                     state_ref[idx])
               for r, pr, (_, _, idx, p) in zip(r_bd, pw_rw, group)]
        for x, q, (bi, sl, idx, _) in zip(res, q_o, group):
            o_bd = x[:lanes] + q[lanes:]
            state_ref[idx] = x[lanes:] + q[:lanes]
            o_ref[bi, :, sl] = jnp.where(left, o_bd[:hd], o_bd[hd:])


def _wkv_chunk(r, lw, k, v, kk, b):
    n_batch, seq, rw = r.shape
    n_pairs = rw // V7X_LANES
    frames = SCAN_FRAMES
    assert seq % frames == 0
    spec = pl.BlockSpec((n_batch, frames, rw), lambda c: (0, c, 0))
    return pl.pallas_call(
        functools.partial(_wkv_chunk_kernel, n_batch, n_pairs),
        grid=(seq // frames,),
        in_specs=[spec] * 6,
        out_specs=spec,
        out_shape=jax.ShapeDtypeStruct((n_batch, seq, rw), F32),
        scratch_shapes=[pltpu.VMEM((n_batch * n_pairs, V7X_LANES, V7X_LANES), F32)],
        compiler_params=_params("arbitrary"),
        name="wkv_chunk",
    )(r, lw, k, v, kk, b)


def _post_mix_kernel(n_experts,
                     o_ref, bonus_ref, gate_ref, yconv_ref, x_ref, gn_w_ref, gn_b_ref,
                     w_out_c_ref, w_out_r_ref, ffn_g_ref, wr_hi_ref, wr_lo_ref, b_router_ref,
                     seg2_ref,
                     x1_ref, h2_ref, top_e_ref, gates_ref):
    seg2 = seg2_ref[...]
    o = o_ref[...]
    mean = _seg_sum(o, seg2) * (1.0 / HEAD_DIM)
    cen = o - mean
    var = _seg_sum(cen * cen, seg2) * (1.0 / HEAD_DIM)
    o = cen * lax.rsqrt(var + GN_EPS) * gn_w_ref[...] + gn_b_ref[...]
    y_rwkv = (o + bonus_ref[...]) * gate_ref[...]
    x1 = (x_ref[...]
          + jnp.dot(yconv_ref[...], w_out_c_ref[...], preferred_element_type=F32)
          + jnp.dot(y_rwkv.astype(BF16), w_out_r_ref[...], preferred_element_type=F32))
    x1_ref[...] = x1
    h2 = _rms_rows(x1, ffn_g_ref[...])
    h2_ref[...] = h2

    h_hi, h_lo = _split_bf16(h2)
    nt = (((1,), (1,)), ((), ()))
    logits = (lax.dot_general(wr_hi_ref[...], h_hi, nt, preferred_element_type=F32)
              + lax.dot_general(wr_hi_ref[...], h_lo, nt, preferred_element_type=F32)
              + lax.dot_general(wr_lo_ref[...], h_hi, nt, preferred_element_type=F32)
              + b_router_ref[...])
    e_id = lax.broadcasted_iota(I32, logits.shape, 0).astype(F32)
    work = logits
    tops, ids = [], []
    for _ in range(TOP_K):
        m = jnp.max(work, axis=0, keepdims=True)
        sel = jnp.min(jnp.where(work == m, e_id, float(n_experts)), axis=0, keepdims=True)
        tops.append(m)
        ids.append(sel)
        work = jnp.where(e_id == sel, -jnp.inf, work)
    ex = [jnp.exp(t - tops[0]) for t in tops]
    denom = ex[0] + ex[1] + ex[2] + ex[3]
    top_e_ref[...] = jnp.concatenate(ids, axis=0).astype(I32)
    gates_ref[...] = jnp.concatenate([e / denom for e in ex], axis=0)


def _post_mix(o, bonus, gate, yconv, x2d, gn_w, gn_b, w_out_c, w_out_r, ffn_g, wr_hi, wr_lo,
              b_router, seg2):
    n, d = x2d.shape
    rw = o.shape[1]
    cw = yconv.shape[1]
    n_experts = wr_hi.shape[0]
    tm = POST_ROWS
    full = lambda a: pl.BlockSpec(a.shape, lambda i: (0,) * a.ndim)
    row_spec = lambda c: pl.BlockSpec((tm, c), lambda i: (i, 0))
    col_spec = pl.BlockSpec((TOP_K, tm), lambda i: (0, i))
    consts = (gn_w, gn_b, w_out_c, w_out_r, ffn_g, wr_hi, wr_lo, b_router, seg2)
    return pl.pallas_call(
        functools.partial(_post_mix_kernel, n_experts),
        grid=(n // tm,),
        in_specs=[row_spec(rw), row_spec(rw), row_spec(rw), row_spec(cw), row_spec(d)]
                 + [full(c) for c in consts],
        out_specs=[row_spec(d), row_spec(d), col_spec, col_spec],
        out_shape=[jax.ShapeDtypeStruct((n, d), F32), jax.ShapeDtypeStruct((n, d), F32),
                   jax.ShapeDtypeStruct((TOP_K, n), I32), jax.ShapeDtypeStruct((TOP_K, n), F32)],
        compiler_params=_params("parallel"),
        name="post_mix",
    )(o, bonus, gate, yconv, x2d, *consts)


def _route_kernel(n_experts, block_rows,
                  top_e_ref, tri_ref, dest_ref, meta_ref,
                  count_ref, start_ref, carry_ref):
    phase = pl.program_id(0)
    j = pl.program_id(1)
    tb = top_e_ref.shape[1]
    e_id = lax.broadcasted_iota(I32, (n_experts, tb), 0)
    top_e = top_e_ref[...]
    onehot = jnp.zeros((n_experts, tb), F32)
    for c in range(TOP_K):
        onehot = onehot + jnp.where(top_e[c:c + 1, :] == e_id, 1.0, 0.0)
    block_count = jnp.sum(onehot, axis=1, keepdims=True)

    @pl.when((phase == 0) & (j == 0))
    def _():
        count_ref[...] = jnp.zeros_like(count_ref)

    @pl.when(phase == 0)
    def _():
        count_ref[...] += block_count

    @pl.when((phase == 1) & (j == 0))
    def _():
        counts = count_ref[...]
        padded = jnp.ceil(counts * (1.0 / block_rows)) * block_rows
        sub = lax.broadcasted_iota(I32, (n_experts, n_experts), 0)
        lane = lax.broadcasted_iota(I32, (n_experts, n_experts), 1)
        padded_row = jnp.sum(jnp.where(sub == lane, padded, 0.0), axis=0, keepdims=True)
        start = jnp.sum(jnp.where(lane < sub, padded_row, 0.0), axis=1, keepdims=True)
        start_ref[...] = start
        carry_ref[...] = jnp.zeros_like(carry_ref)
        end = start + padded
        nb = meta_ref.shape[1]
        slot0 = (lax.broadcasted_iota(I32, (n_experts, nb), 1) * block_rows).astype(F32)
        block_e = jnp.sum(jnp.where(end <= slot0, 1.0, 0.0), axis=0, keepdims=True)
        block_e = jnp.minimum(block_e, n_experts - 1.0)
        used = jnp.max(end, axis=0, keepdims=True) * (1.0 / block_rows)
        e_sub = lax.broadcasted_iota(I32, (n_experts, nb), 0)
        e_lane = lax.broadcasted_iota(I32, (n_experts, nb), 1)
        end_row = jnp.sum(jnp.where(e_sub == e_lane, end, 0.0), axis=0, keepdims=True)
        last_block = jnp.maximum(end_row * (1.0 / block_rows) - 1.0, 0.0)
        row = lax.broadcasted_iota(I32, meta_ref.shape, 0)
        meta_ref[...] = jnp.where(row == META_BLOCK_EXPERT, block_e,
                                  jnp.where(row == META_LAST_BLOCK, last_block, used)).astype(I32)

    @pl.when(phase == 1)
    def _():
        incl = jnp.dot(onehot.astype(BF16), tri_ref[...], preferred_element_type=F32)
        base = incl - onehot + carry_ref[...] + start_ref[...]
        rows = [jnp.sum(jnp.where(top_e[c:c + 1, :] == e_id, base, 0.0), axis=0, keepdims=True)
                for c in range(TOP_K)]
        dest_ref[...] = jnp.concatenate(rows, axis=0).astype(I32)
        carry_ref[...] += block_count


def _route(top_e, n_experts, n_blocks_padded):
    n = top_e.shape[1]
    tb = ROUTE_TOKENS
    assert n % tb == 0
    tri = (lax.broadcasted_iota(I32, (tb, tb), 0) <= lax.broadcasted_iota(I32, (tb, tb), 1)
           ).astype(BF16)
    return pl.pallas_call(
        functools.partial(_route_kernel, n_experts, EXPERT_ROWS),
        grid=(2, n // tb),
        in_specs=[pl.BlockSpec((TOP_K, tb), lambda ph, j: (0, j)),
                  pl.BlockSpec((tb, tb), lambda ph, j: (0, 0))],
        out_specs=[pl.BlockSpec((TOP_K, tb), lambda ph, j: (0, j * ph)),
                   pl.BlockSpec((V7X_SUBLANES, n_blocks_padded), lambda ph, j: (0, 0))],
        out_shape=[jax.ShapeDtypeStruct((TOP_K, n), I32),
                   jax.ShapeDtypeStruct((V7X_SUBLANES, n_blocks_padded), I32)],
        scratch_shapes=[pltpu.VMEM((n_experts, 1), F32)] * 3,
        compiler_params=_params("arbitrary", "arbitrary"),
        name="route",
    )(top_e, tri)


def _dispatch_kernel(n_experts, block_rows,
                     meta_ref, dest_ref, h_ref, slots_ref, zero_ref, sem, zero_sem):
    sub = h_ref.shape[1]
    tb = h_ref.shape[0] * sub

    @pl.when(pl.program_id(0) == 0)
    def _():
        zero_ref[...] = jnp.zeros_like(zero_ref)

        def clear(block):
            row0 = pl.multiple_of(block * block_rows, block_rows)
            return pltpu.make_async_copy(zero_ref, slots_ref.at[pl.ds(row0, block_rows)], zero_sem)

        def start_clear(block, carry):
            clear(block).start()
            return carry

        def wait_clear(block, carry):
            clear(block).wait()
            return carry

        for e in range(n_experts):
            clear(meta_ref[META_LAST_BLOCK, e]).start()
        used = meta_ref[META_USED_BLOCKS, 0]
        n_blocks = slots_ref.shape[0] // block_rows
        lax.fori_loop(used, n_blocks, start_clear, 0)
        for e in range(n_experts):
            clear(meta_ref[META_LAST_BLOCK, e]).wait()
        lax.fori_loop(used, n_blocks, wait_clear, 0)

    def issue(g, carry):
        for j in range(TOKENS_PER_DEST_ROW):
            src = h_ref.at[g * (TOKENS_PER_DEST_ROW // sub) + j // sub, pl.ds(j % sub, 1)]
            for c in range(TOP_K):
                slot = dest_ref[0, g, c * TOKENS_PER_DEST_ROW + j]
                pltpu.make_async_copy(src, slots_ref.at[pl.ds(slot, 1)], sem).start(priority=c % 2)
        return carry

    lax.fori_loop(0, tb // TOKENS_PER_DEST_ROW, issue, 0)
    for c in range(TOP_K):
        pltpu.make_async_copy(slots_ref.at[pl.ds(0, tb)], slots_ref.at[pl.ds(0, tb)], sem).wait()


def _dispatch(meta, dest_blocks, h2, n_slots, n_experts):
    n, d = h2.shape
    tb = DISPATCH_TOKENS
    sub = V7X_SUBLANES
    dest_blocks = dest_blocks.reshape(n // tb, tb // TOKENS_PER_DEST_ROW, V7X_LANES)
    grid_spec = pltpu.PrefetchScalarGridSpec(
        num_scalar_prefetch=1,
        grid=(n // tb,),
        in_specs=[pl.BlockSpec((1,) + dest_blocks.shape[1:], lambda i, m: (i, 0, 0),
                               memory_space=pltpu.SMEM),
                  pl.BlockSpec((tb // sub, sub, d), lambda i, m: (i, 0, 0))],
        out_specs=pl.BlockSpec(memory_space=pl.ANY),
        scratch_shapes=[pltpu.VMEM((EXPERT_ROWS, d), h2.dtype),
                        pltpu.SemaphoreType.DMA(()), pltpu.SemaphoreType.DMA(())],
    )
    return pl.pallas_call(
        functools.partial(_dispatch_kernel, n_experts, EXPERT_ROWS),
        grid_spec=grid_spec,
        out_shape=jax.ShapeDtypeStruct((n_slots, d), h2.dtype),
        compiler_params=_params("arbitrary"),
        name="dispatch",
    )(meta, dest_blocks, h2.reshape(n // sub, sub, d))


def _experts_kernel(d_ff, meta_ref, x_ref, w_gu_ref, b_gu_ref, w_down_ref, b_down_ref, y_ref,
                    w_gu_bf16, w_down_bf16):
    j = pl.program_id(0)
    used = meta_ref[META_USED_BLOCKS, 0]
    expert = meta_ref[META_BLOCK_EXPERT, j]
    prev_expert = meta_ref[META_BLOCK_EXPERT, jnp.maximum(j - 1, 0)]

    @pl.when((j < used) & ((j == 0) | (expert != prev_expert)))
    def _():
        w_gu_bf16[...] = w_gu_ref[0].astype(BF16)
        w_down_bf16[...] = w_down_ref[0].astype(BF16)

    @pl.when(j < used)
    def _():
        gu = (jnp.dot(x_ref[...].astype(BF16), w_gu_bf16[...], preferred_element_type=F32)
              + b_gu_ref[0])
        gate = jnp.minimum(gu[:, :d_ff], SWIGLU_LIMIT)
        up = jnp.clip(gu[:, d_ff:], -SWIGLU_LIMIT, SWIGLU_LIMIT)
        act = (up + 1.0) * (gate * _sigmoid(SWIGLU_ALPHA * gate))
        y_ref[...] = (jnp.dot(act.astype(BF16), w_down_bf16[...], preferred_element_type=F32)
                      + b_down_ref[0])

    @pl.when(j >= used)
    def _():
        y_ref[...] = jnp.zeros_like(y_ref)


def _experts(meta, slots, w_gu, b_gu, w_down, b_down):
    n_slots, d = slots.shape
    n_experts, _, two_ff = w_gu.shape
    d_ff = two_ff // 2
    bm = EXPERT_ROWS
    grid_spec = pltpu.PrefetchScalarGridSpec(
        num_scalar_prefetch=1,
        grid=(n_slots // bm,),
        in_specs=[pl.BlockSpec((bm, d),
                               lambda j, m: (jnp.minimum(j, m[META_USED_BLOCKS, 0] - 1), 0)),
                  pl.BlockSpec((1, d, two_ff), lambda j, m: (m[META_BLOCK_EXPERT, j], 0, 0)),
                  pl.BlockSpec((1, 1, two_ff), lambda j, m: (m[META_BLOCK_EXPERT, j], 0, 0)),
                  pl.BlockSpec((1, d_ff, d), lambda j, m: (m[META_BLOCK_EXPERT, j], 0, 0)),
                  pl.BlockSpec((1, 1, d), lambda j, m: (m[META_BLOCK_EXPERT, j], 0, 0))],
        out_specs=pl.BlockSpec((bm, d), lambda j, m: (j, 0)),
        scratch_shapes=[pltpu.VMEM((d, two_ff), BF16), pltpu.VMEM((d_ff, d), BF16)],
    )
    return pl.pallas_call(
        functools.partial(_experts_kernel, d_ff),
        grid_spec=grid_spec,
        out_shape=jax.ShapeDtypeStruct((n_slots, d), F32),
        compiler_params=_params("arbitrary"),
        name="experts",
    )(meta, slots, w_gu, b_gu, w_down, b_down)


def _combine_kernel(dest_ref, y_ref, x1_ref, gates_ref, g_ref, out_ref, buf_even, buf_odd, sems):
    tb, d = x1_ref.shape
    sub = buf_even.shape[2]
    rows = TOKENS_PER_DEST_ROW
    n_trips = tb // rows
    step = pl.program_id(0)
    last = pl.num_programs(0) - 1

    @pl.when(step == 0)
    def _():
        buf_odd[...] = jnp.zeros_like(buf_odd)

    def whole_buffer_wait(sem):
        for c in range(TOP_K):
            pltpu.make_async_copy(y_ref.at[pl.ds(0, tb)], y_ref.at[pl.ds(0, tb)], sem).wait()

    def run(buf_w, sem_w, buf_r, sem_r):
        def issue(g):
            for j in range(rows):
                group = g * (rows // sub) + j // sub
                for c in range(TOP_K):
                    slot = dest_ref[0, g, c * rows + j]
                    pltpu.make_async_copy(y_ref.at[pl.ds(slot, 1)],
                                          buf_w.at[c, group, pl.ds(j % sub, 1)], sem_w
                                          ).start(priority=c % 2)

        def reduce(g):
            r0 = pl.multiple_of(g * rows, rows)
            gates = gates_ref[pl.ds(r0, rows), :]
            acc = x1_ref[pl.ds(r0, rows), :]
            for c in range(TOP_K):
                acc = acc + (buf_r[c, pl.ds(g * (rows // sub), rows // sub)].reshape(rows, d)
                             * gates[:, c:c + 1])
            out_ref[pl.ds(r0, rows), :] = _rms_rows(acc, g_ref[...])

        issue(0)

        @pl.when(step > 0)
        def _():
            whole_buffer_wait(sem_r)

        def trip(g, carry):
            issue(g)
            reduce(g - 1)
            return carry

        lax.fori_loop(1, n_trips, trip, 0)
        reduce(n_trips - 1)

        @pl.when(step == last)
        def _():
            whole_buffer_wait(sem_w)

    @pl.when(step % 2 == 0)
    def _():
        run(buf_even, sems.at[0], buf_odd, sems.at[1])

    @pl.when(step % 2 == 1)
    def _():
        run(buf_odd, sems.at[1], buf_even, sems.at[0])


def _combine(dest_blocks, y_slots, x1, gates_t, final_g):
    n, d = x1.shape
    tb = MOVE_TOKENS
    n_blocks = n // tb
    gather_block = lambda i: jnp.minimum(i, n_blocks - 1)
    reduce_block = lambda i: jnp.maximum(i - 1, 0)
    landing = pltpu.VMEM((TOP_K, tb // V7X_SUBLANES, V7X_SUBLANES, d), F32)
    return pl.pallas_call(
        _combine_kernel,
        grid=(n_blocks + 1,),
        in_specs=[pl.BlockSpec((1,) + dest_blocks.shape[1:], lambda i: (gather_block(i), 0, 0),
                               memory_space=pltpu.SMEM),
                  pl.BlockSpec(memory_space=pl.ANY),
                  pl.BlockSpec((tb, d), lambda i: (reduce_block(i), 0)),
                  pl.BlockSpec((tb, TOP_K), lambda i: (reduce_block(i), 0)),
                  pl.BlockSpec((1, d), lambda i: (0, 0))],
        out_specs=pl.BlockSpec((tb, d), lambda i: (reduce_block(i), 0)),
        out_shape=jax.ShapeDtypeStruct((n, d), F32),
        scratch_shapes=[landing, landing, pltpu.SemaphoreType.DMA((2,))],
        compiler_params=_params("arbitrary"),
        name="combine",
    )(dest_blocks, y_slots, x1, gates_t, final_g)


def _group_matrix(width):
    a = lax.broadcasted_iota(I32, (width, width), 0) // HEAD_DIM
    b = lax.broadcasted_iota(I32, (width, width), 1) // HEAD_DIM
    g = (a == b).astype(BF16)
    return jnp.concatenate([g, g], axis=0)


def _row(vec):
    return vec.reshape(1, -1).astype(F32)


def kernel(x, w_in, conv_w, conv_norm_g, rwkv_mu, w0, w_up, a0, a_up, g_up, k_k, k_a, r_k,
           gn_w, gn_b, w_out, norm_mix_g, norm_ffn_g, w_router, b_router, w_gu, b_gu, w_down,
           b_down, norm_final_g):
    n_batch, seq, d = x.shape
    n = n_batch * seq
    depth = w_in.shape[0]
    cw = conv_w.shape[2]
    rw = w0.shape[1]
    n_experts = w_router.shape[2]
    decay_lora = w_up.shape[1]
    aaa_lora = a_up.shape[1]
    assert decay_lora + aaa_lora == V7X_LANES
    assert depth == 1
    n_slots = n * TOP_K + n_experts * EXPERT_ROWS
    n_blocks = n_slots // EXPERT_ROWS
    n_blocks_padded = -(-n_blocks // V7X_LANES) * V7X_LANES
    seg2 = _group_matrix(rw)

    x2d = x.reshape(n, d)
    for l in range(depth):
        w_up_pad = jnp.concatenate([w_up[l], jnp.zeros((aaa_lora, rw), F32)], axis=0).astype(BF16)
        a_up_pad = jnp.concatenate([jnp.zeros((decay_lora, rw), F32), a_up[l]], axis=0).astype(BF16)
        (yconv, r, lw, k, v, kk, b, gate, bonus) = _mix_prep(
            x2d, seq, _row(norm_mix_g[l]), w_in[l].astype(BF16), conv_w[l].astype(F32),
            _row(conv_norm_g[l]), _row(rwkv_mu[l]), _row(w0[l]), w_up_pad, _row(a0[l]), a_up_pad,
            g_up[l].astype(BF16), _row(k_k[l]), _row(k_a[l]), _row(r_k[l]), seg2)
        shape3 = (n_batch, seq, rw)
        o = _wkv_chunk(r.reshape(shape3), lw.reshape(shape3), k.reshape(shape3),
                       v.reshape(shape3), kk.reshape(shape3), b.reshape(shape3)).reshape(n, rw)
        w_out_b = w_out[l].astype(BF16)
        wr_t = w_router[l].T.astype(F32)
        wr_hi = wr_t.astype(BF16)
        wr_lo = (wr_t - wr_hi.astype(F32)).astype(BF16)
        x1, h2, top_e, gates = _post_mix(
            o, bonus, gate, yconv, x2d, _row(gn_w[l]), _row(gn_b[l]), w_out_b[:cw], w_out_b[cw:],
            _row(norm_ffn_g[l]), wr_hi, wr_lo, b_router[l].reshape(n_experts, 1).astype(F32), seg2)
        dest, meta = _route(top_e, n_experts, n_blocks_padded)
        tb = MOVE_TOKENS
        dest_blocks = (dest.reshape(TOP_K, n // tb, tb // TOKENS_PER_DEST_ROW, TOKENS_PER_DEST_ROW)
                       .transpose(1, 2, 0, 3).reshape(n // tb, tb // TOKENS_PER_DEST_ROW, V7X_LANES))
        slots = _dispatch(meta, dest_blocks, h2, n_slots, n_experts)
        y_slots = _experts(meta, slots, w_gu[l].astype(F32),
                           b_gu[l].reshape(n_experts, 1, -1).astype(F32),
                           w_down[l].astype(F32),
                           b_down[l].reshape(n_experts, 1, -1).astype(F32))
        x2d = _combine(dest_blocks, y_slots, x1, gates.T, _row(norm_final_g))
    return x2d.reshape(n_batch, seq, d)
```

```python
import functools

import jax
import jax.numpy as jnp
from jax import lax
from jax.experimental import pallas as pl
from jax.experimental.pallas import tpu as pltpu
from jax.experimental.pallas import tpu_sc as plsc

F32 = jnp.float32
BF16 = jnp.bfloat16
I32 = jnp.int32

HEAD_DIM = 64
TOP_K = 4
NORM_EPS = 1e-5
GN_EPS = HEAD_DIM * 1e-5
SWIGLU_LIMIT = 7.0
SWIGLU_ALPHA = 1.702

V7X_LANES = 128
V7X_SUBLANES = 8
V7X_VMEM_LIMIT_BYTES = 56 * 1024 * 1024

PREP_ROWS = 256
POST_ROWS = 512
SCAN_FRAMES = 64
SCAN_BATCHES_PER_GROUP = 4
ROUTE_TOKENS = 1024
MOVE_TOKENS = 256
DISPATCH_TOKENS = 512
SC_WINDOW = 128
ROW_CHUNK = 256
TOKENS_PER_DEST_ROW = V7X_LANES // TOP_K
META_BLOCK_EXPERT = 0
META_USED_BLOCKS = 1
META_LAST_BLOCK = 2
META_VALID_ROWS = 3
EXPERT_ROWS = 512

def _params(*semantics):
    return pltpu.CompilerParams(dimension_semantics=semantics,
                                vmem_limit_bytes=V7X_VMEM_LIMIT_BYTES)


def _split_bf16(x):
    hi = x.astype(BF16)
    lo = (x - hi.astype(F32)).astype(BF16)
    return hi, lo


def _seg_sum(x, seg2):
    hi, lo = _split_bf16(x)
    return jnp.dot(jnp.concatenate([hi, lo], axis=1), seg2, preferred_element_type=F32)


def _rms_rows(x, g):
    return x * lax.rsqrt(jnp.mean(x * x, axis=-1, keepdims=True) + NORM_EPS) * g


def _sigmoid(x):
    return 1.0 / (1.0 + jnp.exp(-x))


def _mix_prep_kernel(blocks_per_seq, cw, rw,
                     xprev_ref, x_ref, g_ref, w_in_ref, conv_w_ref, conv_g_ref, mu_ref,
                     w0_ref, w_up_ref, a0_ref, a_up_ref, g_up_ref, k_k_ref, k_a_ref, r_k_ref,
                     seg2_ref,
                     yconv_ref, r_ref, w_ref, k_ref, v_ref, kk_ref, b_ref, gate_ref, bonus_ref,
                     p_scr, z_scr):
    tm = x_ref.shape[0]
    halo = xprev_ref.shape[0]
    first = (pl.program_id(0) % blocks_per_seq) == 0
    xp = xprev_ref[...] * jnp.where(first, 0.0, 1.0)
    xa = jnp.concatenate([xp, x_ref[...]], axis=0)
    h = _rms_rows(xa, g_ref[...])
    p_scr[...] = jnp.dot(h.astype(BF16), w_in_ref[...], preferred_element_type=F32)
    seg2 = seg2_ref[...]

    z_scr[...] = p_scr[:, 2 * cw:3 * cw] * p_scr[:, 0:cw]
    conv = (conv_w_ref[0:1, :] * z_scr[halo - 2:halo - 2 + tm, :]
            + conv_w_ref[1:2, :] * z_scr[halo - 1:halo - 1 + tm, :]
            + conv_w_ref[2:3, :] * z_scr[halo:halo + tm, :])
    y = p_scr[halo:halo + tm, cw:2 * cw] * conv
    ms = _seg_sum(y * y, seg2) * (1.0 / HEAD_DIM)
    yconv_ref[...] = (y * lax.rsqrt(ms + NORM_EPS) * conv_g_ref[...]).astype(yconv_ref.dtype)

    c0 = 3 * cw
    cur = p_scr[halo:halo + tm, c0:]
    prev = p_scr[halo - 1:halo - 1 + tm, c0:]
    q = cur + (prev - cur) * mu_ref[...]
    r = q[:, 0:rw]
    k = q[:, rw:2 * rw]
    v = q[:, 2 * rw:3 * rw]
    lora_wa = q[:, 3 * rw:3 * rw + V7X_LANES]
    lora_g = q[:, 3 * rw + V7X_LANES:]
    w_lin = w0_ref[...] + jnp.dot(jnp.tanh(lora_wa).astype(BF16), w_up_ref[...],
                                  preferred_element_type=F32)
    neg = -w_lin
    softplus = jnp.maximum(neg, 0.0) + jnp.log(1.0 + jnp.exp(-jnp.abs(neg)))
    log_decay = -jnp.exp(-softplus - 0.5)
    a = _sigmoid(a0_ref[...] + jnp.dot(lora_wa.astype(BF16), a_up_ref[...],
                                       preferred_element_type=F32))
    gate = jnp.dot(_sigmoid(lora_g).astype(BF16), g_up_ref[...], preferred_element_type=F32)
    kk = k * k_k_ref[...]
    kk = kk / jnp.maximum(jnp.sqrt(_seg_sum(kk * kk, seg2)), 1e-12)
    k_mod = k * (1.0 + (a - 1.0) * k_a_ref[...])
    bonus = _seg_sum(r * k_mod * r_k_ref[...], seg2) * v
    r_ref[...] = r
    w_ref[...] = log_decay
    k_ref[...] = k_mod
    v_ref[...] = v
    kk_ref[...] = kk
    b_ref[...] = kk * a
    gate_ref[...] = gate
    bonus_ref[...] = bonus


def _mix_prep(x2d, seq, norm_g, w_in, conv_w, conv_g, mu, w0, w_up, a0, a_up, g_up, k_k, k_a,
              r_k, seg2):
    n, d = x2d.shape
    tm = PREP_ROWS
    halo = V7X_SUBLANES
    cw = conv_w.shape[1]
    rw = w0.shape[1]
    in_cols = w_in.shape[1]
    assert seq % tm == 0 and n % tm == 0
    full = lambda a: pl.BlockSpec(a.shape, lambda i: (0,) * a.ndim)
    row_spec = lambda c: pl.BlockSpec((tm, c), lambda i: (i, 0))
    consts = (norm_g, w_in, conv_w, conv_g, mu, w0, w_up, a0, a_up, g_up, k_k, k_a, r_k, seg2)
    outs = [jax.ShapeDtypeStruct((n, cw), BF16)] + [jax.ShapeDtypeStruct((n, rw), F32)] * 8
    return pl.pallas_call(
        functools.partial(_mix_prep_kernel, seq // tm, cw, rw),
        grid=(n // tm,),
        in_specs=[pl.BlockSpec((halo, d), lambda i: (jnp.maximum(i * (tm // halo) - 1, 0), 0)),
                  row_spec(d)] + [full(c) for c in consts],
        out_specs=[row_spec(cw)] + [row_spec(rw)] * 8,
        out_shape=outs,
        scratch_shapes=[pltpu.VMEM((halo + tm, in_cols), F32), pltpu.VMEM((halo + tm, cw), F32)],
        compiler_params=_params("parallel"),
        name="mix_prep",
    )(x2d, x2d, *consts)


def _bdot(a, b):
    return jnp.dot(a.astype(BF16), b.astype(BF16), preferred_element_type=F32)


def _bdot_nt(a, b):
    return lax.dot_general(a.astype(BF16), b.astype(BF16), (((1,), (1,)), ((), ())),
                           preferred_element_type=F32)


def _wkv_chunk_kernel(n_batch, n_pairs,
                      r_ref, lw_ref, k_ref, v_ref, kk_ref, b_ref, o_ref, state_ref):
    frames = r_ref.shape[1]
    lanes = V7X_LANES
    hd = HEAD_DIM
    assert frames == hd

    @pl.when(pl.program_id(0) == 0)
    def _():
        state_ref[...] = jnp.zeros_like(state_ref)

    row = lax.broadcasted_iota(I32, (lanes, lanes), 0)
    lane = lax.broadcasted_iota(I32, (lanes, lanes), 1)
    same_head = (row // hd) == (lane // hd)
    rt = row % hd
    ls = lane % hd
    strict_same = same_head & (rt > ls)
    strict_cross = jnp.logical_not(same_head) & (rt > ls)
    incl_same = same_head & (rt >= ls)
    incl_cross = jnp.logical_not(same_head) & (rt >= ls)
    eye = row == lane
    level_masks = []
    m = 1
    while m < hd:
        level_masks.append(same_head & ((rt // (2 * m)) == (ls // (2 * m)))
                           & (((rt // m) % 2) == 1) & (((ls // m) % 2) == 0))
        m *= 2
    left = lax.broadcasted_iota(I32, (frames, lanes), 1) < hd
    tri = (lax.broadcasted_iota(I32, (frames, frames), 0)
           >= lax.broadcasted_iota(I32, (frames, frames), 1)).astype(BF16)

    def diag_blocks(x):
        return jnp.concatenate([jnp.where(left, x, 0.0), jnp.where(left, 0.0, x)], axis=0)

    def cross_blocks(x):
        return jnp.concatenate([jnp.where(left, 0.0, x), jnp.where(left, x, 0.0)], axis=0)

    def batch_prep(bi):
        lw = lw_ref[bi]
        hi = lw.astype(BF16)
        rem = lw - hi.astype(F32)
        mid = rem.astype(BF16)
        lo = (rem - mid.astype(F32)).astype(BF16)
        cs = (jnp.dot(tri, hi, preferred_element_type=F32)
              + jnp.dot(tri, mid, preferred_element_type=F32)
              + jnp.dot(tri, lo, preferred_element_type=F32))
        cs_end = cs[frames - 1:frames, :]
        e_neg = jnp.exp(-cs)
        e_end = jnp.exp(cs_end - cs)
        b_in = b_ref[bi]
        k_in = k_ref[bi]
        return dict(a_t=-kk_ref[bi] * jnp.exp(cs - lw), b_t=b_in * e_neg, k_t=k_in * e_neg,
                    r_t=r_ref[bi] * jnp.exp(cs), b_h=b_in * e_end, k_h=k_in * e_end,
                    v=v_ref[bi], g_end=jnp.exp(cs_end))

    for b0 in range(0, n_batch, SCAN_BATCHES_PER_GROUP):
        group = []
        for bi in range(b0, min(b0 + SCAN_BATCHES_PER_GROUP, n_batch)):
            prep = batch_prep(bi)
            for pi in range(n_pairs):
                sl = slice(pi * lanes, (pi + 1) * lanes)
                group.append((bi, sl, bi * n_pairs + pi, {n: x[:, sl] for n, x in prep.items()}))
        a_bd = [diag_blocks(p["a_t"]) for _, _, _, p in group]
        r_bd = [diag_blocks(p["r_t"]) for _, _, _, p in group]
        d0 = [_bdot_nt(jnp.concatenate([a[:hd], r[:hd]], axis=0),
                       jnp.concatenate([p["b_t"], p["k_t"]], axis=0))
              for a, r, (_, _, _, p) in zip(a_bd, r_bd, group)]
        d1 = [_bdot_nt(jnp.concatenate([a[hd:], r[hd:]], axis=0),
                       jnp.concatenate([p["k_t"], p["b_t"]], axis=0))
              for a, r, (_, _, _, p) in zip(a_bd, r_bd, group)]
        a_rows = [jnp.concatenate([x[:hd], y[:hd]], axis=0) for x, y in zip(d0, d1)]
        m_rows = [jnp.concatenate([x[hd:], y[hd:]], axis=0) for x, y in zip(d0, d1)]
        a_ab = [jnp.where(strict_same, x, 0.0) for x in a_rows]
        a_ak = [jnp.where(strict_cross, x, 0.0) for x in a_rows]
        m_rb = [jnp.where(incl_same, x, 0.0) for x in m_rows]
        m_rk = [jnp.where(incl_cross, x, 0.0) for x in m_rows]
        t_inv = [jnp.where(eye, 1.0, jnp.where(level_masks[0], x, 0.0)) for x in a_ab]
        for mask in level_masks[1:]:
            half = [_bdot(t, jnp.where(mask, x, 0.0)) for t, x in zip(t_inv, a_ab)]
            t_inv = [t + _bdot(h, t) for t, h in zip(t_inv, half)]
        v_x = [cross_blocks(p["v"]) for _, _, _, p in group]
        akv = [_bdot(x, v) for x, v in zip(a_ak, v_x)]
        wu = [_bdot(t, jnp.concatenate([a, x], axis=1))
              for t, a, x in zip(t_inv, a_bd, akv)]
        bh_t = [diag_blocks(p["b_h"]).T for _, _, _, p in group]
        kh_t = [cross_blocks(p["k_h"]).T for _, _, _, p in group]
        pw_rw = [_bdot(jnp.concatenate([bt, m], axis=0), x[:, :lanes])
                 for bt, m, x in zip(bh_t, m_rb, wu)]
        q_o = [_bdot(jnp.concatenate([jnp.concatenate([bt, kt], axis=1),
                                      jnp.concatenate([mb, mk], axis=1)], axis=0),
                     jnp.concatenate([x[:, lanes:], v], axis=0))
               for bt, kt, mb, mk, x, v in zip(bh_t, kh_t, m_rb, m_rk, wu, v_x)]
        res = [_bdot(jnp.concatenate([r + pr[lanes:],
                                      pr[:lanes] + jnp.where(eye, p["g_end"], 0.0)], axis=0),
                     state_ref[idx])
               for r, pr, (_, _, idx, p) in zip(r_bd, pw_rw, group)]
        for x, q, (bi, sl, idx, _) in zip(res, q_o, group):
            o_bd = x[:lanes] + q[lanes:]
            state_ref[idx] = x[lanes:] + q[:lanes]
            o_ref[bi, :, sl] = jnp.where(left, o_bd[:hd], o_bd[hd:])


def _wkv_chunk(r, lw, k, v, kk, b):
    n_batch, seq, rw = r.shape
    n_pairs = rw // V7X_LANES
    frames = SCAN_FRAMES
    assert seq % frames == 0
    spec = pl.BlockSpec((n_batch, frames, rw), lambda c: (0, c, 0))
    return pl.pallas_call(
        functools.partial(_wkv_chunk_kernel, n_batch, n_pairs),
        grid=(seq // frames,),
        in_specs=[spec] * 6,
        out_specs=spec,
        out_shape=jax.ShapeDtypeStruct((n_batch, seq, rw), F32),
        scratch_shapes=[pltpu.VMEM((n_batch * n_pairs, V7X_LANES, V7X_LANES), F32)],
        compiler_params=_params("arbitrary"),
        name="wkv_chunk",
    )(r, lw, k, v, kk, b)


def _post_mix_kernel(n_experts,
                     o_ref, bonus_ref, gate_ref, yconv_ref, x_ref, gn_w_ref, gn_b_ref,
                     w_out_c_ref, w_out_r_ref, ffn_g_ref, wr_hi_ref, wr_lo_ref, b_router_ref,
                     seg2_ref,
                     x1_ref, *outs):
    h2_refs, (top_e_ref, gates_ref) = outs[:-2], outs[-2:]
    seg2 = seg2_ref[...]
    o = o_ref[...]
    mean = _seg_sum(o, seg2) * (1.0 / HEAD_DIM)
    cen = o - mean
    var = _seg_sum(cen * cen, seg2) * (1.0 / HEAD_DIM)
    o = cen * lax.rsqrt(var + GN_EPS) * gn_w_ref[...] + gn_b_ref[...]
    y_rwkv = (o + bonus_ref[...]) * gate_ref[...]
    x1 = (x_ref[...]
          + jnp.dot(yconv_ref[...], w_out_c_ref[...], preferred_element_type=F32)
          + jnp.dot(y_rwkv.astype(BF16), w_out_r_ref[...], preferred_element_type=F32))
    x1_ref[...] = x1
    h2 = _rms_rows(x1, ffn_g_ref[...])
    for c, h2_ref in enumerate(h2_refs):
        h2_ref[...] = h2[:, c * ROW_CHUNK:(c + 1) * ROW_CHUNK]

    h_hi, h_lo = _split_bf16(h2)
    nt = (((1,), (1,)), ((), ()))
    logits = (lax.dot_general(wr_hi_ref[...], h_hi, nt, preferred_element_type=F32)
              + lax.dot_general(wr_hi_ref[...], h_lo, nt, preferred_element_type=F32)
              + lax.dot_general(wr_lo_ref[...], h_hi, nt, preferred_element_type=F32)
              + b_router_ref[...])
    e_id = lax.broadcasted_iota(I32, logits.shape, 0).astype(F32)
    work = logits
    tops, ids = [], []
    for _ in range(TOP_K):
        m = jnp.max(work, axis=0, keepdims=True)
        sel = jnp.min(jnp.where(work == m, e_id, float(n_experts)), axis=0, keepdims=True)
        tops.append(m)
        ids.append(sel)
        work = jnp.where(e_id == sel, -jnp.inf, work)
    ex = [jnp.exp(t - tops[0]) for t in tops]
    denom = ex[0] + ex[1] + ex[2] + ex[3]
    top_e_ref[...] = jnp.concatenate(ids, axis=0).astype(I32)
    gates_ref[...] = jnp.concatenate([e / denom for e in ex], axis=0)


def _post_mix(o, bonus, gate, yconv, x2d, gn_w, gn_b, w_out_c, w_out_r, ffn_g, wr_hi, wr_lo,
              b_router, seg2):
    n, d = x2d.shape
    rw = o.shape[1]
    cw = yconv.shape[1]
    n_experts = wr_hi.shape[0]
    tm = POST_ROWS
    full = lambda a: pl.BlockSpec(a.shape, lambda i: (0,) * a.ndim)
    row_spec = lambda c: pl.BlockSpec((tm, c), lambda i: (i, 0))
    col_spec = pl.BlockSpec((TOP_K, tm), lambda i: (0, i))
    consts = (gn_w, gn_b, w_out_c, w_out_r, ffn_g, wr_hi, wr_lo, b_router, seg2)
    n_chunks = d // ROW_CHUNK
    outs = pl.pallas_call(
        functools.partial(_post_mix_kernel, n_experts),
        grid=(n // tm,),
        in_specs=[row_spec(rw), row_spec(rw), row_spec(rw), row_spec(cw), row_spec(d)]
                 + [full(c) for c in consts],
        out_specs=[row_spec(d)] + [row_spec(ROW_CHUNK)] * n_chunks + [col_spec, col_spec],
        out_shape=[jax.ShapeDtypeStruct((n, d), F32)]
                  + [jax.ShapeDtypeStruct((n, ROW_CHUNK), F32)] * n_chunks
                  + [jax.ShapeDtypeStruct((TOP_K, n), I32), jax.ShapeDtypeStruct((TOP_K, n), F32)],
        compiler_params=_params("parallel"),
        name="post_mix",
    )(o, bonus, gate, yconv, x2d, *consts)
    return outs[0], outs[1:1 + n_chunks], outs[-2], outs[-1]


def _route_kernel(n_experts, block_rows,
                  top_e_ref, tri_ref, dest_ref, meta_ref,
                  count_ref, start_ref, carry_ref):
    phase = pl.program_id(0)
    j = pl.program_id(1)
    tb = top_e_ref.shape[1]
    e_id = lax.broadcasted_iota(I32, (n_experts, tb), 0)
    top_e = top_e_ref[...]
    onehot = jnp.zeros((n_experts, tb), F32)
    for c in range(TOP_K):
        onehot = onehot + jnp.where(top_e[c:c + 1, :] == e_id, 1.0, 0.0)
    block_count = jnp.sum(onehot, axis=1, keepdims=True)

    @pl.when((phase == 0) & (j == 0))
    def _():
        count_ref[...] = jnp.zeros_like(count_ref)

    @pl.when(phase == 0)
    def _():
        count_ref[...] += block_count

    @pl.when((phase == 1) & (j == 0))
    def _():
        counts = count_ref[...]
        padded = jnp.ceil(counts * (1.0 / block_rows)) * block_rows
        sub = lax.broadcasted_iota(I32, (n_experts, n_experts), 0)
        lane = lax.broadcasted_iota(I32, (n_experts, n_experts), 1)
        padded_row = jnp.sum(jnp.where(sub == lane, padded, 0.0), axis=0, keepdims=True)
        start = jnp.sum(jnp.where(lane < sub, padded_row, 0.0), axis=1, keepdims=True)
        start_ref[...] = start
        carry_ref[...] = jnp.zeros_like(carry_ref)
        end = start + padded
        nb = meta_ref.shape[1]
        slot0 = (lax.broadcasted_iota(I32, (n_experts, nb), 1) * block_rows).astype(F32)
        block_e = jnp.sum(jnp.where(end <= slot0, 1.0, 0.0), axis=0, keepdims=True)
        block_e = jnp.minimum(block_e, n_experts - 1.0)
        used = jnp.max(end, axis=0, keepdims=True) * (1.0 / block_rows)
        e_sub = lax.broadcasted_iota(I32, (n_experts, nb), 0)
        e_lane = lax.broadcasted_iota(I32, (n_experts, nb), 1)
        end_row = jnp.sum(jnp.where(e_sub == e_lane, end, 0.0), axis=0, keepdims=True)
        last_block = jnp.maximum(end_row * (1.0 / block_rows) - 1.0, 0.0)
        in_region = (start <= slot0) & (slot0 < end)
        valid = jnp.sum(jnp.where(in_region, jnp.minimum(start + counts - slot0, block_rows), 0.0),
                        axis=0, keepdims=True)
        row = lax.broadcasted_iota(I32, meta_ref.shape, 0)
        meta_ref[...] = jnp.where(
            row == META_BLOCK_EXPERT, block_e,
            jnp.where(row == META_LAST_BLOCK, last_block,
                      jnp.where(row == META_VALID_ROWS, valid, used))).astype(I32)

    @pl.when(phase == 1)
    def _():
        incl = jnp.dot(onehot.astype(BF16), tri_ref[...], preferred_element_type=F32)
        base = incl - onehot + carry_ref[...] + start_ref[...]
        rows = [jnp.sum(jnp.where(top_e[c:c + 1, :] == e_id, base, 0.0), axis=0, keepdims=True)
                for c in range(TOP_K)]
        dest_ref[...] = jnp.concatenate(rows, axis=0).astype(I32)
        carry_ref[...] += block_count


def _route(top_e, n_experts, n_blocks_padded):
    n = top_e.shape[1]
    tb = ROUTE_TOKENS
    assert n % tb == 0
    tri = (lax.broadcasted_iota(I32, (tb, tb), 0) <= lax.broadcasted_iota(I32, (tb, tb), 1)
           ).astype(BF16)
    return pl.pallas_call(
        functools.partial(_route_kernel, n_experts, EXPERT_ROWS),
        grid=(2, n // tb),
        in_specs=[pl.BlockSpec((TOP_K, tb), lambda ph, j: (0, j)),
                  pl.BlockSpec((tb, tb), lambda ph, j: (0, 0))],
        out_specs=[pl.BlockSpec((TOP_K, tb), lambda ph, j: (0, j * ph)),
                   pl.BlockSpec((V7X_SUBLANES, n_blocks_padded), lambda ph, j: (0, 0))],
        out_shape=[jax.ShapeDtypeStruct((TOP_K, n), I32),
                   jax.ShapeDtypeStruct((V7X_SUBLANES, n_blocks_padded), I32)],
        scratch_shapes=[pltpu.VMEM((n_experts, 1), F32)] * 3,
        compiler_params=_params("arbitrary", "arbitrary"),
        name="route",
    )(top_e, tri)


def _dispatch_kernel(n_experts, block_rows,
                     meta_ref, dest_ref, h_ref, slots_ref, zero_ref, sem, zero_sem):
    sub = h_ref.shape[1]
    tb = h_ref.shape[0] * sub

    @pl.when(pl.program_id(0) == 0)
    def _():
        zero_ref[...] = jnp.zeros_like(zero_ref)

        def clear(block):
            row0 = pl.multiple_of(block * block_rows, block_rows)
            return pltpu.make_async_copy(zero_ref, slots_ref.at[pl.ds(row0, block_rows)], zero_sem)

        def start_clear(block, carry):
            clear(block).start()
            return carry

        def wait_clear(block, carry):
            clear(block).wait()
            return carry

        for e in range(n_experts):
            clear(meta_ref[META_LAST_BLOCK, e]).start()
        used = meta_ref[META_USED_BLOCKS, 0]
        n_blocks = slots_ref.shape[0] // block_rows
        lax.fori_loop(used, n_blocks, start_clear, 0)
        for e in range(n_experts):
            clear(meta_ref[META_LAST_BLOCK, e]).wait()
        lax.fori_loop(used, n_blocks, wait_clear, 0)

    def issue(g, carry):
        for j in range(TOKENS_PER_DEST_ROW):
            src = h_ref.at[g * (TOKENS_PER_DEST_ROW // sub) + j // sub, pl.ds(j % sub, 1)]
            for c in range(TOP_K):
                slot = dest_ref[0, g, c * TOKENS_PER_DEST_ROW + j]
                pltpu.make_async_copy(src, slots_ref.at[pl.ds(slot, 1)], sem).start(priority=c % 2)
        return carry

    lax.fori_loop(0, tb // TOKENS_PER_DEST_ROW, issue, 0)
    for c in range(TOP_K):
        pltpu.make_async_copy(slots_ref.at[pl.ds(0, tb)], slots_ref.at[pl.ds(0, tb)], sem).wait()


def _dispatch(meta, dest_blocks, h2, n_slots, n_experts):
    n, d = h2.shape
    tb = DISPATCH_TOKENS
    sub = V7X_SUBLANES
    dest_blocks = dest_blocks.reshape(n // tb, tb // TOKENS_PER_DEST_ROW, V7X_LANES)
    grid_spec = pltpu.PrefetchScalarGridSpec(
        num_scalar_prefetch=1,
        grid=(n // tb,),
        in_specs=[pl.BlockSpec((1,) + dest_blocks.shape[1:], lambda i, m: (i, 0, 0),
                               memory_space=pltpu.SMEM),
                  pl.BlockSpec((tb // sub, sub, d), lambda i, m: (i, 0, 0))],
        out_specs=pl.BlockSpec(memory_space=pl.ANY),
        scratch_shapes=[pltpu.VMEM((EXPERT_ROWS, d), h2.dtype),
                        pltpu.SemaphoreType.DMA(()), pltpu.SemaphoreType.DMA(())],
    )
    return pl.pallas_call(
        functools.partial(_dispatch_kernel, n_experts, EXPERT_ROWS),
        grid_spec=grid_spec,
        out_shape=jax.ShapeDtypeStruct((n_slots, d), h2.dtype),
        compiler_params=_params("arbitrary"),
        name="dispatch",
    )(meta, dest_blocks, h2.reshape(n // sub, sub, d))


def _experts_kernel(d_ff, n_chunks, meta_ref, *refs):
    x_refs, refs = refs[:n_chunks], refs[n_chunks:]
    w_gu_ref, b_gu_ref, w_down_ref, b_down_ref = refs[:4]
    y_refs = refs[4:4 + n_chunks]
    w_gu_bf16, w_down_bf16 = refs[4 + n_chunks:]
    j = pl.program_id(0)
    used = meta_ref[META_USED_BLOCKS, 0]
    expert = meta_ref[META_BLOCK_EXPERT, j]
    prev_expert = meta_ref[META_BLOCK_EXPERT, jnp.maximum(j - 1, 0)]

    @pl.when((j < used) & ((j == 0) | (expert != prev_expert)))
    def _():
        w_gu_bf16[...] = w_gu_ref[0].astype(BF16)
        w_down_bf16[...] = w_down_ref[0].astype(BF16)

    @pl.when(j < used)
    def _():
        x = jnp.concatenate([r[...] for r in x_refs], axis=1)
        row = lax.broadcasted_iota(I32, x.shape, 0)
        x = jnp.where(row < meta_ref[META_VALID_ROWS, j], x, 0.0)
        gu = jnp.dot(x.astype(BF16), w_gu_bf16[...], preferred_element_type=F32) + b_gu_ref[0]
        gate = jnp.minimum(gu[:, :d_ff], SWIGLU_LIMIT)
        up = jnp.clip(gu[:, d_ff:], -SWIGLU_LIMIT, SWIGLU_LIMIT)
        act = (up + 1.0) * (gate * _sigmoid(SWIGLU_ALPHA * gate))
        y = jnp.dot(act.astype(BF16), w_down_bf16[...], preferred_element_type=F32) + b_down_ref[0]
        for c, y_ref in enumerate(y_refs):
            y_ref[...] = y[:, c * ROW_CHUNK:(c + 1) * ROW_CHUNK]

    @pl.when(j >= used)
    def _():
        for y_ref in y_refs:
            y_ref[...] = jnp.zeros_like(y_ref)


def _experts(meta, slot_chunks, w_gu, b_gu, w_down, b_down):
    n_chunks = len(slot_chunks)
    n_slots = slot_chunks[0].shape[0]
    n_experts, d, two_ff = w_gu.shape
    d_ff = two_ff // 2
    bm = EXPERT_ROWS
    grid_spec = pltpu.PrefetchScalarGridSpec(
        num_scalar_prefetch=1,
        grid=(n_slots // bm,),
        in_specs=[pl.BlockSpec((bm, ROW_CHUNK),
                               lambda j, m: (jnp.minimum(j, m[META_USED_BLOCKS, 0] - 1), 0))
                  ] * n_chunks
                 + [pl.BlockSpec((1, d, two_ff), lambda j, m: (m[META_BLOCK_EXPERT, j], 0, 0)),
                    pl.BlockSpec((1, 1, two_ff), lambda j, m: (m[META_BLOCK_EXPERT, j], 0, 0)),
                    pl.BlockSpec((1, d_ff, d), lambda j, m: (m[META_BLOCK_EXPERT, j], 0, 0)),
                    pl.BlockSpec((1, 1, d), lambda j, m: (m[META_BLOCK_EXPERT, j], 0, 0))],
        out_specs=[pl.BlockSpec((bm, ROW_CHUNK), lambda j, m: (j, 0))] * n_chunks,
        scratch_shapes=[pltpu.VMEM((d, two_ff), BF16), pltpu.VMEM((d_ff, d), BF16)],
    )
    return pl.pallas_call(
        functools.partial(_experts_kernel, d_ff, n_chunks),
        grid_spec=grid_spec,
        out_shape=[jax.ShapeDtypeStruct((n_slots, ROW_CHUNK), F32)] * n_chunks,
        compiler_params=_params("arbitrary"),
        name="experts",
    )(meta, *slot_chunks, w_gu, b_gu, w_down, b_down)


def _combine_kernel(dest_ref, y_ref, x1_ref, gates_ref, g_ref, out_ref, buf_even, buf_odd, sems):
    tb, d = x1_ref.shape
    sub = buf_even.shape[2]
    rows = TOKENS_PER_DEST_ROW
    n_trips = tb // rows
    step = pl.program_id(0)
    last = pl.num_programs(0) - 1

    @pl.when(step == 0)
    def _():
        buf_odd[...] = jnp.zeros_like(buf_odd)

    def whole_buffer_wait(sem):
        for c in range(TOP_K):
            pltpu.make_async_copy(y_ref.at[pl.ds(0, tb)], y_ref.at[pl.ds(0, tb)], sem).wait()

    def run(buf_w, sem_w, buf_r, sem_r):
        def issue(g):
            for j in range(rows):
                group = g * (rows // sub) + j // sub
                for c in range(TOP_K):
                    slot = dest_ref[0, g, c * rows + j]
                    pltpu.make_async_copy(y_ref.at[pl.ds(slot, 1)],
                                          buf_w.at[c, group, pl.ds(j % sub, 1)], sem_w
                                          ).start(priority=c % 2)

        def reduce(g):
            r0 = pl.multiple_of(g * rows, rows)
            gates = gates_ref[pl.ds(r0, rows), :]
            acc = x1_ref[pl.ds(r0, rows), :]
            for c in range(TOP_K):
                acc = acc + (buf_r[c, pl.ds(g * (rows // sub), rows // sub)].reshape(rows, d)
                             * gates[:, c:c + 1])
            out_ref[pl.ds(r0, rows), :] = _rms_rows(acc, g_ref[...])

        issue(0)

        @pl.when(step > 0)
        def _():
            whole_buffer_wait(sem_r)

        def trip(g, carry):
            issue(g)
            reduce(g - 1)
            return carry

        lax.fori_loop(1, n_trips, trip, 0)
        reduce(n_trips - 1)

        @pl.when(step == last)
        def _():
            whole_buffer_wait(sem_w)

    @pl.when(step % 2 == 0)
    def _():
        run(buf_even, sems.at[0], buf_odd, sems.at[1])

    @pl.when(step % 2 == 1)
    def _():
        run(buf_odd, sems.at[1], buf_even, sems.at[0])


def _combine(dest_blocks, y_slots, x1, gates_t, final_g):
    n, d = x1.shape
    tb = MOVE_TOKENS
    n_blocks = n // tb
    gather_block = lambda i: jnp.minimum(i, n_blocks - 1)
    reduce_block = lambda i: jnp.maximum(i - 1, 0)
    landing = pltpu.VMEM((TOP_K, tb // V7X_SUBLANES, V7X_SUBLANES, d), F32)
    return pl.pallas_call(
        _combine_kernel,
        grid=(n_blocks + 1,),
        in_specs=[pl.BlockSpec((1,) + dest_blocks.shape[1:], lambda i: (gather_block(i), 0, 0),
                               memory_space=pltpu.SMEM),
                  pl.BlockSpec(memory_space=pl.ANY),
                  pl.BlockSpec((tb, d), lambda i: (reduce_block(i), 0)),
                  pl.BlockSpec((tb, TOP_K), lambda i: (reduce_block(i), 0)),
                  pl.BlockSpec((1, d), lambda i: (0, 0))],
        out_specs=pl.BlockSpec((tb, d), lambda i: (reduce_block(i), 0)),
        out_shape=jax.ShapeDtypeStruct((n, d), F32),
        scratch_shapes=[landing, landing, pltpu.SemaphoreType.DMA((2,))],
        compiler_params=_params("arbitrary"),
        name="combine",
    )(dest_blocks, y_slots, x1, gates_t, final_g)


def _sc_mesh():
    return plsc.VectorSubcoreMesh(core_axis_name="core", subcore_axis_name="subcore")


def _sc_scatter_rows(src, dest, n_slots):
    n, width = src.shape
    assert n % SC_WINDOW == 0

    @functools.partial(pl.kernel, out_type=jax.ShapeDtypeStruct((n_slots, width), src.dtype),
                       mesh=_sc_mesh(), name="sc_scatter_rows")
    def scatter(src_hbm, dest_hbm, out_hbm):
        def body(src_vmem, dest_vmem):
            for c in range(TOP_K):
                pltpu.sync_copy(src_vmem, out_hbm.at[dest_vmem.at[c]])

        pltpu.emit_pipeline(
            body,
            grid=(n // SC_WINDOW,),
            in_specs=[pl.BlockSpec((SC_WINDOW, width), lambda i: (i, 0)),
                      pl.BlockSpec((TOP_K, SC_WINDOW), lambda i: (0, i))],
            out_specs=[],
            core_axis_name=("core", "subcore"),
            dimension_semantics=(pltpu.PARALLEL,),
        )(src_hbm, dest_hbm)

    return scatter(src, dest)


def _sc_gather_rows(table, idx_row):
    n_idx = idx_row.shape[1]
    width = table.shape[1]
    assert n_idx % SC_WINDOW == 0

    @functools.partial(pl.kernel, out_type=jax.ShapeDtypeStruct((n_idx, width), table.dtype),
                       mesh=_sc_mesh(), name="sc_gather_rows")
    def gather(table_hbm, idx_hbm, out_hbm):
        def body(idx_vmem, out_vmem):
            pltpu.sync_copy(table_hbm.at[idx_vmem.at[0]], out_vmem)

        pltpu.emit_pipeline(
            body,
            grid=(n_idx // SC_WINDOW,),
            in_specs=[pl.BlockSpec((1, SC_WINDOW), lambda i: (0, i))],
            out_specs=[pl.BlockSpec((SC_WINDOW, width), lambda i: (i, 0))],
            core_axis_name=("core", "subcore"),
            dimension_semantics=(pltpu.PARALLEL,),
        )(idx_hbm, out_hbm)

    return gather(table, idx_row)


def _reduce_kernel(n_chunks, *refs):
    y_refs = refs[:n_chunks]
    x1_ref, gates_ref, g_ref, out_ref = refs[n_chunks:]
    gates = gates_ref[...]
    acc = x1_ref[...]
    for c in range(TOP_K):
        acc = acc + jnp.concatenate([r[c] for r in y_refs], axis=1) * gates[:, c:c + 1]
    out_ref[...] = _rms_rows(acc, g_ref[...])


def _reduce(y_chunks, x1, gates_t, final_g):
    n, d = x1.shape
    tb = MOVE_TOKENS
    n_chunks = len(y_chunks)
    return pl.pallas_call(
        functools.partial(_reduce_kernel, n_chunks),
        grid=(n // tb,),
        in_specs=[pl.BlockSpec((TOP_K, tb, ROW_CHUNK), lambda i: (0, i, 0))] * n_chunks
                 + [pl.BlockSpec((tb, d), lambda i: (i, 0)),
                    pl.BlockSpec((tb, TOP_K), lambda i: (i, 0)),
                    pl.BlockSpec((1, d), lambda i: (0, 0))],
        out_specs=pl.BlockSpec((tb, d), lambda i: (i, 0)),
        out_shape=jax.ShapeDtypeStruct((n, d), F32),
        compiler_params=_params("parallel"),
        name="reduce",
    )(*[y.reshape(TOP_K, n, ROW_CHUNK) for y in y_chunks], x1, gates_t, final_g)


def _group_matrix(width):
    a = lax.broadcasted_iota(I32, (width, width), 0) // HEAD_DIM
    b = lax.broadcasted_iota(I32, (width, width), 1) // HEAD_DIM
    g = (a == b).astype(BF16)
    return jnp.concatenate([g, g], axis=0)


def _row(vec):
    return vec.reshape(1, -1).astype(F32)


def kernel(x, w_in, conv_w, conv_norm_g, rwkv_mu, w0, w_up, a0, a_up, g_up, k_k, k_a, r_k,
           gn_w, gn_b, w_out, norm_mix_g, norm_ffn_g, w_router, b_router, w_gu, b_gu, w_down,
           b_down, norm_final_g):
    n_batch, seq, d = x.shape
    n = n_batch * seq
    depth = w_in.shape[0]
    cw = conv_w.shape[2]
    rw = w0.shape[1]
    n_experts = w_router.shape[2]
    decay_lora = w_up.shape[1]
    aaa_lora = a_up.shape[1]
    assert decay_lora + aaa_lora == V7X_LANES
    assert depth == 1
    n_slots = n * TOP_K + n_experts * EXPERT_ROWS
    n_blocks = n_slots // EXPERT_ROWS
    n_blocks_padded = -(-n_blocks // V7X_LANES) * V7X_LANES
    seg2 = _group_matrix(rw)

    x2d = x.reshape(n, d)
    for l in range(depth):
        w_up_pad = jnp.concatenate([w_up[l], jnp.zeros((aaa_lora, rw), F32)], axis=0).astype(BF16)
        a_up_pad = jnp.concatenate([jnp.zeros((decay_lora, rw), F32), a_up[l]], axis=0).astype(BF16)
        (yconv, r, lw, k, v, kk, b, gate, bonus) = _mix_prep(
            x2d, seq, _row(norm_mix_g[l]), w_in[l].astype(BF16), conv_w[l].astype(F32),
            _row(conv_norm_g[l]), _row(rwkv_mu[l]), _row(w0[l]), w_up_pad, _row(a0[l]), a_up_pad,
            g_up[l].astype(BF16), _row(k_k[l]), _row(k_a[l]), _row(r_k[l]), seg2)
        shape3 = (n_batch, seq, rw)
        o = _wkv_chunk(r.reshape(shape3), lw.reshape(shape3), k.reshape(shape3),
                       v.reshape(shape3), kk.reshape(shape3), b.reshape(shape3)).reshape(n, rw)
        w_out_b = w_out[l].astype(BF16)
        wr_t = w_router[l].T.astype(F32)
        wr_hi = wr_t.astype(BF16)
        wr_lo = (wr_t - wr_hi.astype(F32)).astype(BF16)
        x1, h2_chunks, top_e, gates = _post_mix(
            o, bonus, gate, yconv, x2d, _row(gn_w[l]), _row(gn_b[l]), w_out_b[:cw], w_out_b[cw:],
            _row(norm_ffn_g[l]), wr_hi, wr_lo, b_router[l].reshape(n_experts, 1).astype(F32), seg2)
        dest, meta = _route(top_e, n_experts, n_blocks_padded)
        slot_chunks = [_sc_scatter_rows(h, dest, n_slots) for h in h2_chunks]
        y_chunks = _experts(meta, slot_chunks, w_gu[l].astype(F32),
                            b_gu[l].reshape(n_experts, 1, -1).astype(F32),
                            w_down[l].astype(F32),
                            b_down[l].reshape(n_experts, 1, -1).astype(F32))
        idx_row = dest.reshape(1, TOP_K * n)
        x2d = _reduce([_sc_gather_rows(y, idx_row) for y in y_chunks], x1, gates.T,
                      _row(norm_final_g))
    return x2d.reshape(n_batch, seq, d)
```

```python
import functools

import jax
import jax.numpy as jnp
from jax import lax
from jax.experimental import pallas as pl
from jax.experimental.pallas import tpu as pltpu
from jax.experimental.pallas import tpu_sc as plsc

F32 = jnp.float32
BF16 = jnp.bfloat16
I32 = jnp.int32

HEAD_DIM = 64
TOP_K = 4
NORM_EPS = 1e-5
GN_EPS = HEAD_DIM * 1e-5
SWIGLU_LIMIT = 7.0
SWIGLU_ALPHA = 1.702

V7X_LANES = 128
V7X_SUBLANES = 8
V7X_VMEM_LIMIT_BYTES = 56 * 1024 * 1024

PREP_ROWS = 256
POST_ROWS = 512
SCAN_FRAMES = 64
SCAN_BATCHES_PER_GROUP = 4
ROUTE_TOKENS = 1024
REDUCE_TOKENS = 256
MOE_PARTS = 2
SC_WINDOW = 128
ROW_CHUNK = 256
META_BLOCK_EXPERT = 0
META_USED_BLOCKS = 1
META_VALID_ROWS = 2
EXPERT_ROWS = 512

def _params(*semantics):
    return pltpu.CompilerParams(dimension_semantics=semantics,
                                vmem_limit_bytes=V7X_VMEM_LIMIT_BYTES)


def _split_bf16(x):
    hi = x.astype(BF16)
    lo = (x - hi.astype(F32)).astype(BF16)
    return hi, lo


def _seg_sum(x, seg2):
    hi, lo = _split_bf16(x)
    return jnp.dot(jnp.concatenate([hi, lo], axis=1), seg2, preferred_element_type=F32)


def _rms_rows(x, g):
    return x * lax.rsqrt(jnp.mean(x * x, axis=-1, keepdims=True) + NORM_EPS) * g


def _sigmoid(x):
    return 1.0 / (1.0 + jnp.exp(-x))


def _mix_prep_kernel(blocks_per_seq, cw, rw,
                     xprev_ref, x_ref, g_ref, w_in_ref, conv_w_ref, conv_g_ref, mu_ref,
                     w0_ref, w_up_ref, a0_ref, a_up_ref, g_up_ref, k_k_ref, k_a_ref, r_k_ref,
                     seg2_ref,
                     yconv_ref, r_ref, w_ref, k_ref, v_ref, kk_ref, b_ref, gate_ref, bonus_ref,
                     p_scr, z_scr):
    tm = x_ref.shape[0]
    halo = xprev_ref.shape[0]
    first = (pl.program_id(0) % blocks_per_seq) == 0
    xp = xprev_ref[...] * jnp.where(first, 0.0, 1.0)
    xa = jnp.concatenate([xp, x_ref[...]], axis=0)
    h = _rms_rows(xa, g_ref[...])
    p_scr[...] = jnp.dot(h.astype(BF16), w_in_ref[...], preferred_element_type=F32)
    seg2 = seg2_ref[...]

    z_scr[...] = p_scr[:, 2 * cw:3 * cw] * p_scr[:, 0:cw]
    conv = (conv_w_ref[0:1, :] * z_scr[halo - 2:halo - 2 + tm, :]
            + conv_w_ref[1:2, :] * z_scr[halo - 1:halo - 1 + tm, :]
            + conv_w_ref[2:3, :] * z_scr[halo:halo + tm, :])
    y = p_scr[halo:halo + tm, cw:2 * cw] * conv
    ms = _seg_sum(y * y, seg2) * (1.0 / HEAD_DIM)
    yconv_ref[...] = (y * lax.rsqrt(ms + NORM_EPS) * conv_g_ref[...]).astype(yconv_ref.dtype)

    c0 = 3 * cw
    cur = p_scr[halo:halo + tm, c0:]
    prev = p_scr[halo - 1:halo - 1 + tm, c0:]
    q = cur + (prev - cur) * mu_ref[...]
    r = q[:, 0:rw]
    k = q[:, rw:2 * rw]
    v = q[:, 2 * rw:3 * rw]
    lora_wa = q[:, 3 * rw:3 * rw + V7X_LANES]
    lora_g = q[:, 3 * rw + V7X_LANES:]
    w_lin = w0_ref[...] + jnp.dot(jnp.tanh(lora_wa).astype(BF16), w_up_ref[...],
                                  preferred_element_type=F32)
    neg = -w_lin
    softplus = jnp.maximum(neg, 0.0) + jnp.log(1.0 + jnp.exp(-jnp.abs(neg)))
    log_decay = -jnp.exp(-softplus - 0.5)
    a = _sigmoid(a0_ref[...] + jnp.dot(lora_wa.astype(BF16), a_up_ref[...],
                                       preferred_element_type=F32))
    gate = jnp.dot(_sigmoid(lora_g).astype(BF16), g_up_ref[...], preferred_element_type=F32)
    kk = k * k_k_ref[...]
    kk = kk / jnp.maximum(jnp.sqrt(_seg_sum(kk * kk, seg2)), 1e-12)
    k_mod = k * (1.0 + (a - 1.0) * k_a_ref[...])
    bonus = _seg_sum(r * k_mod * r_k_ref[...], seg2) * v
    r_ref[...] = r
    w_ref[...] = log_decay
    k_ref[...] = k_mod
    v_ref[...] = v
    kk_ref[...] = kk
    b_ref[...] = kk * a
    gate_ref[...] = gate
    bonus_ref[...] = bonus


def _mix_prep(x2d, seq, norm_g, w_in, conv_w, conv_g, mu, w0, w_up, a0, a_up, g_up, k_k, k_a,
              r_k, seg2):
    n, d = x2d.shape
    tm = PREP_ROWS
    halo = V7X_SUBLANES
    cw = conv_w.shape[1]
    rw = w0.shape[1]
    in_cols = w_in.shape[1]
    assert seq % tm == 0 and n % tm == 0
    full = lambda a: pl.BlockSpec(a.shape, lambda i: (0,) * a.ndim)
    row_spec = lambda c: pl.BlockSpec((tm, c), lambda i: (i, 0))
    consts = (norm_g, w_in, conv_w, conv_g, mu, w0, w_up, a0, a_up, g_up, k_k, k_a, r_k, seg2)
    outs = [jax.ShapeDtypeStruct((n, cw), BF16)] + [jax.ShapeDtypeStruct((n, rw), F32)] * 8
    return pl.pallas_call(
        functools.partial(_mix_prep_kernel, seq // tm, cw, rw),
        grid=(n // tm,),
        in_specs=[pl.BlockSpec((halo, d), lambda i: (jnp.maximum(i * (tm // halo) - 1, 0), 0)),
                  row_spec(d)] + [full(c) for c in consts],
        out_specs=[row_spec(cw)] + [row_spec(rw)] * 8,
        out_shape=outs,
        scratch_shapes=[pltpu.VMEM((halo + tm, in_cols), F32), pltpu.VMEM((halo + tm, cw), F32)],
        compiler_params=_params("parallel"),
        name="mix_prep",
    )(x2d, x2d, *consts)


def _bdot(a, b):
    return jnp.dot(a.astype(BF16), b.astype(BF16), preferred_element_type=F32)


def _bdot_nt(a, b):
    return lax.dot_general(a.astype(BF16), b.astype(BF16), (((1,), (1,)), ((), ())),
                           preferred_element_type=F32)


def _wkv_chunk_kernel(n_batch, n_pairs,
                      r_ref, lw_ref, k_ref, v_ref, kk_ref, b_ref, o_ref, state_ref):
    frames = r_ref.shape[1]
    lanes = V7X_LANES
    hd = HEAD_DIM
    assert frames == hd

    @pl.when(pl.program_id(0) == 0)
    def _():
        state_ref[...] = jnp.zeros_like(state_ref)

    row = lax.broadcasted_iota(I32, (lanes, lanes), 0)
    lane = lax.broadcasted_iota(I32, (lanes, lanes), 1)
    same_head = (row // hd) == (lane // hd)
    rt = row % hd
    ls = lane % hd
    strict_same = same_head & (rt > ls)
    strict_cross = jnp.logical_not(same_head) & (rt > ls)
    incl_same = same_head & (rt >= ls)
    incl_cross = jnp.logical_not(same_head) & (rt >= ls)
    eye = row == lane
    level_masks = []
    m = 1
    while m < hd:
        level_masks.append(same_head & ((rt // (2 * m)) == (ls // (2 * m)))
                           & (((rt // m) % 2) == 1) & (((ls // m) % 2) == 0))
        m *= 2
    left = lax.broadcasted_iota(I32, (frames, lanes), 1) < hd
    tri = (lax.broadcasted_iota(I32, (frames, frames), 0)
           >= lax.broadcasted_iota(I32, (frames, frames), 1)).astype(BF16)

    def diag_blocks(x):
        return jnp.concatenate([jnp.where(left, x, 0.0), jnp.where(left, 0.0, x)], axis=0)

    def cross_blocks(x):
        return jnp.concatenate([jnp.where(left, 0.0, x), jnp.where(left, x, 0.0)], axis=0)

    def batch_prep(bi):
        lw = lw_ref[bi]
        hi = lw.astype(BF16)
        rem = lw - hi.astype(F32)
        mid = rem.astype(BF16)
        lo = (rem - mid.astype(F32)).astype(BF16)
        cs = (jnp.dot(tri, hi, preferred_element_type=F32)
              + jnp.dot(tri, mid, preferred_element_type=F32)
              + jnp.dot(tri, lo, preferred_element_type=F32))
        cs_end = cs[frames - 1:frames, :]
        e_neg = jnp.exp(-cs)
        e_end = jnp.exp(cs_end - cs)
        b_in = b_ref[bi]
        k_in = k_ref[bi]
        return dict(a_t=-kk_ref[bi] * jnp.exp(cs - lw), b_t=b_in * e_neg, k_t=k_in * e_neg,
                    r_t=r_ref[bi] * jnp.exp(cs), b_h=b_in * e_end, k_h=k_in * e_end,
                    v=v_ref[bi], g_end=jnp.exp(cs_end))

    for b0 in range(0, n_batch, SCAN_BATCHES_PER_GROUP):
        group = []
        for bi in range(b0, min(b0 + SCAN_BATCHES_PER_GROUP, n_batch)):
            prep = batch_prep(bi)
            for pi in range(n_pairs):
                sl = slice(pi * lanes, (pi + 1) * lanes)
                group.append((bi, sl, bi * n_pairs + pi, {n: x[:, sl] for n, x in prep.items()}))
        a_bd = [diag_blocks(p["a_t"]) for _, _, _, p in group]
        r_bd = [diag_blocks(p["r_t"]) for _, _, _, p in group]
        d0 = [_bdot_nt(jnp.concatenate([a[:hd], r[:hd]], axis=0),
                       jnp.concatenate([p["b_t"], p["k_t"]], axis=0))
              for a, r, (_, _, _, p) in zip(a_bd, r_bd, group)]
        d1 = [_bdot_nt(jnp.concatenate([a[hd:], r[hd:]], axis=0),
                       jnp.concatenate([p["k_t"], p["b_t"]], axis=0))
              for a, r, (_, _, _, p) in zip(a_bd, r_bd, group)]
        a_rows = [jnp.concatenate([x[:hd], y[:hd]], axis=0) for x, y in zip(d0, d1)]
        m_rows = [jnp.concatenate([x[hd:], y[hd:]], axis=0) for x, y in zip(d0, d1)]
        a_ab = [jnp.where(strict_same, x, 0.0) for x in a_rows]
        a_ak = [jnp.where(strict_cross, x, 0.0) for x in a_rows]
        m_rb = [jnp.where(incl_same, x, 0.0) for x in m_rows]
        m_rk = [jnp.where(incl_cross, x, 0.0) for x in m_rows]
        t_inv = [jnp.where(eye, 1.0, jnp.where(level_masks[0], x, 0.0)) for x in a_ab]
        for mask in level_masks[1:]:
            half = [_bdot(t, jnp.where(mask, x, 0.0)) for t, x in zip(t_inv, a_ab)]
            t_inv = [t + _bdot(h, t) for t, h in zip(t_inv, half)]
        v_x = [cross_blocks(p["v"]) for _, _, _, p in group]
        akv = [_bdot(x, v) for x, v in zip(a_ak, v_x)]
        wu = [_bdot(t, jnp.concatenate([a, x], axis=1))
              for t, a, x in zip(t_inv, a_bd, akv)]
        bh_t = [diag_blocks(p["b_h"]).T for _, _, _, p in group]
        kh_t = [cross_blocks(p["k_h"]).T for _, _, _, p in group]
        pw_rw = [_bdot(jnp.concatenate([bt, m], axis=0), x[:, :lanes])
                 for bt, m, x in zip(bh_t, m_rb, wu)]
        q_o = [_bdot(jnp.concatenate([jnp.concatenate([bt, kt], axis=1),
                                      jnp.concatenate([mb, mk], axis=1)], axis=0),
                     jnp.concatenate([x[:, lanes:], v], axis=0))
               for bt, kt, mb, mk, x, v in zip(bh_t, kh_t, m_rb, m_rk, wu, v_x)]
        res = [_bdot(jnp.concatenate([r + pr[lanes:],
                                      pr[:lanes] + jnp.where(eye, p["g_end"], 0.0)], axis=0),
                     state_ref[idx])
               for r, pr, (_, _, idx, p) in zip(r_bd, pw_rw, group)]
        for x, q, (bi, sl, idx, _) in zip(res, q_o, group):
            o_bd = x[:lanes] + q[lanes:]
            state_ref[idx] = x[lanes:] + q[:lanes]
            o_ref[bi, :, sl] = jnp.where(left, o_bd[:hd], o_bd[hd:])


def _wkv_chunk(r, lw, k, v, kk, b):
    n_batch, seq, rw = r.shape
    n_pairs = rw // V7X_LANES
    frames = SCAN_FRAMES
    assert seq % frames == 0
    spec = pl.BlockSpec((n_batch, frames, rw), lambda c: (0, c, 0))
    return pl.pallas_call(
        functools.partial(_wkv_chunk_kernel, n_batch, n_pairs),
        grid=(seq // frames,),
        in_specs=[spec] * 6,
        out_specs=spec,
        out_shape=jax.ShapeDtypeStruct((n_batch, seq, rw), F32),
        scratch_shapes=[pltpu.VMEM((n_batch * n_pairs, V7X_LANES, V7X_LANES), F32)],
        compiler_params=_params("arbitrary"),
        name="wkv_chunk",
    )(r, lw, k, v, kk, b)


def _post_mix_kernel(n_experts,
                     o_ref, bonus_ref, gate_ref, yconv_ref, x_ref, gn_w_ref, gn_b_ref,
                     w_out_c_ref, w_out_r_ref, ffn_g_ref, wr_hi_ref, wr_lo_ref, b_router_ref,
                     seg2_ref,
                     x1_ref, *outs):
    h2_refs, (top_e_ref, gates_ref) = outs[:-2], outs[-2:]
    seg2 = seg2_ref[...]
    o = o_ref[...]
    mean = _seg_sum(o, seg2) * (1.0 / HEAD_DIM)
    cen = o - mean
    var = _seg_sum(cen * cen, seg2) * (1.0 / HEAD_DIM)
    o = cen * lax.rsqrt(var + GN_EPS) * gn_w_ref[...] + gn_b_ref[...]
    y_rwkv = (o + bonus_ref[...]) * gate_ref[...]
    x1 = (x_ref[...]
          + jnp.dot(yconv_ref[...], w_out_c_ref[...], preferred_element_type=F32)
          + jnp.dot(y_rwkv.astype(BF16), w_out_r_ref[...], preferred_element_type=F32))
    x1_ref[...] = x1
    h2 = _rms_rows(x1, ffn_g_ref[...])
    for c, h2_ref in enumerate(h2_refs):
        h2_ref[...] = h2[:, c * ROW_CHUNK:(c + 1) * ROW_CHUNK]

    h_hi, h_lo = _split_bf16(h2)
    nt = (((1,), (1,)), ((), ()))
    logits = (lax.dot_general(wr_hi_ref[...], h_hi, nt, preferred_element_type=F32)
              + lax.dot_general(wr_hi_ref[...], h_lo, nt, preferred_element_type=F32)
              + lax.dot_general(wr_lo_ref[...], h_hi, nt, preferred_element_type=F32)
              + b_router_ref[...])
    e_id = lax.broadcasted_iota(I32, logits.shape, 0).astype(F32)
    work = logits
    tops, ids = [], []
    for _ in range(TOP_K):
        m = jnp.max(work, axis=0, keepdims=True)
        sel = jnp.min(jnp.where(work == m, e_id, float(n_experts)), axis=0, keepdims=True)
        tops.append(m)
        ids.append(sel)
        work = jnp.where(e_id == sel, -jnp.inf, work)
    ex = [jnp.exp(t - tops[0]) for t in tops]
    denom = ex[0] + ex[1] + ex[2] + ex[3]
    top_e_ref[...] = jnp.concatenate(ids, axis=0).astype(I32)
    gates_ref[...] = jnp.concatenate([e / denom for e in ex], axis=0)


def _post_mix(token0, n, o, bonus, gate, yconv, x2d, gn_w, gn_b, w_out_c, w_out_r, ffn_g, wr_hi,
              wr_lo, b_router, seg2):
    d = x2d.shape[1]
    rw = o.shape[1]
    cw = yconv.shape[1]
    n_experts = wr_hi.shape[0]
    tm = POST_ROWS
    assert token0 % tm == 0 and n % tm == 0
    full = lambda a: pl.BlockSpec(a.shape, lambda i: (0,) * a.ndim)
    in_spec = lambda c: pl.BlockSpec((tm, c), lambda i: (i + token0 // tm, 0))
    row_spec = lambda c: pl.BlockSpec((tm, c), lambda i: (i, 0))
    col_spec = pl.BlockSpec((TOP_K, tm), lambda i: (0, i))
    consts = (gn_w, gn_b, w_out_c, w_out_r, ffn_g, wr_hi, wr_lo, b_router, seg2)
    n_chunks = d // ROW_CHUNK
    outs = pl.pallas_call(
        functools.partial(_post_mix_kernel, n_experts),
        grid=(n // tm,),
        in_specs=[in_spec(rw), in_spec(rw), in_spec(rw), in_spec(cw), in_spec(d)]
                 + [full(c) for c in consts],
        out_specs=[row_spec(d)] + [row_spec(ROW_CHUNK)] * n_chunks + [col_spec, col_spec],
        out_shape=[jax.ShapeDtypeStruct((n, d), F32)]
                  + [jax.ShapeDtypeStruct((n, ROW_CHUNK), F32)] * n_chunks
                  + [jax.ShapeDtypeStruct((TOP_K, n), I32), jax.ShapeDtypeStruct((TOP_K, n), F32)],
        compiler_params=_params("parallel"),
        name="post_mix",
    )(o, bonus, gate, yconv, x2d, *consts)
    return outs[0], outs[1:1 + n_chunks], outs[-2], outs[-1]


def _route_kernel(n_experts, block_rows,
                  top_e_ref, tri_ref, dest_ref, meta_ref,
                  count_ref, start_ref, carry_ref):
    phase = pl.program_id(0)
    j = pl.program_id(1)
    tb = top_e_ref.shape[1]
    e_id = lax.broadcasted_iota(I32, (n_experts, tb), 0)
    top_e = top_e_ref[...]
    onehot = jnp.zeros((n_experts, tb), F32)
    for c in range(TOP_K):
        onehot = onehot + jnp.where(top_e[c:c + 1, :] == e_id, 1.0, 0.0)
    block_count = jnp.sum(onehot, axis=1, keepdims=True)

    @pl.when((phase == 0) & (j == 0))
    def _():
        count_ref[...] = jnp.zeros_like(count_ref)

    @pl.when(phase == 0)
    def _():
        count_ref[...] += block_count

    @pl.when((phase == 1) & (j == 0))
    def _():
        counts = count_ref[...]
        padded = jnp.ceil(counts * (1.0 / block_rows)) * block_rows
        sub = lax.broadcasted_iota(I32, (n_experts, n_experts), 0)
        lane = lax.broadcasted_iota(I32, (n_experts, n_experts), 1)
        padded_row = jnp.sum(jnp.where(sub == lane, padded, 0.0), axis=0, keepdims=True)
        start = jnp.sum(jnp.where(lane < sub, padded_row, 0.0), axis=1, keepdims=True)
        start_ref[...] = start
        carry_ref[...] = jnp.zeros_like(carry_ref)
        end = start + padded
        nb = meta_ref.shape[1]
        slot0 = (lax.broadcasted_iota(I32, (n_experts, nb), 1) * block_rows).astype(F32)
        block_e = jnp.sum(jnp.where(end <= slot0, 1.0, 0.0), axis=0, keepdims=True)
        block_e = jnp.minimum(block_e, n_experts - 1.0)
        used = jnp.max(end, axis=0, keepdims=True) * (1.0 / block_rows)
        in_region = (start <= slot0) & (slot0 < end)
        valid = jnp.sum(jnp.where(in_region, jnp.minimum(start + counts - slot0, block_rows), 0.0),
                        axis=0, keepdims=True)
        row = lax.broadcasted_iota(I32, meta_ref.shape, 0)
        meta_ref[...] = jnp.where(row == META_BLOCK_EXPERT, block_e,
                                  jnp.where(row == META_VALID_ROWS, valid, used)).astype(I32)

    @pl.when(phase == 1)
    def _():
        incl = jnp.dot(onehot.astype(BF16), tri_ref[...], preferred_element_type=F32)
        base = incl - onehot + carry_ref[...] + start_ref[...]
        rows = [jnp.sum(jnp.where(top_e[c:c + 1, :] == e_id, base, 0.0), axis=0, keepdims=True)
                for c in range(TOP_K)]
        dest_ref[...] = jnp.concatenate(rows, axis=0).astype(I32)
        carry_ref[...] += block_count


def _route(top_e, n_experts, n_blocks_padded):
    n = top_e.shape[1]
    tb = ROUTE_TOKENS
    assert n % tb == 0
    tri = (lax.broadcasted_iota(I32, (tb, tb), 0) <= lax.broadcasted_iota(I32, (tb, tb), 1)
           ).astype(BF16)
    return pl.pallas_call(
        functools.partial(_route_kernel, n_experts, EXPERT_ROWS),
        grid=(2, n // tb),
        in_specs=[pl.BlockSpec((TOP_K, tb), lambda ph, j: (0, j)),
                  pl.BlockSpec((tb, tb), lambda ph, j: (0, 0))],
        out_specs=[pl.BlockSpec((TOP_K, tb), lambda ph, j: (0, j * ph)),
                   pl.BlockSpec((V7X_SUBLANES, n_blocks_padded), lambda ph, j: (0, 0))],
        out_shape=[jax.ShapeDtypeStruct((TOP_K, n), I32),
                   jax.ShapeDtypeStruct((V7X_SUBLANES, n_blocks_padded), I32)],
        scratch_shapes=[pltpu.VMEM((n_experts, 1), F32)] * 3,
        compiler_params=_params("arbitrary", "arbitrary"),
        name="route",
    )(top_e, tri)


def _sc_mesh():
    return plsc.VectorSubcoreMesh(core_axis_name="core", subcore_axis_name="subcore")


def _sc_scatter_rows(src, dest, n_slots):
    n, width = src.shape
    assert n % SC_WINDOW == 0

    @functools.partial(pl.kernel, out_type=jax.ShapeDtypeStruct((n_slots, width), src.dtype),
                       mesh=_sc_mesh(), name="sc_scatter_rows")
    def scatter(src_hbm, dest_hbm, out_hbm):
        def body(src_vmem, dest_vmem):
            for c in range(TOP_K):
                pltpu.sync_copy(src_vmem, out_hbm.at[dest_vmem.at[c]])

        pltpu.emit_pipeline(
            body,
            grid=(n // SC_WINDOW,),
            in_specs=[pl.BlockSpec((SC_WINDOW, width), lambda i: (i, 0)),
                      pl.BlockSpec((TOP_K, SC_WINDOW), lambda i: (0, i))],
            out_specs=[],
            core_axis_name=("core", "subcore"),
            dimension_semantics=(pltpu.PARALLEL,),
        )(src_hbm, dest_hbm)

    return scatter(src, dest)


def _experts_kernel(d_ff, n_chunks, meta_ref, *refs):
    x_refs, refs = refs[:n_chunks], refs[n_chunks:]
    w_gu_ref, b_gu_ref, w_down_ref, b_down_ref = refs[:4]
    y_refs = refs[4:4 + n_chunks]
    w_gu_bf16, w_down_bf16 = refs[4 + n_chunks:]
    j = pl.program_id(0)
    used = meta_ref[META_USED_BLOCKS, 0]
    expert = meta_ref[META_BLOCK_EXPERT, j]
    prev_expert = meta_ref[META_BLOCK_EXPERT, jnp.maximum(j - 1, 0)]

    @pl.when((j < used) & ((j == 0) | (expert != prev_expert)))
    def _():
        w_gu_bf16[...] = w_gu_ref[0].astype(BF16)
        w_down_bf16[...] = w_down_ref[0].astype(BF16)

    @pl.when(j < used)
    def _():
        x = jnp.concatenate([r[...] for r in x_refs], axis=1)
        row = lax.broadcasted_iota(I32, x.shape, 0)
        x = jnp.where(row < meta_ref[META_VALID_ROWS, j], x, 0.0)
        gu = jnp.dot(x.astype(BF16), w_gu_bf16[...], preferred_element_type=F32) + b_gu_ref[0]
        gate = jnp.minimum(gu[:, :d_ff], SWIGLU_LIMIT)
        up = jnp.clip(gu[:, d_ff:], -SWIGLU_LIMIT, SWIGLU_LIMIT)
        act = (up + 1.0) * (gate * _sigmoid(SWIGLU_ALPHA * gate))
        y = jnp.dot(act.astype(BF16), w_down_bf16[...], preferred_element_type=F32) + b_down_ref[0]
        for c, y_ref in enumerate(y_refs):
            y_ref[...] = y[:, c * ROW_CHUNK:(c + 1) * ROW_CHUNK]

    @pl.when(j >= used)
    def _():
        for y_ref in y_refs:
            y_ref[...] = jnp.zeros_like(y_ref)


def _experts(meta, slot_chunks, w_gu, b_gu, w_down, b_down):
    n_chunks = len(slot_chunks)
    n_slots = slot_chunks[0].shape[0]
    n_experts, d, two_ff = w_gu.shape
    d_ff = two_ff // 2
    bm = EXPERT_ROWS
    grid_spec = pltpu.PrefetchScalarGridSpec(
        num_scalar_prefetch=1,
        grid=(n_slots // bm,),
        in_specs=[pl.BlockSpec((bm, ROW_CHUNK),
                               lambda j, m: (jnp.minimum(j, m[META_USED_BLOCKS, 0] - 1), 0))
                  ] * n_chunks
                 + [pl.BlockSpec((1, d, two_ff), lambda j, m: (m[META_BLOCK_EXPERT, j], 0, 0)),
                    pl.BlockSpec((1, 1, two_ff), lambda j, m: (m[META_BLOCK_EXPERT, j], 0, 0)),
                    pl.BlockSpec((1, d_ff, d), lambda j, m: (m[META_BLOCK_EXPERT, j], 0, 0)),
                    pl.BlockSpec((1, 1, d), lambda j, m: (m[META_BLOCK_EXPERT, j], 0, 0))],
        out_specs=[pl.BlockSpec((bm, ROW_CHUNK), lambda j, m: (j, 0))] * n_chunks,
        scratch_shapes=[pltpu.VMEM((d, two_ff), BF16), pltpu.VMEM((d_ff, d), BF16)],
    )
    return pl.pallas_call(
        functools.partial(_experts_kernel, d_ff, n_chunks),
        grid_spec=grid_spec,
        out_shape=[jax.ShapeDtypeStruct((n_slots, ROW_CHUNK), F32)] * n_chunks,
        compiler_params=_params("arbitrary"),
        name="experts",
    )(meta, *slot_chunks, w_gu, b_gu, w_down, b_down)


def _sc_gather_rows(table, idx_row):
    n_idx = idx_row.shape[1]
    width = table.shape[1]
    assert n_idx % SC_WINDOW == 0

    @functools.partial(pl.kernel, out_type=jax.ShapeDtypeStruct((n_idx, width), table.dtype),
                       mesh=_sc_mesh(), name="sc_gather_rows")
    def gather(table_hbm, idx_hbm, out_hbm):
        def body(idx_vmem, out_vmem):
            pltpu.sync_copy(table_hbm.at[idx_vmem.at[0]], out_vmem)

        pltpu.emit_pipeline(
            body,
            grid=(n_idx // SC_WINDOW,),
            in_specs=[pl.BlockSpec((1, SC_WINDOW), lambda i: (0, i))],
            out_specs=[pl.BlockSpec((SC_WINDOW, width), lambda i: (i, 0))],
            core_axis_name=("core", "subcore"),
            dimension_semantics=(pltpu.PARALLEL,),
        )(idx_hbm, out_hbm)

    return gather(table, idx_row)


def _reduce_kernel(n_chunks, *refs):
    y_refs = refs[:n_chunks]
    x1_ref, gates_ref, g_ref = refs[n_chunks:n_chunks + 3]
    out_ref = refs[-1]
    gates = gates_ref[...]
    acc = x1_ref[...]
    for c in range(TOP_K):
        acc = acc + jnp.concatenate([r[c] for r in y_refs], axis=1) * gates[:, c:c + 1]
    out_ref[...] = _rms_rows(acc, g_ref[...])


def _reduce(token0, n_total, out_so_far, y_chunks, x1, gates_t, final_g):
    n, d = x1.shape
    tb = REDUCE_TOKENS
    n_chunks = len(y_chunks)
    assert token0 % tb == 0
    in_specs = ([pl.BlockSpec((TOP_K, tb, ROW_CHUNK), lambda i: (0, i, 0))] * n_chunks
                + [pl.BlockSpec((tb, d), lambda i: (i, 0)),
                   pl.BlockSpec((tb, TOP_K), lambda i: (i, 0)),
                   pl.BlockSpec((1, d), lambda i: (0, 0))])
    args = [y.reshape(TOP_K, n, ROW_CHUNK) for y in y_chunks] + [x1, gates_t, final_g]
    aliases = {}
    if out_so_far is not None:
        in_specs.append(pl.BlockSpec(memory_space=pl.ANY))
        args.append(out_so_far)
        aliases = {len(args) - 1: 0}
    return pl.pallas_call(
        functools.partial(_reduce_kernel, n_chunks),
        grid=(n // tb,),
        in_specs=in_specs,
        out_specs=pl.BlockSpec((tb, d), lambda i: (i + token0 // tb, 0)),
        out_shape=jax.ShapeDtypeStruct((n_total, d), F32),
        input_output_aliases=aliases,
        compiler_params=_params("parallel"),
        name="reduce",
    )(*args)


def _group_matrix(width):
    a = lax.broadcasted_iota(I32, (width, width), 0) // HEAD_DIM
    b = lax.broadcasted_iota(I32, (width, width), 1) // HEAD_DIM
    g = (a == b).astype(BF16)
    return jnp.concatenate([g, g], axis=0)


def _row(vec):
    return vec.reshape(1, -1).astype(F32)


def kernel(x, w_in, conv_w, conv_norm_g, rwkv_mu, w0, w_up, a0, a_up, g_up, k_k, k_a, r_k,
           gn_w, gn_b, w_out, norm_mix_g, norm_ffn_g, w_router, b_router, w_gu, b_gu, w_down,
           b_down, norm_final_g):
    n_batch, seq, d = x.shape
    n = n_batch * seq
    depth = w_in.shape[0]
    cw = conv_w.shape[2]
    rw = w0.shape[1]
    n_experts = w_router.shape[2]
    decay_lora = w_up.shape[1]
    aaa_lora = a_up.shape[1]
    assert decay_lora + aaa_lora == V7X_LANES
    assert depth == 1
    assert n % MOE_PARTS == 0
    n_part = n // MOE_PARTS
    n_slots = n_part * TOP_K + n_experts * EXPERT_ROWS
    n_blocks = n_slots // EXPERT_ROWS
    n_blocks_padded = -(-n_blocks // V7X_LANES) * V7X_LANES
    seg2 = _group_matrix(rw)

    x2d = x.reshape(n, d)
    for l in range(depth):
        w_up_pad = jnp.concatenate([w_up[l], jnp.zeros((aaa_lora, rw), F32)], axis=0).astype(BF16)
        a_up_pad = jnp.concatenate([jnp.zeros((decay_lora, rw), F32), a_up[l]], axis=0).astype(BF16)
        (yconv, r, lw, k, v, kk, b, gate, bonus) = _mix_prep(
            x2d, seq, _row(norm_mix_g[l]), w_in[l].astype(BF16), conv_w[l].astype(F32),
            _row(conv_norm_g[l]), _row(rwkv_mu[l]), _row(w0[l]), w_up_pad, _row(a0[l]), a_up_pad,
            g_up[l].astype(BF16), _row(k_k[l]), _row(k_a[l]), _row(r_k[l]), seg2)
        shape3 = (n_batch, seq, rw)
        o = _wkv_chunk(r.reshape(shape3), lw.reshape(shape3), k.reshape(shape3),
                       v.reshape(shape3), kk.reshape(shape3), b.reshape(shape3)).reshape(n, rw)
        w_out_b = w_out[l].astype(BF16)
        wr_t = w_router[l].T.astype(F32)
        wr_hi = wr_t.astype(BF16)
        wr_lo = (wr_t - wr_hi.astype(F32)).astype(BF16)
        expert_args = (w_gu[l].astype(F32), b_gu[l].reshape(n_experts, 1, -1).astype(F32),
                       w_down[l].astype(F32), b_down[l].reshape(n_experts, 1, -1).astype(F32))
        parts = []
        for token0 in range(0, n, n_part):
            x1, h2_chunks, top_e, gates = _post_mix(
                token0, n_part, o, bonus, gate, yconv, x2d, _row(gn_w[l]), _row(gn_b[l]),
                w_out_b[:cw], w_out_b[cw:], _row(norm_ffn_g[l]), wr_hi, wr_lo,
                b_router[l].reshape(n_experts, 1).astype(F32), seg2)
            dest, meta = _route(top_e, n_experts, n_blocks_padded)
            slot_chunks = [_sc_scatter_rows(h, dest, n_slots) for h in h2_chunks]
            parts.append((token0, x1, gates, dest, meta, slot_chunks))
        out = None
        for token0, x1, gates, dest, meta, slot_chunks in parts:
            y_chunks = _experts(meta, slot_chunks, *expert_args)
            idx_row = dest.reshape(1, TOP_K * n_part)
            out = _reduce(token0, n, out, [_sc_gather_rows(y, idx_row) for y in y_chunks], x1,
                          gates.T, _row(norm_final_g))
        x2d = out
    return x2d.reshape(n_batch, seq, d)
```

```python
import functools

import jax
import jax.numpy as jnp
from jax import lax
from jax.experimental import pallas as pl
from jax.experimental.pallas import tpu as pltpu
from jax.experimental.pallas import tpu_sc as plsc

F32 = jnp.float32
BF16 = jnp.bfloat16
I32 = jnp.int32

HEAD_DIM = 64
TOP_K = 4
NORM_EPS = 1e-5
GN_EPS = HEAD_DIM * 1e-5
SWIGLU_LIMIT = 7.0
SWIGLU_ALPHA = 1.702

V7X_LANES = 128
V7X_SUBLANES = 8
V7X_VMEM_LIMIT_BYTES = 56 * 1024 * 1024

PREP_ROWS = 256
POST_ROWS = 512
SCAN_FRAMES = 64
SCAN_BATCHES_PER_GROUP = 4
ROUTE_TOKENS = 1024
REDUCE_TOKENS = 256
MOE_PARTS = 2
SC_WINDOW = 128
ROW_CHUNK = 256
META_BLOCK_EXPERT = 0
META_USED_BLOCKS = 1
META_VALID_ROWS = 2
EXPERT_ROWS = 512

def _params(*semantics):
    return pltpu.CompilerParams(dimension_semantics=semantics,
                                vmem_limit_bytes=V7X_VMEM_LIMIT_BYTES)


def _split_bf16(x):
    hi = x.astype(BF16)
    lo = (x - hi.astype(F32)).astype(BF16)
    return hi, lo


def _seg_sum(x, seg2):
    hi, lo = _split_bf16(x)
    return jnp.dot(jnp.concatenate([hi, lo], axis=1), seg2, preferred_element_type=F32)


def _rms_rows(x, g):
    return x * lax.rsqrt(jnp.mean(x * x, axis=-1, keepdims=True) + NORM_EPS) * g


def _sigmoid(x):
    return 1.0 / (1.0 + jnp.exp(-x))


def _mix_prep_kernel(blocks_per_seq, cw, rw,
                     xprev_ref, x_ref, g_ref, w_in_ref, conv_w_ref, conv_g_ref, mu_ref,
                     w0_ref, w_up_ref, a0_ref, a_up_ref, g_up_ref, k_k_ref, k_a_ref, r_k_ref,
                     seg2_ref,
                     yconv_ref, r_ref, w_ref, k_ref, v_ref, kk_ref, b_ref, gate_ref, bonus_ref,
                     p_scr, z_scr):
    tm = x_ref.shape[0]
    halo = xprev_ref.shape[0]
    first = (pl.program_id(0) % blocks_per_seq) == 0
    xp = xprev_ref[...] * jnp.where(first, 0.0, 1.0)
    xa = jnp.concatenate([xp, x_ref[...]], axis=0)
    h = _rms_rows(xa, g_ref[...])
    p_scr[...] = jnp.dot(h.astype(BF16), w_in_ref[...], preferred_element_type=F32)
    seg2 = seg2_ref[...]

    z_scr[...] = p_scr[:, 2 * cw:3 * cw] * p_scr[:, 0:cw]
    conv = (conv_w_ref[0:1, :] * z_scr[halo - 2:halo - 2 + tm, :]
            + conv_w_ref[1:2, :] * z_scr[halo - 1:halo - 1 + tm, :]
            + conv_w_ref[2:3, :] * z_scr[halo:halo + tm, :])
    y = p_scr[halo:halo + tm, cw:2 * cw] * conv
    ms = _seg_sum(y * y, seg2) * (1.0 / HEAD_DIM)
    yconv_ref[...] = (y * lax.rsqrt(ms + NORM_EPS) * conv_g_ref[...]).astype(yconv_ref.dtype)

    c0 = 3 * cw
    cur = p_scr[halo:halo + tm, c0:]
    prev = p_scr[halo - 1:halo - 1 + tm, c0:]
    q = cur + (prev - cur) * mu_ref[...]
    r = q[:, 0:rw]
    k = q[:, rw:2 * rw]
    v = q[:, 2 * rw:3 * rw]
    lora_wa = q[:, 3 * rw:3 * rw + V7X_LANES]
    lora_g = q[:, 3 * rw + V7X_LANES:]
    w_lin = w0_ref[...] + jnp.dot(jnp.tanh(lora_wa).astype(BF16), w_up_ref[...],
                                  preferred_element_type=F32)
    neg = -w_lin
    softplus = jnp.maximum(neg, 0.0) + jnp.log(1.0 + jnp.exp(-jnp.abs(neg)))
    log_decay = -jnp.exp(-softplus - 0.5)
    a = _sigmoid(a0_ref[...] + jnp.dot(lora_wa.astype(BF16), a_up_ref[...],
                                       preferred_element_type=F32))
    gate = jnp.dot(_sigmoid(lora_g).astype(BF16), g_up_ref[...], preferred_element_type=F32)
    kk = k * k_k_ref[...]
    kk = kk / jnp.maximum(jnp.sqrt(_seg_sum(kk * kk, seg2)), 1e-12)
    k_mod = k * (1.0 + (a - 1.0) * k_a_ref[...])
    bonus = _seg_sum(r * k_mod * r_k_ref[...], seg2) * v
    r_ref[...] = r
    w_ref[...] = log_decay
    k_ref[...] = k_mod
    v_ref[...] = v
    kk_ref[...] = kk
    b_ref[...] = kk * a
    gate_ref[...] = gate
    bonus_ref[...] = bonus


def _mix_prep(x2d, seq, norm_g, w_in, conv_w, conv_g, mu, w0, w_up, a0, a_up, g_up, k_k, k_a,
              r_k, seg2):
    n, d = x2d.shape
    tm = PREP_ROWS
    halo = V7X_SUBLANES
    cw = conv_w.shape[1]
    rw = w0.shape[1]
    in_cols = w_in.shape[1]
    assert seq % tm == 0 and n % tm == 0
    full = lambda a: pl.BlockSpec(a.shape, lambda i: (0,) * a.ndim)
    row_spec = lambda c: pl.BlockSpec((tm, c), lambda i: (i, 0))
    consts = (norm_g, w_in, conv_w, conv_g, mu, w0, w_up, a0, a_up, g_up, k_k, k_a, r_k, seg2)
    outs = [jax.ShapeDtypeStruct((n, cw), BF16)] + [jax.ShapeDtypeStruct((n, rw), F32)] * 8
    return pl.pallas_call(
        functools.partial(_mix_prep_kernel, seq // tm, cw, rw),
        grid=(n // tm,),
        in_specs=[pl.BlockSpec((halo, d), lambda i: (jnp.maximum(i * (tm // halo) - 1, 0), 0)),
                  row_spec(d)] + [full(c) for c in consts],
        out_specs=[row_spec(cw)] + [row_spec(rw)] * 8,
        out_shape=outs,
        scratch_shapes=[pltpu.VMEM((halo + tm, in_cols), F32), pltpu.VMEM((halo + tm, cw), F32)],
        compiler_params=_params("parallel"),
        name="mix_prep",
    )(x2d, x2d, *consts)


def _bdot(a, b):
    return jnp.dot(a.astype(BF16), b.astype(BF16), preferred_element_type=F32)


def _bdot_nt(a, b):
    return lax.dot_general(a.astype(BF16), b.astype(BF16), (((1,), (1,)), ((), ())),
                           preferred_element_type=F32)


def _wkv_chunk_kernel(n_batch, n_pairs,
                      r_ref, lw_ref, k_ref, v_ref, kk_ref, b_ref, o_ref, state_ref):
    frames = r_ref.shape[1]
    lanes = V7X_LANES
    hd = HEAD_DIM
    assert frames == hd

    @pl.when(pl.program_id(0) == 0)
    def _():
        state_ref[...] = jnp.zeros_like(state_ref)

    row = lax.broadcasted_iota(I32, (lanes, lanes), 0)
    lane = lax.broadcasted_iota(I32, (lanes, lanes), 1)
    same_head = (row // hd) == (lane // hd)
    rt = row % hd
    ls = lane % hd
    strict_same = same_head & (rt > ls)
    strict_cross = jnp.logical_not(same_head) & (rt > ls)
    incl_same = same_head & (rt >= ls)
    incl_cross = jnp.logical_not(same_head) & (rt >= ls)
    eye = row == lane
    level_masks = []
    m = 1
    while m < hd:
        level_masks.append(same_head & ((rt // (2 * m)) == (ls // (2 * m)))
                           & (((rt // m) % 2) == 1) & (((ls // m) % 2) == 0))
        m *= 2
    left = lax.broadcasted_iota(I32, (frames, lanes), 1) < hd
    tri = (lax.broadcasted_iota(I32, (frames, frames), 0)
           >= lax.broadcasted_iota(I32, (frames, frames), 1)).astype(BF16)

    def diag_blocks(x):
        return jnp.concatenate([jnp.where(left, x, 0.0), jnp.where(left, 0.0, x)], axis=0)

    def cross_blocks(x):
        return jnp.concatenate([jnp.where(left, 0.0, x), jnp.where(left, x, 0.0)], axis=0)

    def batch_prep(bi):
        lw = lw_ref[bi]
        hi = lw.astype(BF16)
        rem = lw - hi.astype(F32)
        mid = rem.astype(BF16)
        lo = (rem - mid.astype(F32)).astype(BF16)
        cs = (jnp.dot(tri, hi, preferred_element_type=F32)
              + jnp.dot(tri, mid, preferred_element_type=F32)
              + jnp.dot(tri, lo, preferred_element_type=F32))
        cs_end = cs[frames - 1:frames, :]
        e_neg = jnp.exp(-cs)
        e_end = jnp.exp(cs_end - cs)
        b_in = b_ref[bi]
        k_in = k_ref[bi]
        return dict(a_t=-kk_ref[bi] * jnp.exp(cs - lw), b_t=b_in * e_neg, k_t=k_in * e_neg,
                    r_t=r_ref[bi] * jnp.exp(cs), b_h=b_in * e_end, k_h=k_in * e_end,
                    v=v_ref[bi], g_end=jnp.exp(cs_end))

    for b0 in range(0, n_batch, SCAN_BATCHES_PER_GROUP):
        group = []
        for bi in range(b0, min(b0 + SCAN_BATCHES_PER_GROUP, n_batch)):
            prep = batch_prep(bi)
            for pi in range(n_pairs):
                sl = slice(pi * lanes, (pi + 1) * lanes)
                group.append((bi, sl, bi * n_pairs + pi, {n: x[:, sl] for n, x in prep.items()}))
        a_bd = [diag_blocks(p["a_t"]) for _, _, _, p in group]
        r_bd = [diag_blocks(p["r_t"]) for _, _, _, p in group]
        d0 = [_bdot_nt(jnp.concatenate([a[:hd], r[:hd]], axis=0),
                       jnp.concatenate([p["b_t"], p["k_t"]], axis=0))
              for a, r, (_, _, _, p) in zip(a_bd, r_bd, group)]
        d1 = [_bdot_nt(jnp.concatenate([a[hd:], r[hd:]], axis=0),
                       jnp.concatenate([p["k_t"], p["b_t"]], axis=0))
              for a, r, (_, _, _, p) in zip(a_bd, r_bd, group)]
        a_rows = [jnp.concatenate([x[:hd], y[:hd]], axis=0) for x, y in zip(d0, d1)]
        m_rows = [jnp.concatenate([x[hd:], y[hd:]], axis=0) for x, y in zip(d0, d1)]
        a_ab = [jnp.where(strict_same, x, 0.0) for x in a_rows]
        a_ak = [jnp.where(strict_cross, x, 0.0) for x in a_rows]
        m_rb = [jnp.where(incl_same, x, 0.0) for x in m_rows]
        m_rk = [jnp.where(incl_cross, x, 0.0) for x in m_rows]
        t_inv = [jnp.where(eye, 1.0, jnp.where(level_masks[0], x, 0.0)) for x in a_ab]
        for mask in level_masks[1:]:
            half = [_bdot(t, jnp.where(mask, x, 0.0)) for t, x in zip(t_inv, a_ab)]
            t_inv = [t + _bdot(h, t) for t, h in zip(t_inv, half)]
        v_x = [cross_blocks(p["v"]) for _, _, _, p in group]
        akv = [_bdot(x, v) for x, v in zip(a_ak, v_x)]
        wu = [_bdot(t, jnp.concatenate([a, x], axis=1))
              for t, a, x in zip(t_inv, a_bd, akv)]
        bh_t = [diag_blocks(p["b_h"]).T for _, _, _, p in group]
        kh_t = [cross_blocks(p["k_h"]).T for _, _, _, p in group]
        pw_rw = [_bdot(jnp.concatenate([bt, m], axis=0), x[:, :lanes])
                 for bt, m, x in zip(bh_t, m_rb, wu)]
        q_o = [_bdot(jnp.concatenate([jnp.concatenate([bt, kt], axis=1),
                                      jnp.concatenate([mb, mk], axis=1)], axis=0),
                     jnp.concatenate([x[:, lanes:], v], axis=0))
               for bt, kt, mb, mk, x, v in zip(bh_t, kh_t, m_rb, m_rk, wu, v_x)]
        res = [_bdot(jnp.concatenate([r + pr[lanes:],
                                      pr[:lanes] + jnp.where(eye, p["g_end"], 0.0)], axis=0),
                     state_ref[idx])
               for r, pr, (_, _, idx, p) in zip(r_bd, pw_rw, group)]
        for x, q, (bi, sl, idx, _) in zip(res, q_o, group):
            o_bd = x[:lanes] + q[lanes:]
            state_ref[idx] = x[lanes:] + q[:lanes]
            o_ref[bi, :, sl] = jnp.where(left, o_bd[:hd], o_bd[hd:])


def _wkv_chunk(r, lw, k, v, kk, b):
    n_batch, seq, rw = r.shape
    n_pairs = rw // V7X_LANES
    frames = SCAN_FRAMES
    assert seq % frames == 0
    spec = pl.BlockSpec((n_batch, frames, rw), lambda c: (0, c, 0))
    return pl.pallas_call(
        functools.partial(_wkv_chunk_kernel, n_batch, n_pairs),
        grid=(seq // frames,),
        in_specs=[spec] * 6,
        out_specs=spec,
        out_shape=jax.ShapeDtypeStruct((n_batch, seq, rw), F32),
        scratch_shapes=[pltpu.VMEM((n_batch * n_pairs, V7X_LANES, V7X_LANES), F32)],
        compiler_params=_params("arbitrary"),
        name="wkv_chunk",
    )(r, lw, k, v, kk, b)


def _post_mix_kernel(n_experts,
                     o_ref, bonus_ref, gate_ref, yconv_ref, x_ref, gn_w_ref, gn_b_ref,
                     w_out_c_ref, w_out_r_ref, ffn_g_ref, wr_hi_ref, wr_lo_ref, b_router_ref,
                     seg2_ref,
                     x1_ref, *outs):
    h2_refs, (top_e_ref, gates_ref) = outs[:-2], outs[-2:]
    seg2 = seg2_ref[...]
    o = o_ref[...]
    mean = _seg_sum(o, seg2) * (1.0 / HEAD_DIM)
    cen = o - mean
    var = _seg_sum(cen * cen, seg2) * (1.0 / HEAD_DIM)
    o = cen * lax.rsqrt(var + GN_EPS) * gn_w_ref[...] + gn_b_ref[...]
    y_rwkv = (o + bonus_ref[...]) * gate_ref[...]
    x1 = (x_ref[...]
          + jnp.dot(yconv_ref[...], w_out_c_ref[...], preferred_element_type=F32)
          + jnp.dot(y_rwkv.astype(BF16), w_out_r_ref[...], preferred_element_type=F32))
    x1_ref[...] = x1
    h2 = _rms_rows(x1, ffn_g_ref[...])
    bits = pltpu.bitcast(h2.astype(BF16).astype(F32), jnp.uint32)
    half = h2.shape[1] // 2
    words = bits[:, half:] | (bits[:, :half] >> 16)
    for c, h2_ref in enumerate(h2_refs):
        h2_ref[...] = words[:, c * ROW_CHUNK:(c + 1) * ROW_CHUNK]

    h_hi, h_lo = _split_bf16(h2)
    nt = (((1,), (1,)), ((), ()))
    logits = (lax.dot_general(wr_hi_ref[...], h_hi, nt, preferred_element_type=F32)
              + lax.dot_general(wr_hi_ref[...], h_lo, nt, preferred_element_type=F32)
              + lax.dot_general(wr_lo_ref[...], h_hi, nt, preferred_element_type=F32)
              + b_router_ref[...])
    e_id = lax.broadcasted_iota(I32, logits.shape, 0).astype(F32)
    work = logits
    tops, ids = [], []
    for _ in range(TOP_K):
        m = jnp.max(work, axis=0, keepdims=True)
        sel = jnp.min(jnp.where(work == m, e_id, float(n_experts)), axis=0, keepdims=True)
        tops.append(m)
        ids.append(sel)
        work = jnp.where(e_id == sel, -jnp.inf, work)
    ex = [jnp.exp(t - tops[0]) for t in tops]
    denom = ex[0] + ex[1] + ex[2] + ex[3]
    top_e_ref[...] = jnp.concatenate(ids, axis=0).astype(I32)
    gates_ref[...] = jnp.concatenate([e / denom for e in ex], axis=0)


def _post_mix(token0, n, o, bonus, gate, yconv, x2d, gn_w, gn_b, w_out_c, w_out_r, ffn_g, wr_hi,
              wr_lo, b_router, seg2):
    d = x2d.shape[1]
    rw = o.shape[1]
    cw = yconv.shape[1]
    n_experts = wr_hi.shape[0]
    tm = POST_ROWS
    assert token0 % tm == 0 and n % tm == 0
    full = lambda a: pl.BlockSpec(a.shape, lambda i: (0,) * a.ndim)
    in_spec = lambda c: pl.BlockSpec((tm, c), lambda i: (i + token0 // tm, 0))
    row_spec = lambda c: pl.BlockSpec((tm, c), lambda i: (i, 0))
    col_spec = pl.BlockSpec((TOP_K, tm), lambda i: (0, i))
    consts = (gn_w, gn_b, w_out_c, w_out_r, ffn_g, wr_hi, wr_lo, b_router, seg2)
    n_chunks = d // 2 // ROW_CHUNK
    outs = pl.pallas_call(
        functools.partial(_post_mix_kernel, n_experts),
        grid=(n // tm,),
        in_specs=[in_spec(rw), in_spec(rw), in_spec(rw), in_spec(cw), in_spec(d)]
                 + [full(c) for c in consts],
        out_specs=[row_spec(d)] + [row_spec(ROW_CHUNK)] * n_chunks + [col_spec, col_spec],
        out_shape=[jax.ShapeDtypeStruct((n, d), F32)]
                  + [jax.ShapeDtypeStruct((n, ROW_CHUNK), jnp.uint32)] * n_chunks
                  + [jax.ShapeDtypeStruct((TOP_K, n), I32), jax.ShapeDtypeStruct((TOP_K, n), F32)],
        compiler_params=_params("parallel"),
        name="post_mix",
    )(o, bonus, gate, yconv, x2d, *consts)
    return outs[0], outs[1:1 + n_chunks], outs[-2], outs[-1]


def _route_kernel(n_experts, block_rows,
                  top_e_ref, tri_ref, dest_ref, meta_ref,
                  count_ref, start_ref, carry_ref):
    phase = pl.program_id(0)
    j = pl.program_id(1)
    tb = top_e_ref.shape[1]
    e_id = lax.broadcasted_iota(I32, (n_experts, tb), 0)
    top_e = top_e_ref[...]
    onehot = jnp.zeros((n_experts, tb), F32)
    for c in range(TOP_K):
        onehot = onehot + jnp.where(top_e[c:c + 1, :] == e_id, 1.0, 0.0)
    block_count = jnp.sum(onehot, axis=1, keepdims=True)

    @pl.when((phase == 0) & (j == 0))
    def _():
        count_ref[...] = jnp.zeros_like(count_ref)

    @pl.when(phase == 0)
    def _():
        count_ref[...] += block_count

    @pl.when((phase == 1) & (j == 0))
    def _():
        counts = count_ref[...]
        padded = jnp.ceil(counts * (1.0 / block_rows)) * block_rows
        sub = lax.broadcasted_iota(I32, (n_experts, n_experts), 0)
        lane = lax.broadcasted_iota(I32, (n_experts, n_experts), 1)
        padded_row = jnp.sum(jnp.where(sub == lane, padded, 0.0), axis=0, keepdims=True)
        start = jnp.sum(jnp.where(lane < sub, padded_row, 0.0), axis=1, keepdims=True)
        start_ref[...] = start
        carry_ref[...] = jnp.zeros_like(carry_ref)
        end = start + padded
        nb = meta_ref.shape[1]
        slot0 = (lax.broadcasted_iota(I32, (n_experts, nb), 1) * block_rows).astype(F32)
        block_e = jnp.sum(jnp.where(end <= slot0, 1.0, 0.0), axis=0, keepdims=True)
        block_e = jnp.minimum(block_e, n_experts - 1.0)
        used = jnp.max(end, axis=0, keepdims=True) * (1.0 / block_rows)
        in_region = (start <= slot0) & (slot0 < end)
        valid = jnp.sum(jnp.where(in_region, jnp.minimum(start + counts - slot0, block_rows), 0.0),
                        axis=0, keepdims=True)
        row = lax.broadcasted_iota(I32, meta_ref.shape, 0)
        meta_ref[...] = jnp.where(row == META_BLOCK_EXPERT, block_e,
                                  jnp.where(row == META_VALID_ROWS, valid, used)).astype(I32)

    @pl.when(phase == 1)
    def _():
        incl = jnp.dot(onehot.astype(BF16), tri_ref[...], preferred_element_type=F32)
        base = incl - onehot + carry_ref[...] + start_ref[...]
        rows = [jnp.sum(jnp.where(top_e[c:c + 1, :] == e_id, base, 0.0), axis=0, keepdims=True)
                for c in range(TOP_K)]
        dest_ref[...] = jnp.concatenate(rows, axis=0).astype(I32)
        carry_ref[...] += block_count


def _route(top_e, n_experts, n_blocks_padded):
    n = top_e.shape[1]
    tb = ROUTE_TOKENS
    assert n % tb == 0
    tri = (lax.broadcasted_iota(I32, (tb, tb), 0) <= lax.broadcasted_iota(I32, (tb, tb), 1)
           ).astype(BF16)
    return pl.pallas_call(
        functools.partial(_route_kernel, n_experts, EXPERT_ROWS),
        grid=(2, n // tb),
        in_specs=[pl.BlockSpec((TOP_K, tb), lambda ph, j: (0, j)),
                  pl.BlockSpec((tb, tb), lambda ph, j: (0, 0))],
        out_specs=[pl.BlockSpec((TOP_K, tb), lambda ph, j: (0, j * ph)),
                   pl.BlockSpec((V7X_SUBLANES, n_blocks_padded), lambda ph, j: (0, 0))],
        out_shape=[jax.ShapeDtypeStruct((TOP_K, n), I32),
                   jax.ShapeDtypeStruct((V7X_SUBLANES, n_blocks_padded), I32)],
        scratch_shapes=[pltpu.VMEM((n_experts, 1), F32)] * 3,
        compiler_params=_params("arbitrary", "arbitrary"),
        name="route",
    )(top_e, tri)


def _sc_mesh():
    return plsc.VectorSubcoreMesh(core_axis_name="core", subcore_axis_name="subcore")


def _sc_scatter_rows(src, dest, n_slots):
    n, width = src.shape
    assert n % SC_WINDOW == 0

    @functools.partial(pl.kernel, out_type=jax.ShapeDtypeStruct((n_slots, width), src.dtype),
                       mesh=_sc_mesh(), name="sc_scatter_rows")
    def scatter(src_hbm, dest_hbm, out_hbm):
        def body(src_vmem, dest_vmem):
            for c in range(TOP_K):
                pltpu.sync_copy(src_vmem, out_hbm.at[dest_vmem.at[c]])

        pltpu.emit_pipeline(
            body,
            grid=(n // SC_WINDOW,),
            in_specs=[pl.BlockSpec((SC_WINDOW, width), lambda i: (i, 0)),
                      pl.BlockSpec((TOP_K, SC_WINDOW), lambda i: (0, i))],
            out_specs=[],
            core_axis_name=("core", "subcore"),
            dimension_semantics=(pltpu.PARALLEL,),
        )(src_hbm, dest_hbm)

    return scatter(src, dest)


def _experts_kernel(d_ff, n_in, n_out, meta_ref, *refs):
    x_refs, refs = refs[:n_in], refs[n_in:]
    w_gu_ref, b_gu_ref, w_down_ref, b_down_ref = refs[:4]
    y_refs = refs[4:4 + n_out]
    w_gu_bf16, w_down_bf16 = refs[4 + n_out:]
    j = pl.program_id(0)
    used = meta_ref[META_USED_BLOCKS, 0]
    expert = meta_ref[META_BLOCK_EXPERT, j]
    prev_expert = meta_ref[META_BLOCK_EXPERT, jnp.maximum(j - 1, 0)]

    @pl.when((j < used) & ((j == 0) | (expert != prev_expert)))
    def _():
        w_gu_bf16[...] = w_gu_ref[0].astype(BF16)
        w_down_bf16[...] = w_down_ref[0].astype(BF16)

    @pl.when(j < used)
    def _():
        words = jnp.concatenate([r[...] for r in x_refs], axis=1)
        x = jnp.concatenate([pltpu.bitcast(words << 16, F32),
                             pltpu.bitcast(words & jnp.uint32(0xFFFF0000), F32)], axis=1)
        row = lax.broadcasted_iota(I32, x.shape, 0)
        x = jnp.where(row < meta_ref[META_VALID_ROWS, j], x, 0.0)
        gu = jnp.dot(x.astype(BF16), w_gu_bf16[...], preferred_element_type=F32) + b_gu_ref[0]
        gate = jnp.minimum(gu[:, :d_ff], SWIGLU_LIMIT)
        up = jnp.clip(gu[:, d_ff:], -SWIGLU_LIMIT, SWIGLU_LIMIT)
        act = (up + 1.0) * (gate * _sigmoid(SWIGLU_ALPHA * gate))
        y = jnp.dot(act.astype(BF16), w_down_bf16[...], preferred_element_type=F32) + b_down_ref[0]
        for c, y_ref in enumerate(y_refs):
            y_ref[...] = y[:, c * ROW_CHUNK:(c + 1) * ROW_CHUNK]

    @pl.when(j >= used)
    def _():
        for y_ref in y_refs:
            y_ref[...] = jnp.zeros_like(y_ref)


def _experts(meta, slot_chunks, w_gu, b_gu, w_down, b_down):
    n_in = len(slot_chunks)
    n_slots = slot_chunks[0].shape[0]
    n_experts, d, two_ff = w_gu.shape
    n_out = d // ROW_CHUNK
    d_ff = two_ff // 2
    bm = EXPERT_ROWS
    grid_spec = pltpu.PrefetchScalarGridSpec(
        num_scalar_prefetch=1,
        grid=(n_slots // bm,),
        in_specs=[pl.BlockSpec((bm, ROW_CHUNK),
                               lambda j, m: (jnp.minimum(j, m[META_USED_BLOCKS, 0] - 1), 0))
                  ] * n_in
                 + [pl.BlockSpec((1, d, two_ff), lambda j, m: (m[META_BLOCK_EXPERT, j], 0, 0)),
                    pl.BlockSpec((1, 1, two_ff), lambda j, m: (m[META_BLOCK_EXPERT, j], 0, 0)),
                    pl.BlockSpec((1, d_ff, d), lambda j, m: (m[META_BLOCK_EXPERT, j], 0, 0)),
                    pl.BlockSpec((1, 1, d), lambda j, m: (m[META_BLOCK_EXPERT, j], 0, 0))],
        out_specs=[pl.BlockSpec((bm, ROW_CHUNK), lambda j, m: (j, 0))] * n_out,
        scratch_shapes=[pltpu.VMEM((d, two_ff), BF16), pltpu.VMEM((d_ff, d), BF16)],
    )
    return pl.pallas_call(
        functools.partial(_experts_kernel, d_ff, n_in, n_out),
        grid_spec=grid_spec,
        out_shape=[jax.ShapeDtypeStruct((n_slots, ROW_CHUNK), F32)] * n_out,
        compiler_params=_params("arbitrary"),
        name="experts",
    )(meta, *slot_chunks, w_gu, b_gu, w_down, b_down)


def _sc_gather_rows(table, idx_row):
    n_idx = idx_row.shape[1]
    width = table.shape[1]
    assert n_idx % SC_WINDOW == 0

    @functools.partial(pl.kernel, out_type=jax.ShapeDtypeStruct((n_idx, width), table.dtype),
                       mesh=_sc_mesh(), name="sc_gather_rows")
    def gather(table_hbm, idx_hbm, out_hbm):
        def body(idx_vmem, out_vmem):
            pltpu.sync_copy(table_hbm.at[idx_vmem.at[0]], out_vmem)

        pltpu.emit_pipeline(
            body,
            grid=(n_idx // SC_WINDOW,),
            in_specs=[pl.BlockSpec((1, SC_WINDOW), lambda i: (0, i))],
            out_specs=[pl.BlockSpec((SC_WINDOW, width), lambda i: (i, 0))],
            core_axis_name=("core", "subcore"),
            dimension_semantics=(pltpu.PARALLEL,),
        )(idx_hbm, out_hbm)

    return gather(table, idx_row)


def _reduce_kernel(n_chunks, *refs):
    y_refs = refs[:n_chunks]
    x1_ref, gates_ref, g_ref = refs[n_chunks:n_chunks + 3]
    out_ref = refs[-1]
    gates = gates_ref[...]
    acc = x1_ref[...]
    for c in range(TOP_K):
        acc = acc + jnp.concatenate([r[c] for r in y_refs], axis=1) * gates[:, c:c + 1]
    out_ref[...] = _rms_rows(acc, g_ref[...])


def _reduce(token0, n_total, out_so_far, y_chunks, x1, gates_t, final_g):
    n, d = x1.shape
    tb = REDUCE_TOKENS
    n_chunks = len(y_chunks)
    assert token0 % tb == 0
    in_specs = ([pl.BlockSpec((TOP_K, tb, ROW_CHUNK), lambda i: (0, i, 0))] * n_chunks
                + [pl.BlockSpec((tb, d), lambda i: (i, 0)),
                   pl.BlockSpec((tb, TOP_K), lambda i: (i, 0)),
                   pl.BlockSpec((1, d), lambda i: (0, 0))])
    args = [y.reshape(TOP_K, n, ROW_CHUNK) for y in y_chunks] + [x1, gates_t, final_g]
    aliases = {}
    if out_so_far is not None:
        in_specs.append(pl.BlockSpec(memory_space=pl.ANY))
        args.append(out_so_far)
        aliases = {len(args) - 1: 0}
    return pl.pallas_call(
        functools.partial(_reduce_kernel, n_chunks),
        grid=(n // tb,),
        in_specs=in_specs,
        out_specs=pl.BlockSpec((tb, d), lambda i: (i + token0 // tb, 0)),
        out_shape=jax.ShapeDtypeStruct((n_total, d), F32),
        input_output_aliases=aliases,
        compiler_params=_params("parallel"),
        name="reduce",
    )(*args)


def _group_matrix(width):
    a = lax.broadcasted_iota(I32, (width, width), 0) // HEAD_DIM
    b = lax.broadcasted_iota(I32, (width, width), 1) // HEAD_DIM
    g = (a == b).astype(BF16)
    return jnp.concatenate([g, g], axis=0)


def _row(vec):
    return vec.reshape(1, -1).astype(F32)


def kernel(x, w_in, conv_w, conv_norm_g, rwkv_mu, w0, w_up, a0, a_up, g_up, k_k, k_a, r_k,
           gn_w, gn_b, w_out, norm_mix_g, norm_ffn_g, w_router, b_router, w_gu, b_gu, w_down,
           b_down, norm_final_g):
    n_batch, seq, d = x.shape
    n = n_batch * seq
    depth = w_in.shape[0]
    cw = conv_w.shape[2]
    rw = w0.shape[1]
    n_experts = w_router.shape[2]
    decay_lora = w_up.shape[1]
    aaa_lora = a_up.shape[1]
    assert decay_lora + aaa_lora == V7X_LANES
    assert depth == 1
    assert n % MOE_PARTS == 0
    n_part = n // MOE_PARTS
    n_slots = n_part * TOP_K + n_experts * EXPERT_ROWS
    n_blocks = n_slots // EXPERT_ROWS
    n_blocks_padded = -(-n_blocks // V7X_LANES) * V7X_LANES
    seg2 = _group_matrix(rw)

    x2d = x.reshape(n, d)
    for l in range(depth):
        w_up_pad = jnp.concatenate([w_up[l], jnp.zeros((aaa_lora, rw), F32)], axis=0).astype(BF16)
        a_up_pad = jnp.concatenate([jnp.zeros((decay_lora, rw), F32), a_up[l]], axis=0).astype(BF16)
        (yconv, r, lw, k, v, kk, b, gate, bonus) = _mix_prep(
            x2d, seq, _row(norm_mix_g[l]), w_in[l].astype(BF16), conv_w[l].astype(F32),
            _row(conv_norm_g[l]), _row(rwkv_mu[l]), _row(w0[l]), w_up_pad, _row(a0[l]), a_up_pad,
            g_up[l].astype(BF16), _row(k_k[l]), _row(k_a[l]), _row(r_k[l]), seg2)
        shape3 = (n_batch, seq, rw)
        o = _wkv_chunk(r.reshape(shape3), lw.reshape(shape3), k.reshape(shape3),
                       v.reshape(shape3), kk.reshape(shape3), b.reshape(shape3)).reshape(n, rw)
        w_out_b = w_out[l].astype(BF16)
        wr_t = w_router[l].T.astype(F32)
        wr_hi = wr_t.astype(BF16)
        wr_lo = (wr_t - wr_hi.astype(F32)).astype(BF16)
        expert_args = (w_gu[l].astype(F32), b_gu[l].reshape(n_experts, 1, -1).astype(F32),
                       w_down[l].astype(F32), b_down[l].reshape(n_experts, 1, -1).astype(F32))
        parts = []
        for token0 in range(0, n, n_part):
            x1, h2_chunks, top_e, gates = _post_mix(
                token0, n_part, o, bonus, gate, yconv, x2d, _row(gn_w[l]), _row(gn_b[l]),
                w_out_b[:cw], w_out_b[cw:], _row(norm_ffn_g[l]), wr_hi, wr_lo,
                b_router[l].reshape(n_experts, 1).astype(F32), seg2)
            dest, meta = _route(top_e, n_experts, n_blocks_padded)
            slot_chunks = [_sc_scatter_rows(h, dest, n_slots) for h in h2_chunks]
            parts.append((token0, x1, gates, dest, meta, slot_chunks))
        out = None
        for token0, x1, gates, dest, meta, slot_chunks in parts:
            y_chunks = _experts(meta, slot_chunks, *expert_args)
            idx_row = dest.reshape(1, TOP_K * n_part)
            out = _reduce(token0, n, out, [_sc_gather_rows(y, idx_row) for y in y_chunks], x1,
                          gates.T, _row(norm_final_g))
        x2d = out
    return x2d.reshape(n_batch, seq, d)
```

```python
import functools

import jax
import jax.numpy as jnp
from jax import lax
from jax.experimental import pallas as pl
from jax.experimental.pallas import tpu as pltpu
from jax.experimental.pallas import tpu_sc as plsc

F32 = jnp.float32
BF16 = jnp.bfloat16
I32 = jnp.int32

HEAD_DIM = 64
TOP_K = 4
NORM_EPS = 1e-5
GN_EPS = HEAD_DIM * 1e-5
SWIGLU_LIMIT = 7.0
SWIGLU_ALPHA = 1.702

V7X_LANES = 128
V7X_SUBLANES = 8
V7X_VMEM_LIMIT_BYTES = 56 * 1024 * 1024

PREP_ROWS = 256
POST_ROWS = 512
SCAN_FRAMES = 64
SCAN_BATCHES_PER_GROUP = 4
ROUTE_TOKENS = 1024
REDUCE_TOKENS = 256
MOE_PARTS = 2
SC_WINDOW = 128
ROW_CHUNK = 256
META_BLOCK_EXPERT = 0
META_USED_BLOCKS = 1
META_VALID_ROWS = 2
EXPERT_ROWS = 512

def _params(*semantics):
    return pltpu.CompilerParams(dimension_semantics=semantics,
                                vmem_limit_bytes=V7X_VMEM_LIMIT_BYTES)


def _split_bf16(x):
    hi = x.astype(BF16)
    lo = (x - hi.astype(F32)).astype(BF16)
    return hi, lo


def _seg_sum(x, seg2):
    hi, lo = _split_bf16(x)
    return jnp.dot(jnp.concatenate([hi, lo], axis=1), seg2, preferred_element_type=F32)


def _pack_bf16_pairs(x):
    bits = pltpu.bitcast(x.astype(BF16).astype(F32), jnp.uint32)
    half = x.shape[1] // 2
    return bits[:, half:] | (bits[:, :half] >> 16)


def _unpack_bf16_pairs(words):
    return jnp.concatenate([pltpu.bitcast(words << 16, F32),
                            pltpu.bitcast(words & jnp.uint32(0xFFFF0000), F32)], axis=1)


def _rms_rows(x, g):
    return x * lax.rsqrt(jnp.mean(x * x, axis=-1, keepdims=True) + NORM_EPS) * g


def _sigmoid(x):
    return 1.0 / (1.0 + jnp.exp(-x))


def _mix_prep_kernel(blocks_per_seq, cw, rw,
                     xprev_ref, x_ref, g_ref, w_in_ref, conv_w_ref, conv_g_ref, mu_ref,
                     w0_ref, w_up_ref, a0_ref, a_up_ref, g_up_ref, k_k_ref, k_a_ref, r_k_ref,
                     seg2_ref,
                     yconv_ref, r_ref, w_ref, k_ref, v_ref, kk_ref, b_ref, gate_ref, bonus_ref,
                     p_scr, z_scr):
    tm = x_ref.shape[0]
    halo = xprev_ref.shape[0]
    first = (pl.program_id(0) % blocks_per_seq) == 0
    xp = xprev_ref[...] * jnp.where(first, 0.0, 1.0)
    xa = jnp.concatenate([xp, x_ref[...]], axis=0)
    h = _rms_rows(xa, g_ref[...])
    p_scr[...] = jnp.dot(h.astype(BF16), w_in_ref[...], preferred_element_type=F32)
    seg2 = seg2_ref[...]

    z_scr[...] = p_scr[:, 2 * cw:3 * cw] * p_scr[:, 0:cw]
    conv = (conv_w_ref[0:1, :] * z_scr[halo - 2:halo - 2 + tm, :]
            + conv_w_ref[1:2, :] * z_scr[halo - 1:halo - 1 + tm, :]
            + conv_w_ref[2:3, :] * z_scr[halo:halo + tm, :])
    y = p_scr[halo:halo + tm, cw:2 * cw] * conv
    ms = _seg_sum(y * y, seg2) * (1.0 / HEAD_DIM)
    yconv_ref[...] = (y * lax.rsqrt(ms + NORM_EPS) * conv_g_ref[...]).astype(yconv_ref.dtype)

    c0 = 3 * cw
    cur = p_scr[halo:halo + tm, c0:]
    prev = p_scr[halo - 1:halo - 1 + tm, c0:]
    q = cur + (prev - cur) * mu_ref[...]
    r = q[:, 0:rw]
    k = q[:, rw:2 * rw]
    v = q[:, 2 * rw:3 * rw]
    lora_wa = q[:, 3 * rw:3 * rw + V7X_LANES]
    lora_g = q[:, 3 * rw + V7X_LANES:]
    w_lin = w0_ref[...] + jnp.dot(jnp.tanh(lora_wa).astype(BF16), w_up_ref[...],
                                  preferred_element_type=F32)
    neg = -w_lin
    softplus = jnp.maximum(neg, 0.0) + jnp.log(1.0 + jnp.exp(-jnp.abs(neg)))
    log_decay = -jnp.exp(-softplus - 0.5)
    a = _sigmoid(a0_ref[...] + jnp.dot(lora_wa.astype(BF16), a_up_ref[...],
                                       preferred_element_type=F32))
    gate = jnp.dot(_sigmoid(lora_g).astype(BF16), g_up_ref[...], preferred_element_type=F32)
    kk = k * k_k_ref[...]
    kk = kk / jnp.maximum(jnp.sqrt(_seg_sum(kk * kk, seg2)), 1e-12)
    k_mod = k * (1.0 + (a - 1.0) * k_a_ref[...])
    bonus = _seg_sum(r * k_mod * r_k_ref[...], seg2) * v
    r_ref[...] = r
    w_ref[...] = log_decay
    k_ref[...] = k_mod
    v_ref[...] = v
    kk_ref[...] = kk
    b_ref[...] = kk * a
    gate_ref[...] = gate
    bonus_ref[...] = bonus


def _mix_prep(x2d, seq, norm_g, w_in, conv_w, conv_g, mu, w0, w_up, a0, a_up, g_up, k_k, k_a,
              r_k, seg2):
    n, d = x2d.shape
    tm = PREP_ROWS
    halo = V7X_SUBLANES
    cw = conv_w.shape[1]
    rw = w0.shape[1]
    in_cols = w_in.shape[1]
    assert seq % tm == 0 and n % tm == 0
    full = lambda a: pl.BlockSpec(a.shape, lambda i: (0,) * a.ndim)
    row_spec = lambda c: pl.BlockSpec((tm, c), lambda i: (i, 0))
    consts = (norm_g, w_in, conv_w, conv_g, mu, w0, w_up, a0, a_up, g_up, k_k, k_a, r_k, seg2)
    outs = [jax.ShapeDtypeStruct((n, cw), BF16)] + [jax.ShapeDtypeStruct((n, rw), F32)] * 8
    return pl.pallas_call(
        functools.partial(_mix_prep_kernel, seq // tm, cw, rw),
        grid=(n // tm,),
        in_specs=[pl.BlockSpec((halo, d), lambda i: (jnp.maximum(i * (tm // halo) - 1, 0), 0)),
                  row_spec(d)] + [full(c) for c in consts],
        out_specs=[row_spec(cw)] + [row_spec(rw)] * 8,
        out_shape=outs,
        scratch_shapes=[pltpu.VMEM((halo + tm, in_cols), F32), pltpu.VMEM((halo + tm, cw), F32)],
        compiler_params=_params("parallel"),
        name="mix_prep",
    )(x2d, x2d, *consts)


def _bdot(a, b):
    return jnp.dot(a.astype(BF16), b.astype(BF16), preferred_element_type=F32)


def _bdot_nt(a, b):
    return lax.dot_general(a.astype(BF16), b.astype(BF16), (((1,), (1,)), ((), ())),
                           preferred_element_type=F32)


def _wkv_chunk_kernel(n_batch, n_pairs,
                      r_ref, lw_ref, k_ref, v_ref, kk_ref, b_ref, o_ref, state_ref):
    frames = r_ref.shape[1]
    lanes = V7X_LANES
    hd = HEAD_DIM
    assert frames == hd

    @pl.when(pl.program_id(0) == 0)
    def _():
        state_ref[...] = jnp.zeros_like(state_ref)

    row = lax.broadcasted_iota(I32, (lanes, lanes), 0)
    lane = lax.broadcasted_iota(I32, (lanes, lanes), 1)
    same_head = (row // hd) == (lane // hd)
    rt = row % hd
    ls = lane % hd
    strict_same = same_head & (rt > ls)
    strict_cross = jnp.logical_not(same_head) & (rt > ls)
    incl_same = same_head & (rt >= ls)
    incl_cross = jnp.logical_not(same_head) & (rt >= ls)
    eye = row == lane
    level_masks = []
    m = 1
    while m < hd:
        level_masks.append(same_head & ((rt // (2 * m)) == (ls // (2 * m)))
                           & (((rt // m) % 2) == 1) & (((ls // m) % 2) == 0))
        m *= 2
    left = lax.broadcasted_iota(I32, (frames, lanes), 1) < hd
    tri = (lax.broadcasted_iota(I32, (frames, frames), 0)
           >= lax.broadcasted_iota(I32, (frames, frames), 1)).astype(BF16)

    def diag_blocks(x):
        return jnp.concatenate([jnp.where(left, x, 0.0), jnp.where(left, 0.0, x)], axis=0)

    def cross_blocks(x):
        return jnp.concatenate([jnp.where(left, 0.0, x), jnp.where(left, x, 0.0)], axis=0)

    def batch_prep(bi):
        lw = lw_ref[bi]
        hi = lw.astype(BF16)
        rem = lw - hi.astype(F32)
        mid = rem.astype(BF16)
        lo = (rem - mid.astype(F32)).astype(BF16)
        cs = (jnp.dot(tri, hi, preferred_element_type=F32)
              + jnp.dot(tri, mid, preferred_element_type=F32)
              + jnp.dot(tri, lo, preferred_element_type=F32))
        cs_end = cs[frames - 1:frames, :]
        e_neg = jnp.exp(-cs)
        e_end = jnp.exp(cs_end - cs)
        b_in = b_ref[bi]
        k_in = k_ref[bi]
        return dict(a_t=-kk_ref[bi] * jnp.exp(cs - lw), b_t=b_in * e_neg, k_t=k_in * e_neg,
                    r_t=r_ref[bi] * jnp.exp(cs), b_h=b_in * e_end, k_h=k_in * e_end,
                    v=v_ref[bi], g_end=jnp.exp(cs_end))

    for b0 in range(0, n_batch, SCAN_BATCHES_PER_GROUP):
        group = []
        for bi in range(b0, min(b0 + SCAN_BATCHES_PER_GROUP, n_batch)):
            prep = batch_prep(bi)
            for pi in range(n_pairs):
                sl = slice(pi * lanes, (pi + 1) * lanes)
                group.append((bi, sl, bi * n_pairs + pi, {n: x[:, sl] for n, x in prep.items()}))
        a_bd = [diag_blocks(p["a_t"]) for _, _, _, p in group]
        r_bd = [diag_blocks(p["r_t"]) for _, _, _, p in group]
        d0 = [_bdot_nt(jnp.concatenate([a[:hd], r[:hd]], axis=0),
                       jnp.concatenate([p["b_t"], p["k_t"]], axis=0))
              for a, r, (_, _, _, p) in zip(a_bd, r_bd, group)]
        d1 = [_bdot_nt(jnp.concatenate([a[hd:], r[hd:]], axis=0),
                       jnp.concatenate([p["k_t"], p["b_t"]], axis=0))
              for a, r, (_, _, _, p) in zip(a_bd, r_bd, group)]
        a_rows = [jnp.concatenate([x[:hd], y[:hd]], axis=0) for x, y in zip(d0, d1)]
        m_rows = [jnp.concatenate([x[hd:], y[hd:]], axis=0) for x, y in zip(d0, d1)]
        a_ab = [jnp.where(strict_same, x, 0.0) for x in a_rows]
        a_ak = [jnp.where(strict_cross, x, 0.0) for x in a_rows]
        m_rb = [jnp.where(incl_same, x, 0.0) for x in m_rows]
        m_rk = [jnp.where(incl_cross, x, 0.0) for x in m_rows]
        t_inv = [jnp.where(eye, 1.0, jnp.where(level_masks[0], x, 0.0)) for x in a_ab]
        for mask in level_masks[1:]:
            half = [_bdot(t, jnp.where(mask, x, 0.0)) for t, x in zip(t_inv, a_ab)]
            t_inv = [t + _bdot(h, t) for t, h in zip(t_inv, half)]
        v_x = [cross_blocks(p["v"]) for _, _, _, p in group]
        akv = [_bdot(x, v) for x, v in zip(a_ak, v_x)]
        wu = [_bdot(t, jnp.concatenate([a, x], axis=1))
              for t, a, x in zip(t_inv, a_bd, akv)]
        bh_t = [diag_blocks(p["b_h"]).T for _, _, _, p in group]
        kh_t = [cross_blocks(p["k_h"]).T for _, _, _, p in group]
        pw_rw = [_bdot(jnp.concatenate([bt, m], axis=0), x[:, :lanes])
                 for bt, m, x in zip(bh_t, m_rb, wu)]
        q_o = [_bdot(jnp.concatenate([jnp.concatenate([bt, kt], axis=1),
                                      jnp.concatenate([mb, mk], axis=1)], axis=0),
                     jnp.concatenate([x[:, lanes:], v], axis=0))
               for bt, kt, mb, mk, x, v in zip(bh_t, kh_t, m_rb, m_rk, wu, v_x)]
        res = [_bdot(jnp.concatenate([r + pr[lanes:],
                                      pr[:lanes] + jnp.where(eye, p["g_end"], 0.0)], axis=0),
                     state_ref[idx])
               for r, pr, (_, _, idx, p) in zip(r_bd, pw_rw, group)]
        for x, q, (bi, sl, idx, _) in zip(res, q_o, group):
            o_bd = x[:lanes] + q[lanes:]
            state_ref[idx] = x[lanes:] + q[:lanes]
            o_ref[bi, :, sl] = jnp.where(left, o_bd[:hd], o_bd[hd:])


def _wkv_chunk(r, lw, k, v, kk, b):
    n_batch, seq, rw = r.shape
    n_pairs = rw // V7X_LANES
    frames = SCAN_FRAMES
    assert seq % frames == 0
    spec = pl.BlockSpec((n_batch, frames, rw), lambda c: (0, c, 0))
    return pl.pallas_call(
        functools.partial(_wkv_chunk_kernel, n_batch, n_pairs),
        grid=(seq // frames,),
        in_specs=[spec] * 6,
        out_specs=spec,
        out_shape=jax.ShapeDtypeStruct((n_batch, seq, rw), F32),
        scratch_shapes=[pltpu.VMEM((n_batch * n_pairs, V7X_LANES, V7X_LANES), F32)],
        compiler_params=_params("arbitrary"),
        name="wkv_chunk",
    )(r, lw, k, v, kk, b)


def _post_mix_kernel(n_experts,
                     o_ref, bonus_ref, gate_ref, yconv_ref, x_ref, gn_w_ref, gn_b_ref,
                     w_out_c_ref, w_out_r_ref, ffn_g_ref, wr_hi_ref, wr_lo_ref, b_router_ref,
                     seg2_ref,
                     x1_ref, *outs):
    h2_refs, (top_e_ref, gates_ref) = outs[:-2], outs[-2:]
    seg2 = seg2_ref[...]
    o = o_ref[...]
    mean = _seg_sum(o, seg2) * (1.0 / HEAD_DIM)
    cen = o - mean
    var = _seg_sum(cen * cen, seg2) * (1.0 / HEAD_DIM)
    o = cen * lax.rsqrt(var + GN_EPS) * gn_w_ref[...] + gn_b_ref[...]
    y_rwkv = (o + bonus_ref[...]) * gate_ref[...]
    x1 = (x_ref[...]
          + jnp.dot(yconv_ref[...], w_out_c_ref[...], preferred_element_type=F32)
          + jnp.dot(y_rwkv.astype(BF16), w_out_r_ref[...], preferred_element_type=F32))
    x1_ref[...] = x1
    h2 = _rms_rows(x1, ffn_g_ref[...])
    words = _pack_bf16_pairs(h2)
    for c, h2_ref in enumerate(h2_refs):
        h2_ref[...] = words[:, c * ROW_CHUNK:(c + 1) * ROW_CHUNK]

    h_hi, h_lo = _split_bf16(h2)
    nt = (((1,), (1,)), ((), ()))
    logits = (lax.dot_general(wr_hi_ref[...], h_hi, nt, preferred_element_type=F32)
              + lax.dot_general(wr_hi_ref[...], h_lo, nt, preferred_element_type=F32)
              + lax.dot_general(wr_lo_ref[...], h_hi, nt, preferred_element_type=F32)
              + b_router_ref[...])
    e_id = lax.broadcasted_iota(I32, logits.shape, 0).astype(F32)
    work = logits
    tops, ids = [], []
    for _ in range(TOP_K):
        m = jnp.max(work, axis=0, keepdims=True)
        sel = jnp.min(jnp.where(work == m, e_id, float(n_experts)), axis=0, keepdims=True)
        tops.append(m)
        ids.append(sel)
        work = jnp.where(e_id == sel, -jnp.inf, work)
    ex = [jnp.exp(t - tops[0]) for t in tops]
    denom = ex[0] + ex[1] + ex[2] + ex[3]
    top_e_ref[...] = jnp.concatenate(ids, axis=0).astype(I32)
    gates_ref[...] = jnp.concatenate([e / denom for e in ex], axis=0)


def _post_mix(token0, n, o, bonus, gate, yconv, x2d, gn_w, gn_b, w_out_c, w_out_r, ffn_g, wr_hi,
              wr_lo, b_router, seg2):
    d = x2d.shape[1]
    rw = o.shape[1]
    cw = yconv.shape[1]
    n_experts = wr_hi.shape[0]
    tm = POST_ROWS
    assert token0 % tm == 0 and n % tm == 0
    full = lambda a: pl.BlockSpec(a.shape, lambda i: (0,) * a.ndim)
    in_spec = lambda c: pl.BlockSpec((tm, c), lambda i: (i + token0 // tm, 0))
    row_spec = lambda c: pl.BlockSpec((tm, c), lambda i: (i, 0))
    col_spec = pl.BlockSpec((TOP_K, tm), lambda i: (0, i))
    consts = (gn_w, gn_b, w_out_c, w_out_r, ffn_g, wr_hi, wr_lo, b_router, seg2)
    n_chunks = d // 2 // ROW_CHUNK
    outs = pl.pallas_call(
        functools.partial(_post_mix_kernel, n_experts),
        grid=(n // tm,),
        in_specs=[in_spec(rw), in_spec(rw), in_spec(rw), in_spec(cw), in_spec(d)]
                 + [full(c) for c in consts],
        out_specs=[row_spec(d)] + [row_spec(ROW_CHUNK)] * n_chunks + [col_spec, col_spec],
        out_shape=[jax.ShapeDtypeStruct((n, d), F32)]
                  + [jax.ShapeDtypeStruct((n, ROW_CHUNK), jnp.uint32)] * n_chunks
                  + [jax.ShapeDtypeStruct((TOP_K, n), I32), jax.ShapeDtypeStruct((TOP_K, n), F32)],
        compiler_params=_params("parallel"),
        name="post_mix",
    )(o, bonus, gate, yconv, x2d, *consts)
    return outs[0], outs[1:1 + n_chunks], outs[-2], outs[-1]


def _route_kernel(n_experts, block_rows,
                  top_e_ref, tri_ref, dest_ref, meta_ref,
                  count_ref, start_ref, carry_ref):
    phase = pl.program_id(0)
    j = pl.program_id(1)
    tb = top_e_ref.shape[1]
    e_id = lax.broadcasted_iota(I32, (n_experts, tb), 0)
    top_e = top_e_ref[...]
    onehot = jnp.zeros((n_experts, tb), F32)
    for c in range(TOP_K):
        onehot = onehot + jnp.where(top_e[c:c + 1, :] == e_id, 1.0, 0.0)
    block_count = jnp.sum(onehot, axis=1, keepdims=True)

    @pl.when((phase == 0) & (j == 0))
    def _():
        count_ref[...] = jnp.zeros_like(count_ref)

    @pl.when(phase == 0)
    def _():
        count_ref[...] += block_count

    @pl.when((phase == 1) & (j == 0))
    def _():
        counts = count_ref[...]
        padded = jnp.ceil(counts * (1.0 / block_rows)) * block_rows
        sub = lax.broadcasted_iota(I32, (n_experts, n_experts), 0)
        lane = lax.broadcasted_iota(I32, (n_experts, n_experts), 1)
        padded_row = jnp.sum(jnp.where(sub == lane, padded, 0.0), axis=0, keepdims=True)
        start = jnp.sum(jnp.where(lane < sub, padded_row, 0.0), axis=1, keepdims=True)
        start_ref[...] = start
        carry_ref[...] = jnp.zeros_like(carry_ref)
        end = start + padded
        nb = meta_ref.shape[1]
        slot0 = (lax.broadcasted_iota(I32, (n_experts, nb), 1) * block_rows).astype(F32)
        block_e = jnp.sum(jnp.where(end <= slot0, 1.0, 0.0), axis=0, keepdims=True)
        block_e = jnp.minimum(block_e, n_experts - 1.0)
        used = jnp.max(end, axis=0, keepdims=True) * (1.0 / block_rows)
        in_region = (start <= slot0) & (slot0 < end)
        valid = jnp.sum(jnp.where(in_region, jnp.minimum(start + counts - slot0, block_rows), 0.0),
                        axis=0, keepdims=True)
        row = lax.broadcasted_iota(I32, meta_ref.shape, 0)
        meta_ref[...] = jnp.where(row == META_BLOCK_EXPERT, block_e,
                                  jnp.where(row == META_VALID_ROWS, valid, used)).astype(I32)

    @pl.when(phase == 1)
    def _():
        incl = jnp.dot(onehot.astype(BF16), tri_ref[...], preferred_element_type=F32)
        base = incl - onehot + carry_ref[...] + start_ref[...]
        rows = [jnp.sum(jnp.where(top_e[c:c + 1, :] == e_id, base, 0.0), axis=0, keepdims=True)
                for c in range(TOP_K)]
        dest_ref[...] = jnp.concatenate(rows, axis=0).astype(I32)
        carry_ref[...] += block_count


def _route(top_e, n_experts, n_blocks_padded):
    n = top_e.shape[1]
    tb = ROUTE_TOKENS
    assert n % tb == 0
    tri = (lax.broadcasted_iota(I32, (tb, tb), 0) <= lax.broadcasted_iota(I32, (tb, tb), 1)
           ).astype(BF16)
    return pl.pallas_call(
        functools.partial(_route_kernel, n_experts, EXPERT_ROWS),
        grid=(2, n // tb),
        in_specs=[pl.BlockSpec((TOP_K, tb), lambda ph, j: (0, j)),
                  pl.BlockSpec((tb, tb), lambda ph, j: (0, 0))],
        out_specs=[pl.BlockSpec((TOP_K, tb), lambda ph, j: (0, j * ph)),
                   pl.BlockSpec((V7X_SUBLANES, n_blocks_padded), lambda ph, j: (0, 0))],
        out_shape=[jax.ShapeDtypeStruct((TOP_K, n), I32),
                   jax.ShapeDtypeStruct((V7X_SUBLANES, n_blocks_padded), I32)],
        scratch_shapes=[pltpu.VMEM((n_experts, 1), F32)] * 3,
        compiler_params=_params("arbitrary", "arbitrary"),
        name="route",
    )(top_e, tri)


def _sc_mesh():
    return plsc.VectorSubcoreMesh(core_axis_name="core", subcore_axis_name="subcore")


def _sc_scatter_rows(src, dest, n_slots):
    n, width = src.shape
    assert n % SC_WINDOW == 0

    @functools.partial(pl.kernel, out_type=jax.ShapeDtypeStruct((n_slots, width), src.dtype),
                       mesh=_sc_mesh(), name="sc_scatter_rows")
    def scatter(src_hbm, dest_hbm, out_hbm):
        def body(src_vmem, dest_vmem):
            for c in range(TOP_K):
                pltpu.sync_copy(src_vmem, out_hbm.at[dest_vmem.at[c]])

        pltpu.emit_pipeline(
            body,
            grid=(n // SC_WINDOW,),
            in_specs=[pl.BlockSpec((SC_WINDOW, width), lambda i: (i, 0)),
                      pl.BlockSpec((TOP_K, SC_WINDOW), lambda i: (0, i))],
            out_specs=[],
            core_axis_name=("core", "subcore"),
            dimension_semantics=(pltpu.PARALLEL,),
        )(src_hbm, dest_hbm)

    return scatter(src, dest)


def _experts_kernel(d_ff, n_in, n_out, meta_ref, *refs):
    x_refs, refs = refs[:n_in], refs[n_in:]
    w_gu_ref, b_gu_ref, w_down_ref, b_down_ref = refs[:4]
    y_refs = refs[4:4 + n_out]
    w_gu_bf16, w_down_bf16 = refs[4 + n_out:]
    j = pl.program_id(0)
    used = meta_ref[META_USED_BLOCKS, 0]
    expert = meta_ref[META_BLOCK_EXPERT, j]
    prev_expert = meta_ref[META_BLOCK_EXPERT, jnp.maximum(j - 1, 0)]

    @pl.when((j < used) & ((j == 0) | (expert != prev_expert)))
    def _():
        w_gu_bf16[...] = w_gu_ref[0].astype(BF16)
        w_down_bf16[...] = w_down_ref[0].astype(BF16)

    @pl.when(j < used)
    def _():
        x = _unpack_bf16_pairs(jnp.concatenate([r[...] for r in x_refs], axis=1))
        row = lax.broadcasted_iota(I32, x.shape, 0)
        x = jnp.where(row < meta_ref[META_VALID_ROWS, j], x, 0.0)
        gu = jnp.dot(x.astype(BF16), w_gu_bf16[...], preferred_element_type=F32) + b_gu_ref[0]
        gate = jnp.minimum(gu[:, :d_ff], SWIGLU_LIMIT)
        up = jnp.clip(gu[:, d_ff:], -SWIGLU_LIMIT, SWIGLU_LIMIT)
        act = (up + 1.0) * (gate * _sigmoid(SWIGLU_ALPHA * gate))
        y = jnp.dot(act.astype(BF16), w_down_bf16[...], preferred_element_type=F32) + b_down_ref[0]
        words = _pack_bf16_pairs(y)
        for c, y_ref in enumerate(y_refs):
            y_ref[...] = words[:, c * ROW_CHUNK:(c + 1) * ROW_CHUNK]

    @pl.when(j >= used)
    def _():
        for y_ref in y_refs:
            y_ref[...] = jnp.zeros_like(y_ref)


def _experts(meta, slot_chunks, w_gu, b_gu, w_down, b_down):
    n_in = len(slot_chunks)
    n_slots = slot_chunks[0].shape[0]
    n_experts, d, two_ff = w_gu.shape
    n_out = d // 2 // ROW_CHUNK
    d_ff = two_ff // 2
    bm = EXPERT_ROWS
    grid_spec = pltpu.PrefetchScalarGridSpec(
        num_scalar_prefetch=1,
        grid=(n_slots // bm,),
        in_specs=[pl.BlockSpec((bm, ROW_CHUNK),
                               lambda j, m: (jnp.minimum(j, m[META_USED_BLOCKS, 0] - 1), 0))
                  ] * n_in
                 + [pl.BlockSpec((1, d, two_ff), lambda j, m: (m[META_BLOCK_EXPERT, j], 0, 0)),
                    pl.BlockSpec((1, 1, two_ff), lambda j, m: (m[META_BLOCK_EXPERT, j], 0, 0)),
                    pl.BlockSpec((1, d_ff, d), lambda j, m: (m[META_BLOCK_EXPERT, j], 0, 0)),
                    pl.BlockSpec((1, 1, d), lambda j, m: (m[META_BLOCK_EXPERT, j], 0, 0))],
        out_specs=[pl.BlockSpec((bm, ROW_CHUNK), lambda j, m: (j, 0))] * n_out,
        scratch_shapes=[pltpu.VMEM((d, two_ff), BF16), pltpu.VMEM((d_ff, d), BF16)],
    )
    return pl.pallas_call(
        functools.partial(_experts_kernel, d_ff, n_in, n_out),
        grid_spec=grid_spec,
        out_shape=[jax.ShapeDtypeStruct((n_slots, ROW_CHUNK), jnp.uint32)] * n_out,
        compiler_params=_params("arbitrary"),
        name="experts",
    )(meta, *slot_chunks, w_gu, b_gu, w_down, b_down)


def _sc_gather_rows(table, idx_row):
    n_idx = idx_row.shape[1]
    width = table.shape[1]
    assert n_idx % SC_WINDOW == 0

    @functools.partial(pl.kernel, out_type=jax.ShapeDtypeStruct((n_idx, width), table.dtype),
                       mesh=_sc_mesh(), name="sc_gather_rows")
    def gather(table_hbm, idx_hbm, out_hbm):
        def body(idx_vmem, out_vmem):
            pltpu.sync_copy(table_hbm.at[idx_vmem.at[0]], out_vmem)

        pltpu.emit_pipeline(
            body,
            grid=(n_idx // SC_WINDOW,),
            in_specs=[pl.BlockSpec((1, SC_WINDOW), lambda i: (0, i))],
            out_specs=[pl.BlockSpec((SC_WINDOW, width), lambda i: (i, 0))],
            core_axis_name=("core", "subcore"),
            dimension_semantics=(pltpu.PARALLEL,),
        )(idx_hbm, out_hbm)

    return gather(table, idx_row)


def _reduce_kernel(n_chunks, *refs):
    y_refs = refs[:n_chunks]
    x1_ref, gates_ref, g_ref = refs[n_chunks:n_chunks + 3]
    out_ref = refs[-1]
    gates = gates_ref[...]
    acc = x1_ref[...]
    for c in range(TOP_K):
        y = _unpack_bf16_pairs(jnp.concatenate([r[c] for r in y_refs], axis=1))
        acc = acc + y * gates[:, c:c + 1]
    out_ref[...] = _rms_rows(acc, g_ref[...])


def _reduce(token0, n_total, out_so_far, y_chunks, x1, gates_t, final_g):
    n, d = x1.shape
    tb = REDUCE_TOKENS
    n_chunks = len(y_chunks)
    assert token0 % tb == 0
    in_specs = ([pl.BlockSpec((TOP_K, tb, ROW_CHUNK), lambda i: (0, i, 0))] * n_chunks
                + [pl.BlockSpec((tb, d), lambda i: (i, 0)),
                   pl.BlockSpec((tb, TOP_K), lambda i: (i, 0)),
                   pl.BlockSpec((1, d), lambda i: (0, 0))])
    args = [y.reshape(TOP_K, n, ROW_CHUNK) for y in y_chunks] + [x1, gates_t, final_g]
    aliases = {}
    if out_so_far is not None:
        in_specs.append(pl.BlockSpec(memory_space=pl.ANY))
        args.append(out_so_far)
        aliases = {len(args) - 1: 0}
    return pl.pallas_call(
        functools.partial(_reduce_kernel, n_chunks),
        grid=(n // tb,),
        in_specs=in_specs,
        out_specs=pl.BlockSpec((tb, d), lambda i: (i + token0 // tb, 0)),
        out_shape=jax.ShapeDtypeStruct((n_total, d), F32),
        input_output_aliases=aliases,
        compiler_params=_params("parallel"),
        name="reduce",
    )(*args)


def _group_matrix(width):
    a = lax.broadcasted_iota(I32, (width, width), 0) // HEAD_DIM
    b = lax.broadcasted_iota(I32, (width, width), 1) // HEAD_DIM
    g = (a == b).astype(BF16)
    return jnp.concatenate([g, g], axis=0)


def _row(vec):
    return vec.reshape(1, -1).astype(F32)


def kernel(x, w_in, conv_w, conv_norm_g, rwkv_mu, w0, w_up, a0, a_up, g_up, k_k, k_a, r_k,
           gn_w, gn_b, w_out, norm_mix_g, norm_ffn_g, w_router, b_router, w_gu, b_gu, w_down,
           b_down, norm_final_g):
    n_batch, seq, d = x.shape
    n = n_batch * seq
    depth = w_in.shape[0]
    cw = conv_w.shape[2]
    rw = w0.shape[1]
    n_experts = w_router.shape[2]
    decay_lora = w_up.shape[1]
    aaa_lora = a_up.shape[1]
    assert decay_lora + aaa_lora == V7X_LANES
    assert depth == 1
    assert n % MOE_PARTS == 0
    n_part = n // MOE_PARTS
    n_slots = n_part * TOP_K + n_experts * EXPERT_ROWS
    n_blocks = n_slots // EXPERT_ROWS
    n_blocks_padded = -(-n_blocks // V7X_LANES) * V7X_LANES
    seg2 = _group_matrix(rw)

    x2d = x.reshape(n, d)
    for l in range(depth):
        w_up_pad = jnp.concatenate([w_up[l], jnp.zeros((aaa_lora, rw), F32)], axis=0).astype(BF16)
        a_up_pad = jnp.concatenate([jnp.zeros((decay_lora, rw), F32), a_up[l]], axis=0).astype(BF16)
        (yconv, r, lw, k, v, kk, b, gate, bonus) = _mix_prep(
            x2d, seq, _row(norm_mix_g[l]), w_in[l].astype(BF16), conv_w[l].astype(F32),
            _row(conv_norm_g[l]), _row(rwkv_mu[l]), _row(w0[l]), w_up_pad, _row(a0[l]), a_up_pad,
            g_up[l].astype(BF16), _row(k_k[l]), _row(k_a[l]), _row(r_k[l]), seg2)
        shape3 = (n_batch, seq, rw)
        o = _wkv_chunk(r.reshape(shape3), lw.reshape(shape3), k.reshape(shape3),
                       v.reshape(shape3), kk.reshape(shape3), b.reshape(shape3)).reshape(n, rw)
        w_out_b = w_out[l].astype(BF16)
        wr_t = w_router[l].T.astype(F32)
        wr_hi = wr_t.astype(BF16)
        wr_lo = (wr_t - wr_hi.astype(F32)).astype(BF16)
        expert_args = (w_gu[l].astype(F32), b_gu[l].reshape(n_experts, 1, -1).astype(F32),
                       w_down[l].astype(F32), b_down[l].reshape(n_experts, 1, -1).astype(F32))
        parts = []
        for token0 in range(0, n, n_part):
            x1, h2_chunks, top_e, gates = _post_mix(
                token0, n_part, o, bonus, gate, yconv, x2d, _row(gn_w[l]), _row(gn_b[l]),
                w_out_b[:cw], w_out_b[cw:], _row(norm_ffn_g[l]), wr_hi, wr_lo,
                b_router[l].reshape(n_experts, 1).astype(F32), seg2)
            dest, meta = _route(top_e, n_experts, n_blocks_padded)
            slot_chunks = [_sc_scatter_rows(h, dest, n_slots) for h in h2_chunks]
            parts.append((token0, x1, gates, dest, meta, slot_chunks))
        out = None
        for token0, x1, gates, dest, meta, slot_chunks in parts:
            y_chunks = _experts(meta, slot_chunks, *expert_args)
            idx_row = dest.reshape(1, TOP_K * n_part)
            out = _reduce(token0, n, out, [_sc_gather_rows(y, idx_row) for y in y_chunks], x1,
                          gates.T, _row(norm_final_g))
        x2d = out
    return x2d.reshape(n_batch, seq, d)
```

```python
import functools

import jax
import jax.numpy as jnp
from jax import lax
from jax.experimental import pallas as pl
from jax.experimental.pallas import tpu as pltpu
from jax.experimental.pallas import tpu_sc as plsc

F32 = jnp.float32
BF16 = jnp.bfloat16
I32 = jnp.int32

HEAD_DIM = 64
TOP_K = 4
NORM_EPS = 1e-5
GN_EPS = HEAD_DIM * 1e-5
SWIGLU_LIMIT = 7.0
SWIGLU_ALPHA = 1.702

V7X_LANES = 128
V7X_SUBLANES = 8
V7X_VMEM_LIMIT_BYTES = 56 * 1024 * 1024

PREP_ROWS = 256
POST_ROWS = 512
SCAN_FRAMES = 64
ROUTE_TOKENS = 1024
REDUCE_TOKENS = 256
MOE_PARTS = 2
SC_WINDOW = 128
ROW_CHUNK = 256
META_BLOCK_EXPERT = 0
META_USED_BLOCKS = 1
META_VALID_ROWS = 2
EXPERT_ROWS = 512

def _params(*semantics):
    return pltpu.CompilerParams(dimension_semantics=semantics,
                                vmem_limit_bytes=V7X_VMEM_LIMIT_BYTES)


def _split_bf16(x):
    hi = x.astype(BF16)
    lo = (x - hi.astype(F32)).astype(BF16)
    return hi, lo


def _seg_sum(x, seg2):
    hi, lo = _split_bf16(x)
    return jnp.dot(jnp.concatenate([hi, lo], axis=1), seg2, preferred_element_type=F32)


def _pack_bf16_pairs(x):
    bits = pltpu.bitcast(x.astype(BF16).astype(F32), jnp.uint32)
    half = x.shape[1] // 2
    return bits[:, half:] | (bits[:, :half] >> 16)


def _unpack_bf16_pairs(words):
    return jnp.concatenate([pltpu.bitcast(words << 16, F32),
                            pltpu.bitcast(words & jnp.uint32(0xFFFF0000), F32)], axis=1)


def _rms_rows(x, g):
    return x * lax.rsqrt(jnp.mean(x * x, axis=-1, keepdims=True) + NORM_EPS) * g


def _sigmoid(x):
    return 1.0 / (1.0 + jnp.exp(-x))


def _mix_prep_kernel(blocks_per_seq, cw, rw,
                     xprev_ref, x_ref, g_ref, w_in_ref, conv_w_ref, conv_g_ref, mu_ref,
                     w0_ref, w_up_ref, a0_ref, a_up_ref, g_up_ref, k_k_ref, k_a_ref, r_k_ref,
                     seg2_ref,
                     yconv_ref, r_ref, w_ref, k_ref, v_ref, kk_ref, b_ref, gate_ref, bonus_ref,
                     p_scr, z_scr):
    tm = x_ref.shape[0]
    halo = xprev_ref.shape[0]
    first = (pl.program_id(0) % blocks_per_seq) == 0
    xp = xprev_ref[...] * jnp.where(first, 0.0, 1.0)
    xa = jnp.concatenate([xp, x_ref[...]], axis=0)
    h = _rms_rows(xa, g_ref[...])
    p_scr[...] = jnp.dot(h.astype(BF16), w_in_ref[...], preferred_element_type=F32)
    seg2 = seg2_ref[...]

    z_scr[...] = p_scr[:, 2 * cw:3 * cw] * p_scr[:, 0:cw]
    conv = (conv_w_ref[0:1, :] * z_scr[halo - 2:halo - 2 + tm, :]
            + conv_w_ref[1:2, :] * z_scr[halo - 1:halo - 1 + tm, :]
            + conv_w_ref[2:3, :] * z_scr[halo:halo + tm, :])
    y = p_scr[halo:halo + tm, cw:2 * cw] * conv
    ms = _seg_sum(y * y, seg2) * (1.0 / HEAD_DIM)
    yconv_ref[...] = (y * lax.rsqrt(ms + NORM_EPS) * conv_g_ref[...]).astype(yconv_ref.dtype)

    c0 = 3 * cw
    cur = p_scr[halo:halo + tm, c0:]
    prev = p_scr[halo - 1:halo - 1 + tm, c0:]
    q = cur + (prev - cur) * mu_ref[...]
    r = q[:, 0:rw]
    k = q[:, rw:2 * rw]
    v = q[:, 2 * rw:3 * rw]
    lora_wa = q[:, 3 * rw:3 * rw + V7X_LANES]
    lora_g = q[:, 3 * rw + V7X_LANES:]
    w_lin = w0_ref[...] + jnp.dot(jnp.tanh(lora_wa).astype(BF16), w_up_ref[...],
                                  preferred_element_type=F32)
    neg = -w_lin
    softplus = jnp.maximum(neg, 0.0) + jnp.log(1.0 + jnp.exp(-jnp.abs(neg)))
    log_decay = -jnp.exp(-softplus - 0.5)
    a = _sigmoid(a0_ref[...] + jnp.dot(lora_wa.astype(BF16), a_up_ref[...],
                                       preferred_element_type=F32))
    gate = jnp.dot(_sigmoid(lora_g).astype(BF16), g_up_ref[...], preferred_element_type=F32)
    kk = k * k_k_ref[...]
    kk = kk / jnp.maximum(jnp.sqrt(_seg_sum(kk * kk, seg2)), 1e-12)
    k_mod = k * (1.0 + (a - 1.0) * k_a_ref[...])
    bonus = _seg_sum(r * k_mod * r_k_ref[...], seg2) * v
    r_ref[...] = r
    w_ref[...] = log_decay
    k_ref[...] = k_mod
    v_ref[...] = v
    kk_ref[...] = kk
    b_ref[...] = kk * a
    gate_ref[...] = gate
    bonus_ref[...] = bonus


def _mix_prep(x2d, seq, norm_g, w_in, conv_w, conv_g, mu, w0, w_up, a0, a_up, g_up, k_k, k_a,
              r_k, seg2):
    n, d = x2d.shape
    tm = PREP_ROWS
    halo = V7X_SUBLANES
    cw = conv_w.shape[1]
    rw = w0.shape[1]
    in_cols = w_in.shape[1]
    assert seq % tm == 0 and n % tm == 0
    full = lambda a: pl.BlockSpec(a.shape, lambda i: (0,) * a.ndim)
    row_spec = lambda c: pl.BlockSpec((tm, c), lambda i: (i, 0))
    consts = (norm_g, w_in, conv_w, conv_g, mu, w0, w_up, a0, a_up, g_up, k_k, k_a, r_k, seg2)
    outs = [jax.ShapeDtypeStruct((n, cw), BF16)] + [jax.ShapeDtypeStruct((n, rw), F32)] * 8
    return pl.pallas_call(
        functools.partial(_mix_prep_kernel, seq // tm, cw, rw),
        grid=(n // tm,),
        in_specs=[pl.BlockSpec((halo, d), lambda i: (jnp.maximum(i * (tm // halo) - 1, 0), 0)),
                  row_spec(d)] + [full(c) for c in consts],
        out_specs=[row_spec(cw)] + [row_spec(rw)] * 8,
        out_shape=outs,
        scratch_shapes=[pltpu.VMEM((halo + tm, in_cols), F32), pltpu.VMEM((halo + tm, cw), F32)],
        compiler_params=_params("parallel"),
        name="mix_prep",
    )(x2d, x2d, *consts)


def _bdot(a, b):
    return jnp.dot(a.astype(BF16), b.astype(BF16), preferred_element_type=F32)


def _bdot_nt(a, b):
    return lax.dot_general(a.astype(BF16), b.astype(BF16), (((1,), (1,)), ((), ())),
                           preferred_element_type=F32)


def _wkv_chunk_kernel(n_batch, n_pairs,
                      r_ref, lw_ref, k_ref, v_ref, kk_ref, b_ref, o_ref, state_ref):
    frames = r_ref.shape[1]
    lanes = V7X_LANES
    hd = HEAD_DIM
    assert frames == hd

    @pl.when(pl.program_id(0) == 0)
    def _():
        state_ref[...] = jnp.zeros_like(state_ref)

    row = lax.broadcasted_iota(I32, (lanes, lanes), 0)
    lane = lax.broadcasted_iota(I32, (lanes, lanes), 1)
    same_head = (row // hd) == (lane // hd)
    rt = row % hd
    ls = lane % hd
    strict_same = same_head & (rt > ls)
    strict_cross = jnp.logical_not(same_head) & (rt > ls)
    incl_same = same_head & (rt >= ls)
    incl_cross = jnp.logical_not(same_head) & (rt >= ls)
    eye = row == lane
    level_masks = []
    m = 1
    while m < hd:
        level_masks.append(same_head & ((rt // (2 * m)) == (ls // (2 * m)))
                           & (((rt // m) % 2) == 1) & (((ls // m) % 2) == 0))
        m *= 2
    left = lax.broadcasted_iota(I32, (frames, lanes), 1) < hd
    tri = (lax.broadcasted_iota(I32, (frames, frames), 0)
           >= lax.broadcasted_iota(I32, (frames, frames), 1)).astype(BF16)

    def diag_blocks(x):
        return jnp.concatenate([jnp.where(left, x, 0.0), jnp.where(left, 0.0, x)], axis=0)

    def cross_blocks(x):
        return jnp.concatenate([jnp.where(left, 0.0, x), jnp.where(left, x, 0.0)], axis=0)

    def batch_prep(bi):
        lw = lw_ref[bi]
        hi = lw.astype(BF16)
        rem = lw - hi.astype(F32)
        mid = rem.astype(BF16)
        lo = (rem - mid.astype(F32)).astype(BF16)
        cs = (jnp.dot(tri, hi, preferred_element_type=F32)
              + jnp.dot(tri, mid, preferred_element_type=F32)
              + jnp.dot(tri, lo, preferred_element_type=F32))
        cs_end = cs[frames - 1:frames, :]
        e_neg = jnp.exp(-cs)
        e_end = jnp.exp(cs_end - cs)
        b_in = b_ref[bi]
        k_in = k_ref[bi]
        return dict(a_t=-kk_ref[bi] * jnp.exp(cs - lw), b_t=b_in * e_neg, k_t=k_in * e_neg,
                    r_t=r_ref[bi] * jnp.exp(cs), b_h=b_in * e_end, k_h=k_in * e_end,
                    v=v_ref[bi], g_end=jnp.exp(cs_end))

    group = []
    for bi in range(n_batch):
        prep = batch_prep(bi)
        for pi in range(n_pairs):
            sl = slice(pi * lanes, (pi + 1) * lanes)
            group.append((bi, sl, bi * n_pairs + pi, {n: x[:, sl] for n, x in prep.items()}))
    bf = lambda x: x.astype(BF16)
    a_bd = [bf(diag_blocks(p["a_t"])) for _, _, _, p in group]
    r_bd = [bf(diag_blocks(p["r_t"])) for _, _, _, p in group]
    d0 = [_bdot_nt(jnp.concatenate([a[:hd], r[:hd]], axis=0),
                   bf(jnp.concatenate([p["b_t"], p["k_t"]], axis=0)))
          for a, r, (_, _, _, p) in zip(a_bd, r_bd, group)]
    d1 = [_bdot_nt(jnp.concatenate([a[hd:], r[hd:]], axis=0),
                   bf(jnp.concatenate([p["k_t"], p["b_t"]], axis=0)))
          for a, r, (_, _, _, p) in zip(a_bd, r_bd, group)]
    a_rows = [jnp.concatenate([x[:hd], y[:hd]], axis=0) for x, y in zip(d0, d1)]
    m_rows = [jnp.concatenate([x[hd:], y[hd:]], axis=0) for x, y in zip(d0, d1)]
    a_ab = [bf(jnp.where(strict_same, x, 0.0)) for x in a_rows]
    a_ak = [bf(jnp.where(strict_cross, x, 0.0)) for x in a_rows]
    m_rb = [bf(jnp.where(incl_same, x, 0.0)) for x in m_rows]
    m_rk = [bf(jnp.where(incl_cross, x, 0.0)) for x in m_rows]
    t_inv = [jnp.where(eye, 1.0, jnp.where(level_masks[0], x, 0.0)) for x in a_rows]
    for mask in level_masks[1:]:
        t_bf = [bf(t) for t in t_inv]
        half = [jnp.where(mask, _bdot(t, x), 0.0) for t, x in zip(t_bf, a_ab)]
        t_inv = [t + _bdot(h, tb) for t, h, tb in zip(t_inv, half, t_bf)]
    t_bf = [bf(t) for t in t_inv]
    v_x = [bf(cross_blocks(p["v"])) for _, _, _, p in group]
    akv = [_bdot(x, v) for x, v in zip(a_ak, v_x)]
    wu = [bf(_bdot(t, jnp.concatenate([a, bf(x)], axis=1)))
          for t, a, x in zip(t_bf, a_bd, akv)]
    bh_t = [bf(diag_blocks(p["b_h"]).T) for _, _, _, p in group]
    kh_t = [bf(cross_blocks(p["k_h"]).T) for _, _, _, p in group]
    pw_rw = [_bdot(jnp.concatenate([bt, m], axis=0), x[:, :lanes])
             for bt, m, x in zip(bh_t, m_rb, wu)]
    q_o = [_bdot(jnp.concatenate([jnp.concatenate([bt, kt], axis=1),
                                  jnp.concatenate([mb, mk], axis=1)], axis=0),
                 jnp.concatenate([x[:, lanes:], v], axis=0))
           for bt, kt, mb, mk, x, v in zip(bh_t, kh_t, m_rb, m_rk, wu, v_x)]
    res = [_bdot(jnp.concatenate([diag_blocks(p["r_t"]) + pr[lanes:],
                                  pr[:lanes] + jnp.where(eye, p["g_end"], 0.0)], axis=0),
                 state_ref[idx])
           for pr, (_, _, idx, p) in zip(pw_rw, group)]
    for x, q, (bi, sl, idx, _) in zip(res, q_o, group):
        o_bd = x[:lanes] + q[lanes:]
        state_ref[idx] = x[lanes:] + q[:lanes]
        o_ref[bi, :, sl] = jnp.where(left, o_bd[:hd], o_bd[hd:])


def _wkv_chunk(r, lw, k, v, kk, b):
    n_batch, seq, rw = r.shape
    n_pairs = rw // V7X_LANES
    frames = SCAN_FRAMES
    assert seq % frames == 0
    spec = pl.BlockSpec((n_batch, frames, rw), lambda c: (0, c, 0))
    return pl.pallas_call(
        functools.partial(_wkv_chunk_kernel, n_batch, n_pairs),
        grid=(seq // frames,),
        in_specs=[spec] * 6,
        out_specs=spec,
        out_shape=jax.ShapeDtypeStruct((n_batch, seq, rw), F32),
        scratch_shapes=[pltpu.VMEM((n_batch * n_pairs, V7X_LANES, V7X_LANES), F32)],
        compiler_params=_params("arbitrary"),
        name="wkv_chunk",
    )(r, lw, k, v, kk, b)


def _post_mix_kernel(n_experts,
                     o_ref, bonus_ref, gate_ref, yconv_ref, x_ref, gn_w_ref, gn_b_ref,
                     w_out_c_ref, w_out_r_ref, ffn_g_ref, wr_hi_ref, wr_lo_ref, b_router_ref,
                     seg2_ref,
                     x1_ref, *outs):
    h2_refs, (top_e_ref, gates_ref) = outs[:-2], outs[-2:]
    seg2 = seg2_ref[...]
    o = o_ref[...]
    mean = _seg_sum(o, seg2) * (1.0 / HEAD_DIM)
    cen = o - mean
    var = _seg_sum(cen * cen, seg2) * (1.0 / HEAD_DIM)
    o = cen * lax.rsqrt(var + GN_EPS) * gn_w_ref[...] + gn_b_ref[...]
    y_rwkv = (o + bonus_ref[...]) * gate_ref[...]
    x1 = (x_ref[...]
          + jnp.dot(yconv_ref[...], w_out_c_ref[...], preferred_element_type=F32)
          + jnp.dot(y_rwkv.astype(BF16), w_out_r_ref[...], preferred_element_type=F32))
    x1_ref[...] = x1
    h2 = _rms_rows(x1, ffn_g_ref[...])
    words = _pack_bf16_pairs(h2)
    for c, h2_ref in enumerate(h2_refs):
        h2_ref[...] = words[:, c * ROW_CHUNK:(c + 1) * ROW_CHUNK]

    h_hi, h_lo = _split_bf16(h2)
    nt = (((1,), (1,)), ((), ()))
    logits = (lax.dot_general(wr_hi_ref[...], h_hi, nt, preferred_element_type=F32)
              + lax.dot_general(wr_hi_ref[...], h_lo, nt, preferred_element_type=F32)
              + lax.dot_general(wr_lo_ref[...], h_hi, nt, preferred_element_type=F32)
              + b_router_ref[...])
    e_id = lax.broadcasted_iota(I32, logits.shape, 0).astype(F32)
    work = logits
    tops, ids = [], []
    for _ in range(TOP_K):
        m = jnp.max(work, axis=0, keepdims=True)
        sel = jnp.min(jnp.where(work == m, e_id, float(n_experts)), axis=0, keepdims=True)
        tops.append(m)
        ids.append(sel)
        work = jnp.where(e_id == sel, -jnp.inf, work)
    ex = [jnp.exp(t - tops[0]) for t in tops]
    denom = ex[0] + ex[1] + ex[2] + ex[3]
    top_e_ref[...] = jnp.concatenate(ids, axis=0).astype(I32)
    gates_ref[...] = jnp.concatenate([e / denom for e in ex], axis=0)


def _post_mix(token0, n, o, bonus, gate, yconv, x2d, gn_w, gn_b, w_out_c, w_out_r, ffn_g, wr_hi,
              wr_lo, b_router, seg2):
    d = x2d.shape[1]
    rw = o.shape[1]
    cw = yconv.shape[1]
    n_experts = wr_hi.shape[0]
    tm = POST_ROWS
    assert token0 % tm == 0 and n % tm == 0
    full = lambda a: pl.BlockSpec(a.shape, lambda i: (0,) * a.ndim)
    in_spec = lambda c: pl.BlockSpec((tm, c), lambda i: (i + token0 // tm, 0))
    row_spec = lambda c: pl.BlockSpec((tm, c), lambda i: (i, 0))
    col_spec = pl.BlockSpec((TOP_K, tm), lambda i: (0, i))
    consts = (gn_w, gn_b, w_out_c, w_out_r, ffn_g, wr_hi, wr_lo, b_router, seg2)
    n_chunks = d // 2 // ROW_CHUNK
    outs = pl.pallas_call(
        functools.partial(_post_mix_kernel, n_experts),
        grid=(n // tm,),
        in_specs=[in_spec(rw), in_spec(rw), in_spec(rw), in_spec(cw), in_spec(d)]
                 + [full(c) for c in consts],
        out_specs=[row_spec(d)] + [row_spec(ROW_CHUNK)] * n_chunks + [col_spec, col_spec],
        out_shape=[jax.ShapeDtypeStruct((n, d), F32)]
                  + [jax.ShapeDtypeStruct((n, ROW_CHUNK), jnp.uint32)] * n_chunks
                  + [jax.ShapeDtypeStruct((TOP_K, n), I32), jax.ShapeDtypeStruct((TOP_K, n), F32)],
        compiler_params=_params("parallel"),
        name="post_mix",
    )(o, bonus, gate, yconv, x2d, *consts)
    return outs[0], outs[1:1 + n_chunks], outs[-2], outs[-1]


def _route_kernel(n_experts, block_rows,
                  top_e_ref, tri_ref, dest_ref, meta_ref,
                  count_ref, start_ref, carry_ref):
    phase = pl.program_id(0)
    j = pl.program_id(1)
    tb = top_e_ref.shape[1]
    e_id = lax.broadcasted_iota(I32, (n_experts, tb), 0)
    top_e = top_e_ref[...]
    onehot = jnp.zeros((n_experts, tb), F32)
    for c in range(TOP_K):
        onehot = onehot + jnp.where(top_e[c:c + 1, :] == e_id, 1.0, 0.0)
    block_count = jnp.sum(onehot, axis=1, keepdims=True)

    @pl.when((phase == 0) & (j == 0))
    def _():
        count_ref[...] = jnp.zeros_like(count_ref)

    @pl.when(phase == 0)
    def _():
        count_ref[...] += block_count

    @pl.when((phase == 1) & (j == 0))
    def _():
        counts = count_ref[...]
        padded = jnp.ceil(counts * (1.0 / block_rows)) * block_rows
        sub = lax.broadcasted_iota(I32, (n_experts, n_experts), 0)
        lane = lax.broadcasted_iota(I32, (n_experts, n_experts), 1)
        padded_row = jnp.sum(jnp.where(sub == lane, padded, 0.0), axis=0, keepdims=True)
        start = jnp.sum(jnp.where(lane < sub, padded_row, 0.0), axis=1, keepdims=True)
        start_ref[...] = start
        carry_ref[...] = jnp.zeros_like(carry_ref)
        end = start + padded
        nb = meta_ref.shape[1]
        slot0 = (lax.broadcasted_iota(I32, (n_experts, nb), 1) * block_rows).astype(F32)
        block_e = jnp.sum(jnp.where(end <= slot0, 1.0, 0.0), axis=0, keepdims=True)
        block_e = jnp.minimum(block_e, n_experts - 1.0)
        used = jnp.max(end, axis=0, keepdims=True) * (1.0 / block_rows)
        in_region = (start <= slot0) & (slot0 < end)
        valid = jnp.sum(jnp.where(in_region, jnp.minimum(start + counts - slot0, block_rows), 0.0),
                        axis=0, keepdims=True)
        row = lax.broadcasted_iota(I32, meta_ref.shape, 0)
        meta_ref[...] = jnp.where(row == META_BLOCK_EXPERT, block_e,
                                  jnp.where(row == META_VALID_ROWS, valid, used)).astype(I32)

    @pl.when(phase == 1)
    def _():
        incl = jnp.dot(onehot.astype(BF16), tri_ref[...], preferred_element_type=F32)
        base = incl - onehot + carry_ref[...] + start_ref[...]
        rows = [jnp.sum(jnp.where(top_e[c:c + 1, :] == e_id, base, 0.0), axis=0, keepdims=True)
                for c in range(TOP_K)]
        dest_ref[...] = jnp.concatenate(rows, axis=0).astype(I32)
        carry_ref[...] += block_count


def _route(top_e, n_experts, n_blocks_padded):
    n = top_e.shape[1]
    tb = ROUTE_TOKENS
    assert n % tb == 0
    tri = (lax.broadcasted_iota(I32, (tb, tb), 0) <= lax.broadcasted_iota(I32, (tb, tb), 1)
           ).astype(BF16)
    return pl.pallas_call(
        functools.partial(_route_kernel, n_experts, EXPERT_ROWS),
        grid=(2, n // tb),
        in_specs=[pl.BlockSpec((TOP_K, tb), lambda ph, j: (0, j)),
                  pl.BlockSpec((tb, tb), lambda ph, j: (0, 0))],
        out_specs=[pl.BlockSpec((TOP_K, tb), lambda ph, j: (0, j * ph)),
                   pl.BlockSpec((V7X_SUBLANES, n_blocks_padded), lambda ph, j: (0, 0))],
        out_shape=[jax.ShapeDtypeStruct((TOP_K, n), I32),
                   jax.ShapeDtypeStruct((V7X_SUBLANES, n_blocks_padded), I32)],
        scratch_shapes=[pltpu.VMEM((n_experts, 1), F32)] * 3,
        compiler_params=_params("arbitrary", "arbitrary"),
        name="route",
    )(top_e, tri)


def _sc_mesh():
    return plsc.VectorSubcoreMesh(core_axis_name="core", subcore_axis_name="subcore")


def _sc_scatter_rows(src, dest, n_slots):
    n, width = src.shape
    assert n % SC_WINDOW == 0

    @functools.partial(pl.kernel, out_type=jax.ShapeDtypeStruct((n_slots, width), src.dtype),
                       mesh=_sc_mesh(), name="sc_scatter_rows")
    def scatter(src_hbm, dest_hbm, out_hbm):
        def body(src_vmem, dest_vmem):
            for c in range(TOP_K):
                pltpu.sync_copy(src_vmem, out_hbm.at[dest_vmem.at[c]])

        pltpu.emit_pipeline(
            body,
            grid=(n // SC_WINDOW,),
            in_specs=[pl.BlockSpec((SC_WINDOW, width), lambda i: (i, 0)),
                      pl.BlockSpec((TOP_K, SC_WINDOW), lambda i: (0, i))],
            out_specs=[],
            core_axis_name=("core", "subcore"),
            dimension_semantics=(pltpu.PARALLEL,),
        )(src_hbm, dest_hbm)

    return scatter(src, dest)


def _experts_kernel(d_ff, n_in, n_out, meta_ref, *refs):
    x_refs, refs = refs[:n_in], refs[n_in:]
    w_gu_ref, b_gu_ref, w_down_ref, b_down_ref = refs[:4]
    y_refs = refs[4:4 + n_out]
    w_gu_bf16, w_down_bf16 = refs[4 + n_out:]
    j = pl.program_id(0)
    used = meta_ref[META_USED_BLOCKS, 0]
    expert = meta_ref[META_BLOCK_EXPERT, j]
    prev_expert = meta_ref[META_BLOCK_EXPERT, jnp.maximum(j - 1, 0)]

    @pl.when((j < used) & ((j == 0) | (expert != prev_expert)))
    def _():
        w_gu_bf16[...] = w_gu_ref[0].astype(BF16)
        w_down_bf16[...] = w_down_ref[0].astype(BF16)

    @pl.when(j < used)
    def _():
        x = _unpack_bf16_pairs(jnp.concatenate([r[...] for r in x_refs], axis=1))
        row = lax.broadcasted_iota(I32, x.shape, 0)
        x = jnp.where(row < meta_ref[META_VALID_ROWS, j], x, 0.0)
        gu = jnp.dot(x.astype(BF16), w_gu_bf16[...], preferred_element_type=F32) + b_gu_ref[0]
        gate = jnp.minimum(gu[:, :d_ff], SWIGLU_LIMIT)
        up = jnp.clip(gu[:, d_ff:], -SWIGLU_LIMIT, SWIGLU_LIMIT)
        act = (up + 1.0) * (gate * _sigmoid(SWIGLU_ALPHA * gate))
        y = jnp.dot(act.astype(BF16), w_down_bf16[...], preferred_element_type=F32) + b_down_ref[0]
        words = _pack_bf16_pairs(y)
        for c, y_ref in enumerate(y_refs):
            y_ref[...] = words[:, c * ROW_CHUNK:(c + 1) * ROW_CHUNK]

    @pl.when(j >= used)
    def _():
        for y_ref in y_refs:
            y_ref[...] = jnp.zeros_like(y_ref)


def _experts(meta, slot_chunks, w_gu, b_gu, w_down, b_down):
    n_in = len(slot_chunks)
    n_slots = slot_chunks[0].shape[0]
    n_experts, d, two_ff = w_gu.shape
    n_out = d // 2 // ROW_CHUNK
    d_ff = two_ff // 2
    bm = EXPERT_ROWS
    grid_spec = pltpu.PrefetchScalarGridSpec(
        num_scalar_prefetch=1,
        grid=(n_slots // bm,),
        in_specs=[pl.BlockSpec((bm, ROW_CHUNK),
                               lambda j, m: (jnp.minimum(j, m[META_USED_BLOCKS, 0] - 1), 0))
                  ] * n_in
                 + [pl.BlockSpec((1, d, two_ff), lambda j, m: (m[META_BLOCK_EXPERT, j], 0, 0)),
                    pl.BlockSpec((1, 1, two_ff), lambda j, m: (m[META_BLOCK_EXPERT, j], 0, 0)),
                    pl.BlockSpec((1, d_ff, d), lambda j, m: (m[META_BLOCK_EXPERT, j], 0, 0)),
                    pl.BlockSpec((1, 1, d), lambda j, m: (m[META_BLOCK_EXPERT, j], 0, 0))],
        out_specs=[pl.BlockSpec((bm, ROW_CHUNK), lambda j, m: (j, 0))] * n_out,
        scratch_shapes=[pltpu.VMEM((d, two_ff), BF16), pltpu.VMEM((d_ff, d), BF16)],
    )
    return pl.pallas_call(
        functools.partial(_experts_kernel, d_ff, n_in, n_out),
        grid_spec=grid_spec,
        out_shape=[jax.ShapeDtypeStruct((n_slots, ROW_CHUNK), jnp.uint32)] * n_out,
        compiler_params=_params("arbitrary"),
        name="experts",
    )(meta, *slot_chunks, w_gu, b_gu, w_down, b_down)


def _sc_gather_rows(table, idx_row):
    n_idx = idx_row.shape[1]
    width = table.shape[1]
    assert n_idx % SC_WINDOW == 0

    @functools.partial(pl.kernel, out_type=jax.ShapeDtypeStruct((n_idx, width), table.dtype),
                       mesh=_sc_mesh(), name="sc_gather_rows")
    def gather(table_hbm, idx_hbm, out_hbm):
        def body(idx_vmem, out_vmem):
            pltpu.sync_copy(table_hbm.at[idx_vmem.at[0]], out_vmem)

        pltpu.emit_pipeline(
            body,
            grid=(n_idx // SC_WINDOW,),
            in_specs=[pl.BlockSpec((1, SC_WINDOW), lambda i: (0, i))],
            out_specs=[pl.BlockSpec((SC_WINDOW, width), lambda i: (i, 0))],
            core_axis_name=("core", "subcore"),
            dimension_semantics=(pltpu.PARALLEL,),
        )(idx_hbm, out_hbm)

    return gather(table, idx_row)


def _reduce_kernel(n_chunks, *refs):
    y_refs = refs[:n_chunks]
    x1_ref, gates_ref, g_ref = refs[n_chunks:n_chunks + 3]
    out_ref = refs[-1]
    gates = gates_ref[...]
    acc = x1_ref[...]
    for c in range(TOP_K):
        y = _unpack_bf16_pairs(jnp.concatenate([r[c] for r in y_refs], axis=1))
        acc = acc + y * gates[:, c:c + 1]
    out_ref[...] = _rms_rows(acc, g_ref[...])


def _reduce(token0, n_total, out_so_far, y_chunks, x1, gates_t, final_g):
    n, d = x1.shape
    tb = REDUCE_TOKENS
    n_chunks = len(y_chunks)
    assert token0 % tb == 0
    in_specs = ([pl.BlockSpec((TOP_K, tb, ROW_CHUNK), lambda i: (0, i, 0))] * n_chunks
                + [pl.BlockSpec((tb, d), lambda i: (i, 0)),
                   pl.BlockSpec((tb, TOP_K), lambda i: (i, 0)),
                   pl.BlockSpec((1, d), lambda i: (0, 0))])
    args = [y.reshape(TOP_K, n, ROW_CHUNK) for y in y_chunks] + [x1, gates_t, final_g]
    aliases = {}
    if out_so_far is not None:
        in_specs.append(pl.BlockSpec(memory_space=pl.ANY))
        args.append(out_so_far)
        aliases = {len(args) - 1: 0}
    return pl.pallas_call(
        functools.partial(_reduce_kernel, n_chunks),
        grid=(n // tb,),
        in_specs=in_specs,
        out_specs=pl.BlockSpec((tb, d), lambda i: (i + token0 // tb, 0)),
        out_shape=jax.ShapeDtypeStruct((n_total, d), F32),
        input_output_aliases=aliases,
        compiler_params=_params("parallel"),
        name="reduce",
    )(*args)


def _group_matrix(width):
    a = lax.broadcasted_iota(I32, (width, width), 0) // HEAD_DIM
    b = lax.broadcasted_iota(I32, (width, width), 1) // HEAD_DIM
    g = (a == b).astype(BF16)
    return jnp.concatenate([g, g], axis=0)


def _row(vec):
    return vec.reshape(1, -1).astype(F32)


def kernel(x, w_in, conv_w, conv_norm_g, rwkv_mu, w0, w_up, a0, a_up, g_up, k_k, k_a, r_k,
           gn_w, gn_b, w_out, norm_mix_g, norm_ffn_g, w_router, b_router, w_gu, b_gu, w_down,
           b_down, norm_final_g):
    n_batch, seq, d = x.shape
    n = n_batch * seq
    depth = w_in.shape[0]
    cw = conv_w.shape[2]
    rw = w0.shape[1]
    n_experts = w_router.shape[2]
    decay_lora = w_up.shape[1]
    aaa_lora = a_up.shape[1]
    assert decay_lora + aaa_lora == V7X_LANES
    assert depth == 1
    assert n % MOE_PARTS == 0
    n_part = n // MOE_PARTS
    n_slots = n_part * TOP_K + n_experts * EXPERT_ROWS
    n_blocks = n_slots // EXPERT_ROWS
    n_blocks_padded = -(-n_blocks // V7X_LANES) * V7X_LANES
    seg2 = _group_matrix(rw)

    x2d = x.reshape(n, d)
    for l in range(depth):
        w_up_pad = jnp.concatenate([w_up[l], jnp.zeros((aaa_lora, rw), F32)], axis=0).astype(BF16)
        a_up_pad = jnp.concatenate([jnp.zeros((decay_lora, rw), F32), a_up[l]], axis=0).astype(BF16)
        (yconv, r, lw, k, v, kk, b, gate, bonus) = _mix_prep(
            x2d, seq, _row(norm_mix_g[l]), w_in[l].astype(BF16), conv_w[l].astype(F32),
            _row(conv_norm_g[l]), _row(rwkv_mu[l]), _row(w0[l]), w_up_pad, _row(a0[l]), a_up_pad,
            g_up[l].astype(BF16), _row(k_k[l]), _row(k_a[l]), _row(r_k[l]), seg2)
        shape3 = (n_batch, seq, rw)
        o = _wkv_chunk(r.reshape(shape3), lw.reshape(shape3), k.reshape(shape3),
                       v.reshape(shape3), kk.reshape(shape3), b.reshape(shape3)).reshape(n, rw)
        w_out_b = w_out[l].astype(BF16)
        wr_t = w_router[l].T.astype(F32)
        wr_hi = wr_t.astype(BF16)
        wr_lo = (wr_t - wr_hi.astype(F32)).astype(BF16)
        expert_args = (w_gu[l].astype(F32), b_gu[l].reshape(n_experts, 1, -1).astype(F32),
                       w_down[l].astype(F32), b_down[l].reshape(n_experts, 1, -1).astype(F32))
        parts = []
        for token0 in range(0, n, n_part):
            x1, h2_chunks, top_e, gates = _post_mix(
                token0, n_part, o, bonus, gate, yconv, x2d, _row(gn_w[l]), _row(gn_b[l]),
                w_out_b[:cw], w_out_b[cw:], _row(norm_ffn_g[l]), wr_hi, wr_lo,
                b_router[l].reshape(n_experts, 1).astype(F32), seg2)
            dest, meta = _route(top_e, n_experts, n_blocks_padded)
            slot_chunks = [_sc_scatter_rows(h, dest, n_slots) for h in h2_chunks]
            parts.append((token0, x1, gates, dest, meta, slot_chunks))
        out = None
        for token0, x1, gates, dest, meta, slot_chunks in parts:
            y_chunks = _experts(meta, slot_chunks, *expert_args)
            idx_row = dest.reshape(1, TOP_K * n_part)
            out = _reduce(token0, n, out, [_sc_gather_rows(y, idx_row) for y in y_chunks], x1,
                          gates.T, _row(norm_final_g))
        x2d = out
    return x2d.reshape(n_batch, seq, d)
```

```python
import functools

import jax
import jax.numpy as jnp
from jax import lax
from jax.experimental import pallas as pl
from jax.experimental.pallas import tpu as pltpu
from jax.experimental.pallas import tpu_sc as plsc

F32 = jnp.float32
BF16 = jnp.bfloat16
I32 = jnp.int32

HEAD_DIM = 64
TOP_K = 4
NORM_EPS = 1e-5
GN_EPS = HEAD_DIM * 1e-5
SWIGLU_LIMIT = 7.0
SWIGLU_ALPHA = 1.702

V7X_LANES = 128
V7X_SUBLANES = 8
V7X_VMEM_LIMIT_BYTES = 56 * 1024 * 1024

PREP_ROWS = 256
POST_ROWS = 512
SCAN_FRAMES = 64
ROUTE_TOKENS = 1024
REDUCE_TOKENS = 256
MOE_PARTS = 2
SC_WINDOW = 128
ROW_CHUNK = 256
META_BLOCK_EXPERT = 0
META_USED_BLOCKS = 1
META_VALID_ROWS = 2
EXPERT_ROWS = 512

def _params(*semantics):
    return pltpu.CompilerParams(dimension_semantics=semantics,
                                vmem_limit_bytes=V7X_VMEM_LIMIT_BYTES)


def _split_bf16(x):
    hi = x.astype(BF16)
    lo = (x - hi.astype(F32)).astype(BF16)
    return hi, lo


def _seg_sum(x, seg2):
    hi, lo = _split_bf16(x)
    return jnp.dot(jnp.concatenate([hi, lo], axis=1), seg2, preferred_element_type=F32)


def _pack_bf16_pairs(x):
    bits = pltpu.bitcast(x.astype(BF16).astype(F32), jnp.uint32)
    half = x.shape[1] // 2
    return bits[:, half:] | (bits[:, :half] >> 16)


def _unpack_bf16_pairs(words):
    return jnp.concatenate([pltpu.bitcast(words << 16, F32),
                            pltpu.bitcast(words & jnp.uint32(0xFFFF0000), F32)], axis=1)


def _rms_rows(x, g):
    return x * lax.rsqrt(jnp.mean(x * x, axis=-1, keepdims=True) + NORM_EPS) * g


def _sigmoid(x):
    return 1.0 / (1.0 + jnp.exp(-x))


def _mix_prep_kernel(blocks_per_seq, cw, rw,
                     xprev_ref, x_ref, g_ref, w_in_ref, conv_w_ref, conv_g_ref, mu_ref,
                     w0_ref, w_up_ref, a0_ref, a_up_ref, g_up_ref, k_k_ref, k_a_ref, r_k_ref,
                     seg2_ref,
                     yconv_ref, r_ref, w_ref, k_ref, v_ref, kk_ref, b_ref, gate_ref, bonus_ref,
                     p_scr, z_scr):
    tm = x_ref.shape[0]
    halo = xprev_ref.shape[0]
    first = (pl.program_id(0) % blocks_per_seq) == 0
    xp = xprev_ref[...] * jnp.where(first, 0.0, 1.0)
    xa = jnp.concatenate([xp, x_ref[...]], axis=0)
    h = _rms_rows(xa, g_ref[...])
    p_scr[...] = jnp.dot(h.astype(BF16), w_in_ref[...], preferred_element_type=F32)
    seg2 = seg2_ref[...]

    z_scr[...] = p_scr[:, 2 * cw:3 * cw] * p_scr[:, 0:cw]
    conv = (conv_w_ref[0:1, :] * z_scr[halo - 2:halo - 2 + tm, :]
            + conv_w_ref[1:2, :] * z_scr[halo - 1:halo - 1 + tm, :]
            + conv_w_ref[2:3, :] * z_scr[halo:halo + tm, :])
    y = p_scr[halo:halo + tm, cw:2 * cw] * conv
    ms = _seg_sum(y * y, seg2) * (1.0 / HEAD_DIM)
    yconv_ref[...] = (y * lax.rsqrt(ms + NORM_EPS) * conv_g_ref[...]).astype(yconv_ref.dtype)

    c0 = 3 * cw
    cur = p_scr[halo:halo + tm, c0:]
    prev = p_scr[halo - 1:halo - 1 + tm, c0:]
    q = cur + (prev - cur) * mu_ref[...]
    r = q[:, 0:rw]
    k = q[:, rw:2 * rw]
    v = q[:, 2 * rw:3 * rw]
    lora_wa = q[:, 3 * rw:3 * rw + V7X_LANES]
    lora_g = q[:, 3 * rw + V7X_LANES:]
    w_lin = w0_ref[...] + jnp.dot(jnp.tanh(lora_wa).astype(BF16), w_up_ref[...],
                                  preferred_element_type=F32)
    neg = -w_lin
    softplus = jnp.maximum(neg, 0.0) + jnp.log(1.0 + jnp.exp(-jnp.abs(neg)))
    log_decay = -jnp.exp(-softplus - 0.5)
    a = _sigmoid(a0_ref[...] + jnp.dot(lora_wa.astype(BF16), a_up_ref[...],
                                       preferred_element_type=F32))
    gate = jnp.dot(_sigmoid(lora_g).astype(BF16), g_up_ref[...], preferred_element_type=F32)
    kk = k * k_k_ref[...]
    kk = kk / jnp.maximum(jnp.sqrt(_seg_sum(kk * kk, seg2)), 1e-12)
    k_mod = k * (1.0 + (a - 1.0) * k_a_ref[...])
    bonus = _seg_sum(r * k_mod * r_k_ref[...], seg2) * v
    r_ref[...] = r
    w_ref[...] = log_decay
    k_ref[...] = k_mod
    v_ref[...] = v
    kk_ref[...] = kk
    b_ref[...] = kk * a
    gate_ref[...] = gate
    bonus_ref[...] = bonus


def _mix_prep(x2d, seq, norm_g, w_in, conv_w, conv_g, mu, w0, w_up, a0, a_up, g_up, k_k, k_a,
              r_k, seg2):
    n, d = x2d.shape
    tm = PREP_ROWS
    halo = V7X_SUBLANES
    cw = conv_w.shape[1]
    rw = w0.shape[1]
    in_cols = w_in.shape[1]
    assert seq % tm == 0 and n % tm == 0
    full = lambda a: pl.BlockSpec(a.shape, lambda i: (0,) * a.ndim)
    row_spec = lambda c: pl.BlockSpec((tm, c), lambda i: (i, 0))
    consts = (norm_g, w_in, conv_w, conv_g, mu, w0, w_up, a0, a_up, g_up, k_k, k_a, r_k, seg2)
    outs = [jax.ShapeDtypeStruct((n, cw), BF16)] + [jax.ShapeDtypeStruct((n, rw), F32)] * 8
    return pl.pallas_call(
        functools.partial(_mix_prep_kernel, seq // tm, cw, rw),
        grid=(n // tm,),
        in_specs=[pl.BlockSpec((halo, d), lambda i: (jnp.maximum(i * (tm // halo) - 1, 0), 0)),
                  row_spec(d)] + [full(c) for c in consts],
        out_specs=[row_spec(cw)] + [row_spec(rw)] * 8,
        out_shape=outs,
        scratch_shapes=[pltpu.VMEM((halo + tm, in_cols), F32), pltpu.VMEM((halo + tm, cw), F32)],
        compiler_params=_params("parallel"),
        name="mix_prep",
    )(x2d, x2d, *consts)


def _bdot(a, b):
    return jnp.dot(a.astype(BF16), b.astype(BF16), preferred_element_type=F32)


def _bdot_nt(a, b):
    return lax.dot_general(a.astype(BF16), b.astype(BF16), (((1,), (1,)), ((), ())),
                           preferred_element_type=F32)


def _wkv_chunk_kernel(n_batch, n_pairs,
                      r_ref, lw_ref, k_ref, v_ref, kk_ref, b_ref, o_ref, state_ref):
    frames = r_ref.shape[1]
    lanes = V7X_LANES
    hd = HEAD_DIM
    assert frames == hd

    @pl.when(pl.program_id(0) == 0)
    def _():
        state_ref[...] = jnp.zeros_like(state_ref)

    row = lax.broadcasted_iota(I32, (lanes, lanes), 0)
    lane = lax.broadcasted_iota(I32, (lanes, lanes), 1)
    same_head = (row // hd) == (lane // hd)
    rt = row % hd
    ls = lane % hd
    strict_same = same_head & (rt > ls)
    strict_cross = jnp.logical_not(same_head) & (rt > ls)
    incl_same = same_head & (rt >= ls)
    incl_cross = jnp.logical_not(same_head) & (rt >= ls)
    eye = row == lane
    level_masks = []
    m = 1
    while m < hd:
        level_masks.append(same_head & ((rt // (2 * m)) == (ls // (2 * m)))
                           & (((rt // m) % 2) == 1) & (((ls // m) % 2) == 0))
        m *= 2
    left = lax.broadcasted_iota(I32, (frames, lanes), 1) < hd
    tri = (lax.broadcasted_iota(I32, (frames, frames), 0)
           >= lax.broadcasted_iota(I32, (frames, frames), 1)).astype(BF16)

    def diag_blocks(x):
        return jnp.concatenate([jnp.where(left, x, 0.0), jnp.where(left, 0.0, x)], axis=0)

    def cross_blocks(x):
        return jnp.concatenate([jnp.where(left, 0.0, x), jnp.where(left, x, 0.0)], axis=0)

    def batch_prep(bi):
        lw = lw_ref[bi]
        hi = lw.astype(BF16)
        rem = lw - hi.astype(F32)
        mid = rem.astype(BF16)
        lo = (rem - mid.astype(F32)).astype(BF16)
        cs = (jnp.dot(tri, hi, preferred_element_type=F32)
              + jnp.dot(tri, mid, preferred_element_type=F32)
              + jnp.dot(tri, lo, preferred_element_type=F32))
        cs_end = cs[frames - 1:frames, :]
        e_neg = jnp.exp(-cs)
        e_end = jnp.exp(cs_end - cs)
        b_in = b_ref[bi]
        k_in = k_ref[bi]
        return dict(a_t=-kk_ref[bi] * jnp.exp(cs - lw), b_t=b_in * e_neg, k_t=k_in * e_neg,
                    r_t=r_ref[bi] * jnp.exp(cs), b_h=b_in * e_end, k_h=k_in * e_end,
                    v=v_ref[bi], g_end=jnp.exp(cs_end))

    group = []
    for bi in range(n_batch):
        prep = batch_prep(bi)
        for pi in range(n_pairs):
            sl = slice(pi * lanes, (pi + 1) * lanes)
            group.append((bi, sl, bi * n_pairs + pi, {n: x[:, sl] for n, x in prep.items()}))
    bf = lambda x: x.astype(BF16)
    a_bd = [bf(diag_blocks(p["a_t"])) for _, _, _, p in group]
    r_bd = [bf(diag_blocks(p["r_t"])) for _, _, _, p in group]
    d0 = [_bdot_nt(jnp.concatenate([a[:hd], r[:hd]], axis=0),
                   bf(jnp.concatenate([p["b_t"], p["k_t"]], axis=0)))
          for a, r, (_, _, _, p) in zip(a_bd, r_bd, group)]
    d1 = [_bdot_nt(jnp.concatenate([a[hd:], r[hd:]], axis=0),
                   bf(jnp.concatenate([p["k_t"], p["b_t"]], axis=0)))
          for a, r, (_, _, _, p) in zip(a_bd, r_bd, group)]
    a_rows = [jnp.concatenate([x[:hd], y[:hd]], axis=0) for x, y in zip(d0, d1)]
    m_rows = [jnp.concatenate([x[hd:], y[hd:]], axis=0) for x, y in zip(d0, d1)]
    a_ab = [bf(jnp.where(strict_same, x, 0.0)) for x in a_rows]
    a_ak = [bf(jnp.where(strict_cross, x, 0.0)) for x in a_rows]
    m_rb = [bf(jnp.where(incl_same, x, 0.0)) for x in m_rows]
    m_rk = [bf(jnp.where(incl_cross, x, 0.0)) for x in m_rows]
    t_inv = [jnp.where(eye, 1.0, jnp.where(level_masks[0], x, 0.0)) for x in a_rows]
    for mask in level_masks[1:]:
        t_bf = [bf(t) for t in t_inv]
        half = [jnp.where(mask, _bdot(t, x), 0.0) for t, x in zip(t_bf, a_ab)]
        t_inv = [t + _bdot(h, tb) for t, h, tb in zip(t_inv, half, t_bf)]
    t_bf = [bf(t) for t in t_inv]
    v_x = [bf(cross_blocks(p["v"])) for _, _, _, p in group]
    akv = [_bdot(x, v) for x, v in zip(a_ak, v_x)]
    wu = [bf(_bdot(t, jnp.concatenate([a, bf(x)], axis=1)))
          for t, a, x in zip(t_bf, a_bd, akv)]
    bh_t = [bf(diag_blocks(p["b_h"]).T) for _, _, _, p in group]
    kh_t = [bf(cross_blocks(p["k_h"]).T) for _, _, _, p in group]
    pw_rw = [_bdot(jnp.concatenate([bt, m], axis=0), x[:, :lanes])
             for bt, m, x in zip(bh_t, m_rb, wu)]
    q_o = [_bdot(jnp.concatenate([jnp.concatenate([bt, kt], axis=1),
                                  jnp.concatenate([mb, mk], axis=1)], axis=0),
                 jnp.concatenate([x[:, lanes:], v], axis=0))
           for bt, kt, mb, mk, x, v in zip(bh_t, kh_t, m_rb, m_rk, wu, v_x)]
    res = [_bdot(jnp.concatenate([diag_blocks(p["r_t"]) + pr[lanes:],
                                  pr[:lanes] + jnp.where(eye, p["g_end"], 0.0)], axis=0),
                 state_ref[idx])
           for pr, (_, _, idx, p) in zip(pw_rw, group)]
    for x, q, (bi, sl, idx, _) in zip(res, q_o, group):
        o_bd = x[:lanes] + q[lanes:]
        state_ref[idx] = x[lanes:] + q[:lanes]
        o_ref[bi, :, sl] = jnp.where(left, o_bd[:hd], o_bd[hd:])


def _wkv_chunk(r, lw, k, v, kk, b):
    n_batch, seq, rw = r.shape
    n_pairs = rw // V7X_LANES
    frames = SCAN_FRAMES
    assert seq % frames == 0
    spec = pl.BlockSpec((n_batch, frames, rw), lambda c: (0, c, 0))
    return pl.pallas_call(
        functools.partial(_wkv_chunk_kernel, n_batch, n_pairs),
        grid=(seq // frames,),
        in_specs=[spec] * 6,
        out_specs=spec,
        out_shape=jax.ShapeDtypeStruct((n_batch, seq, rw), F32),
        scratch_shapes=[pltpu.VMEM((n_batch * n_pairs, V7X_LANES, V7X_LANES), F32)],
        compiler_params=_params("arbitrary"),
        name="wkv_chunk",
    )(r, lw, k, v, kk, b)


def _post_mix_kernel(n_experts,
                     o_ref, bonus_ref, gate_ref, yconv_ref, x_ref, gn_w_ref, gn_b_ref,
                     w_out_c_ref, w_out_r_ref, ffn_g_ref, wr_hi_ref, wr_lo_ref, b_router_ref,
                     seg2_ref,
                     x1_ref, *outs):
    h2_refs, (top_e_ref, gates_ref) = outs[:-2], outs[-2:]
    seg2 = seg2_ref[...]
    o = o_ref[...]
    mean = _seg_sum(o, seg2) * (1.0 / HEAD_DIM)
    cen = o - mean
    var = _seg_sum(cen * cen, seg2) * (1.0 / HEAD_DIM)
    o = cen * lax.rsqrt(var + GN_EPS) * gn_w_ref[...] + gn_b_ref[...]
    y_rwkv = (o + bonus_ref[...]) * gate_ref[...]
    x1 = (x_ref[...]
          + jnp.dot(yconv_ref[...], w_out_c_ref[...], preferred_element_type=F32)
          + jnp.dot(y_rwkv.astype(BF16), w_out_r_ref[...], preferred_element_type=F32))
    x1_ref[...] = x1
    h2 = _rms_rows(x1, ffn_g_ref[...])
    words = _pack_bf16_pairs(h2)
    for c, h2_ref in enumerate(h2_refs):
        h2_ref[...] = words[:, c * ROW_CHUNK:(c + 1) * ROW_CHUNK]

    h_hi, h_lo = _split_bf16(h2)
    nt = (((1,), (1,)), ((), ()))
    logits = (lax.dot_general(wr_hi_ref[...], h_hi, nt, preferred_element_type=F32)
              + lax.dot_general(wr_hi_ref[...], h_lo, nt, preferred_element_type=F32)
              + lax.dot_general(wr_lo_ref[...], h_hi, nt, preferred_element_type=F32)
              + b_router_ref[...])
    e_id = lax.broadcasted_iota(I32, logits.shape, 0).astype(F32)
    work = logits
    tops, ids = [], []
    for _ in range(TOP_K):
        m = jnp.max(work, axis=0, keepdims=True)
        sel = jnp.min(jnp.where(work == m, e_id, float(n_experts)), axis=0, keepdims=True)
        tops.append(m)
        ids.append(sel)
        work = jnp.where(e_id == sel, -jnp.inf, work)
    ex = [jnp.exp(t - tops[0]) for t in tops]
    denom = ex[0] + ex[1] + ex[2] + ex[3]
    top_e_ref[...] = jnp.concatenate(ids, axis=0).astype(I32)
    gates_ref[...] = jnp.concatenate([e / denom for e in ex], axis=0)


def _post_mix(token0, n, o, bonus, gate, yconv, x2d, gn_w, gn_b, w_out_c, w_out_r, ffn_g, wr_hi,
              wr_lo, b_router, seg2):
    d = x2d.shape[1]
    rw = o.shape[1]
    cw = yconv.shape[1]
    n_experts = wr_hi.shape[0]
    tm = POST_ROWS
    assert token0 % tm == 0 and n % tm == 0
    full = lambda a: pl.BlockSpec(a.shape, lambda i: (0,) * a.ndim)
    in_spec = lambda c: pl.BlockSpec((tm, c), lambda i: (i + token0 // tm, 0))
    row_spec = lambda c: pl.BlockSpec((tm, c), lambda i: (i, 0))
    col_spec = pl.BlockSpec((TOP_K, tm), lambda i: (0, i))
    consts = (gn_w, gn_b, w_out_c, w_out_r, ffn_g, wr_hi, wr_lo, b_router, seg2)
    n_chunks = d // 2 // ROW_CHUNK
    outs = pl.pallas_call(
        functools.partial(_post_mix_kernel, n_experts),
        grid=(n // tm,),
        in_specs=[in_spec(rw), in_spec(rw), in_spec(rw), in_spec(cw), in_spec(d)]
                 + [full(c) for c in consts],
        out_specs=[row_spec(d)] + [row_spec(ROW_CHUNK)] * n_chunks + [col_spec, col_spec],
        out_shape=[jax.ShapeDtypeStruct((n, d), F32)]
                  + [jax.ShapeDtypeStruct((n, ROW_CHUNK), jnp.uint32)] * n_chunks
                  + [jax.ShapeDtypeStruct((TOP_K, n), I32), jax.ShapeDtypeStruct((TOP_K, n), F32)],
        compiler_params=_params("parallel"),
        name="post_mix",
    )(o, bonus, gate, yconv, x2d, *consts)
    return outs[0], outs[1:1 + n_chunks], outs[-2], outs[-1]


def _route_kernel(n_experts, block_rows,
                  top_e_ref, tri_ref, dest_ref, meta_ref,
                  count_ref, start_ref, carry_ref):
    phase = pl.program_id(0)
    j = pl.program_id(1)
    tb = top_e_ref.shape[1]
    e_id = lax.broadcasted_iota(I32, (n_experts, tb), 0)
    top_e = top_e_ref[...]
    onehot = jnp.zeros((n_experts, tb), F32)
    for c in range(TOP_K):
        onehot = onehot + jnp.where(top_e[c:c + 1, :] == e_id, 1.0, 0.0)
    block_count = jnp.sum(onehot, axis=1, keepdims=True)

    @pl.when((phase == 0) & (j == 0))
    def _():
        count_ref[...] = jnp.zeros_like(count_ref)

    @pl.when(phase == 0)
    def _():
        count_ref[...] += block_count

    @pl.when((phase == 1) & (j == 0))
    def _():
        counts = count_ref[...]
        padded = jnp.maximum(jnp.ceil(counts * (1.0 / block_rows)), 1.0) * block_rows
        sub = lax.broadcasted_iota(I32, (n_experts, n_experts), 0)
        lane = lax.broadcasted_iota(I32, (n_experts, n_experts), 1)
        padded_row = jnp.sum(jnp.where(sub == lane, padded, 0.0), axis=0, keepdims=True)
        start = jnp.sum(jnp.where(lane < sub, padded_row, 0.0), axis=1, keepdims=True)
        start_ref[...] = start
        carry_ref[...] = jnp.zeros_like(carry_ref)
        end = start + padded
        nb = meta_ref.shape[1]
        slot0 = (lax.broadcasted_iota(I32, (n_experts, nb), 1) * block_rows).astype(F32)
        block_e = jnp.sum(jnp.where(end <= slot0, 1.0, 0.0), axis=0, keepdims=True)
        block_e = jnp.minimum(block_e, n_experts - 1.0)
        used = jnp.max(end, axis=0, keepdims=True) * (1.0 / block_rows)
        in_region = (start <= slot0) & (slot0 < end)
        valid = jnp.sum(jnp.where(in_region, jnp.minimum(start + counts - slot0, block_rows), 0.0),
                        axis=0, keepdims=True)
        row = lax.broadcasted_iota(I32, meta_ref.shape, 0)
        meta_ref[...] = jnp.where(row == META_BLOCK_EXPERT, block_e,
                                  jnp.where(row == META_VALID_ROWS, valid, used)).astype(I32)

    @pl.when(phase == 1)
    def _():
        incl = jnp.dot(onehot.astype(BF16), tri_ref[...], preferred_element_type=F32)
        base = incl - onehot + carry_ref[...] + start_ref[...]
        rows = [jnp.sum(jnp.where(top_e[c:c + 1, :] == e_id, base, 0.0), axis=0, keepdims=True)
                for c in range(TOP_K)]
        dest_ref[...] = jnp.concatenate(rows, axis=0).astype(I32)
        carry_ref[...] += block_count


def _route(top_e, n_experts, n_blocks_padded):
    n = top_e.shape[1]
    tb = ROUTE_TOKENS
    assert n % tb == 0
    tri = (lax.broadcasted_iota(I32, (tb, tb), 0) <= lax.broadcasted_iota(I32, (tb, tb), 1)
           ).astype(BF16)
    return pl.pallas_call(
        functools.partial(_route_kernel, n_experts, EXPERT_ROWS),
        grid=(2, n // tb),
        in_specs=[pl.BlockSpec((TOP_K, tb), lambda ph, j: (0, j)),
                  pl.BlockSpec((tb, tb), lambda ph, j: (0, 0))],
        out_specs=[pl.BlockSpec((TOP_K, tb), lambda ph, j: (0, j * ph)),
                   pl.BlockSpec((V7X_SUBLANES, n_blocks_padded), lambda ph, j: (0, 0))],
        out_shape=[jax.ShapeDtypeStruct((TOP_K, n), I32),
                   jax.ShapeDtypeStruct((V7X_SUBLANES, n_blocks_padded), I32)],
        scratch_shapes=[pltpu.VMEM((n_experts, 1), F32)] * 3,
        compiler_params=_params("arbitrary", "arbitrary"),
        name="route",
    )(top_e, tri)


def _sc_mesh():
    return plsc.VectorSubcoreMesh(core_axis_name="core", subcore_axis_name="subcore")


def _sc_scatter_rows(src, dest, n_slots):
    n, width = src.shape
    assert n % SC_WINDOW == 0

    @functools.partial(pl.kernel, out_type=jax.ShapeDtypeStruct((n_slots, width), src.dtype),
                       mesh=_sc_mesh(), name="sc_scatter_rows")
    def scatter(src_hbm, dest_hbm, out_hbm):
        def body(src_vmem, dest_vmem):
            for c in range(TOP_K):
                pltpu.sync_copy(src_vmem, out_hbm.at[dest_vmem.at[c]])

        pltpu.emit_pipeline(
            body,
            grid=(n // SC_WINDOW,),
            in_specs=[pl.BlockSpec((SC_WINDOW, width), lambda i: (i, 0)),
                      pl.BlockSpec((TOP_K, SC_WINDOW), lambda i: (0, i))],
            out_specs=[],
            core_axis_name=("core", "subcore"),
            dimension_semantics=(pltpu.PARALLEL,),
        )(src_hbm, dest_hbm)

    return scatter(src, dest)


def _experts_kernel(d_ff, n_in, n_out, cast_weights, meta_ref, *refs):
    x_refs, refs = refs[:n_in], refs[n_in:]
    w_gu_ref, b_gu_ref, w_down_ref, b_down_ref = refs[:4]
    y_refs = refs[4:4 + n_out]
    w_gu_mxu, w_down_mxu = refs[4 + n_out:] if cast_weights else (w_gu_ref, w_down_ref)
    j = pl.program_id(0)
    used = meta_ref[META_USED_BLOCKS, 0]

    if cast_weights:
        expert = meta_ref[META_BLOCK_EXPERT, j]
        prev_expert = meta_ref[META_BLOCK_EXPERT, jnp.maximum(j - 1, 0)]

        @pl.when((j < used) & ((j == 0) | (expert != prev_expert)))
        def _():
            w_gu_mxu[0] = w_gu_ref[0].astype(BF16)
            w_down_mxu[0] = w_down_ref[0].astype(BF16)

    @pl.when(j < used)
    def _():
        x = _unpack_bf16_pairs(jnp.concatenate([r[...] for r in x_refs], axis=1))
        row = lax.broadcasted_iota(I32, x.shape, 0)
        x = jnp.where(row < meta_ref[META_VALID_ROWS, j], x, 0.0)
        gu = jnp.dot(x.astype(BF16), w_gu_mxu[0], preferred_element_type=F32) + b_gu_ref[0]
        gate = jnp.minimum(gu[:, :d_ff], SWIGLU_LIMIT)
        up = jnp.clip(gu[:, d_ff:], -SWIGLU_LIMIT, SWIGLU_LIMIT)
        act = (up + 1.0) * (gate * _sigmoid(SWIGLU_ALPHA * gate))
        y = jnp.dot(act.astype(BF16), w_down_mxu[0], preferred_element_type=F32) + b_down_ref[0]
        words = _pack_bf16_pairs(y)
        for c, y_ref in enumerate(y_refs):
            y_ref[...] = words[:, c * ROW_CHUNK:(c + 1) * ROW_CHUNK]

    @pl.when(j >= used)
    def _():
        for y_ref in y_refs:
            y_ref[...] = jnp.zeros_like(y_ref)


def _experts(meta, slot_chunks, w_gu, b_gu, w_down, b_down):
    n_in = len(slot_chunks)
    n_slots = slot_chunks[0].shape[0]
    n_experts, d, two_ff = w_gu.shape
    n_out = d // 2 // ROW_CHUNK
    d_ff = two_ff // 2
    bm = EXPERT_ROWS
    cast_weights = w_gu.dtype != BF16
    by_expert = lambda j, m: (m[META_BLOCK_EXPERT, j], 0, 0)
    out_specs = [pl.BlockSpec((bm, ROW_CHUNK), lambda j, m: (j, 0))] * n_out
    out_shape = [jax.ShapeDtypeStruct((n_slots, ROW_CHUNK), jnp.uint32)] * n_out
    if cast_weights:
        out_specs += [pl.BlockSpec((1, d, two_ff), by_expert), pl.BlockSpec((1, d_ff, d), by_expert)]
        out_shape += [jax.ShapeDtypeStruct(w_gu.shape, BF16), jax.ShapeDtypeStruct(w_down.shape, BF16)]
    grid_spec = pltpu.PrefetchScalarGridSpec(
        num_scalar_prefetch=1,
        grid=(n_slots // bm,),
        in_specs=[pl.BlockSpec((bm, ROW_CHUNK),
                               lambda j, m: (jnp.minimum(j, m[META_USED_BLOCKS, 0] - 1), 0))
                  ] * n_in
                 + [pl.BlockSpec((1, d, two_ff), by_expert), pl.BlockSpec((1, 1, two_ff), by_expert),
                    pl.BlockSpec((1, d_ff, d), by_expert), pl.BlockSpec((1, 1, d), by_expert)],
        out_specs=out_specs,
    )
    outs = pl.pallas_call(
        functools.partial(_experts_kernel, d_ff, n_in, n_out, cast_weights),
        grid_spec=grid_spec,
        out_shape=out_shape,
        compiler_params=_params("arbitrary"),
        name="experts",
    )(meta, *slot_chunks, w_gu, b_gu, w_down, b_down)
    return outs[:n_out], (tuple(outs[n_out:]) if cast_weights else (w_gu, w_down))


def _sc_gather_rows(table, idx_row):
    n_idx = idx_row.shape[1]
    width = table.shape[1]
    assert n_idx % SC_WINDOW == 0

    @functools.partial(pl.kernel, out_type=jax.ShapeDtypeStruct((n_idx, width), table.dtype),
                       mesh=_sc_mesh(), name="sc_gather_rows")
    def gather(table_hbm, idx_hbm, out_hbm):
        def body(idx_vmem, out_vmem):
            pltpu.sync_copy(table_hbm.at[idx_vmem.at[0]], out_vmem)

        pltpu.emit_pipeline(
            body,
            grid=(n_idx // SC_WINDOW,),
            in_specs=[pl.BlockSpec((1, SC_WINDOW), lambda i: (0, i))],
            out_specs=[pl.BlockSpec((SC_WINDOW, width), lambda i: (i, 0))],
            core_axis_name=("core", "subcore"),
            dimension_semantics=(pltpu.PARALLEL,),
        )(idx_hbm, out_hbm)

    return gather(table, idx_row)


def _reduce_kernel(n_chunks, *refs):
    y_refs = refs[:n_chunks]
    x1_ref, gates_ref, g_ref = refs[n_chunks:n_chunks + 3]
    out_ref = refs[-1]
    gates = gates_ref[...]
    acc = x1_ref[...]
    for c in range(TOP_K):
        y = _unpack_bf16_pairs(jnp.concatenate([r[c] for r in y_refs], axis=1))
        acc = acc + y * gates[:, c:c + 1]
    out_ref[...] = _rms_rows(acc, g_ref[...])


def _reduce(token0, n_total, out_so_far, y_chunks, x1, gates_t, final_g):
    n, d = x1.shape
    tb = REDUCE_TOKENS
    n_chunks = len(y_chunks)
    assert token0 % tb == 0
    in_specs = ([pl.BlockSpec((TOP_K, tb, ROW_CHUNK), lambda i: (0, i, 0))] * n_chunks
                + [pl.BlockSpec((tb, d), lambda i: (i, 0)),
                   pl.BlockSpec((tb, TOP_K), lambda i: (i, 0)),
                   pl.BlockSpec((1, d), lambda i: (0, 0))])
    args = [y.reshape(TOP_K, n, ROW_CHUNK) for y in y_chunks] + [x1, gates_t, final_g]
    aliases = {}
    if out_so_far is not None:
        in_specs.append(pl.BlockSpec(memory_space=pl.ANY))
        args.append(out_so_far)
        aliases = {len(args) - 1: 0}
    return pl.pallas_call(
        functools.partial(_reduce_kernel, n_chunks),
        grid=(n // tb,),
        in_specs=in_specs,
        out_specs=pl.BlockSpec((tb, d), lambda i: (i + token0 // tb, 0)),
        out_shape=jax.ShapeDtypeStruct((n_total, d), F32),
        input_output_aliases=aliases,
        compiler_params=_params("parallel"),
        name="reduce",
    )(*args)


def _group_matrix(width):
    a = lax.broadcasted_iota(I32, (width, width), 0) // HEAD_DIM
    b = lax.broadcasted_iota(I32, (width, width), 1) // HEAD_DIM
    g = (a == b).astype(BF16)
    return jnp.concatenate([g, g], axis=0)


def _row(vec):
    return vec.reshape(1, -1).astype(F32)


def kernel(x, w_in, conv_w, conv_norm_g, rwkv_mu, w0, w_up, a0, a_up, g_up, k_k, k_a, r_k,
           gn_w, gn_b, w_out, norm_mix_g, norm_ffn_g, w_router, b_router, w_gu, b_gu, w_down,
           b_down, norm_final_g):
    n_batch, seq, d = x.shape
    n = n_batch * seq
    depth = w_in.shape[0]
    cw = conv_w.shape[2]
    rw = w0.shape[1]
    n_experts = w_router.shape[2]
    decay_lora = w_up.shape[1]
    aaa_lora = a_up.shape[1]
    assert decay_lora + aaa_lora == V7X_LANES
    assert depth == 1
    assert n % MOE_PARTS == 0
    n_part = n // MOE_PARTS
    n_slots = n_part * TOP_K + n_experts * EXPERT_ROWS
    n_blocks = n_slots // EXPERT_ROWS
    n_blocks_padded = -(-n_blocks // V7X_LANES) * V7X_LANES
    seg2 = _group_matrix(rw)

    x2d = x.reshape(n, d)
    for l in range(depth):
        w_up_pad = jnp.concatenate([w_up[l], jnp.zeros((aaa_lora, rw), F32)], axis=0).astype(BF16)
        a_up_pad = jnp.concatenate([jnp.zeros((decay_lora, rw), F32), a_up[l]], axis=0).astype(BF16)
        (yconv, r, lw, k, v, kk, b, gate, bonus) = _mix_prep(
            x2d, seq, _row(norm_mix_g[l]), w_in[l].astype(BF16), conv_w[l].astype(F32),
            _row(conv_norm_g[l]), _row(rwkv_mu[l]), _row(w0[l]), w_up_pad, _row(a0[l]), a_up_pad,
            g_up[l].astype(BF16), _row(k_k[l]), _row(k_a[l]), _row(r_k[l]), seg2)
        shape3 = (n_batch, seq, rw)
        o = _wkv_chunk(r.reshape(shape3), lw.reshape(shape3), k.reshape(shape3),
                       v.reshape(shape3), kk.reshape(shape3), b.reshape(shape3)).reshape(n, rw)
        w_out_b = w_out[l].astype(BF16)
        wr_t = w_router[l].T.astype(F32)
        wr_hi = wr_t.astype(BF16)
        wr_lo = (wr_t - wr_hi.astype(F32)).astype(BF16)
        expert_w = (w_gu[l].astype(F32), w_down[l].astype(F32))
        expert_b = (b_gu[l].reshape(n_experts, 1, -1).astype(F32),
                    b_down[l].reshape(n_experts, 1, -1).astype(F32))
        parts = []
        for token0 in range(0, n, n_part):
            x1, h2_chunks, top_e, gates = _post_mix(
                token0, n_part, o, bonus, gate, yconv, x2d, _row(gn_w[l]), _row(gn_b[l]),
                w_out_b[:cw], w_out_b[cw:], _row(norm_ffn_g[l]), wr_hi, wr_lo,
                b_router[l].reshape(n_experts, 1).astype(F32), seg2)
            dest, meta = _route(top_e, n_experts, n_blocks_padded)
            slot_chunks = [_sc_scatter_rows(h, dest, n_slots) for h in h2_chunks]
            parts.append((token0, x1, gates, dest, meta, slot_chunks))
        out = None
        for token0, x1, gates, dest, meta, slot_chunks in parts:
            y_chunks, expert_w = _experts(meta, slot_chunks, expert_w[0], expert_b[0],
                                          expert_w[1], expert_b[1])
            idx_row = dest.reshape(1, TOP_K * n_part)
            out = _reduce(token0, n, out, [_sc_gather_rows(y, idx_row) for y in y_chunks], x1,
                          gates.T, _row(norm_final_g))
        x2d = out
    return x2d.reshape(n_batch, seq, d)
```

```python
import functools

import jax
import jax.numpy as jnp
from jax import lax
from jax.experimental import pallas as pl
from jax.experimental.pallas import tpu as pltpu
from jax.experimental.pallas import tpu_sc as plsc

F32 = jnp.float32
BF16 = jnp.bfloat16
I32 = jnp.int32

HEAD_DIM = 64
TOP_K = 4
NORM_EPS = 1e-5
GN_EPS = HEAD_DIM * 1e-5
SWIGLU_LIMIT = 7.0
SWIGLU_ALPHA = 1.702

V7X_LANES = 128
V7X_SUBLANES = 8
V7X_VMEM_LIMIT_BYTES = 56 * 1024 * 1024

PREP_ROWS = 256
POST_ROWS = 512
SCAN_FRAMES = 64
ROUTE_TOKENS = 1024
REDUCE_TOKENS = 256
MOE_PARTS = 2
SC_WINDOW = 128
ROW_CHUNK = 256
META_BLOCK_EXPERT = 0
META_USED_BLOCKS = 1
META_VALID_ROWS = 2
EXPERT_ROWS = 512

def _params(*semantics):
    return pltpu.CompilerParams(dimension_semantics=semantics,
                                vmem_limit_bytes=V7X_VMEM_LIMIT_BYTES)


def _split_bf16(x):
    hi = x.astype(BF16)
    lo = (x - hi.astype(F32)).astype(BF16)
    return hi, lo


def _seg_sum(x, seg2):
    hi, lo = _split_bf16(x)
    return jnp.dot(jnp.concatenate([hi, lo], axis=1), seg2, preferred_element_type=F32)


def _pack_bf16_pairs(x):
    bits = pltpu.bitcast(x.astype(BF16).astype(F32), jnp.uint32)
    half = x.shape[1] // 2
    return bits[:, half:] | (bits[:, :half] >> 16)


def _unpack_bf16_pairs(words):
    return jnp.concatenate([pltpu.bitcast(words << 16, F32),
                            pltpu.bitcast(words & jnp.uint32(0xFFFF0000), F32)], axis=1)


def _rms_rows(x, g):
    return x * lax.rsqrt(jnp.mean(x * x, axis=-1, keepdims=True) + NORM_EPS) * g


def _sigmoid(x):
    return 1.0 / (1.0 + jnp.exp(-x))


def _mix_prep_kernel(blocks_per_seq, cw, rw,
                     xprev_ref, x_ref, g_ref, w_in_ref, conv_w_ref, conv_g_ref, mu_ref,
                     w0_ref, w_up_ref, a0_ref, a_up_ref, g_up_ref, k_k_ref, k_a_ref, r_k_ref,
                     seg2_ref,
                     yconv_ref, r_ref, w_ref, k_ref, v_ref, kk_ref, b_ref, gate_ref, bonus_ref,
                     p_scr, z_scr):
    tm = x_ref.shape[0]
    halo = xprev_ref.shape[0]
    first = (pl.program_id(0) % blocks_per_seq) == 0
    xp = xprev_ref[...] * jnp.where(first, 0.0, 1.0)
    xa = jnp.concatenate([xp, x_ref[...]], axis=0)
    h = _rms_rows(xa, g_ref[...])
    p_scr[...] = jnp.dot(h.astype(BF16), w_in_ref[...], preferred_element_type=F32)
    seg2 = seg2_ref[...]

    z_scr[...] = p_scr[:, 2 * cw:3 * cw] * p_scr[:, 0:cw]
    conv = (conv_w_ref[0:1, :] * z_scr[halo - 2:halo - 2 + tm, :]
            + conv_w_ref[1:2, :] * z_scr[halo - 1:halo - 1 + tm, :]
            + conv_w_ref[2:3, :] * z_scr[halo:halo + tm, :])
    y = p_scr[halo:halo + tm, cw:2 * cw] * conv
    ms = _seg_sum(y * y, seg2) * (1.0 / HEAD_DIM)
    yconv_ref[...] = (y * lax.rsqrt(ms + NORM_EPS) * conv_g_ref[...]).astype(yconv_ref.dtype)

    c0 = 3 * cw
    cur = p_scr[halo:halo + tm, c0:]
    prev = p_scr[halo - 1:halo - 1 + tm, c0:]
    q = cur + (prev - cur) * mu_ref[...]
    r = q[:, 0:rw]
    k = q[:, rw:2 * rw]
    v = q[:, 2 * rw:3 * rw]
    lora_wa = q[:, 3 * rw:3 * rw + V7X_LANES]
    lora_g = q[:, 3 * rw + V7X_LANES:]
    w_lin = w0_ref[...] + jnp.dot(jnp.tanh(lora_wa).astype(BF16), w_up_ref[...],
                                  preferred_element_type=F32)
    neg = -w_lin
    softplus = jnp.maximum(neg, 0.0) + jnp.log(1.0 + jnp.exp(-jnp.abs(neg)))
    log_decay = -jnp.exp(-softplus - 0.5)
    a = _sigmoid(a0_ref[...] + jnp.dot(lora_wa.astype(BF16), a_up_ref[...],
                                       preferred_element_type=F32))
    gate = jnp.dot(_sigmoid(lora_g).astype(BF16), g_up_ref[...], preferred_element_type=F32)
    kk = k * k_k_ref[...]
    kk = kk / jnp.maximum(jnp.sqrt(_seg_sum(kk * kk, seg2)), 1e-12)
    k_mod = k * (1.0 + (a - 1.0) * k_a_ref[...])
    bonus = _seg_sum(r * k_mod * r_k_ref[...], seg2) * v
    r_ref[...] = r
    w_ref[...] = log_decay
    k_ref[...] = k_mod
    v_ref[...] = v
    kk_ref[...] = kk
    b_ref[...] = kk * a
    gate_ref[...] = gate
    bonus_ref[...] = bonus


def _mix_prep(x2d, seq, norm_g, w_in, conv_w, conv_g, mu, w0, w_up, a0, a_up, g_up, k_k, k_a,
              r_k, seg2):
    n, d = x2d.shape
    tm = PREP_ROWS
    halo = V7X_SUBLANES
    cw = conv_w.shape[1]
    rw = w0.shape[1]
    in_cols = w_in.shape[1]
    assert seq % tm == 0 and n % tm == 0
    full = lambda a: pl.BlockSpec(a.shape, lambda i: (0,) * a.ndim)
    row_spec = lambda c: pl.BlockSpec((tm, c), lambda i: (i, 0))
    consts = (norm_g, w_in, conv_w, conv_g, mu, w0, w_up, a0, a_up, g_up, k_k, k_a, r_k, seg2)
    outs = [jax.ShapeDtypeStruct((n, cw), BF16)] + [jax.ShapeDtypeStruct((n, rw), F32)] * 8
    return pl.pallas_call(
        functools.partial(_mix_prep_kernel, seq // tm, cw, rw),
        grid=(n // tm,),
        in_specs=[pl.BlockSpec((halo, d), lambda i: (jnp.maximum(i * (tm // halo) - 1, 0), 0)),
                  row_spec(d)] + [full(c) for c in consts],
        out_specs=[row_spec(cw)] + [row_spec(rw)] * 8,
        out_shape=outs,
        scratch_shapes=[pltpu.VMEM((halo + tm, in_cols), F32), pltpu.VMEM((halo + tm, cw), F32)],
        compiler_params=_params("parallel"),
        name="mix_prep",
    )(x2d, x2d, *consts)


def _bdot(a, b):
    return jnp.dot(a.astype(BF16), b.astype(BF16), preferred_element_type=F32)


def _bdot_nt(a, b):
    return lax.dot_general(a.astype(BF16), b.astype(BF16), (((1,), (1,)), ((), ())),
                           preferred_element_type=F32)


def _wkv_chunk_kernel(n_batch, n_pairs,
                      r_ref, lw_ref, k_ref, v_ref, kk_ref, b_ref, o_ref, state_ref):
    frames = r_ref.shape[1]
    lanes = V7X_LANES
    hd = HEAD_DIM
    assert frames == hd

    @pl.when(pl.program_id(0) == 0)
    def _():
        state_ref[...] = jnp.zeros_like(state_ref)

    row = lax.broadcasted_iota(I32, (lanes, lanes), 0)
    lane = lax.broadcasted_iota(I32, (lanes, lanes), 1)
    same_head = (row // hd) == (lane // hd)
    rt = row % hd
    ls = lane % hd
    strict_same = same_head & (rt > ls)
    strict_cross = jnp.logical_not(same_head) & (rt > ls)
    incl_same = same_head & (rt >= ls)
    incl_cross = jnp.logical_not(same_head) & (rt >= ls)
    eye = row == lane
    level_masks = []
    m = 1
    while m < hd:
        level_masks.append(same_head & ((rt // (2 * m)) == (ls // (2 * m)))
                           & (((rt // m) % 2) == 1) & (((ls // m) % 2) == 0))
        m *= 2
    left = lax.broadcasted_iota(I32, (frames, lanes), 1) < hd
    tri = (lax.broadcasted_iota(I32, (frames, frames), 0)
           >= lax.broadcasted_iota(I32, (frames, frames), 1)).astype(BF16)

    def diag_blocks(x):
        return jnp.concatenate([jnp.where(left, x, 0.0), jnp.where(left, 0.0, x)], axis=0)

    def cross_blocks(x):
        return jnp.concatenate([jnp.where(left, 0.0, x), jnp.where(left, x, 0.0)], axis=0)

    def batch_prep(bi):
        lw = lw_ref[bi]
        hi = lw.astype(BF16)
        rem = lw - hi.astype(F32)
        mid = rem.astype(BF16)
        lo = (rem - mid.astype(F32)).astype(BF16)
        cs = (jnp.dot(tri, hi, preferred_element_type=F32)
              + jnp.dot(tri, mid, preferred_element_type=F32)
              + jnp.dot(tri, lo, preferred_element_type=F32))
        cs_end = cs[frames - 1:frames, :]
        e_neg = jnp.exp(-cs)
        e_end = jnp.exp(cs_end - cs)
        b_in = b_ref[bi]
        k_in = k_ref[bi]
        return dict(a_t=-kk_ref[bi] * jnp.exp(cs - lw), b_t=b_in * e_neg, k_t=k_in * e_neg,
                    r_t=r_ref[bi] * jnp.exp(cs), b_h=b_in * e_end, k_h=k_in * e_end,
                    v=v_ref[bi], g_end=jnp.exp(cs_end))

    group = []
    for bi in range(n_batch):
        prep = batch_prep(bi)
        for pi in range(n_pairs):
            sl = slice(pi * lanes, (pi + 1) * lanes)
            group.append((bi, sl, bi * n_pairs + pi, {n: x[:, sl] for n, x in prep.items()}))
    bf = lambda x: x.astype(BF16)
    a_bd = [bf(diag_blocks(p["a_t"])) for _, _, _, p in group]
    r_bd = [bf(diag_blocks(p["r_t"])) for _, _, _, p in group]
    d0 = [_bdot_nt(jnp.concatenate([a[:hd], r[:hd]], axis=0),
                   bf(jnp.concatenate([p["b_t"], p["k_t"]], axis=0)))
          for a, r, (_, _, _, p) in zip(a_bd, r_bd, group)]
    d1 = [_bdot_nt(jnp.concatenate([a[hd:], r[hd:]], axis=0),
                   bf(jnp.concatenate([p["k_t"], p["b_t"]], axis=0)))
          for a, r, (_, _, _, p) in zip(a_bd, r_bd, group)]
    a_rows = [jnp.concatenate([x[:hd], y[:hd]], axis=0) for x, y in zip(d0, d1)]
    m_rows = [jnp.concatenate([x[hd:], y[hd:]], axis=0) for x, y in zip(d0, d1)]
    a_ab = [bf(jnp.where(strict_same, x, 0.0)) for x in a_rows]
    a_ak = [bf(jnp.where(strict_cross, x, 0.0)) for x in a_rows]
    m_rb = [bf(jnp.where(incl_same, x, 0.0)) for x in m_rows]
    m_rk = [bf(jnp.where(incl_cross, x, 0.0)) for x in m_rows]
    t_inv = [jnp.where(eye, 1.0, jnp.where(level_masks[0], x, 0.0)) for x in a_rows]
    for mask in level_masks[1:]:
        t_bf = [bf(t) for t in t_inv]
        half = [jnp.where(mask, _bdot(t, x), 0.0) for t, x in zip(t_bf, a_ab)]
        t_inv = [t + _bdot(h, tb) for t, h, tb in zip(t_inv, half, t_bf)]
    t_bf = [bf(t) for t in t_inv]
    v_x = [bf(cross_blocks(p["v"])) for _, _, _, p in group]
    akv = [_bdot(x, v) for x, v in zip(a_ak, v_x)]
    wu = [bf(_bdot(t, jnp.concatenate([a, bf(x)], axis=1)))
          for t, a, x in zip(t_bf, a_bd, akv)]
    bh_t = [bf(diag_blocks(p["b_h"]).T) for _, _, _, p in group]
    kh_t = [bf(cross_blocks(p["k_h"]).T) for _, _, _, p in group]
    pw_rw = [_bdot(jnp.concatenate([bt, m], axis=0), x[:, :lanes])
             for bt, m, x in zip(bh_t, m_rb, wu)]
    q_o = [_bdot(jnp.concatenate([jnp.concatenate([bt, kt], axis=1),
                                  jnp.concatenate([mb, mk], axis=1)], axis=0),
                 jnp.concatenate([x[:, lanes:], v], axis=0))
           for bt, kt, mb, mk, x, v in zip(bh_t, kh_t, m_rb, m_rk, wu, v_x)]
    res = [_bdot(jnp.concatenate([diag_blocks(p["r_t"]) + pr[lanes:],
                                  pr[:lanes] + jnp.where(eye, p["g_end"], 0.0)], axis=0),
                 state_ref[idx])
           for pr, (_, _, idx, p) in zip(pw_rw, group)]
    for x, q, (bi, sl, idx, _) in zip(res, q_o, group):
        o_bd = x[:lanes] + q[lanes:]
        state_ref[idx] = x[lanes:] + q[:lanes]
        o_ref[bi, :, sl] = jnp.where(left, o_bd[:hd], o_bd[hd:])


def _wkv_chunk(r, lw, k, v, kk, b):
    n_batch, seq, rw = r.shape
    n_pairs = rw // V7X_LANES
    frames = SCAN_FRAMES
    assert seq % frames == 0
    spec = pl.BlockSpec((n_batch, frames, rw), lambda c: (0, c, 0))
    return pl.pallas_call(
        functools.partial(_wkv_chunk_kernel, n_batch, n_pairs),
        grid=(seq // frames,),
        in_specs=[spec] * 6,
        out_specs=spec,
        out_shape=jax.ShapeDtypeStruct((n_batch, seq, rw), F32),
        scratch_shapes=[pltpu.VMEM((n_batch * n_pairs, V7X_LANES, V7X_LANES), F32)],
        compiler_params=_params("arbitrary"),
        name="wkv_chunk",
    )(r, lw, k, v, kk, b)


def _post_mix_kernel(n_experts,
                     o_ref, bonus_ref, gate_ref, yconv_ref, x_ref, gn_w_ref, gn_b_ref,
                     w_out_c_ref, w_out_r_ref, ffn_g_ref, wr_hi_ref, wr_lo_ref, b_router_ref,
                     seg2_ref,
                     x1_ref, *outs):
    h2_refs, (top_e_ref, gates_ref) = outs[:-2], outs[-2:]
    seg2 = seg2_ref[...]
    o = o_ref[...]
    mean = _seg_sum(o, seg2) * (1.0 / HEAD_DIM)
    cen = o - mean
    var = _seg_sum(cen * cen, seg2) * (1.0 / HEAD_DIM)
    o = cen * lax.rsqrt(var + GN_EPS) * gn_w_ref[...] + gn_b_ref[...]
    y_rwkv = (o + bonus_ref[...]) * gate_ref[...]
    x1 = (x_ref[...]
          + jnp.dot(yconv_ref[...], w_out_c_ref[...], preferred_element_type=F32)
          + jnp.dot(y_rwkv.astype(BF16), w_out_r_ref[...], preferred_element_type=F32))
    x1_ref[...] = x1
    h2 = _rms_rows(x1, ffn_g_ref[...])
    words = _pack_bf16_pairs(h2)
    for c, h2_ref in enumerate(h2_refs):
        h2_ref[...] = words[:, c * ROW_CHUNK:(c + 1) * ROW_CHUNK]

    h_hi, h_lo = _split_bf16(h2)
    nt = (((1,), (1,)), ((), ()))
    logits = (lax.dot_general(wr_hi_ref[...], h_hi, nt, preferred_element_type=F32)
              + lax.dot_general(wr_hi_ref[...], h_lo, nt, preferred_element_type=F32)
              + lax.dot_general(wr_lo_ref[...], h_hi, nt, preferred_element_type=F32)
              + b_router_ref[...])
    e_id = lax.broadcasted_iota(I32, logits.shape, 0).astype(F32)
    work = logits
    tops, ids = [], []
    for _ in range(TOP_K):
        m = jnp.max(work, axis=0, keepdims=True)
        sel = jnp.min(jnp.where(work == m, e_id, float(n_experts)), axis=0, keepdims=True)
        tops.append(m)
        ids.append(sel)
        work = jnp.where(e_id == sel, -jnp.inf, work)
    ex = [jnp.exp(t - tops[0]) for t in tops]
    denom = ex[0] + ex[1] + ex[2] + ex[3]
    top_e_ref[...] = jnp.concatenate(ids, axis=0).astype(I32)
    gates_ref[...] = jnp.concatenate([e / denom for e in ex], axis=0)


def _post_mix(token0, n, o, bonus, gate, yconv, x2d, gn_w, gn_b, w_out_c, w_out_r, ffn_g, wr_hi,
              wr_lo, b_router, seg2):
    d = x2d.shape[1]
    rw = o.shape[1]
    cw = yconv.shape[1]
    n_experts = wr_hi.shape[0]
    tm = POST_ROWS
    assert token0 % tm == 0 and n % tm == 0
    full = lambda a: pl.BlockSpec(a.shape, lambda i: (0,) * a.ndim)
    in_spec = lambda c: pl.BlockSpec((tm, c), lambda i: (i + token0 // tm, 0))
    row_spec = lambda c: pl.BlockSpec((tm, c), lambda i: (i, 0))
    col_spec = pl.BlockSpec((TOP_K, tm), lambda i: (0, i))
    consts = (gn_w, gn_b, w_out_c, w_out_r, ffn_g, wr_hi, wr_lo, b_router, seg2)
    n_chunks = d // 2 // ROW_CHUNK
    outs = pl.pallas_call(
        functools.partial(_post_mix_kernel, n_experts),
        grid=(n // tm,),
        in_specs=[in_spec(rw), in_spec(rw), in_spec(rw), in_spec(cw), in_spec(d)]
                 + [full(c) for c in consts],
        out_specs=[row_spec(d)] + [row_spec(ROW_CHUNK)] * n_chunks + [col_spec, col_spec],
        out_shape=[jax.ShapeDtypeStruct((n, d), F32)]
                  + [jax.ShapeDtypeStruct((n, ROW_CHUNK), jnp.uint32)] * n_chunks
                  + [jax.ShapeDtypeStruct((TOP_K, n), I32), jax.ShapeDtypeStruct((TOP_K, n), F32)],
        compiler_params=_params("parallel"),
        name="post_mix",
    )(o, bonus, gate, yconv, x2d, *consts)
    return outs[0], outs[1:1 + n_chunks], outs[-2], outs[-1]


def _route_kernel(n_experts, block_rows,
                  top_e_ref, tri_ref, dest_ref, meta_ref,
                  count_ref, start_ref, carry_ref):
    phase = pl.program_id(0)
    j = pl.program_id(1)
    tb = top_e_ref.shape[1]
    e_id = lax.broadcasted_iota(I32, (n_experts, tb), 0)
    top_e = top_e_ref[...]
    onehot = jnp.zeros((n_experts, tb), F32)
    for c in range(TOP_K):
        onehot = onehot + jnp.where(top_e[c:c + 1, :] == e_id, 1.0, 0.0)
    block_count = jnp.sum(onehot, axis=1, keepdims=True)

    @pl.when((phase == 0) & (j == 0))
    def _():
        count_ref[...] = jnp.zeros_like(count_ref)

    @pl.when(phase == 0)
    def _():
        count_ref[...] += block_count

    @pl.when((phase == 1) & (j == 0))
    def _():
        counts = count_ref[...]
        padded = jnp.maximum(jnp.ceil(counts * (1.0 / block_rows)), 1.0) * block_rows
        sub = lax.broadcasted_iota(I32, (n_experts, n_experts), 0)
        lane = lax.broadcasted_iota(I32, (n_experts, n_experts), 1)
        padded_row = jnp.sum(jnp.where(sub == lane, padded, 0.0), axis=0, keepdims=True)
        start = jnp.sum(jnp.where(lane < sub, padded_row, 0.0), axis=1, keepdims=True)
        start_ref[...] = start
        carry_ref[...] = jnp.zeros_like(carry_ref)
        end = start + padded
        nb = meta_ref.shape[1]
        slot0 = (lax.broadcasted_iota(I32, (n_experts, nb), 1) * block_rows).astype(F32)
        block_e = jnp.sum(jnp.where(end <= slot0, 1.0, 0.0), axis=0, keepdims=True)
        block_e = jnp.minimum(block_e, n_experts - 1.0)
        used = jnp.max(end, axis=0, keepdims=True) * (1.0 / block_rows)
        in_region = (start <= slot0) & (slot0 < end)
        valid = jnp.sum(jnp.where(in_region, jnp.minimum(start + counts - slot0, block_rows), 0.0),
                        axis=0, keepdims=True)
        row = lax.broadcasted_iota(I32, meta_ref.shape, 0)
        meta_ref[...] = jnp.where(row == META_BLOCK_EXPERT, block_e,
                                  jnp.where(row == META_VALID_ROWS, valid, used)).astype(I32)

    @pl.when(phase == 1)
    def _():
        incl = jnp.dot(onehot.astype(BF16), tri_ref[...], preferred_element_type=F32)
        base = incl - onehot + carry_ref[...] + start_ref[...]
        rows = [jnp.sum(jnp.where(top_e[c:c + 1, :] == e_id, base, 0.0), axis=0, keepdims=True)
                for c in range(TOP_K)]
        dest_ref[...] = jnp.concatenate(rows, axis=0).astype(I32)
        carry_ref[...] += block_count


def _route(top_e, n_experts, n_blocks_padded):
    n = top_e.shape[1]
    tb = ROUTE_TOKENS
    assert n % tb == 0
    tri = (lax.broadcasted_iota(I32, (tb, tb), 0) <= lax.broadcasted_iota(I32, (tb, tb), 1)
           ).astype(BF16)
    return pl.pallas_call(
        functools.partial(_route_kernel, n_experts, EXPERT_ROWS),
        grid=(2, n // tb),
        in_specs=[pl.BlockSpec((TOP_K, tb), lambda ph, j: (0, j)),
                  pl.BlockSpec((tb, tb), lambda ph, j: (0, 0))],
        out_specs=[pl.BlockSpec((TOP_K, tb), lambda ph, j: (0, j * ph)),
                   pl.BlockSpec((V7X_SUBLANES, n_blocks_padded), lambda ph, j: (0, 0))],
        out_shape=[jax.ShapeDtypeStruct((TOP_K, n), I32),
                   jax.ShapeDtypeStruct((V7X_SUBLANES, n_blocks_padded), I32)],
        scratch_shapes=[pltpu.VMEM((n_experts, 1), F32)] * 3,
        compiler_params=_params("arbitrary", "arbitrary"),
        name="route",
    )(top_e, tri)


def _sc_mesh():
    return plsc.VectorSubcoreMesh(core_axis_name="core", subcore_axis_name="subcore")


def _sc_scatter_rows(src, dest, n_slots):
    n, width = src.shape
    assert n % SC_WINDOW == 0

    @functools.partial(pl.kernel, out_type=jax.ShapeDtypeStruct((n_slots, width), src.dtype),
                       mesh=_sc_mesh(), name="sc_scatter_rows")
    def scatter(src_hbm, dest_hbm, out_hbm):
        def body(src_vmem, dest_vmem):
            for c in range(TOP_K):
                pltpu.sync_copy(src_vmem, out_hbm.at[dest_vmem.at[c]])

        pltpu.emit_pipeline(
            body,
            grid=(n // SC_WINDOW,),
            in_specs=[pl.BlockSpec((SC_WINDOW, width), lambda i: (i, 0)),
                      pl.BlockSpec((TOP_K, SC_WINDOW), lambda i: (0, i))],
            out_specs=[],
            core_axis_name=("core", "subcore"),
            dimension_semantics=(pltpu.PARALLEL,),
        )(src_hbm, dest_hbm)

    return scatter(src, dest)


def _experts_kernel(d_ff, n_in, n_out, cast_weights, meta_ref, *refs):
    x_refs, refs = refs[:n_in], refs[n_in:]
    w_gu_ref, b_gu_ref, w_down_ref, b_down_ref = refs[:4]
    y_refs = refs[4:4 + n_out]
    w_gu_mxu, w_down_mxu = refs[4 + n_out:] if cast_weights else (w_gu_ref, w_down_ref)
    j = pl.program_id(0)
    used = meta_ref[META_USED_BLOCKS, 0]

    if cast_weights:
        expert = meta_ref[META_BLOCK_EXPERT, j]
        prev_expert = meta_ref[META_BLOCK_EXPERT, jnp.maximum(j - 1, 0)]

        @pl.when((j < used) & ((j == 0) | (expert != prev_expert)))
        def _():
            w_gu_mxu[0] = w_gu_ref[0].astype(BF16)
            w_down_mxu[0] = w_down_ref[0].astype(BF16)

    valid = meta_ref[META_VALID_ROWS, j]
    bm = y_refs[0].shape[0]

    def run_rows(rows):
        x = _unpack_bf16_pairs(jnp.concatenate([r[0:rows, :] for r in x_refs], axis=1))
        row = lax.broadcasted_iota(I32, x.shape, 0)
        x = jnp.where(row < valid, x, 0.0)
        gu = jnp.dot(x.astype(BF16), w_gu_mxu[0], preferred_element_type=F32) + b_gu_ref[0]
        gate = jnp.minimum(gu[:, :d_ff], SWIGLU_LIMIT)
        up = jnp.clip(gu[:, d_ff:], -SWIGLU_LIMIT, SWIGLU_LIMIT)
        act = (up + 1.0) * (gate * _sigmoid(SWIGLU_ALPHA * gate))
        y = jnp.dot(act.astype(BF16), w_down_mxu[0], preferred_element_type=F32) + b_down_ref[0]
        words = _pack_bf16_pairs(y)
        for c, y_ref in enumerate(y_refs):
            y_ref[0:rows, :] = words[:, c * ROW_CHUNK:(c + 1) * ROW_CHUNK]
            if rows < bm:
                y_ref[rows:, :] = jnp.zeros((bm - rows, ROW_CHUNK), y_ref.dtype)

    @pl.when((j < used) & (valid > bm // 2))
    def _():
        run_rows(bm)

    @pl.when((j < used) & (valid <= bm // 2))
    def _():
        run_rows(bm // 2)

    @pl.when(j >= used)
    def _():
        for y_ref in y_refs:
            y_ref[...] = jnp.zeros_like(y_ref)


def _experts(meta, slot_chunks, w_gu, b_gu, w_down, b_down):
    n_in = len(slot_chunks)
    n_slots = slot_chunks[0].shape[0]
    n_experts, d, two_ff = w_gu.shape
    n_out = d // 2 // ROW_CHUNK
    d_ff = two_ff // 2
    bm = EXPERT_ROWS
    cast_weights = w_gu.dtype != BF16
    by_expert = lambda j, m: (m[META_BLOCK_EXPERT, j], 0, 0)
    out_specs = [pl.BlockSpec((bm, ROW_CHUNK), lambda j, m: (j, 0))] * n_out
    out_shape = [jax.ShapeDtypeStruct((n_slots, ROW_CHUNK), jnp.uint32)] * n_out
    if cast_weights:
        out_specs += [pl.BlockSpec((1, d, two_ff), by_expert), pl.BlockSpec((1, d_ff, d), by_expert)]
        out_shape += [jax.ShapeDtypeStruct(w_gu.shape, BF16), jax.ShapeDtypeStruct(w_down.shape, BF16)]
    grid_spec = pltpu.PrefetchScalarGridSpec(
        num_scalar_prefetch=1,
        grid=(n_slots // bm,),
        in_specs=[pl.BlockSpec((bm, ROW_CHUNK),
                               lambda j, m: (jnp.minimum(j, m[META_USED_BLOCKS, 0] - 1), 0))
                  ] * n_in
                 + [pl.BlockSpec((1, d, two_ff), by_expert), pl.BlockSpec((1, 1, two_ff), by_expert),
                    pl.BlockSpec((1, d_ff, d), by_expert), pl.BlockSpec((1, 1, d), by_expert)],
        out_specs=out_specs,
    )
    outs = pl.pallas_call(
        functools.partial(_experts_kernel, d_ff, n_in, n_out, cast_weights),
        grid_spec=grid_spec,
        out_shape=out_shape,
        compiler_params=_params("arbitrary"),
        name="experts",
    )(meta, *slot_chunks, w_gu, b_gu, w_down, b_down)
    return outs[:n_out], (tuple(outs[n_out:]) if cast_weights else (w_gu, w_down))


def _sc_gather_rows(table, idx_row):
    n_idx = idx_row.shape[1]
    width = table.shape[1]
    assert n_idx % SC_WINDOW == 0

    @functools.partial(pl.kernel, out_type=jax.ShapeDtypeStruct((n_idx, width), table.dtype),
                       mesh=_sc_mesh(), name="sc_gather_rows")
    def gather(table_hbm, idx_hbm, out_hbm):
        def body(idx_vmem, out_vmem):
            pltpu.sync_copy(table_hbm.at[idx_vmem.at[0]], out_vmem)

        pltpu.emit_pipeline(
            body,
            grid=(n_idx // SC_WINDOW,),
            in_specs=[pl.BlockSpec((1, SC_WINDOW), lambda i: (0, i))],
            out_specs=[pl.BlockSpec((SC_WINDOW, width), lambda i: (i, 0))],
            core_axis_name=("core", "subcore"),
            dimension_semantics=(pltpu.PARALLEL,),
        )(idx_hbm, out_hbm)

    return gather(table, idx_row)


def _reduce_kernel(n_chunks, *refs):
    y_refs = refs[:n_chunks]
    x1_ref, gates_ref, g_ref = refs[n_chunks:n_chunks + 3]
    out_ref = refs[-1]
    gates = gates_ref[...]
    acc = x1_ref[...]
    for c in range(TOP_K):
        y = _unpack_bf16_pairs(jnp.concatenate([r[c] for r in y_refs], axis=1))
        acc = acc + y * gates[:, c:c + 1]
    out_ref[...] = _rms_rows(acc, g_ref[...])


def _reduce(token0, n_total, out_so_far, y_chunks, x1, gates_t, final_g):
    n, d = x1.shape
    tb = REDUCE_TOKENS
    n_chunks = len(y_chunks)
    assert token0 % tb == 0
    in_specs = ([pl.BlockSpec((TOP_K, tb, ROW_CHUNK), lambda i: (0, i, 0))] * n_chunks
                + [pl.BlockSpec((tb, d), lambda i: (i, 0)),
                   pl.BlockSpec((tb, TOP_K), lambda i: (i, 0)),
                   pl.BlockSpec((1, d), lambda i: (0, 0))])
    args = [y.reshape(TOP_K, n, ROW_CHUNK) for y in y_chunks] + [x1, gates_t, final_g]
    aliases = {}
    if out_so_far is not None:
        in_specs.append(pl.BlockSpec(memory_space=pl.ANY))
        args.append(out_so_far)
        aliases = {len(args) - 1: 0}
    return pl.pallas_call(
        functools.partial(_reduce_kernel, n_chunks),
        grid=(n // tb,),
        in_specs=in_specs,
        out_specs=pl.BlockSpec((tb, d), lambda i: (i + token0 // tb, 0)),
        out_shape=jax.ShapeDtypeStruct((n_total, d), F32),
        input_output_aliases=aliases,
        compiler_params=_params("parallel"),
        name="reduce",
    )(*args)


def _group_matrix(width):
    a = lax.broadcasted_iota(I32, (width, width), 0) // HEAD_DIM
    b = lax.broadcasted_iota(I32, (width, width), 1) // HEAD_DIM
    g = (a == b).astype(BF16)
    return jnp.concatenate([g, g], axis=0)


def _row(vec):
    return vec.reshape(1, -1).astype(F32)


def kernel(x, w_in, conv_w, conv_norm_g, rwkv_mu, w0, w_up, a0, a_up, g_up, k_k, k_a, r_k,
           gn_w, gn_b, w_out, norm_mix_g, norm_ffn_g, w_router, b_router, w_gu, b_gu, w_down,
           b_down, norm_final_g):
    n_batch, seq, d = x.shape
    n = n_batch * seq
    depth = w_in.shape[0]
    cw = conv_w.shape[2]
    rw = w0.shape[1]
    n_experts = w_router.shape[2]
    decay_lora = w_up.shape[1]
    aaa_lora = a_up.shape[1]
    assert decay_lora + aaa_lora == V7X_LANES
    assert depth == 1
    assert n % MOE_PARTS == 0
    n_part = n // MOE_PARTS
    n_slots = n_part * TOP_K + n_experts * EXPERT_ROWS
    n_blocks = n_slots // EXPERT_ROWS
    n_blocks_padded = -(-n_blocks // V7X_LANES) * V7X_LANES
    seg2 = _group_matrix(rw)

    x2d = x.reshape(n, d)
    for l in range(depth):
        w_up_pad = jnp.concatenate([w_up[l], jnp.zeros((aaa_lora, rw), F32)], axis=0).astype(BF16)
        a_up_pad = jnp.concatenate([jnp.zeros((decay_lora, rw), F32), a_up[l]], axis=0).astype(BF16)
        (yconv, r, lw, k, v, kk, b, gate, bonus) = _mix_prep(
            x2d, seq, _row(norm_mix_g[l]), w_in[l].astype(BF16), conv_w[l].astype(F32),
            _row(conv_norm_g[l]), _row(rwkv_mu[l]), _row(w0[l]), w_up_pad, _row(a0[l]), a_up_pad,
            g_up[l].astype(BF16), _row(k_k[l]), _row(k_a[l]), _row(r_k[l]), seg2)
        shape3 = (n_batch, seq, rw)
        o = _wkv_chunk(r.reshape(shape3), lw.reshape(shape3), k.reshape(shape3),
                       v.reshape(shape3), kk.reshape(shape3), b.reshape(shape3)).reshape(n, rw)
        w_out_b = w_out[l].astype(BF16)
        wr_t = w_router[l].T.astype(F32)
        wr_hi = wr_t.astype(BF16)
        wr_lo = (wr_t - wr_hi.astype(F32)).astype(BF16)
        expert_w = (w_gu[l].astype(F32), w_down[l].astype(F32))
        expert_b = (b_gu[l].reshape(n_experts, 1, -1).astype(F32),
                    b_down[l].reshape(n_experts, 1, -1).astype(F32))
        parts = []
        for token0 in range(0, n, n_part):
            x1, h2_chunks, top_e, gates = _post_mix(
                token0, n_part, o, bonus, gate, yconv, x2d, _row(gn_w[l]), _row(gn_b[l]),
                w_out_b[:cw], w_out_b[cw:], _row(norm_ffn_g[l]), wr_hi, wr_lo,
                b_router[l].reshape(n_experts, 1).astype(F32), seg2)
            dest, meta = _route(top_e, n_experts, n_blocks_padded)
            slot_chunks = [_sc_scatter_rows(h, dest, n_slots) for h in h2_chunks]
            parts.append((token0, x1, gates, dest, meta, slot_chunks))
        out = None
        for token0, x1, gates, dest, meta, slot_chunks in parts:
            y_chunks, expert_w = _experts(meta, slot_chunks, expert_w[0], expert_b[0],
                                          expert_w[1], expert_b[1])
            idx_row = dest.reshape(1, TOP_K * n_part)
            out = _reduce(token0, n, out, [_sc_gather_rows(y, idx_row) for y in y_chunks], x1,
                          gates.T, _row(norm_final_g))
        x2d = out
    return x2d.reshape(n_batch, seq, d)
```

```python
import functools

import jax
import jax.numpy as jnp
from jax import lax
from jax.experimental import pallas as pl
from jax.experimental.pallas import tpu as pltpu
from jax.experimental.pallas import tpu_sc as plsc

F32 = jnp.float32
BF16 = jnp.bfloat16
I32 = jnp.int32

HEAD_DIM = 64
TOP_K = 4
NORM_EPS = 1e-5
GN_EPS = HEAD_DIM * 1e-5
SWIGLU_LIMIT = 7.0
SWIGLU_ALPHA = 1.702

V7X_LANES = 128
V7X_SUBLANES = 8
V7X_VMEM_LIMIT_BYTES = 56 * 1024 * 1024

PREP_ROWS = 256
POST_ROWS = 512
SCAN_FRAMES = 64
ROUTE_TOKENS = 1024
REDUCE_TOKENS = 256
MOE_PARTS = 2
SC_WINDOW = 128
ROW_CHUNK = 256
META_BLOCK_EXPERT = 0
META_USED_BLOCKS = 1
META_VALID_ROWS = 2
EXPERT_ROWS = 512

def _params(*semantics):
    return pltpu.CompilerParams(dimension_semantics=semantics,
                                vmem_limit_bytes=V7X_VMEM_LIMIT_BYTES)


def _split_bf16(x):
    hi = x.astype(BF16)
    lo = (x - hi.astype(F32)).astype(BF16)
    return hi, lo


def _seg_sum(x, seg2):
    hi, lo = _split_bf16(x)
    return jnp.dot(jnp.concatenate([hi, lo], axis=1), seg2, preferred_element_type=F32)


def _pack_bf16_pairs(x):
    bits = pltpu.bitcast(x.astype(BF16).astype(F32), jnp.uint32)
    half = x.shape[1] // 2
    return bits[:, half:] | (bits[:, :half] >> 16)


def _unpack_bf16_pairs(words):
    return jnp.concatenate([pltpu.bitcast(words << 16, F32),
                            pltpu.bitcast(words & jnp.uint32(0xFFFF0000), F32)], axis=1)


def _rms_rows(x, g):
    return x * lax.rsqrt(jnp.mean(x * x, axis=-1, keepdims=True) + NORM_EPS) * g


def _sigmoid(x):
    return 1.0 / (1.0 + jnp.exp(-x))


def _mix_prep_kernel(blocks_per_seq, cw, rw,
                     xprev_ref, x_ref, g_ref, w_in_ref, conv_w_ref, conv_g_ref, mu_ref,
                     w0_ref, w_up_ref, a0_ref, a_up_ref, g_up_ref, k_k_ref, k_a_ref, r_k_ref,
                     seg2_ref,
                     yconv_ref, r_ref, w_ref, k_ref, v_ref, kk_ref, b_ref, gate_ref, bonus_ref,
                     p_scr, z_scr):
    tm = x_ref.shape[0]
    halo = xprev_ref.shape[0]
    first = (pl.program_id(0) % blocks_per_seq) == 0
    xp = xprev_ref[...] * jnp.where(first, 0.0, 1.0)
    xa = jnp.concatenate([xp, x_ref[...]], axis=0)
    h = _rms_rows(xa, g_ref[...])
    p_scr[...] = jnp.dot(h.astype(BF16), w_in_ref[...], preferred_element_type=F32)
    seg2 = seg2_ref[...]

    z_scr[...] = p_scr[:, 2 * cw:3 * cw] * p_scr[:, 0:cw]
    conv = (conv_w_ref[0:1, :] * z_scr[halo - 2:halo - 2 + tm, :]
            + conv_w_ref[1:2, :] * z_scr[halo - 1:halo - 1 + tm, :]
            + conv_w_ref[2:3, :] * z_scr[halo:halo + tm, :])
    y = p_scr[halo:halo + tm, cw:2 * cw] * conv
    ms = _seg_sum(y * y, seg2) * (1.0 / HEAD_DIM)
    yconv_ref[...] = (y * lax.rsqrt(ms + NORM_EPS) * conv_g_ref[...]).astype(yconv_ref.dtype)

    c0 = 3 * cw
    cur = p_scr[halo:halo + tm, c0:]
    prev = p_scr[halo - 1:halo - 1 + tm, c0:]
    q = cur + (prev - cur) * mu_ref[...]
    r = q[:, 0:rw]
    k = q[:, rw:2 * rw]
    v = q[:, 2 * rw:3 * rw]
    lora_wa = q[:, 3 * rw:3 * rw + V7X_LANES]
    lora_g = q[:, 3 * rw + V7X_LANES:]
    w_lin = w0_ref[...] + jnp.dot(jnp.tanh(lora_wa).astype(BF16), w_up_ref[...],
                                  preferred_element_type=F32)
    neg = -w_lin
    softplus = jnp.maximum(neg, 0.0) + jnp.log(1.0 + jnp.exp(-jnp.abs(neg)))
    log_decay = -jnp.exp(-softplus - 0.5)
    a = _sigmoid(a0_ref[...] + jnp.dot(lora_wa.astype(BF16), a_up_ref[...],
                                       preferred_element_type=F32))
    gate = jnp.dot(_sigmoid(lora_g).astype(BF16), g_up_ref[...], preferred_element_type=F32)
    kk = k * k_k_ref[...]
    kk = kk / jnp.maximum(jnp.sqrt(_seg_sum(kk * kk, seg2)), 1e-12)
    k_mod = k * (1.0 + (a - 1.0) * k_a_ref[...])
    bonus = _seg_sum(r * k_mod * r_k_ref[...], seg2) * v
    r_ref[...] = r
    w_ref[...] = log_decay
    k_ref[...] = k_mod
    v_ref[...] = v
    kk_ref[...] = kk
    b_ref[...] = kk * a
    gate_ref[...] = gate
    bonus_ref[...] = bonus


def _mix_prep(x2d, seq, norm_g, w_in, conv_w, conv_g, mu, w0, w_up, a0, a_up, g_up, k_k, k_a,
              r_k, seg2):
    n, d = x2d.shape
    tm = PREP_ROWS
    halo = V7X_SUBLANES
    cw = conv_w.shape[1]
    rw = w0.shape[1]
    in_cols = w_in.shape[1]
    assert seq % tm == 0 and n % tm == 0
    full = lambda a: pl.BlockSpec(a.shape, lambda i: (0,) * a.ndim)
    row_spec = lambda c: pl.BlockSpec((tm, c), lambda i: (i, 0))
    consts = (norm_g, w_in, conv_w, conv_g, mu, w0, w_up, a0, a_up, g_up, k_k, k_a, r_k, seg2)
    outs = [jax.ShapeDtypeStruct((n, cw), BF16)] + [jax.ShapeDtypeStruct((n, rw), F32)] * 8
    return pl.pallas_call(
        functools.partial(_mix_prep_kernel, seq // tm, cw, rw),
        grid=(n // tm,),
        in_specs=[pl.BlockSpec((halo, d), lambda i: (jnp.maximum(i * (tm // halo) - 1, 0), 0)),
                  row_spec(d)] + [full(c) for c in consts],
        out_specs=[row_spec(cw)] + [row_spec(rw)] * 8,
        out_shape=outs,
        scratch_shapes=[pltpu.VMEM((halo + tm, in_cols), F32), pltpu.VMEM((halo + tm, cw), F32)],
        compiler_params=_params("parallel"),
        name="mix_prep",
    )(x2d, x2d, *consts)


def _bdot(a, b):
    return jnp.dot(a.astype(BF16), b.astype(BF16), preferred_element_type=F32)


def _bdot_nt(a, b):
    return lax.dot_general(a.astype(BF16), b.astype(BF16), (((1,), (1,)), ((), ())),
                           preferred_element_type=F32)


def _wkv_chunk_kernel(n_batch, n_pairs, n_side,
                      r_ref, lw_ref, k_ref, v_ref, kk_ref, b_ref, *refs):
    frames = r_ref.shape[1]
    lanes = V7X_LANES
    hd = HEAD_DIM
    assert frames == hd
    side_in, o_ref, side_out, state_ref = (refs[:n_side], refs[n_side],
                                           refs[n_side + 1:2 * n_side + 1], refs[-1])
    for src, dst in zip(side_in, side_out):
        dst[...] = src[...].astype(dst.dtype)

    @pl.when(pl.program_id(0) == 0)
    def _():
        state_ref[...] = jnp.zeros_like(state_ref)

    row = lax.broadcasted_iota(I32, (lanes, lanes), 0)
    lane = lax.broadcasted_iota(I32, (lanes, lanes), 1)
    same_head = (row // hd) == (lane // hd)
    rt = row % hd
    ls = lane % hd
    strict_same = same_head & (rt > ls)
    strict_cross = jnp.logical_not(same_head) & (rt > ls)
    incl_same = same_head & (rt >= ls)
    incl_cross = jnp.logical_not(same_head) & (rt >= ls)
    eye = row == lane
    level_masks = []
    m = 1
    while m < hd:
        level_masks.append(same_head & ((rt // (2 * m)) == (ls // (2 * m)))
                           & (((rt // m) % 2) == 1) & (((ls // m) % 2) == 0))
        m *= 2
    left = lax.broadcasted_iota(I32, (frames, lanes), 1) < hd
    tri = (lax.broadcasted_iota(I32, (frames, frames), 0)
           >= lax.broadcasted_iota(I32, (frames, frames), 1)).astype(BF16)

    def diag_blocks(x):
        return jnp.concatenate([jnp.where(left, x, 0.0), jnp.where(left, 0.0, x)], axis=0)

    def cross_blocks(x):
        return jnp.concatenate([jnp.where(left, 0.0, x), jnp.where(left, x, 0.0)], axis=0)

    def batch_prep(bi):
        lw = lw_ref[bi]
        hi = lw.astype(BF16)
        rem = lw - hi.astype(F32)
        mid = rem.astype(BF16)
        lo = (rem - mid.astype(F32)).astype(BF16)
        cs = (jnp.dot(tri, hi, preferred_element_type=F32)
              + jnp.dot(tri, mid, preferred_element_type=F32)
              + jnp.dot(tri, lo, preferred_element_type=F32))
        cs_end = cs[frames - 1:frames, :]
        e_neg = jnp.exp(-cs)
        e_end = jnp.exp(cs_end - cs)
        b_in = b_ref[bi]
        k_in = k_ref[bi]
        return dict(a_t=-kk_ref[bi] * jnp.exp(cs - lw), b_t=b_in * e_neg, k_t=k_in * e_neg,
                    r_t=r_ref[bi] * jnp.exp(cs), b_h=b_in * e_end, k_h=k_in * e_end,
                    v=v_ref[bi], g_end=jnp.exp(cs_end))

    group = []
    for bi in range(n_batch):
        prep = batch_prep(bi)
        for pi in range(n_pairs):
            sl = slice(pi * lanes, (pi + 1) * lanes)
            group.append((bi, sl, bi * n_pairs + pi, {n: x[:, sl] for n, x in prep.items()}))
    bf = lambda x: x.astype(BF16)
    a_bd = [bf(diag_blocks(p["a_t"])) for _, _, _, p in group]
    r_bd = [bf(diag_blocks(p["r_t"])) for _, _, _, p in group]
    d0 = [_bdot_nt(jnp.concatenate([a[:hd], r[:hd]], axis=0),
                   bf(jnp.concatenate([p["b_t"], p["k_t"]], axis=0)))
          for a, r, (_, _, _, p) in zip(a_bd, r_bd, group)]
    d1 = [_bdot_nt(jnp.concatenate([a[hd:], r[hd:]], axis=0),
                   bf(jnp.concatenate([p["k_t"], p["b_t"]], axis=0)))
          for a, r, (_, _, _, p) in zip(a_bd, r_bd, group)]
    a_rows = [jnp.concatenate([x[:hd], y[:hd]], axis=0) for x, y in zip(d0, d1)]
    m_rows = [jnp.concatenate([x[hd:], y[hd:]], axis=0) for x, y in zip(d0, d1)]
    a_ab = [bf(jnp.where(strict_same, x, 0.0)) for x in a_rows]
    a_ak = [bf(jnp.where(strict_cross, x, 0.0)) for x in a_rows]
    m_rb = [bf(jnp.where(incl_same, x, 0.0)) for x in m_rows]
    m_rk = [bf(jnp.where(incl_cross, x, 0.0)) for x in m_rows]
    t_inv = [jnp.where(eye, 1.0, jnp.where(level_masks[0], x, 0.0)) for x in a_rows]
    for mask in level_masks[1:]:
        t_bf = [bf(t) for t in t_inv]
        half = [jnp.where(mask, _bdot(t, x), 0.0) for t, x in zip(t_bf, a_ab)]
        t_inv = [t + _bdot(h, tb) for t, h, tb in zip(t_inv, half, t_bf)]
    t_bf = [bf(t) for t in t_inv]
    v_x = [bf(cross_blocks(p["v"])) for _, _, _, p in group]
    akv = [_bdot(x, v) for x, v in zip(a_ak, v_x)]
    wu = [bf(_bdot(t, jnp.concatenate([a, bf(x)], axis=1)))
          for t, a, x in zip(t_bf, a_bd, akv)]
    bh_t = [bf(diag_blocks(p["b_h"]).T) for _, _, _, p in group]
    kh_t = [bf(cross_blocks(p["k_h"]).T) for _, _, _, p in group]
    pw_rw = [_bdot(jnp.concatenate([bt, m], axis=0), x[:, :lanes])
             for bt, m, x in zip(bh_t, m_rb, wu)]
    q_o = [_bdot(jnp.concatenate([jnp.concatenate([bt, kt], axis=1),
                                  jnp.concatenate([mb, mk], axis=1)], axis=0),
                 jnp.concatenate([x[:, lanes:], v], axis=0))
           for bt, kt, mb, mk, x, v in zip(bh_t, kh_t, m_rb, m_rk, wu, v_x)]
    res = [_bdot(jnp.concatenate([diag_blocks(p["r_t"]) + pr[lanes:],
                                  pr[:lanes] + jnp.where(eye, p["g_end"], 0.0)], axis=0),
                 state_ref[idx])
           for pr, (_, _, idx, p) in zip(pw_rw, group)]
    for x, q, (bi, sl, idx, _) in zip(res, q_o, group):
        o_bd = x[:lanes] + q[lanes:]
        state_ref[idx] = x[lanes:] + q[:lanes]
        o_ref[bi, :, sl] = jnp.where(left, o_bd[:hd], o_bd[hd:])


def _wkv_chunk(r, lw, k, v, kk, b, side_f32=()):
    n_batch, seq, rw = r.shape
    n_pairs = rw // V7X_LANES
    frames = SCAN_FRAMES
    assert seq % frames == 0
    n_steps = seq // frames
    spec = pl.BlockSpec((n_batch, frames, rw), lambda c: (0, c, 0))
    sliced = [a.reshape(n_steps, -1, a.shape[-1]) for a in side_f32]
    side_specs = [pl.BlockSpec((1,) + a.shape[1:], lambda c: (c, 0, 0)) for a in sliced]
    outs = pl.pallas_call(
        functools.partial(_wkv_chunk_kernel, n_batch, n_pairs, len(sliced)),
        grid=(n_steps,),
        in_specs=[spec] * 6 + side_specs,
        out_specs=[spec] + side_specs,
        out_shape=[jax.ShapeDtypeStruct((n_batch, seq, rw), F32)]
                  + [jax.ShapeDtypeStruct(a.shape, BF16) for a in sliced],
        scratch_shapes=[pltpu.VMEM((n_batch * n_pairs, V7X_LANES, V7X_LANES), F32)],
        compiler_params=_params("arbitrary"),
        name="wkv_chunk",
    )(r, lw, k, v, kk, b, *sliced)
    return outs[0], [y.reshape(a.shape) for y, a in zip(outs[1:], side_f32)]


def _post_mix_kernel(n_experts,
                     o_ref, bonus_ref, gate_ref, yconv_ref, x_ref, gn_w_ref, gn_b_ref,
                     w_out_c_ref, w_out_r_ref, ffn_g_ref, wr_hi_ref, wr_lo_ref, b_router_ref,
                     seg2_ref,
                     x1_ref, *outs):
    h2_refs, (top_e_ref, gates_ref) = outs[:-2], outs[-2:]
    seg2 = seg2_ref[...]
    o = o_ref[...]
    mean = _seg_sum(o, seg2) * (1.0 / HEAD_DIM)
    cen = o - mean
    var = _seg_sum(cen * cen, seg2) * (1.0 / HEAD_DIM)
    o = cen * lax.rsqrt(var + GN_EPS) * gn_w_ref[...] + gn_b_ref[...]
    y_rwkv = (o + bonus_ref[...]) * gate_ref[...]
    x1 = (x_ref[...]
          + jnp.dot(yconv_ref[...], w_out_c_ref[...], preferred_element_type=F32)
          + jnp.dot(y_rwkv.astype(BF16), w_out_r_ref[...], preferred_element_type=F32))
    x1_ref[...] = x1
    h2 = _rms_rows(x1, ffn_g_ref[...])
    words = _pack_bf16_pairs(h2)
    for c, h2_ref in enumerate(h2_refs):
        h2_ref[...] = words[:, c * ROW_CHUNK:(c + 1) * ROW_CHUNK]

    h_hi, h_lo = _split_bf16(h2)
    nt = (((1,), (1,)), ((), ()))
    logits = (lax.dot_general(wr_hi_ref[...], h_hi, nt, preferred_element_type=F32)
              + lax.dot_general(wr_hi_ref[...], h_lo, nt, preferred_element_type=F32)
              + lax.dot_general(wr_lo_ref[...], h_hi, nt, preferred_element_type=F32)
              + b_router_ref[...])
    e_id = lax.broadcasted_iota(I32, logits.shape, 0).astype(F32)
    work = logits
    tops, ids = [], []
    for _ in range(TOP_K):
        m = jnp.max(work, axis=0, keepdims=True)
        sel = jnp.min(jnp.where(work == m, e_id, float(n_experts)), axis=0, keepdims=True)
        tops.append(m)
        ids.append(sel)
        work = jnp.where(e_id == sel, -jnp.inf, work)
    ex = [jnp.exp(t - tops[0]) for t in tops]
    denom = ex[0] + ex[1] + ex[2] + ex[3]
    top_e_ref[...] = jnp.concatenate(ids, axis=0).astype(I32)
    gates_ref[...] = jnp.concatenate([e / denom for e in ex], axis=0)


def _post_mix(token0, n, o, bonus, gate, yconv, x2d, gn_w, gn_b, w_out_c, w_out_r, ffn_g, wr_hi,
              wr_lo, b_router, seg2):
    d = x2d.shape[1]
    rw = o.shape[1]
    cw = yconv.shape[1]
    n_experts = wr_hi.shape[0]
    tm = POST_ROWS
    assert token0 % tm == 0 and n % tm == 0
    full = lambda a: pl.BlockSpec(a.shape, lambda i: (0,) * a.ndim)
    in_spec = lambda c: pl.BlockSpec((tm, c), lambda i: (i + token0 // tm, 0))
    row_spec = lambda c: pl.BlockSpec((tm, c), lambda i: (i, 0))
    col_spec = pl.BlockSpec((TOP_K, tm), lambda i: (0, i))
    consts = (gn_w, gn_b, w_out_c, w_out_r, ffn_g, wr_hi, wr_lo, b_router, seg2)
    n_chunks = d // 2 // ROW_CHUNK
    outs = pl.pallas_call(
        functools.partial(_post_mix_kernel, n_experts),
        grid=(n // tm,),
        in_specs=[in_spec(rw), in_spec(rw), in_spec(rw), in_spec(cw), in_spec(d)]
                 + [full(c) for c in consts],
        out_specs=[row_spec(d)] + [row_spec(ROW_CHUNK)] * n_chunks + [col_spec, col_spec],
        out_shape=[jax.ShapeDtypeStruct((n, d), F32)]
                  + [jax.ShapeDtypeStruct((n, ROW_CHUNK), jnp.uint32)] * n_chunks
                  + [jax.ShapeDtypeStruct((TOP_K, n), I32), jax.ShapeDtypeStruct((TOP_K, n), F32)],
        compiler_params=_params("parallel"),
        name="post_mix",
    )(o, bonus, gate, yconv, x2d, *consts)
    return outs[0], outs[1:1 + n_chunks], outs[-2], outs[-1]


def _route_kernel(n_experts, block_rows,
                  top_e_ref, tri_ref, dest_ref, meta_ref,
                  count_ref, start_ref, carry_ref):
    phase = pl.program_id(0)
    j = pl.program_id(1)
    tb = top_e_ref.shape[1]
    e_id = lax.broadcasted_iota(I32, (n_experts, tb), 0)
    top_e = top_e_ref[...]
    onehot = jnp.zeros((n_experts, tb), F32)
    for c in range(TOP_K):
        onehot = onehot + jnp.where(top_e[c:c + 1, :] == e_id, 1.0, 0.0)
    block_count = jnp.sum(onehot, axis=1, keepdims=True)

    @pl.when((phase == 0) & (j == 0))
    def _():
        count_ref[...] = jnp.zeros_like(count_ref)

    @pl.when(phase == 0)
    def _():
        count_ref[...] += block_count

    @pl.when((phase == 1) & (j == 0))
    def _():
        counts = count_ref[...]
        padded = jnp.ceil(counts * (1.0 / block_rows)) * block_rows
        sub = lax.broadcasted_iota(I32, (n_experts, n_experts), 0)
        lane = lax.broadcasted_iota(I32, (n_experts, n_experts), 1)
        padded_row = jnp.sum(jnp.where(sub == lane, padded, 0.0), axis=0, keepdims=True)
        start = jnp.sum(jnp.where(lane < sub, padded_row, 0.0), axis=1, keepdims=True)
        start_ref[...] = start
        carry_ref[...] = jnp.zeros_like(carry_ref)
        end = start + padded
        nb = meta_ref.shape[1]
        slot0 = (lax.broadcasted_iota(I32, (n_experts, nb), 1) * block_rows).astype(F32)
        block_e = jnp.sum(jnp.where(end <= slot0, 1.0, 0.0), axis=0, keepdims=True)
        block_e = jnp.minimum(block_e, n_experts - 1.0)
        used = jnp.max(end, axis=0, keepdims=True) * (1.0 / block_rows)
        in_region = (start <= slot0) & (slot0 < end)
        valid = jnp.sum(jnp.where(in_region, jnp.minimum(start + counts - slot0, block_rows), 0.0),
                        axis=0, keepdims=True)
        row = lax.broadcasted_iota(I32, meta_ref.shape, 0)
        meta_ref[...] = jnp.where(row == META_BLOCK_EXPERT, block_e,
                                  jnp.where(row == META_VALID_ROWS, valid, used)).astype(I32)

    @pl.when(phase == 1)
    def _():
        incl = jnp.dot(onehot.astype(BF16), tri_ref[...], preferred_element_type=F32)
        base = incl - onehot + carry_ref[...] + start_ref[...]
        rows = [jnp.sum(jnp.where(top_e[c:c + 1, :] == e_id, base, 0.0), axis=0, keepdims=True)
                for c in range(TOP_K)]
        dest_ref[...] = jnp.concatenate(rows, axis=0).astype(I32)
        carry_ref[...] += block_count


def _route(top_e, n_experts, n_blocks_padded):
    n = top_e.shape[1]
    tb = ROUTE_TOKENS
    assert n % tb == 0
    tri = (lax.broadcasted_iota(I32, (tb, tb), 0) <= lax.broadcasted_iota(I32, (tb, tb), 1)
           ).astype(BF16)
    return pl.pallas_call(
        functools.partial(_route_kernel, n_experts, EXPERT_ROWS),
        grid=(2, n // tb),
        in_specs=[pl.BlockSpec((TOP_K, tb), lambda ph, j: (0, j)),
                  pl.BlockSpec((tb, tb), lambda ph, j: (0, 0))],
        out_specs=[pl.BlockSpec((TOP_K, tb), lambda ph, j: (0, j * ph)),
                   pl.BlockSpec((V7X_SUBLANES, n_blocks_padded), lambda ph, j: (0, 0))],
        out_shape=[jax.ShapeDtypeStruct((TOP_K, n), I32),
                   jax.ShapeDtypeStruct((V7X_SUBLANES, n_blocks_padded), I32)],
        scratch_shapes=[pltpu.VMEM((n_experts, 1), F32)] * 3,
        compiler_params=_params("arbitrary", "arbitrary"),
        name="route",
    )(top_e, tri)


def _sc_mesh():
    return plsc.VectorSubcoreMesh(core_axis_name="core", subcore_axis_name="subcore")


def _sc_scatter_rows(src, dest, n_slots):
    n, width = src.shape
    assert n % SC_WINDOW == 0

    @functools.partial(pl.kernel, out_type=jax.ShapeDtypeStruct((n_slots, width), src.dtype),
                       mesh=_sc_mesh(), name="sc_scatter_rows")
    def scatter(src_hbm, dest_hbm, out_hbm):
        def body(src_vmem, dest_vmem):
            for c in range(TOP_K):
                pltpu.sync_copy(src_vmem, out_hbm.at[dest_vmem.at[c]])

        pltpu.emit_pipeline(
            body,
            grid=(n // SC_WINDOW,),
            in_specs=[pl.BlockSpec((SC_WINDOW, width), lambda i: (i, 0)),
                      pl.BlockSpec((TOP_K, SC_WINDOW), lambda i: (0, i))],
            out_specs=[],
            core_axis_name=("core", "subcore"),
            dimension_semantics=(pltpu.PARALLEL,),
        )(src_hbm, dest_hbm)

    return scatter(src, dest)


def _experts_kernel(d_ff, n_in, n_out, meta_ref, *refs):
    x_refs, refs = refs[:n_in], refs[n_in:]
    w_gu_mxu, b_gu_ref, w_down_mxu, b_down_ref = refs[:4]
    y_refs = refs[4:]
    j = pl.program_id(0)
    used = meta_ref[META_USED_BLOCKS, 0]

    @pl.when(j < used)
    def _():
        x = _unpack_bf16_pairs(jnp.concatenate([r[...] for r in x_refs], axis=1))
        row = lax.broadcasted_iota(I32, x.shape, 0)
        x = jnp.where(row < meta_ref[META_VALID_ROWS, j], x, 0.0)
        gu = jnp.dot(x.astype(BF16), w_gu_mxu[0], preferred_element_type=F32) + b_gu_ref[0]
        gate = jnp.minimum(gu[:, :d_ff], SWIGLU_LIMIT)
        up = jnp.clip(gu[:, d_ff:], -SWIGLU_LIMIT, SWIGLU_LIMIT)
        act = (up + 1.0) * (gate * _sigmoid(SWIGLU_ALPHA * gate))
        y = jnp.dot(act.astype(BF16), w_down_mxu[0], preferred_element_type=F32) + b_down_ref[0]
        words = _pack_bf16_pairs(y)
        for c, y_ref in enumerate(y_refs):
            y_ref[...] = words[:, c * ROW_CHUNK:(c + 1) * ROW_CHUNK]

    @pl.when(j >= used)
    def _():
        for y_ref in y_refs:
            y_ref[...] = jnp.zeros_like(y_ref)


def _experts(meta, slot_chunks, w_gu, b_gu, w_down, b_down):
    assert w_gu.dtype == BF16 and w_down.dtype == BF16
    n_in = len(slot_chunks)
    n_slots = slot_chunks[0].shape[0]
    n_experts, d, two_ff = w_gu.shape
    n_out = d // 2 // ROW_CHUNK
    d_ff = two_ff // 2
    bm = EXPERT_ROWS
    by_expert = lambda j, m: (m[META_BLOCK_EXPERT, j], 0, 0)
    grid_spec = pltpu.PrefetchScalarGridSpec(
        num_scalar_prefetch=1,
        grid=(n_slots // bm,),
        in_specs=[pl.BlockSpec((bm, ROW_CHUNK),
                               lambda j, m: (jnp.minimum(j, m[META_USED_BLOCKS, 0] - 1), 0))
                  ] * n_in
                 + [pl.BlockSpec((1, d, two_ff), by_expert), pl.BlockSpec((1, 1, two_ff), by_expert),
                    pl.BlockSpec((1, d_ff, d), by_expert), pl.BlockSpec((1, 1, d), by_expert)],
        out_specs=[pl.BlockSpec((bm, ROW_CHUNK), lambda j, m: (j, 0))] * n_out,
    )
    return pl.pallas_call(
        functools.partial(_experts_kernel, d_ff, n_in, n_out),
        grid_spec=grid_spec,
        out_shape=[jax.ShapeDtypeStruct((n_slots, ROW_CHUNK), jnp.uint32)] * n_out,
        compiler_params=_params("arbitrary"),
        name="experts",
    )(meta, *slot_chunks, w_gu, b_gu, w_down, b_down)


def _sc_gather_rows(table, idx_row):
    n_idx = idx_row.shape[1]
    width = table.shape[1]
    assert n_idx % SC_WINDOW == 0

    @functools.partial(pl.kernel, out_type=jax.ShapeDtypeStruct((n_idx, width), table.dtype),
                       mesh=_sc_mesh(), name="sc_gather_rows")
    def gather(table_hbm, idx_hbm, out_hbm):
        def body(idx_vmem, out_vmem):
            pltpu.sync_copy(table_hbm.at[idx_vmem.at[0]], out_vmem)

        pltpu.emit_pipeline(
            body,
            grid=(n_idx // SC_WINDOW,),
            in_specs=[pl.BlockSpec((1, SC_WINDOW), lambda i: (0, i))],
            out_specs=[pl.BlockSpec((SC_WINDOW, width), lambda i: (i, 0))],
            core_axis_name=("core", "subcore"),
            dimension_semantics=(pltpu.PARALLEL,),
        )(idx_hbm, out_hbm)

    return gather(table, idx_row)


def _reduce_kernel(n_chunks, *refs):
    y_refs = refs[:n_chunks]
    x1_ref, gates_ref, g_ref = refs[n_chunks:n_chunks + 3]
    out_ref = refs[-1]
    gates = gates_ref[...]
    acc = x1_ref[...]
    for c in range(TOP_K):
        y = _unpack_bf16_pairs(jnp.concatenate([r[c] for r in y_refs], axis=1))
        acc = acc + y * gates[:, c:c + 1]
    out_ref[...] = _rms_rows(acc, g_ref[...])


def _reduce(token0, n_total, out_so_far, y_chunks, x1, gates_t, final_g):
    n, d = x1.shape
    tb = REDUCE_TOKENS
    n_chunks = len(y_chunks)
    assert token0 % tb == 0
    in_specs = ([pl.BlockSpec((TOP_K, tb, ROW_CHUNK), lambda i: (0, i, 0))] * n_chunks
                + [pl.BlockSpec((tb, d), lambda i: (i, 0)),
                   pl.BlockSpec((tb, TOP_K), lambda i: (i, 0)),
                   pl.BlockSpec((1, d), lambda i: (0, 0))])
    args = [y.reshape(TOP_K, n, ROW_CHUNK) for y in y_chunks] + [x1, gates_t, final_g]
    aliases = {}
    if out_so_far is not None:
        in_specs.append(pl.BlockSpec(memory_space=pl.ANY))
        args.append(out_so_far)
        aliases = {len(args) - 1: 0}
    return pl.pallas_call(
        functools.partial(_reduce_kernel, n_chunks),
        grid=(n // tb,),
        in_specs=in_specs,
        out_specs=pl.BlockSpec((tb, d), lambda i: (i + token0 // tb, 0)),
        out_shape=jax.ShapeDtypeStruct((n_total, d), F32),
        input_output_aliases=aliases,
        compiler_params=_params("parallel"),
        name="reduce",
    )(*args)


def _group_matrix(width):
    a = lax.broadcasted_iota(I32, (width, width), 0) // HEAD_DIM
    b = lax.broadcasted_iota(I32, (width, width), 1) // HEAD_DIM
    g = (a == b).astype(BF16)
    return jnp.concatenate([g, g], axis=0)


def _row(vec):
    return vec.reshape(1, -1).astype(F32)


def kernel(x, w_in, conv_w, conv_norm_g, rwkv_mu, w0, w_up, a0, a_up, g_up, k_k, k_a, r_k,
           gn_w, gn_b, w_out, norm_mix_g, norm_ffn_g, w_router, b_router, w_gu, b_gu, w_down,
           b_down, norm_final_g):
    n_batch, seq, d = x.shape
    n = n_batch * seq
    depth = w_in.shape[0]
    cw = conv_w.shape[2]
    rw = w0.shape[1]
    n_experts = w_router.shape[2]
    decay_lora = w_up.shape[1]
    aaa_lora = a_up.shape[1]
    assert decay_lora + aaa_lora == V7X_LANES
    assert depth == 1
    assert n % MOE_PARTS == 0
    n_part = n // MOE_PARTS
    n_slots = n_part * TOP_K + n_experts * EXPERT_ROWS
    n_blocks = n_slots // EXPERT_ROWS
    n_blocks_padded = -(-n_blocks // V7X_LANES) * V7X_LANES
    seg2 = _group_matrix(rw)

    x2d = x.reshape(n, d)
    for l in range(depth):
        w_up_pad = jnp.concatenate([w_up[l], jnp.zeros((aaa_lora, rw), F32)], axis=0).astype(BF16)
        a_up_pad = jnp.concatenate([jnp.zeros((decay_lora, rw), F32), a_up[l]], axis=0).astype(BF16)
        (yconv, r, lw, k, v, kk, b, gate, bonus) = _mix_prep(
            x2d, seq, _row(norm_mix_g[l]), w_in[l].astype(BF16), conv_w[l].astype(F32),
            _row(conv_norm_g[l]), _row(rwkv_mu[l]), _row(w0[l]), w_up_pad, _row(a0[l]), a_up_pad,
            g_up[l].astype(BF16), _row(k_k[l]), _row(k_a[l]), _row(r_k[l]), seg2)
        shape3 = (n_batch, seq, rw)
        o, (w_gu_b, w_down_b) = _wkv_chunk(
            r.reshape(shape3), lw.reshape(shape3), k.reshape(shape3), v.reshape(shape3),
            kk.reshape(shape3), b.reshape(shape3),
            side_f32=(w_gu[l].astype(F32), w_down[l].astype(F32)))
        o = o.reshape(n, rw)
        w_out_b = w_out[l].astype(BF16)
        wr_t = w_router[l].T.astype(F32)
        wr_hi = wr_t.astype(BF16)
        wr_lo = (wr_t - wr_hi.astype(F32)).astype(BF16)
        expert_args = (w_gu_b, b_gu[l].reshape(n_experts, 1, -1).astype(F32),
                       w_down_b, b_down[l].reshape(n_experts, 1, -1).astype(F32))
        parts = []
        for token0 in range(0, n, n_part):
            x1, h2_chunks, top_e, gates = _post_mix(
                token0, n_part, o, bonus, gate, yconv, x2d, _row(gn_w[l]), _row(gn_b[l]),
                w_out_b[:cw], w_out_b[cw:], _row(norm_ffn_g[l]), wr_hi, wr_lo,
                b_router[l].reshape(n_experts, 1).astype(F32), seg2)
            dest, meta = _route(top_e, n_experts, n_blocks_padded)
            slot_chunks = [_sc_scatter_rows(h, dest, n_slots) for h in h2_chunks]
            parts.append((token0, x1, gates, dest, meta, slot_chunks))
        out = None
        for token0, x1, gates, dest, meta, slot_chunks in parts:
            y_chunks = _experts(meta, slot_chunks, *expert_args)
            idx_row = dest.reshape(1, TOP_K * n_part)
            out = _reduce(token0, n, out, [_sc_gather_rows(y, idx_row) for y in y_chunks], x1,
                          gates.T, _row(norm_final_g))
        x2d = out
    return x2d.reshape(n_batch, seq, d)
```

```python
import functools

import jax
import jax.numpy as jnp
from jax import lax
from jax.experimental import pallas as pl
from jax.experimental.pallas import tpu as pltpu
from jax.experimental.pallas import tpu_sc as plsc

F32 = jnp.float32
BF16 = jnp.bfloat16
I32 = jnp.int32

HEAD_DIM = 64
TOP_K = 4
NORM_EPS = 1e-5
GN_EPS = HEAD_DIM * 1e-5
SWIGLU_LIMIT = 7.0
SWIGLU_ALPHA = 1.702

V7X_LANES = 128
V7X_SUBLANES = 8
V7X_VMEM_LIMIT_BYTES = 56 * 1024 * 1024

PREP_ROWS = 256
POST_ROWS = 512
SCAN_FRAMES = 64
ROUTE_TOKENS = 1024
REDUCE_TOKENS = 512
MOE_PARTS = 2
SC_WINDOW = 128
ROW_CHUNK = 256
META_BLOCK_EXPERT = 0
META_USED_BLOCKS = 1
META_VALID_ROWS = 2
EXPERT_ROWS = 512

def _params(*semantics):
    return pltpu.CompilerParams(dimension_semantics=semantics,
                                vmem_limit_bytes=V7X_VMEM_LIMIT_BYTES)


def _split_bf16(x):
    hi = x.astype(BF16)
    lo = (x - hi.astype(F32)).astype(BF16)
    return hi, lo


def _seg_sum(x, seg2):
    hi, lo = _split_bf16(x)
    return jnp.dot(jnp.concatenate([hi, lo], axis=1), seg2, preferred_element_type=F32)


def _pack_bf16_pairs(x):
    bits = pltpu.bitcast(x.astype(BF16).astype(F32), jnp.uint32)
    half = x.shape[1] // 2
    return bits[:, half:] | (bits[:, :half] >> 16)


def _unpack_bf16_pairs(words):
    return jnp.concatenate([pltpu.bitcast(words << 16, F32),
                            pltpu.bitcast(words & jnp.uint32(0xFFFF0000), F32)], axis=1)


def _rms_rows(x, g):
    return x * lax.rsqrt(jnp.mean(x * x, axis=-1, keepdims=True) + NORM_EPS) * g


def _sigmoid(x):
    return 1.0 / (1.0 + jnp.exp(-x))


def _mix_prep_kernel(blocks_per_seq, cw, rw,
                     xprev_ref, x_ref, g_ref, w_in_ref, conv_w_ref, conv_g_ref, mu_ref,
                     w0_ref, w_up_ref, a0_ref, a_up_ref, g_up_ref, k_k_ref, k_a_ref, r_k_ref,
                     seg2_ref,
                     yconv_ref, r_ref, w_ref, k_ref, v_ref, kk_ref, b_ref, gate_ref, bonus_ref,
                     p_scr, z_scr):
    tm = x_ref.shape[0]
    halo = xprev_ref.shape[0]
    first = (pl.program_id(0) % blocks_per_seq) == 0
    xp = xprev_ref[...] * jnp.where(first, 0.0, 1.0)
    xa = jnp.concatenate([xp, x_ref[...]], axis=0)
    h = _rms_rows(xa, g_ref[...])
    p_scr[...] = jnp.dot(h.astype(BF16), w_in_ref[...], preferred_element_type=F32)
    seg2 = seg2_ref[...]

    z_scr[...] = p_scr[:, 2 * cw:3 * cw] * p_scr[:, 0:cw]
    conv = (conv_w_ref[0:1, :] * z_scr[halo - 2:halo - 2 + tm, :]
            + conv_w_ref[1:2, :] * z_scr[halo - 1:halo - 1 + tm, :]
            + conv_w_ref[2:3, :] * z_scr[halo:halo + tm, :])
    y = p_scr[halo:halo + tm, cw:2 * cw] * conv
    ms = _seg_sum(y * y, seg2) * (1.0 / HEAD_DIM)
    yconv_ref[...] = (y * lax.rsqrt(ms + NORM_EPS) * conv_g_ref[...]).astype(yconv_ref.dtype)

    c0 = 3 * cw
    cur = p_scr[halo:halo + tm, c0:]
    prev = p_scr[halo - 1:halo - 1 + tm, c0:]
    q = cur + (prev - cur) * mu_ref[...]
    r = q[:, 0:rw]
    k = q[:, rw:2 * rw]
    v = q[:, 2 * rw:3 * rw]
    lora_wa = q[:, 3 * rw:3 * rw + V7X_LANES]
    lora_g = q[:, 3 * rw + V7X_LANES:]
    w_lin = w0_ref[...] + jnp.dot(jnp.tanh(lora_wa).astype(BF16), w_up_ref[...],
                                  preferred_element_type=F32)
    neg = -w_lin
    softplus = jnp.maximum(neg, 0.0) + jnp.log(1.0 + jnp.exp(-jnp.abs(neg)))
    log_decay = -jnp.exp(-softplus - 0.5)
    a = _sigmoid(a0_ref[...] + jnp.dot(lora_wa.astype(BF16), a_up_ref[...],
                                       preferred_element_type=F32))
    gate = jnp.dot(_sigmoid(lora_g).astype(BF16), g_up_ref[...], preferred_element_type=F32)
    kk = k * k_k_ref[...]
    kk = kk / jnp.maximum(jnp.sqrt(_seg_sum(kk * kk, seg2)), 1e-12)
    k_mod = k * (1.0 + (a - 1.0) * k_a_ref[...])
    bonus = _seg_sum(r * k_mod * r_k_ref[...], seg2) * v
    r_ref[...] = r
    w_ref[...] = log_decay
    k_ref[...] = k_mod
    v_ref[...] = v
    kk_ref[...] = kk
    b_ref[...] = kk * a
    gate_ref[...] = gate
    bonus_ref[...] = bonus


def _mix_prep(x2d, seq, norm_g, w_in, conv_w, conv_g, mu, w0, w_up, a0, a_up, g_up, k_k, k_a,
              r_k, seg2):
    n, d = x2d.shape
    tm = PREP_ROWS
    halo = V7X_SUBLANES
    cw = conv_w.shape[1]
    rw = w0.shape[1]
    in_cols = w_in.shape[1]
    assert seq % tm == 0 and n % tm == 0
    full = lambda a: pl.BlockSpec(a.shape, lambda i: (0,) * a.ndim)
    row_spec = lambda c: pl.BlockSpec((tm, c), lambda i: (i, 0))
    consts = (norm_g, w_in, conv_w, conv_g, mu, w0, w_up, a0, a_up, g_up, k_k, k_a, r_k, seg2)
    outs = [jax.ShapeDtypeStruct((n, cw), BF16)] + [jax.ShapeDtypeStruct((n, rw), F32)] * 8
    return pl.pallas_call(
        functools.partial(_mix_prep_kernel, seq // tm, cw, rw),
        grid=(n // tm,),
        in_specs=[pl.BlockSpec((halo, d), lambda i: (jnp.maximum(i * (tm // halo) - 1, 0), 0)),
                  row_spec(d)] + [full(c) for c in consts],
        out_specs=[row_spec(cw)] + [row_spec(rw)] * 8,
        out_shape=outs,
        scratch_shapes=[pltpu.VMEM((halo + tm, in_cols), F32), pltpu.VMEM((halo + tm, cw), F32)],
        compiler_params=_params("parallel"),
        name="mix_prep",
    )(x2d, x2d, *consts)


def _bdot(a, b):
    return jnp.dot(a.astype(BF16), b.astype(BF16), preferred_element_type=F32)


def _bdot_nt(a, b):
    return lax.dot_general(a.astype(BF16), b.astype(BF16), (((1,), (1,)), ((), ())),
                           preferred_element_type=F32)


def _wkv_chunk_kernel(n_batch, n_pairs, n_side,
                      r_ref, lw_ref, k_ref, v_ref, kk_ref, b_ref, *refs):
    frames = r_ref.shape[1]
    lanes = V7X_LANES
    hd = HEAD_DIM
    assert frames == hd
    side_in, o_ref, side_out, state_ref = (refs[:n_side], refs[n_side],
                                           refs[n_side + 1:2 * n_side + 1], refs[-1])
    for src, dst in zip(side_in, side_out):
        dst[...] = src[...].astype(dst.dtype)

    @pl.when(pl.program_id(0) == 0)
    def _():
        state_ref[...] = jnp.zeros_like(state_ref)

    row = lax.broadcasted_iota(I32, (lanes, lanes), 0)
    lane = lax.broadcasted_iota(I32, (lanes, lanes), 1)
    same_head = (row // hd) == (lane // hd)
    rt = row % hd
    ls = lane % hd
    strict_same = same_head & (rt > ls)
    strict_cross = jnp.logical_not(same_head) & (rt > ls)
    incl_same = same_head & (rt >= ls)
    incl_cross = jnp.logical_not(same_head) & (rt >= ls)
    eye = row == lane
    level_masks = []
    m = 1
    while m < hd:
        level_masks.append(same_head & ((rt // (2 * m)) == (ls // (2 * m)))
                           & (((rt // m) % 2) == 1) & (((ls // m) % 2) == 0))
        m *= 2
    left = lax.broadcasted_iota(I32, (frames, lanes), 1) < hd
    tri = (lax.broadcasted_iota(I32, (frames, frames), 0)
           >= lax.broadcasted_iota(I32, (frames, frames), 1)).astype(BF16)

    def diag_blocks(x):
        return jnp.concatenate([jnp.where(left, x, 0.0), jnp.where(left, 0.0, x)], axis=0)

    def cross_blocks(x):
        return jnp.concatenate([jnp.where(left, 0.0, x), jnp.where(left, x, 0.0)], axis=0)

    def batch_prep(bi):
        lw = lw_ref[bi]
        hi = lw.astype(BF16)
        rem = lw - hi.astype(F32)
        mid = rem.astype(BF16)
        lo = (rem - mid.astype(F32)).astype(BF16)
        cs = (jnp.dot(tri, hi, preferred_element_type=F32)
              + jnp.dot(tri, mid, preferred_element_type=F32)
              + jnp.dot(tri, lo, preferred_element_type=F32))
        cs_end = cs[frames - 1:frames, :]
        e_neg = jnp.exp(-cs)
        e_end = jnp.exp(cs_end - cs)
        b_in = b_ref[bi]
        k_in = k_ref[bi]
        return dict(a_t=-kk_ref[bi] * jnp.exp(cs - lw), b_t=b_in * e_neg, k_t=k_in * e_neg,
                    r_t=r_ref[bi] * jnp.exp(cs), b_h=b_in * e_end, k_h=k_in * e_end,
                    v=v_ref[bi], g_end=jnp.exp(cs_end))

    group = []
    for bi in range(n_batch):
        prep = batch_prep(bi)
        for pi in range(n_pairs):
            sl = slice(pi * lanes, (pi + 1) * lanes)
            group.append((bi, sl, bi * n_pairs + pi, {n: x[:, sl] for n, x in prep.items()}))
    bf = lambda x: x.astype(BF16)
    a_bd = [bf(diag_blocks(p["a_t"])) for _, _, _, p in group]
    r_bd = [bf(diag_blocks(p["r_t"])) for _, _, _, p in group]
    d0 = [_bdot_nt(jnp.concatenate([a[:hd], r[:hd]], axis=0),
                   bf(jnp.concatenate([p["b_t"], p["k_t"]], axis=0)))
          for a, r, (_, _, _, p) in zip(a_bd, r_bd, group)]
    d1 = [_bdot_nt(jnp.concatenate([a[hd:], r[hd:]], axis=0),
                   bf(jnp.concatenate([p["k_t"], p["b_t"]], axis=0)))
          for a, r, (_, _, _, p) in zip(a_bd, r_bd, group)]
    a_rows = [jnp.concatenate([x[:hd], y[:hd]], axis=0) for x, y in zip(d0, d1)]
    m_rows = [jnp.concatenate([x[hd:], y[hd:]], axis=0) for x, y in zip(d0, d1)]
    a_ab = [bf(jnp.where(strict_same, x, 0.0)) for x in a_rows]
    a_ak = [bf(jnp.where(strict_cross, x, 0.0)) for x in a_rows]
    m_rb = [bf(jnp.where(incl_same, x, 0.0)) for x in m_rows]
    m_rk = [bf(jnp.where(incl_cross, x, 0.0)) for x in m_rows]
    t_inv = [jnp.where(eye, 1.0, jnp.where(level_masks[0], x, 0.0)) for x in a_rows]
    for mask in level_masks[1:]:
        t_bf = [bf(t) for t in t_inv]
        half = [jnp.where(mask, _bdot(t, x), 0.0) for t, x in zip(t_bf, a_ab)]
        t_inv = [t + _bdot(h, tb) for t, h, tb in zip(t_inv, half, t_bf)]
    t_bf = [bf(t) for t in t_inv]
    v_x = [bf(cross_blocks(p["v"])) for _, _, _, p in group]
    akv = [_bdot(x, v) for x, v in zip(a_ak, v_x)]
    wu = [bf(_bdot(t, jnp.concatenate([a, bf(x)], axis=1)))
          for t, a, x in zip(t_bf, a_bd, akv)]
    bh_t = [bf(diag_blocks(p["b_h"]).T) for _, _, _, p in group]
    kh_t = [bf(cross_blocks(p["k_h"]).T) for _, _, _, p in group]
    pw_rw = [_bdot(jnp.concatenate([bt, m], axis=0), x[:, :lanes])
             for bt, m, x in zip(bh_t, m_rb, wu)]
    q_o = [_bdot(jnp.concatenate([jnp.concatenate([bt, kt], axis=1),
                                  jnp.concatenate([mb, mk], axis=1)], axis=0),
                 jnp.concatenate([x[:, lanes:], v], axis=0))
           for bt, kt, mb, mk, x, v in zip(bh_t, kh_t, m_rb, m_rk, wu, v_x)]
    res = [_bdot(jnp.concatenate([diag_blocks(p["r_t"]) + pr[lanes:],
                                  pr[:lanes] + jnp.where(eye, p["g_end"], 0.0)], axis=0),
                 state_ref[idx])
           for pr, (_, _, idx, p) in zip(pw_rw, group)]
    for x, q, (bi, sl, idx, _) in zip(res, q_o, group):
        o_bd = x[:lanes] + q[lanes:]
        state_ref[idx] = x[lanes:] + q[:lanes]
        o_ref[bi, :, sl] = jnp.where(left, o_bd[:hd], o_bd[hd:])


def _wkv_chunk(r, lw, k, v, kk, b, side_f32=()):
    n_batch, seq, rw = r.shape
    n_pairs = rw // V7X_LANES
    frames = SCAN_FRAMES
    assert seq % frames == 0
    n_steps = seq // frames
    spec = pl.BlockSpec((n_batch, frames, rw), lambda c: (0, c, 0))
    sliced = [a.reshape(n_steps, -1, a.shape[-1]) for a in side_f32]
    side_specs = [pl.BlockSpec((1,) + a.shape[1:], lambda c: (c, 0, 0)) for a in sliced]
    outs = pl.pallas_call(
        functools.partial(_wkv_chunk_kernel, n_batch, n_pairs, len(sliced)),
        grid=(n_steps,),
        in_specs=[spec] * 6 + side_specs,
        out_specs=[spec] + side_specs,
        out_shape=[jax.ShapeDtypeStruct((n_batch, seq, rw), F32)]
                  + [jax.ShapeDtypeStruct(a.shape, BF16) for a in sliced],
        scratch_shapes=[pltpu.VMEM((n_batch * n_pairs, V7X_LANES, V7X_LANES), F32)],
        compiler_params=_params("arbitrary"),
        name="wkv_chunk",
    )(r, lw, k, v, kk, b, *sliced)
    return outs[0], [y.reshape(a.shape) for y, a in zip(outs[1:], side_f32)]


def _post_mix_kernel(n_experts,
                     o_ref, bonus_ref, gate_ref, yconv_ref, x_ref, gn_w_ref, gn_b_ref,
                     w_out_c_ref, w_out_r_ref, ffn_g_ref, wr_hi_ref, wr_lo_ref, b_router_ref,
                     seg2_ref,
                     x1_ref, *outs):
    h2_refs, (top_e_ref, gates_ref) = outs[:-2], outs[-2:]
    seg2 = seg2_ref[...]
    o = o_ref[...]
    mean = _seg_sum(o, seg2) * (1.0 / HEAD_DIM)
    cen = o - mean
    var = _seg_sum(cen * cen, seg2) * (1.0 / HEAD_DIM)
    o = cen * lax.rsqrt(var + GN_EPS) * gn_w_ref[...] + gn_b_ref[...]
    y_rwkv = (o + bonus_ref[...]) * gate_ref[...]
    x1 = (x_ref[...]
          + jnp.dot(yconv_ref[...], w_out_c_ref[...], preferred_element_type=F32)
          + jnp.dot(y_rwkv.astype(BF16), w_out_r_ref[...], preferred_element_type=F32))
    x1_ref[...] = x1
    h2 = _rms_rows(x1, ffn_g_ref[...])
    words = _pack_bf16_pairs(h2)
    for c, h2_ref in enumerate(h2_refs):
        h2_ref[...] = words[:, c * ROW_CHUNK:(c + 1) * ROW_CHUNK]

    h_hi, h_lo = _split_bf16(h2)
    nt = (((1,), (1,)), ((), ()))
    logits = (lax.dot_general(wr_hi_ref[...], h_hi, nt, preferred_element_type=F32)
              + lax.dot_general(wr_hi_ref[...], h_lo, nt, preferred_element_type=F32)
              + lax.dot_general(wr_lo_ref[...], h_hi, nt, preferred_element_type=F32)
              + b_router_ref[...])
    e_id = lax.broadcasted_iota(I32, logits.shape, 0).astype(F32)
    work = logits
    tops, ids = [], []
    for _ in range(TOP_K):
        m = jnp.max(work, axis=0, keepdims=True)
        sel = jnp.min(jnp.where(work == m, e_id, float(n_experts)), axis=0, keepdims=True)
        tops.append(m)
        ids.append(sel)
        work = jnp.where(e_id == sel, -jnp.inf, work)
    ex = [jnp.exp(t - tops[0]) for t in tops]
    denom = ex[0] + ex[1] + ex[2] + ex[3]
    top_e_ref[...] = jnp.concatenate(ids, axis=0).astype(I32)
    gates_ref[...] = jnp.concatenate([e / denom for e in ex], axis=0)


def _post_mix(token0, n, o, bonus, gate, yconv, x2d, gn_w, gn_b, w_out_c, w_out_r, ffn_g, wr_hi,
              wr_lo, b_router, seg2):
    d = x2d.shape[1]
    rw = o.shape[1]
    cw = yconv.shape[1]
    n_experts = wr_hi.shape[0]
    tm = POST_ROWS
    assert token0 % tm == 0 and n % tm == 0
    full = lambda a: pl.BlockSpec(a.shape, lambda i: (0,) * a.ndim)
    in_spec = lambda c: pl.BlockSpec((tm, c), lambda i: (i + token0 // tm, 0))
    row_spec = lambda c: pl.BlockSpec((tm, c), lambda i: (i, 0))
    col_spec = pl.BlockSpec((TOP_K, tm), lambda i: (0, i))
    consts = (gn_w, gn_b, w_out_c, w_out_r, ffn_g, wr_hi, wr_lo, b_router, seg2)
    n_chunks = d // 2 // ROW_CHUNK
    outs = pl.pallas_call(
        functools.partial(_post_mix_kernel, n_experts),
        grid=(n // tm,),
        in_specs=[in_spec(rw), in_spec(rw), in_spec(rw), in_spec(cw), in_spec(d)]
                 + [full(c) for c in consts],
        out_specs=[row_spec(d)] + [row_spec(ROW_CHUNK)] * n_chunks + [col_spec, col_spec],
        out_shape=[jax.ShapeDtypeStruct((n, d), F32)]
                  + [jax.ShapeDtypeStruct((n, ROW_CHUNK), jnp.uint32)] * n_chunks
                  + [jax.ShapeDtypeStruct((TOP_K, n), I32), jax.ShapeDtypeStruct((TOP_K, n), F32)],
        compiler_params=_params("parallel"),
        name="post_mix",
    )(o, bonus, gate, yconv, x2d, *consts)
    return outs[0], outs[1:1 + n_chunks], outs[-2], outs[-1]


def _route_kernel(n_experts, block_rows,
                  top_e_ref, tri_ref, dest_ref, meta_ref,
                  count_ref, start_ref, carry_ref):
    phase = pl.program_id(0)
    j = pl.program_id(1)
    tb = top_e_ref.shape[1]
    e_id = lax.broadcasted_iota(I32, (n_experts, tb), 0)
    top_e = top_e_ref[...]
    onehot = jnp.zeros((n_experts, tb), F32)
    for c in range(TOP_K):
        onehot = onehot + jnp.where(top_e[c:c + 1, :] == e_id, 1.0, 0.0)
    block_count = jnp.sum(onehot, axis=1, keepdims=True)

    @pl.when((phase == 0) & (j == 0))
    def _():
        count_ref[...] = jnp.zeros_like(count_ref)

    @pl.when(phase == 0)
    def _():
        count_ref[...] += block_count

    @pl.when((phase == 1) & (j == 0))
    def _():
        counts = count_ref[...]
        padded = jnp.ceil(counts * (1.0 / block_rows)) * block_rows
        sub = lax.broadcasted_iota(I32, (n_experts, n_experts), 0)
        lane = lax.broadcasted_iota(I32, (n_experts, n_experts), 1)
        padded_row = jnp.sum(jnp.where(sub == lane, padded, 0.0), axis=0, keepdims=True)
        start = jnp.sum(jnp.where(lane < sub, padded_row, 0.0), axis=1, keepdims=True)
        start_ref[...] = start
        carry_ref[...] = jnp.zeros_like(carry_ref)
        end = start + padded
        nb = meta_ref.shape[1]
        slot0 = (lax.broadcasted_iota(I32, (n_experts, nb), 1) * block_rows).astype(F32)
        block_e = jnp.sum(jnp.where(end <= slot0, 1.0, 0.0), axis=0, keepdims=True)
        block_e = jnp.minimum(block_e, n_experts - 1.0)
        used = jnp.max(end, axis=0, keepdims=True) * (1.0 / block_rows)
        in_region = (start <= slot0) & (slot0 < end)
        valid = jnp.sum(jnp.where(in_region, jnp.minimum(start + counts - slot0, block_rows), 0.0),
                        axis=0, keepdims=True)
        row = lax.broadcasted_iota(I32, meta_ref.shape, 0)
        meta_ref[...] = jnp.where(row == META_BLOCK_EXPERT, block_e,
                                  jnp.where(row == META_VALID_ROWS, valid, used)).astype(I32)

    @pl.when(phase == 1)
    def _():
        incl = jnp.dot(onehot.astype(BF16), tri_ref[...], preferred_element_type=F32)
        base = incl - onehot + carry_ref[...] + start_ref[...]
        rows = [jnp.sum(jnp.where(top_e[c:c + 1, :] == e_id, base, 0.0), axis=0, keepdims=True)
                for c in range(TOP_K)]
        dest_ref[...] = jnp.concatenate(rows, axis=0).astype(I32)
        carry_ref[...] += block_count


def _route(top_e, n_experts, n_blocks_padded):
    n = top_e.shape[1]
    tb = ROUTE_TOKENS
    assert n % tb == 0
    tri = (lax.broadcasted_iota(I32, (tb, tb), 0) <= lax.broadcasted_iota(I32, (tb, tb), 1)
           ).astype(BF16)
    return pl.pallas_call(
        functools.partial(_route_kernel, n_experts, EXPERT_ROWS),
        grid=(2, n // tb),
        in_specs=[pl.BlockSpec((TOP_K, tb), lambda ph, j: (0, j)),
                  pl.BlockSpec((tb, tb), lambda ph, j: (0, 0))],
        out_specs=[pl.BlockSpec((TOP_K, tb), lambda ph, j: (0, j * ph)),
                   pl.BlockSpec((V7X_SUBLANES, n_blocks_padded), lambda ph, j: (0, 0))],
        out_shape=[jax.ShapeDtypeStruct((TOP_K, n), I32),
                   jax.ShapeDtypeStruct((V7X_SUBLANES, n_blocks_padded), I32)],
        scratch_shapes=[pltpu.VMEM((n_experts, 1), F32)] * 3,
        compiler_params=_params("arbitrary", "arbitrary"),
        name="route",
    )(top_e, tri)


def _sc_mesh():
    return plsc.VectorSubcoreMesh(core_axis_name="core", subcore_axis_name="subcore")


def _sc_scatter_rows(src, dest, n_slots):
    n, width = src.shape
    assert n % SC_WINDOW == 0

    @functools.partial(pl.kernel, out_type=jax.ShapeDtypeStruct((n_slots, width), src.dtype),
                       mesh=_sc_mesh(), name="sc_scatter_rows")
    def scatter(src_hbm, dest_hbm, out_hbm):
        def body(src_vmem, dest_vmem):
            for c in range(TOP_K):
                pltpu.sync_copy(src_vmem, out_hbm.at[dest_vmem.at[c]])

        pltpu.emit_pipeline(
            body,
            grid=(n // SC_WINDOW,),
            in_specs=[pl.BlockSpec((SC_WINDOW, width), lambda i: (i, 0)),
                      pl.BlockSpec((TOP_K, SC_WINDOW), lambda i: (0, i))],
            out_specs=[],
            core_axis_name=("core", "subcore"),
            dimension_semantics=(pltpu.PARALLEL,),
        )(src_hbm, dest_hbm)

    return scatter(src, dest)


def _experts_kernel(d_ff, n_in, n_out, meta_ref, *refs):
    x_refs, refs = refs[:n_in], refs[n_in:]
    w_gu_mxu, b_gu_ref, w_down_mxu, b_down_ref = refs[:4]
    y_refs = refs[4:]
    j = pl.program_id(0)
    used = meta_ref[META_USED_BLOCKS, 0]

    @pl.when(j < used)
    def _():
        x = _unpack_bf16_pairs(jnp.concatenate([r[...] for r in x_refs], axis=1))
        row = lax.broadcasted_iota(I32, x.shape, 0)
        x = jnp.where(row < meta_ref[META_VALID_ROWS, j], x, 0.0)
        gu = jnp.dot(x.astype(BF16), w_gu_mxu[0], preferred_element_type=F32) + b_gu_ref[0]
        gate = jnp.minimum(gu[:, :d_ff], SWIGLU_LIMIT)
        up = jnp.clip(gu[:, d_ff:], -SWIGLU_LIMIT, SWIGLU_LIMIT)
        act = (up + 1.0) * (gate * _sigmoid(SWIGLU_ALPHA * gate))
        y = jnp.dot(act.astype(BF16), w_down_mxu[0], preferred_element_type=F32) + b_down_ref[0]
        words = _pack_bf16_pairs(y)
        for c, y_ref in enumerate(y_refs):
            y_ref[...] = words[:, c * ROW_CHUNK:(c + 1) * ROW_CHUNK]

    @pl.when(j >= used)
    def _():
        for y_ref in y_refs:
            y_ref[...] = jnp.zeros_like(y_ref)


def _experts(meta, slot_chunks, w_gu, b_gu, w_down, b_down):
    assert w_gu.dtype == BF16 and w_down.dtype == BF16
    n_in = len(slot_chunks)
    n_slots = slot_chunks[0].shape[0]
    n_experts, d, two_ff = w_gu.shape
    n_out = d // 2 // ROW_CHUNK
    d_ff = two_ff // 2
    bm = EXPERT_ROWS
    by_expert = lambda j, m: (m[META_BLOCK_EXPERT, j], 0, 0)
    grid_spec = pltpu.PrefetchScalarGridSpec(
        num_scalar_prefetch=1,
        grid=(n_slots // bm,),
        in_specs=[pl.BlockSpec((bm, ROW_CHUNK),
                               lambda j, m: (jnp.minimum(j, m[META_USED_BLOCKS, 0] - 1), 0))
                  ] * n_in
                 + [pl.BlockSpec((1, d, two_ff), by_expert), pl.BlockSpec((1, 1, two_ff), by_expert),
                    pl.BlockSpec((1, d_ff, d), by_expert), pl.BlockSpec((1, 1, d), by_expert)],
        out_specs=[pl.BlockSpec((bm, ROW_CHUNK), lambda j, m: (j, 0))] * n_out,
    )
    return pl.pallas_call(
        functools.partial(_experts_kernel, d_ff, n_in, n_out),
        grid_spec=grid_spec,
        out_shape=[jax.ShapeDtypeStruct((n_slots, ROW_CHUNK), jnp.uint32)] * n_out,
        compiler_params=_params("arbitrary"),
        name="experts",
    )(meta, *slot_chunks, w_gu, b_gu, w_down, b_down)


def _sc_gather_rows(table, idx_row):
    n_idx = idx_row.shape[1]
    width = table.shape[1]
    assert n_idx % SC_WINDOW == 0

    @functools.partial(pl.kernel, out_type=jax.ShapeDtypeStruct((n_idx, width), table.dtype),
                       mesh=_sc_mesh(), name="sc_gather_rows")
    def gather(table_hbm, idx_hbm, out_hbm):
        def body(idx_vmem, out_vmem):
            pltpu.sync_copy(table_hbm.at[idx_vmem.at[0]], out_vmem)

        pltpu.emit_pipeline(
            body,
            grid=(n_idx // SC_WINDOW,),
            in_specs=[pl.BlockSpec((1, SC_WINDOW), lambda i: (0, i))],
            out_specs=[pl.BlockSpec((SC_WINDOW, width), lambda i: (i, 0))],
            core_axis_name=("core", "subcore"),
            dimension_semantics=(pltpu.PARALLEL,),
        )(idx_hbm, out_hbm)

    return gather(table, idx_row)


def _reduce_kernel(n_chunks, *refs):
    y_refs = refs[:n_chunks]
    x1_ref, gates_ref, g_ref = refs[n_chunks:n_chunks + 3]
    out_ref = refs[-1]
    gates = gates_ref[...]
    acc = x1_ref[...]
    for c in range(TOP_K):
        y = _unpack_bf16_pairs(jnp.concatenate([r[c] for r in y_refs], axis=1))
        acc = acc + y * gates[:, c:c + 1]
    out_ref[...] = _rms_rows(acc, g_ref[...])


def _reduce(token0, n_total, out_so_far, y_chunks, x1, gates_t, final_g):
    n, d = x1.shape
    tb = REDUCE_TOKENS
    n_chunks = len(y_chunks)
    assert token0 % tb == 0
    in_specs = ([pl.BlockSpec((TOP_K, tb, ROW_CHUNK), lambda i: (0, i, 0))] * n_chunks
                + [pl.BlockSpec((tb, d), lambda i: (i, 0)),
                   pl.BlockSpec((tb, TOP_K), lambda i: (i, 0)),
                   pl.BlockSpec((1, d), lambda i: (0, 0))])
    args = [y.reshape(TOP_K, n, ROW_CHUNK) for y in y_chunks] + [x1, gates_t, final_g]
    aliases = {}
    if out_so_far is not None:
        in_specs.append(pl.BlockSpec(memory_space=pl.ANY))
        args.append(out_so_far)
        aliases = {len(args) - 1: 0}
    return pl.pallas_call(
        functools.partial(_reduce_kernel, n_chunks),
        grid=(n // tb,),
        in_specs=in_specs,
        out_specs=pl.BlockSpec((tb, d), lambda i: (i + token0 // tb, 0)),
        out_shape=jax.ShapeDtypeStruct((n_total, d), F32),
        input_output_aliases=aliases,
        compiler_params=_params("parallel"),
        name="reduce",
    )(*args)


def _group_matrix(width):
    a = lax.broadcasted_iota(I32, (width, width), 0) // HEAD_DIM
    b = lax.broadcasted_iota(I32, (width, width), 1) // HEAD_DIM
    g = (a == b).astype(BF16)
    return jnp.concatenate([g, g], axis=0)


def _row(vec):
    return vec.reshape(1, -1).astype(F32)


def kernel(x, w_in, conv_w, conv_norm_g, rwkv_mu, w0, w_up, a0, a_up, g_up, k_k, k_a, r_k,
           gn_w, gn_b, w_out, norm_mix_g, norm_ffn_g, w_router, b_router, w_gu, b_gu, w_down,
           b_down, norm_final_g):
    n_batch, seq, d = x.shape
    n = n_batch * seq
    depth = w_in.shape[0]
    cw = conv_w.shape[2]
    rw = w0.shape[1]
    n_experts = w_router.shape[2]
    decay_lora = w_up.shape[1]
    aaa_lora = a_up.shape[1]
    assert decay_lora + aaa_lora == V7X_LANES
    assert depth == 1
    assert n % MOE_PARTS == 0
    n_part = n // MOE_PARTS
    n_slots = n_part * TOP_K + n_experts * EXPERT_ROWS
    n_blocks = n_slots // EXPERT_ROWS
    n_blocks_padded = -(-n_blocks // V7X_LANES) * V7X_LANES
    seg2 = _group_matrix(rw)

    x2d = x.reshape(n, d)
    for l in range(depth):
        w_up_pad = jnp.concatenate([w_up[l], jnp.zeros((aaa_lora, rw), F32)], axis=0).astype(BF16)
        a_up_pad = jnp.concatenate([jnp.zeros((decay_lora, rw), F32), a_up[l]], axis=0).astype(BF16)
        (yconv, r, lw, k, v, kk, b, gate, bonus) = _mix_prep(
            x2d, seq, _row(norm_mix_g[l]), w_in[l].astype(BF16), conv_w[l].astype(F32),
            _row(conv_norm_g[l]), _row(rwkv_mu[l]), _row(w0[l]), w_up_pad, _row(a0[l]), a_up_pad,
            g_up[l].astype(BF16), _row(k_k[l]), _row(k_a[l]), _row(r_k[l]), seg2)
        shape3 = (n_batch, seq, rw)
        o, (w_gu_b, w_down_b) = _wkv_chunk(
            r.reshape(shape3), lw.reshape(shape3), k.reshape(shape3), v.reshape(shape3),
            kk.reshape(shape3), b.reshape(shape3),
            side_f32=(w_gu[l].astype(F32), w_down[l].astype(F32)))
        o = o.reshape(n, rw)
        w_out_b = w_out[l].astype(BF16)
        wr_t = w_router[l].T.astype(F32)
        wr_hi = wr_t.astype(BF16)
        wr_lo = (wr_t - wr_hi.astype(F32)).astype(BF16)
        expert_args = (w_gu_b, b_gu[l].reshape(n_experts, 1, -1).astype(F32),
                       w_down_b, b_down[l].reshape(n_experts, 1, -1).astype(F32))
        parts = []
        for token0 in range(0, n, n_part):
            x1, h2_chunks, top_e, gates = _post_mix(
                token0, n_part, o, bonus, gate, yconv, x2d, _row(gn_w[l]), _row(gn_b[l]),
                w_out_b[:cw], w_out_b[cw:], _row(norm_ffn_g[l]), wr_hi, wr_lo,
                b_router[l].reshape(n_experts, 1).astype(F32), seg2)
            dest, meta = _route(top_e, n_experts, n_blocks_padded)
            slot_chunks = [_sc_scatter_rows(h, dest, n_slots) for h in h2_chunks]
            parts.append((token0, x1, gates, dest, meta, slot_chunks))
        out = None
        for token0, x1, gates, dest, meta, slot_chunks in parts:
            y_chunks = _experts(meta, slot_chunks, *expert_args)
            idx_row = dest.reshape(1, TOP_K * n_part)
            out = _reduce(token0, n, out, [_sc_gather_rows(y, idx_row) for y in y_chunks], x1,
                          gates.T, _row(norm_final_g))
        x2d = out
    return x2d.reshape(n_batch, seq, d)
```

```python
import functools

import jax
import jax.numpy as jnp
from jax import lax
from jax.experimental import pallas as pl
from jax.experimental.pallas import tpu as pltpu
from jax.experimental.pallas import tpu_sc as plsc

F32 = jnp.float32
BF16 = jnp.bfloat16
I32 = jnp.int32

HEAD_DIM = 64
TOP_K = 4
NORM_EPS = 1e-5
GN_EPS = HEAD_DIM * 1e-5
SWIGLU_LIMIT = 7.0
SWIGLU_ALPHA = 1.702

V7X_LANES = 128
V7X_SUBLANES = 8
V7X_VMEM_LIMIT_BYTES = 56 * 1024 * 1024

PREP_ROWS = 512
POST_ROWS = 512
SCAN_FRAMES = 64
ROUTE_TOKENS = 1024
REDUCE_TOKENS = 1024
MOE_PARTS = 2
SC_WINDOW = 128
ROW_CHUNK = 256
META_BLOCK_EXPERT = 0
META_USED_BLOCKS = 1
META_VALID_ROWS = 2
EXPERT_ROWS = 512

def _params(*semantics):
    return pltpu.CompilerParams(dimension_semantics=semantics,
                                vmem_limit_bytes=V7X_VMEM_LIMIT_BYTES)


def _split_bf16(x):
    hi = x.astype(BF16)
    lo = (x - hi.astype(F32)).astype(BF16)
    return hi, lo


def _seg_sum(x, seg2):
    hi, lo = _split_bf16(x)
    return jnp.dot(jnp.concatenate([hi, lo], axis=1), seg2, preferred_element_type=F32)


def _pack_bf16_pairs(x):
    bits = pltpu.bitcast(x.astype(BF16).astype(F32), jnp.uint32)
    half = x.shape[1] // 2
    return bits[:, half:] | (bits[:, :half] >> 16)


def _unpack_bf16_pairs(words):
    return jnp.concatenate([pltpu.bitcast(words << 16, F32),
                            pltpu.bitcast(words & jnp.uint32(0xFFFF0000), F32)], axis=1)


def _rms_rows(x, g):
    return x * lax.rsqrt(jnp.mean(x * x, axis=-1, keepdims=True) + NORM_EPS) * g


def _sigmoid(x):
    return 1.0 / (1.0 + jnp.exp(-x))


def _mix_prep_kernel(blocks_per_seq, cw, rw,
                     xprev_ref, x_ref, g_ref, w_in_ref, conv_w_ref, conv_g_ref, mu_ref,
                     w0_ref, w_up_ref, a0_ref, a_up_ref, g_up_ref, k_k_ref, k_a_ref, r_k_ref,
                     seg2_ref,
                     yconv_ref, r_ref, w_ref, k_ref, v_ref, kk_ref, b_ref, gate_ref, bonus_ref,
                     p_scr, z_scr):
    tm = x_ref.shape[0]
    halo = xprev_ref.shape[0]
    first = (pl.program_id(0) % blocks_per_seq) == 0
    xp = xprev_ref[...] * jnp.where(first, 0.0, 1.0)
    xa = jnp.concatenate([xp, x_ref[...]], axis=0)
    h = _rms_rows(xa, g_ref[...])
    p_scr[...] = jnp.dot(h.astype(BF16), w_in_ref[...], preferred_element_type=F32)
    seg2 = seg2_ref[...]

    z_scr[...] = p_scr[:, 2 * cw:3 * cw] * p_scr[:, 0:cw]
    conv = (conv_w_ref[0:1, :] * z_scr[halo - 2:halo - 2 + tm, :]
            + conv_w_ref[1:2, :] * z_scr[halo - 1:halo - 1 + tm, :]
            + conv_w_ref[2:3, :] * z_scr[halo:halo + tm, :])
    y = p_scr[halo:halo + tm, cw:2 * cw] * conv
    ms = _seg_sum(y * y, seg2) * (1.0 / HEAD_DIM)
    yconv_ref[...] = (y * lax.rsqrt(ms + NORM_EPS) * conv_g_ref[...]).astype(yconv_ref.dtype)

    c0 = 3 * cw
    cur = p_scr[halo:halo + tm, c0:]
    prev = p_scr[halo - 1:halo - 1 + tm, c0:]
    q = cur + (prev - cur) * mu_ref[...]
    r = q[:, 0:rw]
    k = q[:, rw:2 * rw]
    v = q[:, 2 * rw:3 * rw]
    lora_wa = q[:, 3 * rw:3 * rw + V7X_LANES]
    lora_g = q[:, 3 * rw + V7X_LANES:]
    w_lin = w0_ref[...] + jnp.dot(jnp.tanh(lora_wa).astype(BF16), w_up_ref[...],
                                  preferred_element_type=F32)
    neg = -w_lin
    softplus = jnp.maximum(neg, 0.0) + jnp.log(1.0 + jnp.exp(-jnp.abs(neg)))
    log_decay = -jnp.exp(-softplus - 0.5)
    a = _sigmoid(a0_ref[...] + jnp.dot(lora_wa.astype(BF16), a_up_ref[...],
                                       preferred_element_type=F32))
    gate = jnp.dot(_sigmoid(lora_g).astype(BF16), g_up_ref[...], preferred_element_type=F32)
    kk = k * k_k_ref[...]
    kk = kk / jnp.maximum(jnp.sqrt(_seg_sum(kk * kk, seg2)), 1e-12)
    k_mod = k * (1.0 + (a - 1.0) * k_a_ref[...])
    bonus = _seg_sum(r * k_mod * r_k_ref[...], seg2) * v
    r_ref[...] = r
    w_ref[...] = log_decay
    k_ref[...] = k_mod
    v_ref[...] = v
    kk_ref[...] = kk
    b_ref[...] = kk * a
    gate_ref[...] = gate
    bonus_ref[...] = bonus


def _mix_prep(x2d, seq, norm_g, w_in, conv_w, conv_g, mu, w0, w_up, a0, a_up, g_up, k_k, k_a,
              r_k, seg2):
    n, d = x2d.shape
    tm = PREP_ROWS
    halo = V7X_SUBLANES
    cw = conv_w.shape[1]
    rw = w0.shape[1]
    in_cols = w_in.shape[1]
    assert seq % tm == 0 and n % tm == 0
    full = lambda a: pl.BlockSpec(a.shape, lambda i: (0,) * a.ndim)
    row_spec = lambda c: pl.BlockSpec((tm, c), lambda i: (i, 0))
    consts = (norm_g, w_in, conv_w, conv_g, mu, w0, w_up, a0, a_up, g_up, k_k, k_a, r_k, seg2)
    outs = [jax.ShapeDtypeStruct((n, cw), BF16)] + [jax.ShapeDtypeStruct((n, rw), F32)] * 8
    return pl.pallas_call(
        functools.partial(_mix_prep_kernel, seq // tm, cw, rw),
        grid=(n // tm,),
        in_specs=[pl.BlockSpec((halo, d), lambda i: (jnp.maximum(i * (tm // halo) - 1, 0), 0)),
                  row_spec(d)] + [full(c) for c in consts],
        out_specs=[row_spec(cw)] + [row_spec(rw)] * 8,
        out_shape=outs,
        scratch_shapes=[pltpu.VMEM((halo + tm, in_cols), F32), pltpu.VMEM((halo + tm, cw), F32)],
        compiler_params=_params("parallel"),
        name="mix_prep",
    )(x2d, x2d, *consts)


def _bdot(a, b):
    return jnp.dot(a.astype(BF16), b.astype(BF16), preferred_element_type=F32)


def _bdot_nt(a, b):
    return lax.dot_general(a.astype(BF16), b.astype(BF16), (((1,), (1,)), ((), ())),
                           preferred_element_type=F32)


def _wkv_chunk_kernel(n_batch, n_pairs, n_side,
                      r_ref, lw_ref, k_ref, v_ref, kk_ref, b_ref, *refs):
    frames = r_ref.shape[1]
    lanes = V7X_LANES
    hd = HEAD_DIM
    assert frames == hd
    side_in, o_ref, side_out, state_ref = (refs[:n_side], refs[n_side],
                                           refs[n_side + 1:2 * n_side + 1], refs[-1])
    for src, dst in zip(side_in, side_out):
        dst[...] = src[...].astype(dst.dtype)

    @pl.when(pl.program_id(0) == 0)
    def _():
        state_ref[...] = jnp.zeros_like(state_ref)

    row = lax.broadcasted_iota(I32, (lanes, lanes), 0)
    lane = lax.broadcasted_iota(I32, (lanes, lanes), 1)
    same_head = (row // hd) == (lane // hd)
    rt = row % hd
    ls = lane % hd
    strict_same = same_head & (rt > ls)
    strict_cross = jnp.logical_not(same_head) & (rt > ls)
    incl_same = same_head & (rt >= ls)
    incl_cross = jnp.logical_not(same_head) & (rt >= ls)
    eye = row == lane
    level_masks = []
    m = 1
    while m < hd:
        level_masks.append(same_head & ((rt // (2 * m)) == (ls // (2 * m)))
                           & (((rt // m) % 2) == 1) & (((ls // m) % 2) == 0))
        m *= 2
    left = lax.broadcasted_iota(I32, (frames, lanes), 1) < hd
    tri = (lax.broadcasted_iota(I32, (frames, frames), 0)
           >= lax.broadcasted_iota(I32, (frames, frames), 1)).astype(BF16)

    def diag_blocks(x):
        return jnp.concatenate([jnp.where(left, x, 0.0), jnp.where(left, 0.0, x)], axis=0)

    def cross_blocks(x):
        return jnp.concatenate([jnp.where(left, 0.0, x), jnp.where(left, x, 0.0)], axis=0)

    def batch_prep(bi):
        lw = lw_ref[bi]
        hi = lw.astype(BF16)
        rem = lw - hi.astype(F32)
        mid = rem.astype(BF16)
        lo = (rem - mid.astype(F32)).astype(BF16)
        cs = (jnp.dot(tri, hi, preferred_element_type=F32)
              + jnp.dot(tri, mid, preferred_element_type=F32)
              + jnp.dot(tri, lo, preferred_element_type=F32))
        cs_end = cs[frames - 1:frames, :]
        e_neg = jnp.exp(-cs)
        e_end = jnp.exp(cs_end - cs)
        b_in = b_ref[bi]
        k_in = k_ref[bi]
        return dict(a_t=-kk_ref[bi] * jnp.exp(cs - lw), b_t=b_in * e_neg, k_t=k_in * e_neg,
                    r_t=r_ref[bi] * jnp.exp(cs), b_h=b_in * e_end, k_h=k_in * e_end,
                    v=v_ref[bi], g_end=jnp.exp(cs_end))

    group = []
    for bi in range(n_batch):
        prep = batch_prep(bi)
        for pi in range(n_pairs):
            sl = slice(pi * lanes, (pi + 1) * lanes)
            group.append((bi, sl, bi * n_pairs + pi, {n: x[:, sl] for n, x in prep.items()}))
    bf = lambda x: x.astype(BF16)
    a_bd = [bf(diag_blocks(p["a_t"])) for _, _, _, p in group]
    r_bd = [bf(diag_blocks(p["r_t"])) for _, _, _, p in group]
    d0 = [_bdot_nt(jnp.concatenate([a[:hd], r[:hd]], axis=0),
                   bf(jnp.concatenate([p["b_t"], p["k_t"]], axis=0)))
          for a, r, (_, _, _, p) in zip(a_bd, r_bd, group)]
    d1 = [_bdot_nt(jnp.concatenate([a[hd:], r[hd:]], axis=0),
                   bf(jnp.concatenate([p["k_t"], p["b_t"]], axis=0)))
          for a, r, (_, _, _, p) in zip(a_bd, r_bd, group)]
    a_rows = [jnp.concatenate([x[:hd], y[:hd]], axis=0) for x, y in zip(d0, d1)]
    m_rows = [jnp.concatenate([x[hd:], y[hd:]], axis=0) for x, y in zip(d0, d1)]
    a_ab = [bf(jnp.where(strict_same, x, 0.0)) for x in a_rows]
    a_ak = [bf(jnp.where(strict_cross, x, 0.0)) for x in a_rows]
    m_rb = [bf(jnp.where(incl_same, x, 0.0)) for x in m_rows]
    m_rk = [bf(jnp.where(incl_cross, x, 0.0)) for x in m_rows]
    t_inv = [jnp.where(eye, 1.0, jnp.where(level_masks[0], x, 0.0)) for x in a_rows]
    for mask in level_masks[1:]:
        t_bf = [bf(t) for t in t_inv]
        half = [jnp.where(mask, _bdot(t, x), 0.0) for t, x in zip(t_bf, a_ab)]
        t_inv = [t + _bdot(h, tb) for t, h, tb in zip(t_inv, half, t_bf)]
    t_bf = [bf(t) for t in t_inv]
    v_x = [bf(cross_blocks(p["v"])) for _, _, _, p in group]
    akv = [_bdot(x, v) for x, v in zip(a_ak, v_x)]
    wu = [bf(_bdot(t, jnp.concatenate([a, bf(x)], axis=1)))
          for t, a, x in zip(t_bf, a_bd, akv)]
    bh_t = [bf(diag_blocks(p["b_h"]).T) for _, _, _, p in group]
    kh_t = [bf(cross_blocks(p["k_h"]).T) for _, _, _, p in group]
    pw_rw = [_bdot(jnp.concatenate([bt, m], axis=0), x[:, :lanes])
             for bt, m, x in zip(bh_t, m_rb, wu)]
    q_o = [_bdot(jnp.concatenate([jnp.concatenate([bt, kt], axis=1),
                                  jnp.concatenate([mb, mk], axis=1)], axis=0),
                 jnp.concatenate([x[:, lanes:], v], axis=0))
           for bt, kt, mb, mk, x, v in zip(bh_t, kh_t, m_rb, m_rk, wu, v_x)]
    res = [_bdot(jnp.concatenate([diag_blocks(p["r_t"]) + pr[lanes:],
                                  pr[:lanes] + jnp.where(eye, p["g_end"], 0.0)], axis=0),
                 state_ref[idx])
           for pr, (_, _, idx, p) in zip(pw_rw, group)]
    for x, q, (bi, sl, idx, _) in zip(res, q_o, group):
        o_bd = x[:lanes] + q[lanes:]
        state_ref[idx] = x[lanes:] + q[:lanes]
        o_ref[bi, :, sl] = jnp.where(left, o_bd[:hd], o_bd[hd:])


def _wkv_chunk(r, lw, k, v, kk, b, side_f32=()):
    n_batch, seq, rw = r.shape
    n_pairs = rw // V7X_LANES
    frames = SCAN_FRAMES
    assert seq % frames == 0
    n_steps = seq // frames
    spec = pl.BlockSpec((n_batch, frames, rw), lambda c: (0, c, 0))
    sliced = [a.reshape(n_steps, -1, a.shape[-1]) for a in side_f32]
    side_specs = [pl.BlockSpec((1,) + a.shape[1:], lambda c: (c, 0, 0)) for a in sliced]
    outs = pl.pallas_call(
        functools.partial(_wkv_chunk_kernel, n_batch, n_pairs, len(sliced)),
        grid=(n_steps,),
        in_specs=[spec] * 6 + side_specs,
        out_specs=[spec] + side_specs,
        out_shape=[jax.ShapeDtypeStruct((n_batch, seq, rw), F32)]
                  + [jax.ShapeDtypeStruct(a.shape, BF16) for a in sliced],
        scratch_shapes=[pltpu.VMEM((n_batch * n_pairs, V7X_LANES, V7X_LANES), F32)],
        compiler_params=_params("arbitrary"),
        name="wkv_chunk",
    )(r, lw, k, v, kk, b, *sliced)
    return outs[0], [y.reshape(a.shape) for y, a in zip(outs[1:], side_f32)]


def _post_mix_kernel(n_experts,
                     o_ref, bonus_ref, gate_ref, yconv_ref, x_ref, gn_w_ref, gn_b_ref,
                     w_out_c_ref, w_out_r_ref, ffn_g_ref, wr_hi_ref, wr_lo_ref, b_router_ref,
                     seg2_ref,
                     x1_ref, *outs):
    h2_refs, (top_e_ref, gates_ref) = outs[:-2], outs[-2:]
    seg2 = seg2_ref[...]
    o = o_ref[...]
    mean = _seg_sum(o, seg2) * (1.0 / HEAD_DIM)
    cen = o - mean
    var = _seg_sum(cen * cen, seg2) * (1.0 / HEAD_DIM)
    o = cen * lax.rsqrt(var + GN_EPS) * gn_w_ref[...] + gn_b_ref[...]
    y_rwkv = (o + bonus_ref[...]) * gate_ref[...]
    x1 = (x_ref[...]
          + jnp.dot(yconv_ref[...], w_out_c_ref[...], preferred_element_type=F32)
          + jnp.dot(y_rwkv.astype(BF16), w_out_r_ref[...], preferred_element_type=F32))
    x1_ref[...] = x1
    h2 = _rms_rows(x1, ffn_g_ref[...])
    words = _pack_bf16_pairs(h2)
    for c, h2_ref in enumerate(h2_refs):
        h2_ref[...] = words[:, c * ROW_CHUNK:(c + 1) * ROW_CHUNK]

    h_hi, h_lo = _split_bf16(h2)
    nt = (((1,), (1,)), ((), ()))
    logits = (lax.dot_general(wr_hi_ref[...], h_hi, nt, preferred_element_type=F32)
              + lax.dot_general(wr_hi_ref[...], h_lo, nt, preferred_element_type=F32)
              + lax.dot_general(wr_lo_ref[...], h_hi, nt, preferred_element_type=F32)
              + b_router_ref[...])
    e_id = lax.broadcasted_iota(I32, logits.shape, 0).astype(F32)
    work = logits
    tops, ids = [], []
    for _ in range(TOP_K):
        m = jnp.max(work, axis=0, keepdims=True)
        sel = jnp.min(jnp.where(work == m, e_id, float(n_experts)), axis=0, keepdims=True)
        tops.append(m)
        ids.append(sel)
        work = jnp.where(e_id == sel, -jnp.inf, work)
    ex = [jnp.exp(t - tops[0]) for t in tops]
    denom = ex[0] + ex[1] + ex[2] + ex[3]
    top_e_ref[...] = jnp.concatenate(ids, axis=0).astype(I32)
    gates_ref[...] = jnp.concatenate([e / denom for e in ex], axis=0)


def _post_mix(token0, n, o, bonus, gate, yconv, x2d, gn_w, gn_b, w_out_c, w_out_r, ffn_g, wr_hi,
              wr_lo, b_router, seg2):
    d = x2d.shape[1]
    rw = o.shape[1]
    cw = yconv.shape[1]
    n_experts = wr_hi.shape[0]
    tm = POST_ROWS
    assert token0 % tm == 0 and n % tm == 0
    full = lambda a: pl.BlockSpec(a.shape, lambda i: (0,) * a.ndim)
    in_spec = lambda c: pl.BlockSpec((tm, c), lambda i: (i + token0 // tm, 0))
    row_spec = lambda c: pl.BlockSpec((tm, c), lambda i: (i, 0))
    col_spec = pl.BlockSpec((TOP_K, tm), lambda i: (0, i))
    consts = (gn_w, gn_b, w_out_c, w_out_r, ffn_g, wr_hi, wr_lo, b_router, seg2)
    n_chunks = d // 2 // ROW_CHUNK
    outs = pl.pallas_call(
        functools.partial(_post_mix_kernel, n_experts),
        grid=(n // tm,),
        in_specs=[in_spec(rw), in_spec(rw), in_spec(rw), in_spec(cw), in_spec(d)]
                 + [full(c) for c in consts],
        out_specs=[row_spec(d)] + [row_spec(ROW_CHUNK)] * n_chunks + [col_spec, col_spec],
        out_shape=[jax.ShapeDtypeStruct((n, d), F32)]
                  + [jax.ShapeDtypeStruct((n, ROW_CHUNK), jnp.uint32)] * n_chunks
                  + [jax.ShapeDtypeStruct((TOP_K, n), I32), jax.ShapeDtypeStruct((TOP_K, n), F32)],
        compiler_params=_params("parallel"),
        name="post_mix",
    )(o, bonus, gate, yconv, x2d, *consts)
    return outs[0], outs[1:1 + n_chunks], outs[-2], outs[-1]


def _route_kernel(n_experts, block_rows,
                  top_e_ref, tri_ref, dest_ref, meta_ref,
                  count_ref, start_ref, carry_ref):
    phase = pl.program_id(0)
    j = pl.program_id(1)
    tb = top_e_ref.shape[1]
    e_id = lax.broadcasted_iota(I32, (n_experts, tb), 0)
    top_e = top_e_ref[...]
    onehot = jnp.zeros((n_experts, tb), F32)
    for c in range(TOP_K):
        onehot = onehot + jnp.where(top_e[c:c + 1, :] == e_id, 1.0, 0.0)
    block_count = jnp.sum(onehot, axis=1, keepdims=True)

    @pl.when((phase == 0) & (j == 0))
    def _():
        count_ref[...] = jnp.zeros_like(count_ref)

    @pl.when(phase == 0)
    def _():
        count_ref[...] += block_count

    @pl.when((phase == 1) & (j == 0))
    def _():
        counts = count_ref[...]
        padded = jnp.ceil(counts * (1.0 / block_rows)) * block_rows
        sub = lax.broadcasted_iota(I32, (n_experts, n_experts), 0)
        lane = lax.broadcasted_iota(I32, (n_experts, n_experts), 1)
        padded_row = jnp.sum(jnp.where(sub == lane, padded, 0.0), axis=0, keepdims=True)
        start = jnp.sum(jnp.where(lane < sub, padded_row, 0.0), axis=1, keepdims=True)
        start_ref[...] = start
        carry_ref[...] = jnp.zeros_like(carry_ref)
        end = start + padded
        nb = meta_ref.shape[1]
        slot0 = (lax.broadcasted_iota(I32, (n_experts, nb), 1) * block_rows).astype(F32)
        block_e = jnp.sum(jnp.where(end <= slot0, 1.0, 0.0), axis=0, keepdims=True)
        block_e = jnp.minimum(block_e, n_experts - 1.0)
        used = jnp.max(end, axis=0, keepdims=True) * (1.0 / block_rows)
        in_region = (start <= slot0) & (slot0 < end)
        valid = jnp.sum(jnp.where(in_region, jnp.minimum(start + counts - slot0, block_rows), 0.0),
                        axis=0, keepdims=True)
        row = lax.broadcasted_iota(I32, meta_ref.shape, 0)
        meta_ref[...] = jnp.where(row == META_BLOCK_EXPERT, block_e,
                                  jnp.where(row == META_VALID_ROWS, valid, used)).astype(I32)

    @pl.when(phase == 1)
    def _():
        incl = jnp.dot(onehot.astype(BF16), tri_ref[...], preferred_element_type=F32)
        base = incl - onehot + carry_ref[...] + start_ref[...]
        rows = [jnp.sum(jnp.where(top_e[c:c + 1, :] == e_id, base, 0.0), axis=0, keepdims=True)
                for c in range(TOP_K)]
        dest_ref[...] = jnp.concatenate(rows, axis=0).astype(I32)
        carry_ref[...] += block_count


def _route(top_e, n_experts, n_blocks_padded):
    n = top_e.shape[1]
    tb = ROUTE_TOKENS
    assert n % tb == 0
    tri = (lax.broadcasted_iota(I32, (tb, tb), 0) <= lax.broadcasted_iota(I32, (tb, tb), 1)
           ).astype(BF16)
    return pl.pallas_call(
        functools.partial(_route_kernel, n_experts, EXPERT_ROWS),
        grid=(2, n // tb),
        in_specs=[pl.BlockSpec((TOP_K, tb), lambda ph, j: (0, j)),
                  pl.BlockSpec((tb, tb), lambda ph, j: (0, 0))],
        out_specs=[pl.BlockSpec((TOP_K, tb), lambda ph, j: (0, j * ph)),
                   pl.BlockSpec((V7X_SUBLANES, n_blocks_padded), lambda ph, j: (0, 0))],
        out_shape=[jax.ShapeDtypeStruct((TOP_K, n), I32),
                   jax.ShapeDtypeStruct((V7X_SUBLANES, n_blocks_padded), I32)],
        scratch_shapes=[pltpu.VMEM((n_experts, 1), F32)] * 3,
        compiler_params=_params("arbitrary", "arbitrary"),
        name="route",
    )(top_e, tri)


def _sc_mesh():
    return plsc.VectorSubcoreMesh(core_axis_name="core", subcore_axis_name="subcore")


def _sc_scatter_rows(src, dest, n_slots):
    n, width = src.shape
    assert n % SC_WINDOW == 0

    @functools.partial(pl.kernel, out_type=jax.ShapeDtypeStruct((n_slots, width), src.dtype),
                       mesh=_sc_mesh(), name="sc_scatter_rows")
    def scatter(src_hbm, dest_hbm, out_hbm):
        def body(src_vmem, dest_vmem):
            for c in range(TOP_K):
                pltpu.sync_copy(src_vmem, out_hbm.at[dest_vmem.at[c]])

        pltpu.emit_pipeline(
            body,
            grid=(n // SC_WINDOW,),
            in_specs=[pl.BlockSpec((SC_WINDOW, width), lambda i: (i, 0)),
                      pl.BlockSpec((TOP_K, SC_WINDOW), lambda i: (0, i))],
            out_specs=[],
            core_axis_name=("core", "subcore"),
            dimension_semantics=(pltpu.PARALLEL,),
        )(src_hbm, dest_hbm)

    return scatter(src, dest)


def _experts_kernel(d_ff, n_in, n_out, meta_ref, *refs):
    x_refs, refs = refs[:n_in], refs[n_in:]
    w_gu_mxu, b_gu_ref, w_down_mxu, b_down_ref = refs[:4]
    y_refs = refs[4:]
    j = pl.program_id(0)
    used = meta_ref[META_USED_BLOCKS, 0]

    @pl.when(j < used)
    def _():
        x = _unpack_bf16_pairs(jnp.concatenate([r[...] for r in x_refs], axis=1))
        row = lax.broadcasted_iota(I32, x.shape, 0)
        x = jnp.where(row < meta_ref[META_VALID_ROWS, j], x, 0.0)
        gu = jnp.dot(x.astype(BF16), w_gu_mxu[0], preferred_element_type=F32) + b_gu_ref[0]
        gate = jnp.minimum(gu[:, :d_ff], SWIGLU_LIMIT)
        up = jnp.clip(gu[:, d_ff:], -SWIGLU_LIMIT, SWIGLU_LIMIT)
        act = (up + 1.0) * (gate * _sigmoid(SWIGLU_ALPHA * gate))
        y = jnp.dot(act.astype(BF16), w_down_mxu[0], preferred_element_type=F32) + b_down_ref[0]
        words = _pack_bf16_pairs(y)
        for c, y_ref in enumerate(y_refs):
            y_ref[...] = words[:, c * ROW_CHUNK:(c + 1) * ROW_CHUNK]

    @pl.when(j >= used)
    def _():
        for y_ref in y_refs:
            y_ref[...] = jnp.zeros_like(y_ref)


def _experts(meta, slot_chunks, w_gu, b_gu, w_down, b_down):
    assert w_gu.dtype == BF16 and w_down.dtype == BF16
    n_in = len(slot_chunks)
    n_slots = slot_chunks[0].shape[0]
    n_experts, d, two_ff = w_gu.shape
    n_out = d // 2 // ROW_CHUNK
    d_ff = two_ff // 2
    bm = EXPERT_ROWS
    by_expert = lambda j, m: (m[META_BLOCK_EXPERT, j], 0, 0)
    grid_spec = pltpu.PrefetchScalarGridSpec(
        num_scalar_prefetch=1,
        grid=(n_slots // bm,),
        in_specs=[pl.BlockSpec((bm, ROW_CHUNK),
                               lambda j, m: (jnp.minimum(j, m[META_USED_BLOCKS, 0] - 1), 0))
                  ] * n_in
                 + [pl.BlockSpec((1, d, two_ff), by_expert), pl.BlockSpec((1, 1, two_ff), by_expert),
                    pl.BlockSpec((1, d_ff, d), by_expert), pl.BlockSpec((1, 1, d), by_expert)],
        out_specs=[pl.BlockSpec((bm, ROW_CHUNK), lambda j, m: (j, 0))] * n_out,
    )
    return pl.pallas_call(
        functools.partial(_experts_kernel, d_ff, n_in, n_out),
        grid_spec=grid_spec,
        out_shape=[jax.ShapeDtypeStruct((n_slots, ROW_CHUNK), jnp.uint32)] * n_out,
        compiler_params=_params("arbitrary"),
        name="experts",
    )(meta, *slot_chunks, w_gu, b_gu, w_down, b_down)


def _sc_gather_rows(table, idx_row):
    n_idx = idx_row.shape[1]
    width = table.shape[1]
    assert n_idx % SC_WINDOW == 0

    @functools.partial(pl.kernel, out_type=jax.ShapeDtypeStruct((n_idx, width), table.dtype),
                       mesh=_sc_mesh(), name="sc_gather_rows")
    def gather(table_hbm, idx_hbm, out_hbm):
        def body(idx_vmem, out_vmem):
            pltpu.sync_copy(table_hbm.at[idx_vmem.at[0]], out_vmem)

        pltpu.emit_pipeline(
            body,
            grid=(n_idx // SC_WINDOW,),
            in_specs=[pl.BlockSpec((1, SC_WINDOW), lambda i: (0, i))],
            out_specs=[pl.BlockSpec((SC_WINDOW, width), lambda i: (i, 0))],
            core_axis_name=("core", "subcore"),
            dimension_semantics=(pltpu.PARALLEL,),
        )(idx_hbm, out_hbm)

    return gather(table, idx_row)


def _reduce_kernel(n_chunks, *refs):
    y_refs = refs[:n_chunks]
    x1_ref, gates_ref, g_ref = refs[n_chunks:n_chunks + 3]
    out_ref = refs[-1]
    gates = gates_ref[...]
    acc = x1_ref[...]
    for c in range(TOP_K):
        y = _unpack_bf16_pairs(jnp.concatenate([r[c] for r in y_refs], axis=1))
        acc = acc + y * gates[:, c:c + 1]
    out_ref[...] = _rms_rows(acc, g_ref[...])


def _reduce(token0, n_total, out_so_far, y_chunks, x1, gates_t, final_g):
    n, d = x1.shape
    tb = REDUCE_TOKENS
    n_chunks = len(y_chunks)
    assert token0 % tb == 0
    in_specs = ([pl.BlockSpec((TOP_K, tb, ROW_CHUNK), lambda i: (0, i, 0))] * n_chunks
                + [pl.BlockSpec((tb, d), lambda i: (i, 0)),
                   pl.BlockSpec((tb, TOP_K), lambda i: (i, 0)),
                   pl.BlockSpec((1, d), lambda i: (0, 0))])
    args = [y.reshape(TOP_K, n, ROW_CHUNK) for y in y_chunks] + [x1, gates_t, final_g]
    aliases = {}
    if out_so_far is not None:
        in_specs.append(pl.BlockSpec(memory_space=pl.ANY))
        args.append(out_so_far)
        aliases = {len(args) - 1: 0}
    return pl.pallas_call(
        functools.partial(_reduce_kernel, n_chunks),
        grid=(n // tb,),
        in_specs=in_specs,
        out_specs=pl.BlockSpec((tb, d), lambda i: (i + token0 // tb, 0)),
        out_shape=jax.ShapeDtypeStruct((n_total, d), F32),
        input_output_aliases=aliases,
        compiler_params=_params("parallel"),
        name="reduce",
    )(*args)


def _group_matrix(width):
    a = lax.broadcasted_iota(I32, (width, width), 0) // HEAD_DIM
    b = lax.broadcasted_iota(I32, (width, width), 1) // HEAD_DIM
    g = (a == b).astype(BF16)
    return jnp.concatenate([g, g], axis=0)


def _row(vec):
    return vec.reshape(1, -1).astype(F32)


def kernel(x, w_in, conv_w, conv_norm_g, rwkv_mu, w0, w_up, a0, a_up, g_up, k_k, k_a, r_k,
           gn_w, gn_b, w_out, norm_mix_g, norm_ffn_g, w_router, b_router, w_gu, b_gu, w_down,
           b_down, norm_final_g):
    n_batch, seq, d = x.shape
    n = n_batch * seq
    depth = w_in.shape[0]
    cw = conv_w.shape[2]
    rw = w0.shape[1]
    n_experts = w_router.shape[2]
    decay_lora = w_up.shape[1]
    aaa_lora = a_up.shape[1]
    assert decay_lora + aaa_lora == V7X_LANES
    assert depth == 1
    assert n % MOE_PARTS == 0
    n_part = n // MOE_PARTS
    n_slots = n_part * TOP_K + n_experts * EXPERT_ROWS
    n_blocks = n_slots // EXPERT_ROWS
    n_blocks_padded = -(-n_blocks // V7X_LANES) * V7X_LANES
    seg2 = _group_matrix(rw)

    x2d = x.reshape(n, d)
    for l in range(depth):
        w_up_pad = jnp.concatenate([w_up[l], jnp.zeros((aaa_lora, rw), F32)], axis=0).astype(BF16)
        a_up_pad = jnp.concatenate([jnp.zeros((decay_lora, rw), F32), a_up[l]], axis=0).astype(BF16)
        (yconv, r, lw, k, v, kk, b, gate, bonus) = _mix_prep(
            x2d, seq, _row(norm_mix_g[l]), w_in[l].astype(BF16), conv_w[l].astype(F32),
            _row(conv_norm_g[l]), _row(rwkv_mu[l]), _row(w0[l]), w_up_pad, _row(a0[l]), a_up_pad,
            g_up[l].astype(BF16), _row(k_k[l]), _row(k_a[l]), _row(r_k[l]), seg2)
        shape3 = (n_batch, seq, rw)
        o, (w_gu_b, w_down_b) = _wkv_chunk(
            r.reshape(shape3), lw.reshape(shape3), k.reshape(shape3), v.reshape(shape3),
            kk.reshape(shape3), b.reshape(shape3),
            side_f32=(w_gu[l].astype(F32), w_down[l].astype(F32)))
        o = o.reshape(n, rw)
        w_out_b = w_out[l].astype(BF16)
        wr_t = w_router[l].T.astype(F32)
        wr_hi = wr_t.astype(BF16)
        wr_lo = (wr_t - wr_hi.astype(F32)).astype(BF16)
        expert_args = (w_gu_b, b_gu[l].reshape(n_experts, 1, -1).astype(F32),
                       w_down_b, b_down[l].reshape(n_experts, 1, -1).astype(F32))
        parts = []
        for token0 in range(0, n, n_part):
            x1, h2_chunks, top_e, gates = _post_mix(
                token0, n_part, o, bonus, gate, yconv, x2d, _row(gn_w[l]), _row(gn_b[l]),
                w_out_b[:cw], w_out_b[cw:], _row(norm_ffn_g[l]), wr_hi, wr_lo,
                b_router[l].reshape(n_experts, 1).astype(F32), seg2)
            dest, meta = _route(top_e, n_experts, n_blocks_padded)
            slot_chunks = [_sc_scatter_rows(h, dest, n_slots) for h in h2_chunks]
            parts.append((token0, x1, gates, dest, meta, slot_chunks))
        out = None
        for token0, x1, gates, dest, meta, slot_chunks in parts:
            y_chunks = _experts(meta, slot_chunks, *expert_args)
            idx_row = dest.reshape(1, TOP_K * n_part)
            out = _reduce(token0, n, out, [_sc_gather_rows(y, idx_row) for y in y_chunks], x1,
                          gates.T, _row(norm_final_g))
        x2d = out
    return x2d.reshape(n_batch, seq, d)
```

```python
import functools

import jax
import jax.numpy as jnp
from jax import lax
from jax.experimental import pallas as pl
from jax.experimental.pallas import tpu as pltpu
from jax.experimental.pallas import tpu_sc as plsc

F32 = jnp.float32
BF16 = jnp.bfloat16
I32 = jnp.int32

HEAD_DIM = 64
TOP_K = 4
NORM_EPS = 1e-5
GN_EPS = HEAD_DIM * 1e-5
SWIGLU_LIMIT = 7.0
SWIGLU_ALPHA = 1.702

V7X_LANES = 128
V7X_SUBLANES = 8
V7X_VMEM_LIMIT_BYTES = 56 * 1024 * 1024

PREP_ROWS = 512
POST_ROWS = 1024
SCAN_FRAMES = 64
ROUTE_TOKENS = 1024
REDUCE_TOKENS = 1024
MOE_PARTS = 2
SC_WINDOW = 128
ROW_CHUNK = 256
META_BLOCK_EXPERT = 0
META_USED_BLOCKS = 1
META_VALID_ROWS = 2
EXPERT_ROWS = 512

def _params(*semantics):
    return pltpu.CompilerParams(dimension_semantics=semantics,
                                vmem_limit_bytes=V7X_VMEM_LIMIT_BYTES)


def _split_bf16(x):
    hi = x.astype(BF16)
    lo = (x - hi.astype(F32)).astype(BF16)
    return hi, lo


def _seg_sum(x, seg2):
    hi, lo = _split_bf16(x)
    return jnp.dot(jnp.concatenate([hi, lo], axis=1), seg2, preferred_element_type=F32)


def _pack_bf16_pairs(x):
    bits = pltpu.bitcast(x.astype(BF16).astype(F32), jnp.uint32)
    half = x.shape[1] // 2
    return bits[:, half:] | (bits[:, :half] >> 16)


def _unpack_bf16_pairs(words):
    return jnp.concatenate([pltpu.bitcast(words << 16, F32),
                            pltpu.bitcast(words & jnp.uint32(0xFFFF0000), F32)], axis=1)


def _rms_rows(x, g):
    return x * lax.rsqrt(jnp.mean(x * x, axis=-1, keepdims=True) + NORM_EPS) * g


def _sigmoid(x):
    return 1.0 / (1.0 + jnp.exp(-x))


def _mix_prep_kernel(blocks_per_seq, cw, rw,
                     xprev_ref, x_ref, g_ref, w_in_ref, conv_w_ref, conv_g_ref, mu_ref,
                     w0_ref, w_up_ref, a0_ref, a_up_ref, g_up_ref, k_k_ref, k_a_ref, r_k_ref,
                     seg2_ref,
                     yconv_ref, r_ref, w_ref, k_ref, v_ref, kk_ref, b_ref, gate_ref, bonus_ref,
                     p_scr, z_scr):
    tm = x_ref.shape[0]
    halo = xprev_ref.shape[0]
    first = (pl.program_id(0) % blocks_per_seq) == 0
    xp = xprev_ref[...] * jnp.where(first, 0.0, 1.0)
    xa = jnp.concatenate([xp, x_ref[...]], axis=0)
    h = _rms_rows(xa, g_ref[...])
    p_scr[...] = jnp.dot(h.astype(BF16), w_in_ref[...], preferred_element_type=F32)
    seg2 = seg2_ref[...]

    z_scr[...] = p_scr[:, 2 * cw:3 * cw] * p_scr[:, 0:cw]
    conv = (conv_w_ref[0:1, :] * z_scr[halo - 2:halo - 2 + tm, :]
            + conv_w_ref[1:2, :] * z_scr[halo - 1:halo - 1 + tm, :]
            + conv_w_ref[2:3, :] * z_scr[halo:halo + tm, :])
    y = p_scr[halo:halo + tm, cw:2 * cw] * conv
    ms = _seg_sum(y * y, seg2) * (1.0 / HEAD_DIM)
    yconv_ref[...] = (y * lax.rsqrt(ms + NORM_EPS) * conv_g_ref[...]).astype(yconv_ref.dtype)

    c0 = 3 * cw
    cur = p_scr[halo:halo + tm, c0:]
    prev = p_scr[halo - 1:halo - 1 + tm, c0:]
    q = cur + (prev - cur) * mu_ref[...]
    r = q[:, 0:rw]
    k = q[:, rw:2 * rw]
    v = q[:, 2 * rw:3 * rw]
    lora_wa = q[:, 3 * rw:3 * rw + V7X_LANES]
    lora_g = q[:, 3 * rw + V7X_LANES:]
    w_lin = w0_ref[...] + jnp.dot(jnp.tanh(lora_wa).astype(BF16), w_up_ref[...],
                                  preferred_element_type=F32)
    neg = -w_lin
    softplus = jnp.maximum(neg, 0.0) + jnp.log(1.0 + jnp.exp(-jnp.abs(neg)))
    log_decay = -jnp.exp(-softplus - 0.5)
    a = _sigmoid(a0_ref[...] + jnp.dot(lora_wa.astype(BF16), a_up_ref[...],
                                       preferred_element_type=F32))
    gate = jnp.dot(_sigmoid(lora_g).astype(BF16), g_up_ref[...], preferred_element_type=F32)
    kk = k * k_k_ref[...]
    kk = kk / jnp.maximum(jnp.sqrt(_seg_sum(kk * kk, seg2)), 1e-12)
    k_mod = k * (1.0 + (a - 1.0) * k_a_ref[...])
    bonus = _seg_sum(r * k_mod * r_k_ref[...], seg2) * v
    r_ref[...] = r
    w_ref[...] = log_decay
    k_ref[...] = k_mod
    v_ref[...] = v
    kk_ref[...] = kk
    b_ref[...] = kk * a
    gate_ref[...] = gate
    bonus_ref[...] = bonus


def _mix_prep(x2d, seq, norm_g, w_in, conv_w, conv_g, mu, w0, w_up, a0, a_up, g_up, k_k, k_a,
              r_k, seg2):
    n, d = x2d.shape
    tm = PREP_ROWS
    halo = V7X_SUBLANES
    cw = conv_w.shape[1]
    rw = w0.shape[1]
    in_cols = w_in.shape[1]
    assert seq % tm == 0 and n % tm == 0
    full = lambda a: pl.BlockSpec(a.shape, lambda i: (0,) * a.ndim)
    row_spec = lambda c: pl.BlockSpec((tm, c), lambda i: (i, 0))
    consts = (norm_g, w_in, conv_w, conv_g, mu, w0, w_up, a0, a_up, g_up, k_k, k_a, r_k, seg2)
    outs = [jax.ShapeDtypeStruct((n, cw), BF16)] + [jax.ShapeDtypeStruct((n, rw), F32)] * 8
    return pl.pallas_call(
        functools.partial(_mix_prep_kernel, seq // tm, cw, rw),
        grid=(n // tm,),
        in_specs=[pl.BlockSpec((halo, d), lambda i: (jnp.maximum(i * (tm // halo) - 1, 0), 0)),
                  row_spec(d)] + [full(c) for c in consts],
        out_specs=[row_spec(cw)] + [row_spec(rw)] * 8,
        out_shape=outs,
        scratch_shapes=[pltpu.VMEM((halo + tm, in_cols), F32), pltpu.VMEM((halo + tm, cw), F32)],
        compiler_params=_params("parallel"),
        name="mix_prep",
    )(x2d, x2d, *consts)


def _bdot(a, b):
    return jnp.dot(a.astype(BF16), b.astype(BF16), preferred_element_type=F32)


def _bdot_nt(a, b):
    return lax.dot_general(a.astype(BF16), b.astype(BF16), (((1,), (1,)), ((), ())),
                           preferred_element_type=F32)


def _wkv_chunk_kernel(n_batch, n_pairs, n_side,
                      r_ref, lw_ref, k_ref, v_ref, kk_ref, b_ref, *refs):
    frames = r_ref.shape[1]
    lanes = V7X_LANES
    hd = HEAD_DIM
    assert frames == hd
    side_in, o_ref, side_out, state_ref = (refs[:n_side], refs[n_side],
                                           refs[n_side + 1:2 * n_side + 1], refs[-1])
    for src, dst in zip(side_in, side_out):
        dst[...] = src[...].astype(dst.dtype)

    @pl.when(pl.program_id(0) == 0)
    def _():
        state_ref[...] = jnp.zeros_like(state_ref)

    row = lax.broadcasted_iota(I32, (lanes, lanes), 0)
    lane = lax.broadcasted_iota(I32, (lanes, lanes), 1)
    same_head = (row // hd) == (lane // hd)
    rt = row % hd
    ls = lane % hd
    strict_same = same_head & (rt > ls)
    strict_cross = jnp.logical_not(same_head) & (rt > ls)
    incl_same = same_head & (rt >= ls)
    incl_cross = jnp.logical_not(same_head) & (rt >= ls)
    eye = row == lane
    level_masks = []
    m = 1
    while m < hd:
        level_masks.append(same_head & ((rt // (2 * m)) == (ls // (2 * m)))
                           & (((rt // m) % 2) == 1) & (((ls // m) % 2) == 0))
        m *= 2
    left = lax.broadcasted_iota(I32, (frames, lanes), 1) < hd
    tri = (lax.broadcasted_iota(I32, (frames, frames), 0)
           >= lax.broadcasted_iota(I32, (frames, frames), 1)).astype(BF16)

    def diag_blocks(x):
        return jnp.concatenate([jnp.where(left, x, 0.0), jnp.where(left, 0.0, x)], axis=0)

    def cross_blocks(x):
        return jnp.concatenate([jnp.where(left, 0.0, x), jnp.where(left, x, 0.0)], axis=0)

    def batch_prep(bi):
        lw = lw_ref[bi]
        hi = lw.astype(BF16)
        rem = lw - hi.astype(F32)
        mid = rem.astype(BF16)
        lo = (rem - mid.astype(F32)).astype(BF16)
        cs = (jnp.dot(tri, hi, preferred_element_type=F32)
              + jnp.dot(tri, mid, preferred_element_type=F32)
              + jnp.dot(tri, lo, preferred_element_type=F32))
        cs_end = cs[frames - 1:frames, :]
        e_neg = jnp.exp(-cs)
        e_end = jnp.exp(cs_end - cs)
        b_in = b_ref[bi]
        k_in = k_ref[bi]
        return dict(a_t=-kk_ref[bi] * jnp.exp(cs - lw), b_t=b_in * e_neg, k_t=k_in * e_neg,
                    r_t=r_ref[bi] * jnp.exp(cs), b_h=b_in * e_end, k_h=k_in * e_end,
                    v=v_ref[bi], g_end=jnp.exp(cs_end))

    group = []
    for bi in range(n_batch):
        prep = batch_prep(bi)
        for pi in range(n_pairs):
            sl = slice(pi * lanes, (pi + 1) * lanes)
            group.append((bi, sl, bi * n_pairs + pi, {n: x[:, sl] for n, x in prep.items()}))
    bf = lambda x: x.astype(BF16)
    a_bd = [bf(diag_blocks(p["a_t"])) for _, _, _, p in group]
    r_bd = [bf(diag_blocks(p["r_t"])) for _, _, _, p in group]
    d0 = [_bdot_nt(jnp.concatenate([a[:hd], r[:hd]], axis=0),
                   bf(jnp.concatenate([p["b_t"], p["k_t"]], axis=0)))
          for a, r, (_, _, _, p) in zip(a_bd, r_bd, group)]
    d1 = [_bdot_nt(jnp.concatenate([a[hd:], r[hd:]], axis=0),
                   bf(jnp.concatenate([p["k_t"], p["b_t"]], axis=0)))
          for a, r, (_, _, _, p) in zip(a_bd, r_bd, group)]
    a_rows = [jnp.concatenate([x[:hd], y[:hd]], axis=0) for x, y in zip(d0, d1)]
    m_rows = [jnp.concatenate([x[hd:], y[hd:]], axis=0) for x, y in zip(d0, d1)]
    a_ab = [bf(jnp.where(strict_same, x, 0.0)) for x in a_rows]
    a_ak = [bf(jnp.where(strict_cross, x, 0.0)) for x in a_rows]
    m_rb = [bf(jnp.where(incl_same, x, 0.0)) for x in m_rows]
    m_rk = [bf(jnp.where(incl_cross, x, 0.0)) for x in m_rows]
    t_inv = [jnp.where(eye, 1.0, jnp.where(level_masks[0], x, 0.0)) for x in a_rows]
    for mask in level_masks[1:]:
        t_bf = [bf(t) for t in t_inv]
        half = [jnp.where(mask, _bdot(t, x), 0.0) for t, x in zip(t_bf, a_ab)]
        t_inv = [t + _bdot(h, tb) for t, h, tb in zip(t_inv, half, t_bf)]
    t_bf = [bf(t) for t in t_inv]
    v_x = [bf(cross_blocks(p["v"])) for _, _, _, p in group]
    akv = [_bdot(x, v) for x, v in zip(a_ak, v_x)]
    wu = [bf(_bdot(t, jnp.concatenate([a, bf(x)], axis=1)))
          for t, a, x in zip(t_bf, a_bd, akv)]
    bh_t = [bf(diag_blocks(p["b_h"]).T) for _, _, _, p in group]
    kh_t = [bf(cross_blocks(p["k_h"]).T) for _, _, _, p in group]
    pw_rw = [_bdot(jnp.concatenate([bt, m], axis=0), x[:, :lanes])
             for bt, m, x in zip(bh_t, m_rb, wu)]
    q_o = [_bdot(jnp.concatenate([jnp.concatenate([bt, kt], axis=1),
                                  jnp.concatenate([mb, mk], axis=1)], axis=0),
                 jnp.concatenate([x[:, lanes:], v], axis=0))
           for bt, kt, mb, mk, x, v in zip(bh_t, kh_t, m_rb, m_rk, wu, v_x)]
    res = [_bdot(jnp.concatenate([diag_blocks(p["r_t"]) + pr[lanes:],
                                  pr[:lanes] + jnp.where(eye, p["g_end"], 0.0)], axis=0),
                 state_ref[idx])
           for pr, (_, _, idx, p) in zip(pw_rw, group)]
    for x, q, (bi, sl, idx, _) in zip(res, q_o, group):
        o_bd = x[:lanes] + q[lanes:]
        state_ref[idx] = x[lanes:] + q[:lanes]
        o_ref[bi, :, sl] = jnp.where(left, o_bd[:hd], o_bd[hd:])


def _wkv_chunk(r, lw, k, v, kk, b, side_f32=()):
    n_batch, seq, rw = r.shape
    n_pairs = rw // V7X_LANES
    frames = SCAN_FRAMES
    assert seq % frames == 0
    n_steps = seq // frames
    spec = pl.BlockSpec((n_batch, frames, rw), lambda c: (0, c, 0))
    sliced = [a.reshape(n_steps, -1, a.shape[-1]) for a in side_f32]
    side_specs = [pl.BlockSpec((1,) + a.shape[1:], lambda c: (c, 0, 0)) for a in sliced]
    outs = pl.pallas_call(
        functools.partial(_wkv_chunk_kernel, n_batch, n_pairs, len(sliced)),
        grid=(n_steps,),
        in_specs=[spec] * 6 + side_specs,
        out_specs=[spec] + side_specs,
        out_shape=[jax.ShapeDtypeStruct((n_batch, seq, rw), F32)]
                  + [jax.ShapeDtypeStruct(a.shape, BF16) for a in sliced],
        scratch_shapes=[pltpu.VMEM((n_batch * n_pairs, V7X_LANES, V7X_LANES), F32)],
        compiler_params=_params("arbitrary"),
        name="wkv_chunk",
    )(r, lw, k, v, kk, b, *sliced)
    return outs[0], [y.reshape(a.shape) for y, a in zip(outs[1:], side_f32)]


def _post_mix_kernel(n_experts,
                     o_ref, bonus_ref, gate_ref, yconv_ref, x_ref, gn_w_ref, gn_b_ref,
                     w_out_c_ref, w_out_r_ref, ffn_g_ref, wr_hi_ref, wr_lo_ref, b_router_ref,
                     seg2_ref,
                     x1_ref, *outs):
    h2_refs, (top_e_ref, gates_ref) = outs[:-2], outs[-2:]
    seg2 = seg2_ref[...]
    o = o_ref[...]
    mean = _seg_sum(o, seg2) * (1.0 / HEAD_DIM)
    cen = o - mean
    var = _seg_sum(cen * cen, seg2) * (1.0 / HEAD_DIM)
    o = cen * lax.rsqrt(var + GN_EPS) * gn_w_ref[...] + gn_b_ref[...]
    y_rwkv = (o + bonus_ref[...]) * gate_ref[...]
    x1 = (x_ref[...]
          + jnp.dot(yconv_ref[...], w_out_c_ref[...], preferred_element_type=F32)
          + jnp.dot(y_rwkv.astype(BF16), w_out_r_ref[...], preferred_element_type=F32))
    x1_ref[...] = x1
    h2 = _rms_rows(x1, ffn_g_ref[...])
    words = _pack_bf16_pairs(h2)
    for c, h2_ref in enumerate(h2_refs):
        h2_ref[...] = words[:, c * ROW_CHUNK:(c + 1) * ROW_CHUNK]

    h_hi, h_lo = _split_bf16(h2)
    nt = (((1,), (1,)), ((), ()))
    logits = (lax.dot_general(wr_hi_ref[...], h_hi, nt, preferred_element_type=F32)
              + lax.dot_general(wr_hi_ref[...], h_lo, nt, preferred_element_type=F32)
              + lax.dot_general(wr_lo_ref[...], h_hi, nt, preferred_element_type=F32)
              + b_router_ref[...])
    e_id = lax.broadcasted_iota(I32, logits.shape, 0).astype(F32)
    work = logits
    tops, ids = [], []
    for _ in range(TOP_K):
        m = jnp.max(work, axis=0, keepdims=True)
        sel = jnp.min(jnp.where(work == m, e_id, float(n_experts)), axis=0, keepdims=True)
        tops.append(m)
        ids.append(sel)
        work = jnp.where(e_id == sel, -jnp.inf, work)
    ex = [jnp.exp(t - tops[0]) for t in tops]
    denom = ex[0] + ex[1] + ex[2] + ex[3]
    top_e_ref[...] = jnp.concatenate(ids, axis=0).astype(I32)
    gates_ref[...] = jnp.concatenate([e / denom for e in ex], axis=0)


def _post_mix(token0, n, o, bonus, gate, yconv, x2d, gn_w, gn_b, w_out_c, w_out_r, ffn_g, wr_hi,
              wr_lo, b_router, seg2):
    d = x2d.shape[1]
    rw = o.shape[1]
    cw = yconv.shape[1]
    n_experts = wr_hi.shape[0]
    tm = POST_ROWS
    assert token0 % tm == 0 and n % tm == 0
    full = lambda a: pl.BlockSpec(a.shape, lambda i: (0,) * a.ndim)
    in_spec = lambda c: pl.BlockSpec((tm, c), lambda i: (i + token0 // tm, 0))
    row_spec = lambda c: pl.BlockSpec((tm, c), lambda i: (i, 0))
    col_spec = pl.BlockSpec((TOP_K, tm), lambda i: (0, i))
    consts = (gn_w, gn_b, w_out_c, w_out_r, ffn_g, wr_hi, wr_lo, b_router, seg2)
    n_chunks = d // 2 // ROW_CHUNK
    outs = pl.pallas_call(
        functools.partial(_post_mix_kernel, n_experts),
        grid=(n // tm,),
        in_specs=[in_spec(rw), in_spec(rw), in_spec(rw), in_spec(cw), in_spec(d)]
                 + [full(c) for c in consts],
        out_specs=[row_spec(d)] + [row_spec(ROW_CHUNK)] * n_chunks + [col_spec, col_spec],
        out_shape=[jax.ShapeDtypeStruct((n, d), F32)]
                  + [jax.ShapeDtypeStruct((n, ROW_CHUNK), jnp.uint32)] * n_chunks
                  + [jax.ShapeDtypeStruct((TOP_K, n), I32), jax.ShapeDtypeStruct((TOP_K, n), F32)],
        compiler_params=_params("parallel"),
        name="post_mix",
    )(o, bonus, gate, yconv, x2d, *consts)
    return outs[0], outs[1:1 + n_chunks], outs[-2], outs[-1]


def _route_kernel(n_experts, block_rows,
                  top_e_ref, tri_ref, dest_ref, meta_ref,
                  count_ref, start_ref, carry_ref):
    phase = pl.program_id(0)
    j = pl.program_id(1)
    tb = top_e_ref.shape[1]
    e_id = lax.broadcasted_iota(I32, (n_experts, tb), 0)
    top_e = top_e_ref[...]
    onehot = jnp.zeros((n_experts, tb), F32)
    for c in range(TOP_K):
        onehot = onehot + jnp.where(top_e[c:c + 1, :] == e_id, 1.0, 0.0)
    block_count = jnp.sum(onehot, axis=1, keepdims=True)

    @pl.when((phase == 0) & (j == 0))
    def _():
        count_ref[...] = jnp.zeros_like(count_ref)

    @pl.when(phase == 0)
    def _():
        count_ref[...] += block_count

    @pl.when((phase == 1) & (j == 0))
    def _():
        counts = count_ref[...]
        padded = jnp.ceil(counts * (1.0 / block_rows)) * block_rows
        sub = lax.broadcasted_iota(I32, (n_experts, n_experts), 0)
        lane = lax.broadcasted_iota(I32, (n_experts, n_experts), 1)
        padded_row = jnp.sum(jnp.where(sub == lane, padded, 0.0), axis=0, keepdims=True)
        start = jnp.sum(jnp.where(lane < sub, padded_row, 0.0), axis=1, keepdims=True)
        start_ref[...] = start
        carry_ref[...] = jnp.zeros_like(carry_ref)
        end = start + padded
        nb = meta_ref.shape[1]
        slot0 = (lax.broadcasted_iota(I32, (n_experts, nb), 1) * block_rows).astype(F32)
        block_e = jnp.sum(jnp.where(end <= slot0, 1.0, 0.0), axis=0, keepdims=True)
        block_e = jnp.minimum(block_e, n_experts - 1.0)
        used = jnp.max(end, axis=0, keepdims=True) * (1.0 / block_rows)
        in_region = (start <= slot0) & (slot0 < end)
        valid = jnp.sum(jnp.where(in_region, jnp.minimum(start + counts - slot0, block_rows), 0.0),
                        axis=0, keepdims=True)
        row = lax.broadcasted_iota(I32, meta_ref.shape, 0)
        meta_ref[...] = jnp.where(row == META_BLOCK_EXPERT, block_e,
                                  jnp.where(row == META_VALID_ROWS, valid, used)).astype(I32)

    @pl.when(phase == 1)
    def _():
        incl = jnp.dot(onehot.astype(BF16), tri_ref[...], preferred_element_type=F32)
        base = incl - onehot + carry_ref[...] + start_ref[...]
        rows = [jnp.sum(jnp.where(top_e[c:c + 1, :] == e_id, base, 0.0), axis=0, keepdims=True)
                for c in range(TOP_K)]
        dest_ref[...] = jnp.concatenate(rows, axis=0).astype(I32)
        carry_ref[...] += block_count


def _route(top_e, n_experts, n_blocks_padded):
    n = top_e.shape[1]
    tb = ROUTE_TOKENS
    assert n % tb == 0
    tri = (lax.broadcasted_iota(I32, (tb, tb), 0) <= lax.broadcasted_iota(I32, (tb, tb), 1)
           ).astype(BF16)
    return pl.pallas_call(
        functools.partial(_route_kernel, n_experts, EXPERT_ROWS),
        grid=(2, n // tb),
        in_specs=[pl.BlockSpec((TOP_K, tb), lambda ph, j: (0, j)),
                  pl.BlockSpec((tb, tb), lambda ph, j: (0, 0))],
        out_specs=[pl.BlockSpec((TOP_K, tb), lambda ph, j: (0, j * ph)),
                   pl.BlockSpec((V7X_SUBLANES, n_blocks_padded), lambda ph, j: (0, 0))],
        out_shape=[jax.ShapeDtypeStruct((TOP_K, n), I32),
                   jax.ShapeDtypeStruct((V7X_SUBLANES, n_blocks_padded), I32)],
        scratch_shapes=[pltpu.VMEM((n_experts, 1), F32)] * 3,
        compiler_params=_params("arbitrary", "arbitrary"),
        name="route",
    )(top_e, tri)


def _sc_mesh():
    return plsc.VectorSubcoreMesh(core_axis_name="core", subcore_axis_name="subcore")


def _sc_scatter_rows(src, dest, n_slots):
    n, width = src.shape
    assert n % SC_WINDOW == 0

    @functools.partial(pl.kernel, out_type=jax.ShapeDtypeStruct((n_slots, width), src.dtype),
                       mesh=_sc_mesh(), name="sc_scatter_rows")
    def scatter(src_hbm, dest_hbm, out_hbm):
        def body(src_vmem, dest_vmem):
            for c in range(TOP_K):
                pltpu.sync_copy(src_vmem, out_hbm.at[dest_vmem.at[c]])

        pltpu.emit_pipeline(
            body,
            grid=(n // SC_WINDOW,),
            in_specs=[pl.BlockSpec((SC_WINDOW, width), lambda i: (i, 0)),
                      pl.BlockSpec((TOP_K, SC_WINDOW), lambda i: (0, i))],
            out_specs=[],
            core_axis_name=("core", "subcore"),
            dimension_semantics=(pltpu.PARALLEL,),
        )(src_hbm, dest_hbm)

    return scatter(src, dest)


def _experts_kernel(d_ff, n_in, n_out, meta_ref, *refs):
    x_refs, refs = refs[:n_in], refs[n_in:]
    w_gu_mxu, b_gu_ref, w_down_mxu, b_down_ref = refs[:4]
    y_refs = refs[4:]
    j = pl.program_id(0)
    used = meta_ref[META_USED_BLOCKS, 0]

    @pl.when(j < used)
    def _():
        x = _unpack_bf16_pairs(jnp.concatenate([r[...] for r in x_refs], axis=1))
        row = lax.broadcasted_iota(I32, x.shape, 0)
        x = jnp.where(row < meta_ref[META_VALID_ROWS, j], x, 0.0)
        gu = jnp.dot(x.astype(BF16), w_gu_mxu[0], preferred_element_type=F32) + b_gu_ref[0]
        gate = jnp.minimum(gu[:, :d_ff], SWIGLU_LIMIT)
        up = jnp.clip(gu[:, d_ff:], -SWIGLU_LIMIT, SWIGLU_LIMIT)
        act = (up + 1.0) * (gate * _sigmoid(SWIGLU_ALPHA * gate))
        y = jnp.dot(act.astype(BF16), w_down_mxu[0], preferred_element_type=F32) + b_down_ref[0]
        words = _pack_bf16_pairs(y)
        for c, y_ref in enumerate(y_refs):
            y_ref[...] = words[:, c * ROW_CHUNK:(c + 1) * ROW_CHUNK]

    @pl.when(j >= used)
    def _():
        for y_ref in y_refs:
            y_ref[...] = jnp.zeros_like(y_ref)


def _experts(meta, slot_chunks, w_gu, b_gu, w_down, b_down):
    assert w_gu.dtype == BF16 and w_down.dtype == BF16
    n_in = len(slot_chunks)
    n_slots = slot_chunks[0].shape[0]
    n_experts, d, two_ff = w_gu.shape
    n_out = d // 2 // ROW_CHUNK
    d_ff = two_ff // 2
    bm = EXPERT_ROWS
    by_expert = lambda j, m: (m[META_BLOCK_EXPERT, j], 0, 0)
    grid_spec = pltpu.PrefetchScalarGridSpec(
        num_scalar_prefetch=1,
        grid=(n_slots // bm,),
        in_specs=[pl.BlockSpec((bm, ROW_CHUNK),
                               lambda j, m: (jnp.minimum(j, m[META_USED_BLOCKS, 0] - 1), 0))
                  ] * n_in
                 + [pl.BlockSpec((1, d, two_ff), by_expert), pl.BlockSpec((1, 1, two_ff), by_expert),
                    pl.BlockSpec((1, d_ff, d), by_expert), pl.BlockSpec((1, 1, d), by_expert)],
        out_specs=[pl.BlockSpec((bm, ROW_CHUNK), lambda j, m: (j, 0))] * n_out,
    )
    return pl.pallas_call(
        functools.partial(_experts_kernel, d_ff, n_in, n_out),
        grid_spec=grid_spec,
        out_shape=[jax.ShapeDtypeStruct((n_slots, ROW_CHUNK), jnp.uint32)] * n_out,
        compiler_params=_params("arbitrary"),
        name="experts",
    )(meta, *slot_chunks, w_gu, b_gu, w_down, b_down)


def _sc_gather_rows(table, idx_row):
    n_idx = idx_row.shape[1]
    width = table.shape[1]
    assert n_idx % SC_WINDOW == 0

    @functools.partial(pl.kernel, out_type=jax.ShapeDtypeStruct((n_idx, width), table.dtype),
                       mesh=_sc_mesh(), name="sc_gather_rows")
    def gather(table_hbm, idx_hbm, out_hbm):
        def body(idx_vmem, out_vmem):
            pltpu.sync_copy(table_hbm.at[idx_vmem.at[0]], out_vmem)

        pltpu.emit_pipeline(
            body,
            grid=(n_idx // SC_WINDOW,),
            in_specs=[pl.BlockSpec((1, SC_WINDOW), lambda i: (0, i))],
            out_specs=[pl.BlockSpec((SC_WINDOW, width), lambda i: (i, 0))],
            core_axis_name=("core", "subcore"),
            dimension_semantics=(pltpu.PARALLEL,),
        )(idx_hbm, out_hbm)

    return gather(table, idx_row)


def _reduce_kernel(n_chunks, *refs):
    y_refs = refs[:n_chunks]
    x1_ref, gates_ref, g_ref = refs[n_chunks:n_chunks + 3]
    out_ref = refs[-1]
    gates = gates_ref[...]
    acc = x1_ref[...]
    for c in range(TOP_K):
        y = _unpack_bf16_pairs(jnp.concatenate([r[c] for r in y_refs], axis=1))
        acc = acc + y * gates[:, c:c + 1]
    out_ref[...] = _rms_rows(acc, g_ref[...])


def _reduce(token0, n_total, out_so_far, y_chunks, x1, gates_t, final_g):
    n, d = x1.shape
    tb = REDUCE_TOKENS
    n_chunks = len(y_chunks)
    assert token0 % tb == 0
    in_specs = ([pl.BlockSpec((TOP_K, tb, ROW_CHUNK), lambda i: (0, i, 0))] * n_chunks
                + [pl.BlockSpec((tb, d), lambda i: (i, 0)),
                   pl.BlockSpec((tb, TOP_K), lambda i: (i, 0)),
                   pl.BlockSpec((1, d), lambda i: (0, 0))])
    args = [y.reshape(TOP_K, n, ROW_CHUNK) for y in y_chunks] + [x1, gates_t, final_g]
    aliases = {}
    if out_so_far is not None:
        in_specs.append(pl.BlockSpec(memory_space=pl.ANY))
        args.append(out_so_far)
        aliases = {len(args) - 1: 0}
    return pl.pallas_call(
        functools.partial(_reduce_kernel, n_chunks),
        grid=(n // tb,),
        in_specs=in_specs,
        out_specs=pl.BlockSpec((tb, d), lambda i: (i + token0 // tb, 0)),
        out_shape=jax.ShapeDtypeStruct((n_total, d), F32),
        input_output_aliases=aliases,
        compiler_params=_params("parallel"),
        name="reduce",
    )(*args)


def _group_matrix(width):
    a = lax.broadcasted_iota(I32, (width, width), 0) // HEAD_DIM
    b = lax.broadcasted_iota(I32, (width, width), 1) // HEAD_DIM
    g = (a == b).astype(BF16)
    return jnp.concatenate([g, g], axis=0)


def _row(vec):
    return vec.reshape(1, -1).astype(F32)


def kernel(x, w_in, conv_w, conv_norm_g, rwkv_mu, w0, w_up, a0, a_up, g_up, k_k, k_a, r_k,
           gn_w, gn_b, w_out, norm_mix_g, norm_ffn_g, w_router, b_router, w_gu, b_gu, w_down,
           b_down, norm_final_g):
    n_batch, seq, d = x.shape
    n = n_batch * seq
    depth = w_in.shape[0]
    cw = conv_w.shape[2]
    rw = w0.shape[1]
    n_experts = w_router.shape[2]
    decay_lora = w_up.shape[1]
    aaa_lora = a_up.shape[1]
    assert decay_lora + aaa_lora == V7X_LANES
    assert depth == 1
    assert n % MOE_PARTS == 0
    n_part = n // MOE_PARTS
    n_slots = n_part * TOP_K + n_experts * EXPERT_ROWS
    n_blocks = n_slots // EXPERT_ROWS
    n_blocks_padded = -(-n_blocks // V7X_LANES) * V7X_LANES
    seg2 = _group_matrix(rw)

    x2d = x.reshape(n, d)
    for l in range(depth):
        w_up_pad = jnp.concatenate([w_up[l], jnp.zeros((aaa_lora, rw), F32)], axis=0).astype(BF16)
        a_up_pad = jnp.concatenate([jnp.zeros((decay_lora, rw), F32), a_up[l]], axis=0).astype(BF16)
        (yconv, r, lw, k, v, kk, b, gate, bonus) = _mix_prep(
            x2d, seq, _row(norm_mix_g[l]), w_in[l].astype(BF16), conv_w[l].astype(F32),
            _row(conv_norm_g[l]), _row(rwkv_mu[l]), _row(w0[l]), w_up_pad, _row(a0[l]), a_up_pad,
            g_up[l].astype(BF16), _row(k_k[l]), _row(k_a[l]), _row(r_k[l]), seg2)
        shape3 = (n_batch, seq, rw)
        o, (w_gu_b, w_down_b) = _wkv_chunk(
            r.reshape(shape3), lw.reshape(shape3), k.reshape(shape3), v.reshape(shape3),
            kk.reshape(shape3), b.reshape(shape3),
            side_f32=(w_gu[l].astype(F32), w_down[l].astype(F32)))
        o = o.reshape(n, rw)
        w_out_b = w_out[l].astype(BF16)
        wr_t = w_router[l].T.astype(F32)
        wr_hi = wr_t.astype(BF16)
        wr_lo = (wr_t - wr_hi.astype(F32)).astype(BF16)
        expert_args = (w_gu_b, b_gu[l].reshape(n_experts, 1, -1).astype(F32),
                       w_down_b, b_down[l].reshape(n_experts, 1, -1).astype(F32))
        parts = []
        for token0 in range(0, n, n_part):
            x1, h2_chunks, top_e, gates = _post_mix(
                token0, n_part, o, bonus, gate, yconv, x2d, _row(gn_w[l]), _row(gn_b[l]),
                w_out_b[:cw], w_out_b[cw:], _row(norm_ffn_g[l]), wr_hi, wr_lo,
                b_router[l].reshape(n_experts, 1).astype(F32), seg2)
            dest, meta = _route(top_e, n_experts, n_blocks_padded)
            slot_chunks = [_sc_scatter_rows(h, dest, n_slots) for h in h2_chunks]
            parts.append((token0, x1, gates, dest, meta, slot_chunks))
        out = None
        for token0, x1, gates, dest, meta, slot_chunks in parts:
            y_chunks = _experts(meta, slot_chunks, *expert_args)
            idx_row = dest.reshape(1, TOP_K * n_part)
            out = _reduce(token0, n, out, [_sc_gather_rows(y, idx_row) for y in y_chunks], x1,
                          gates.T, _row(norm_final_g))
        x2d = out
    return x2d.reshape(n_batch, seq, d)
```

```python
import functools

import jax
import jax.numpy as jnp
from jax import lax
from jax.experimental import pallas as pl
from jax.experimental.pallas import tpu as pltpu
from jax.experimental.pallas import tpu_sc as plsc

F32 = jnp.float32
BF16 = jnp.bfloat16
I32 = jnp.int32

HEAD_DIM = 64
TOP_K = 4
NORM_EPS = 1e-5
GN_EPS = HEAD_DIM * 1e-5
SWIGLU_LIMIT = 7.0
SWIGLU_ALPHA = 1.702

V7X_LANES = 128
V7X_SUBLANES = 8
V7X_VMEM_LIMIT_BYTES = 56 * 1024 * 1024

PREP_ROWS = 512
POST_ROWS = 1024
SCAN_FRAMES = 64
ROUTE_TOKENS = 1024
REDUCE_TOKENS = 1024
MOE_PARTS = 2
SC_WINDOW = 128
ROW_CHUNK = 256
META_BLOCK_EXPERT = 0
META_USED_BLOCKS = 1
META_VALID_ROWS = 2
EXPERT_ROWS = 256
EXPERT_STEP_BLOCKS = 2

def _params(*semantics):
    return pltpu.CompilerParams(dimension_semantics=semantics,
                                vmem_limit_bytes=V7X_VMEM_LIMIT_BYTES)


def _split_bf16(x):
    hi = x.astype(BF16)
    lo = (x - hi.astype(F32)).astype(BF16)
    return hi, lo


def _seg_sum(x, seg2):
    hi, lo = _split_bf16(x)
    return jnp.dot(jnp.concatenate([hi, lo], axis=1), seg2, preferred_element_type=F32)


def _pack_bf16_pairs(x):
    bits = pltpu.bitcast(x.astype(BF16).astype(F32), jnp.uint32)
    half = x.shape[1] // 2
    return bits[:, half:] | (bits[:, :half] >> 16)


def _unpack_bf16_pairs(words):
    return jnp.concatenate([pltpu.bitcast(words << 16, F32),
                            pltpu.bitcast(words & jnp.uint32(0xFFFF0000), F32)], axis=1)


def _rms_rows(x, g):
    return x * lax.rsqrt(jnp.mean(x * x, axis=-1, keepdims=True) + NORM_EPS) * g


def _sigmoid(x):
    return 1.0 / (1.0 + jnp.exp(-x))


def _mix_prep_kernel(blocks_per_seq, cw, rw,
                     xprev_ref, x_ref, g_ref, w_in_ref, conv_w_ref, conv_g_ref, mu_ref,
                     w0_ref, w_up_ref, a0_ref, a_up_ref, g_up_ref, k_k_ref, k_a_ref, r_k_ref,
                     seg2_ref,
                     yconv_ref, r_ref, w_ref, k_ref, v_ref, kk_ref, b_ref, gate_ref, bonus_ref,
                     p_scr, z_scr):
    tm = x_ref.shape[0]
    halo = xprev_ref.shape[0]
    first = (pl.program_id(0) % blocks_per_seq) == 0
    xp = xprev_ref[...] * jnp.where(first, 0.0, 1.0)
    xa = jnp.concatenate([xp, x_ref[...]], axis=0)
    h = _rms_rows(xa, g_ref[...])
    p_scr[...] = jnp.dot(h.astype(BF16), w_in_ref[...], preferred_element_type=F32)
    seg2 = seg2_ref[...]

    z_scr[...] = p_scr[:, 2 * cw:3 * cw] * p_scr[:, 0:cw]
    conv = (conv_w_ref[0:1, :] * z_scr[halo - 2:halo - 2 + tm, :]
            + conv_w_ref[1:2, :] * z_scr[halo - 1:halo - 1 + tm, :]
            + conv_w_ref[2:3, :] * z_scr[halo:halo + tm, :])
    y = p_scr[halo:halo + tm, cw:2 * cw] * conv
    ms = _seg_sum(y * y, seg2) * (1.0 / HEAD_DIM)
    yconv_ref[...] = (y * lax.rsqrt(ms + NORM_EPS) * conv_g_ref[...]).astype(yconv_ref.dtype)

    c0 = 3 * cw
    cur = p_scr[halo:halo + tm, c0:]
    prev = p_scr[halo - 1:halo - 1 + tm, c0:]
    q = cur + (prev - cur) * mu_ref[...]
    r = q[:, 0:rw]
    k = q[:, rw:2 * rw]
    v = q[:, 2 * rw:3 * rw]
    lora_wa = q[:, 3 * rw:3 * rw + V7X_LANES]
    lora_g = q[:, 3 * rw + V7X_LANES:]
    w_lin = w0_ref[...] + jnp.dot(jnp.tanh(lora_wa).astype(BF16), w_up_ref[...],
                                  preferred_element_type=F32)
    neg = -w_lin
    softplus = jnp.maximum(neg, 0.0) + jnp.log(1.0 + jnp.exp(-jnp.abs(neg)))
    log_decay = -jnp.exp(-softplus - 0.5)
    a = _sigmoid(a0_ref[...] + jnp.dot(lora_wa.astype(BF16), a_up_ref[...],
                                       preferred_element_type=F32))
    gate = jnp.dot(_sigmoid(lora_g).astype(BF16), g_up_ref[...], preferred_element_type=F32)
    kk = k * k_k_ref[...]
    kk = kk / jnp.maximum(jnp.sqrt(_seg_sum(kk * kk, seg2)), 1e-12)
    k_mod = k * (1.0 + (a - 1.0) * k_a_ref[...])
    bonus = _seg_sum(r * k_mod * r_k_ref[...], seg2) * v
    r_ref[...] = r
    w_ref[...] = log_decay
    k_ref[...] = k_mod
    v_ref[...] = v
    kk_ref[...] = kk
    b_ref[...] = kk * a
    gate_ref[...] = gate
    bonus_ref[...] = bonus


def _mix_prep(x2d, seq, norm_g, w_in, conv_w, conv_g, mu, w0, w_up, a0, a_up, g_up, k_k, k_a,
              r_k, seg2):
    n, d = x2d.shape
    tm = PREP_ROWS
    halo = V7X_SUBLANES
    cw = conv_w.shape[1]
    rw = w0.shape[1]
    in_cols = w_in.shape[1]
    assert seq % tm == 0 and n % tm == 0
    full = lambda a: pl.BlockSpec(a.shape, lambda i: (0,) * a.ndim)
    row_spec = lambda c: pl.BlockSpec((tm, c), lambda i: (i, 0))
    consts = (norm_g, w_in, conv_w, conv_g, mu, w0, w_up, a0, a_up, g_up, k_k, k_a, r_k, seg2)
    outs = [jax.ShapeDtypeStruct((n, cw), BF16)] + [jax.ShapeDtypeStruct((n, rw), F32)] * 8
    return pl.pallas_call(
        functools.partial(_mix_prep_kernel, seq // tm, cw, rw),
        grid=(n // tm,),
        in_specs=[pl.BlockSpec((halo, d), lambda i: (jnp.maximum(i * (tm // halo) - 1, 0), 0)),
                  row_spec(d)] + [full(c) for c in consts],
        out_specs=[row_spec(cw)] + [row_spec(rw)] * 8,
        out_shape=outs,
        scratch_shapes=[pltpu.VMEM((halo + tm, in_cols), F32), pltpu.VMEM((halo + tm, cw), F32)],
        compiler_params=_params("parallel"),
        name="mix_prep",
    )(x2d, x2d, *consts)


def _bdot(a, b):
    return jnp.dot(a.astype(BF16), b.astype(BF16), preferred_element_type=F32)


def _bdot_nt(a, b):
    return lax.dot_general(a.astype(BF16), b.astype(BF16), (((1,), (1,)), ((), ())),
                           preferred_element_type=F32)


def _wkv_chunk_kernel(n_batch, n_pairs, n_side,
                      r_ref, lw_ref, k_ref, v_ref, kk_ref, b_ref, *refs):
    frames = r_ref.shape[1]
    lanes = V7X_LANES
    hd = HEAD_DIM
    assert frames == hd
    side_in, o_ref, side_out, state_ref = (refs[:n_side], refs[n_side],
                                           refs[n_side + 1:2 * n_side + 1], refs[-1])
    for src, dst in zip(side_in, side_out):
        dst[...] = src[...].astype(dst.dtype)

    @pl.when(pl.program_id(0) == 0)
    def _():
        state_ref[...] = jnp.zeros_like(state_ref)

    row = lax.broadcasted_iota(I32, (lanes, lanes), 0)
    lane = lax.broadcasted_iota(I32, (lanes, lanes), 1)
    same_head = (row // hd) == (lane // hd)
    rt = row % hd
    ls = lane % hd
    strict_same = same_head & (rt > ls)
    strict_cross = jnp.logical_not(same_head) & (rt > ls)
    incl_same = same_head & (rt >= ls)
    incl_cross = jnp.logical_not(same_head) & (rt >= ls)
    eye = row == lane
    level_masks = []
    m = 1
    while m < hd:
        level_masks.append(same_head & ((rt // (2 * m)) == (ls // (2 * m)))
                           & (((rt // m) % 2) == 1) & (((ls // m) % 2) == 0))
        m *= 2
    left = lax.broadcasted_iota(I32, (frames, lanes), 1) < hd
    tri = (lax.broadcasted_iota(I32, (frames, frames), 0)
           >= lax.broadcasted_iota(I32, (frames, frames), 1)).astype(BF16)

    def diag_blocks(x):
        return jnp.concatenate([jnp.where(left, x, 0.0), jnp.where(left, 0.0, x)], axis=0)

    def cross_blocks(x):
        return jnp.concatenate([jnp.where(left, 0.0, x), jnp.where(left, x, 0.0)], axis=0)

    def batch_prep(bi):
        lw = lw_ref[bi]
        hi = lw.astype(BF16)
        rem = lw - hi.astype(F32)
        mid = rem.astype(BF16)
        lo = (rem - mid.astype(F32)).astype(BF16)
        cs = (jnp.dot(tri, hi, preferred_element_type=F32)
              + jnp.dot(tri, mid, preferred_element_type=F32)
              + jnp.dot(tri, lo, preferred_element_type=F32))
        cs_end = cs[frames - 1:frames, :]
        e_neg = jnp.exp(-cs)
        e_end = jnp.exp(cs_end - cs)
        b_in = b_ref[bi]
        k_in = k_ref[bi]
        return dict(a_t=-kk_ref[bi] * jnp.exp(cs - lw), b_t=b_in * e_neg, k_t=k_in * e_neg,
                    r_t=r_ref[bi] * jnp.exp(cs), b_h=b_in * e_end, k_h=k_in * e_end,
                    v=v_ref[bi], g_end=jnp.exp(cs_end))

    group = []
    for bi in range(n_batch):
        prep = batch_prep(bi)
        for pi in range(n_pairs):
            sl = slice(pi * lanes, (pi + 1) * lanes)
            group.append((bi, sl, bi * n_pairs + pi, {n: x[:, sl] for n, x in prep.items()}))
    bf = lambda x: x.astype(BF16)
    a_bd = [bf(diag_blocks(p["a_t"])) for _, _, _, p in group]
    r_bd = [bf(diag_blocks(p["r_t"])) for _, _, _, p in group]
    d0 = [_bdot_nt(jnp.concatenate([a[:hd], r[:hd]], axis=0),
                   bf(jnp.concatenate([p["b_t"], p["k_t"]], axis=0)))
          for a, r, (_, _, _, p) in zip(a_bd, r_bd, group)]
    d1 = [_bdot_nt(jnp.concatenate([a[hd:], r[hd:]], axis=0),
                   bf(jnp.concatenate([p["k_t"], p["b_t"]], axis=0)))
          for a, r, (_, _, _, p) in zip(a_bd, r_bd, group)]
    a_rows = [jnp.concatenate([x[:hd], y[:hd]], axis=0) for x, y in zip(d0, d1)]
    m_rows = [jnp.concatenate([x[hd:], y[hd:]], axis=0) for x, y in zip(d0, d1)]
    a_ab = [bf(jnp.where(strict_same, x, 0.0)) for x in a_rows]
    a_ak = [bf(jnp.where(strict_cross, x, 0.0)) for x in a_rows]
    m_rb = [bf(jnp.where(incl_same, x, 0.0)) for x in m_rows]
    m_rk = [bf(jnp.where(incl_cross, x, 0.0)) for x in m_rows]
    t_inv = [jnp.where(eye, 1.0, jnp.where(level_masks[0], x, 0.0)) for x in a_rows]
    for mask in level_masks[1:]:
        t_bf = [bf(t) for t in t_inv]
        half = [jnp.where(mask, _bdot(t, x), 0.0) for t, x in zip(t_bf, a_ab)]
        t_inv = [t + _bdot(h, tb) for t, h, tb in zip(t_inv, half, t_bf)]
    t_bf = [bf(t) for t in t_inv]
    v_x = [bf(cross_blocks(p["v"])) for _, _, _, p in group]
    akv = [_bdot(x, v) for x, v in zip(a_ak, v_x)]
    wu = [bf(_bdot(t, jnp.concatenate([a, bf(x)], axis=1)))
          for t, a, x in zip(t_bf, a_bd, akv)]
    bh_t = [bf(diag_blocks(p["b_h"]).T) for _, _, _, p in group]
    kh_t = [bf(cross_blocks(p["k_h"]).T) for _, _, _, p in group]
    pw_rw = [_bdot(jnp.concatenate([bt, m], axis=0), x[:, :lanes])
             for bt, m, x in zip(bh_t, m_rb, wu)]
    q_o = [_bdot(jnp.concatenate([jnp.concatenate([bt, kt], axis=1),
                                  jnp.concatenate([mb, mk], axis=1)], axis=0),
                 jnp.concatenate([x[:, lanes:], v], axis=0))
           for bt, kt, mb, mk, x, v in zip(bh_t, kh_t, m_rb, m_rk, wu, v_x)]
    res = [_bdot(jnp.concatenate([diag_blocks(p["r_t"]) + pr[lanes:],
                                  pr[:lanes] + jnp.where(eye, p["g_end"], 0.0)], axis=0),
                 state_ref[idx])
           for pr, (_, _, idx, p) in zip(pw_rw, group)]
    for x, q, (bi, sl, idx, _) in zip(res, q_o, group):
        o_bd = x[:lanes] + q[lanes:]
        state_ref[idx] = x[lanes:] + q[:lanes]
        o_ref[bi, :, sl] = jnp.where(left, o_bd[:hd], o_bd[hd:])


def _wkv_chunk(r, lw, k, v, kk, b, side_f32=()):
    n_batch, seq, rw = r.shape
    n_pairs = rw // V7X_LANES
    frames = SCAN_FRAMES
    assert seq % frames == 0
    n_steps = seq // frames
    spec = pl.BlockSpec((n_batch, frames, rw), lambda c: (0, c, 0))
    sliced = [a.reshape(n_steps, -1, a.shape[-1]) for a in side_f32]
    side_specs = [pl.BlockSpec((1,) + a.shape[1:], lambda c: (c, 0, 0)) for a in sliced]
    outs = pl.pallas_call(
        functools.partial(_wkv_chunk_kernel, n_batch, n_pairs, len(sliced)),
        grid=(n_steps,),
        in_specs=[spec] * 6 + side_specs,
        out_specs=[spec] + side_specs,
        out_shape=[jax.ShapeDtypeStruct((n_batch, seq, rw), F32)]
                  + [jax.ShapeDtypeStruct(a.shape, BF16) for a in sliced],
        scratch_shapes=[pltpu.VMEM((n_batch * n_pairs, V7X_LANES, V7X_LANES), F32)],
        compiler_params=_params("arbitrary"),
        name="wkv_chunk",
    )(r, lw, k, v, kk, b, *sliced)
    return outs[0], [y.reshape(a.shape) for y, a in zip(outs[1:], side_f32)]


def _post_mix_kernel(n_experts,
                     o_ref, bonus_ref, gate_ref, yconv_ref, x_ref, gn_w_ref, gn_b_ref,
                     w_out_c_ref, w_out_r_ref, ffn_g_ref, wr_hi_ref, wr_lo_ref, b_router_ref,
                     seg2_ref,
                     x1_ref, *outs):
    h2_refs, (top_e_ref, gates_ref) = outs[:-2], outs[-2:]
    seg2 = seg2_ref[...]
    o = o_ref[...]
    mean = _seg_sum(o, seg2) * (1.0 / HEAD_DIM)
    cen = o - mean
    var = _seg_sum(cen * cen, seg2) * (1.0 / HEAD_DIM)
    o = cen * lax.rsqrt(var + GN_EPS) * gn_w_ref[...] + gn_b_ref[...]
    y_rwkv = (o + bonus_ref[...]) * gate_ref[...]
    x1 = (x_ref[...]
          + jnp.dot(yconv_ref[...], w_out_c_ref[...], preferred_element_type=F32)
          + jnp.dot(y_rwkv.astype(BF16), w_out_r_ref[...], preferred_element_type=F32))
    x1_ref[...] = x1
    h2 = _rms_rows(x1, ffn_g_ref[...])
    words = _pack_bf16_pairs(h2)
    for c, h2_ref in enumerate(h2_refs):
        h2_ref[...] = words[:, c * ROW_CHUNK:(c + 1) * ROW_CHUNK]

    h_hi, h_lo = _split_bf16(h2)
    nt = (((1,), (1,)), ((), ()))
    logits = (lax.dot_general(wr_hi_ref[...], h_hi, nt, preferred_element_type=F32)
              + lax.dot_general(wr_hi_ref[...], h_lo, nt, preferred_element_type=F32)
              + lax.dot_general(wr_lo_ref[...], h_hi, nt, preferred_element_type=F32)
              + b_router_ref[...])
    e_id = lax.broadcasted_iota(I32, logits.shape, 0).astype(F32)
    work = logits
    tops, ids = [], []
    for _ in range(TOP_K):
        m = jnp.max(work, axis=0, keepdims=True)
        sel = jnp.min(jnp.where(work == m, e_id, float(n_experts)), axis=0, keepdims=True)
        tops.append(m)
        ids.append(sel)
        work = jnp.where(e_id == sel, -jnp.inf, work)
    ex = [jnp.exp(t - tops[0]) for t in tops]
    denom = ex[0] + ex[1] + ex[2] + ex[3]
    top_e_ref[...] = jnp.concatenate(ids, axis=0).astype(I32)
    gates_ref[...] = jnp.concatenate([e / denom for e in ex], axis=0)


def _post_mix(token0, n, o, bonus, gate, yconv, x2d, gn_w, gn_b, w_out_c, w_out_r, ffn_g, wr_hi,
              wr_lo, b_router, seg2):
    d = x2d.shape[1]
    rw = o.shape[1]
    cw = yconv.shape[1]
    n_experts = wr_hi.shape[0]
    tm = POST_ROWS
    assert token0 % tm == 0 and n % tm == 0
    full = lambda a: pl.BlockSpec(a.shape, lambda i: (0,) * a.ndim)
    in_spec = lambda c: pl.BlockSpec((tm, c), lambda i: (i + token0 // tm, 0))
    row_spec = lambda c: pl.BlockSpec((tm, c), lambda i: (i, 0))
    col_spec = pl.BlockSpec((TOP_K, tm), lambda i: (0, i))
    consts = (gn_w, gn_b, w_out_c, w_out_r, ffn_g, wr_hi, wr_lo, b_router, seg2)
    n_chunks = d // 2 // ROW_CHUNK
    outs = pl.pallas_call(
        functools.partial(_post_mix_kernel, n_experts),
        grid=(n // tm,),
        in_specs=[in_spec(rw), in_spec(rw), in_spec(rw), in_spec(cw), in_spec(d)]
                 + [full(c) for c in consts],
        out_specs=[row_spec(d)] + [row_spec(ROW_CHUNK)] * n_chunks + [col_spec, col_spec],
        out_shape=[jax.ShapeDtypeStruct((n, d), F32)]
                  + [jax.ShapeDtypeStruct((n, ROW_CHUNK), jnp.uint32)] * n_chunks
                  + [jax.ShapeDtypeStruct((TOP_K, n), I32), jax.ShapeDtypeStruct((TOP_K, n), F32)],
        compiler_params=_params("parallel"),
        name="post_mix",
    )(o, bonus, gate, yconv, x2d, *consts)
    return outs[0], outs[1:1 + n_chunks], outs[-2], outs[-1]


def _route_kernel(n_experts, block_rows,
                  top_e_ref, tri_ref, dest_ref, meta_ref,
                  count_ref, start_ref, carry_ref):
    phase = pl.program_id(0)
    j = pl.program_id(1)
    tb = top_e_ref.shape[1]
    e_id = lax.broadcasted_iota(I32, (n_experts, tb), 0)
    top_e = top_e_ref[...]
    onehot = jnp.zeros((n_experts, tb), F32)
    for c in range(TOP_K):
        onehot = onehot + jnp.where(top_e[c:c + 1, :] == e_id, 1.0, 0.0)
    block_count = jnp.sum(onehot, axis=1, keepdims=True)

    @pl.when((phase == 0) & (j == 0))
    def _():
        count_ref[...] = jnp.zeros_like(count_ref)

    @pl.when(phase == 0)
    def _():
        count_ref[...] += block_count

    @pl.when((phase == 1) & (j == 0))
    def _():
        counts = count_ref[...]
        padded = jnp.ceil(counts * (1.0 / block_rows)) * block_rows
        sub = lax.broadcasted_iota(I32, (n_experts, n_experts), 0)
        lane = lax.broadcasted_iota(I32, (n_experts, n_experts), 1)
        padded_row = jnp.sum(jnp.where(sub == lane, padded, 0.0), axis=0, keepdims=True)
        start = jnp.sum(jnp.where(lane < sub, padded_row, 0.0), axis=1, keepdims=True)
        start_ref[...] = start
        carry_ref[...] = jnp.zeros_like(carry_ref)
        end = start + padded
        nb = meta_ref.shape[1]
        slot0 = (lax.broadcasted_iota(I32, (n_experts, nb), 1) * block_rows).astype(F32)
        block_e = jnp.sum(jnp.where(end <= slot0, 1.0, 0.0), axis=0, keepdims=True)
        block_e = jnp.minimum(block_e, n_experts - 1.0)
        used = jnp.max(end, axis=0, keepdims=True) * (1.0 / block_rows)
        in_region = (start <= slot0) & (slot0 < end)
        valid = jnp.sum(jnp.where(in_region, jnp.minimum(start + counts - slot0, block_rows), 0.0),
                        axis=0, keepdims=True)
        row = lax.broadcasted_iota(I32, meta_ref.shape, 0)
        meta_ref[...] = jnp.where(row == META_BLOCK_EXPERT, block_e,
                                  jnp.where(row == META_VALID_ROWS, valid, used)).astype(I32)

    @pl.when(phase == 1)
    def _():
        incl = jnp.dot(onehot.astype(BF16), tri_ref[...], preferred_element_type=F32)
        base = incl - onehot + carry_ref[...] + start_ref[...]
        rows = [jnp.sum(jnp.where(top_e[c:c + 1, :] == e_id, base, 0.0), axis=0, keepdims=True)
                for c in range(TOP_K)]
        dest_ref[...] = jnp.concatenate(rows, axis=0).astype(I32)
        carry_ref[...] += block_count


def _route(top_e, n_experts, n_blocks_padded):
    n = top_e.shape[1]
    tb = ROUTE_TOKENS
    assert n % tb == 0
    tri = (lax.broadcasted_iota(I32, (tb, tb), 0) <= lax.broadcasted_iota(I32, (tb, tb), 1)
           ).astype(BF16)
    return pl.pallas_call(
        functools.partial(_route_kernel, n_experts, EXPERT_ROWS),
        grid=(2, n // tb),
        in_specs=[pl.BlockSpec((TOP_K, tb), lambda ph, j: (0, j)),
                  pl.BlockSpec((tb, tb), lambda ph, j: (0, 0))],
        out_specs=[pl.BlockSpec((TOP_K, tb), lambda ph, j: (0, j * ph)),
                   pl.BlockSpec((V7X_SUBLANES, n_blocks_padded), lambda ph, j: (0, 0))],
        out_shape=[jax.ShapeDtypeStruct((TOP_K, n), I32),
                   jax.ShapeDtypeStruct((V7X_SUBLANES, n_blocks_padded), I32)],
        scratch_shapes=[pltpu.VMEM((n_experts, 1), F32)] * 3,
        compiler_params=_params("arbitrary", "arbitrary"),
        name="route",
    )(top_e, tri)


def _sc_mesh():
    return plsc.VectorSubcoreMesh(core_axis_name="core", subcore_axis_name="subcore")


def _sc_scatter_rows(src, dest, n_slots):
    n, width = src.shape
    assert n % SC_WINDOW == 0

    @functools.partial(pl.kernel, out_type=jax.ShapeDtypeStruct((n_slots, width), src.dtype),
                       mesh=_sc_mesh(), name="sc_scatter_rows")
    def scatter(src_hbm, dest_hbm, out_hbm):
        def body(src_vmem, dest_vmem):
            for c in range(TOP_K):
                pltpu.sync_copy(src_vmem, out_hbm.at[dest_vmem.at[c]])

        pltpu.emit_pipeline(
            body,
            grid=(n // SC_WINDOW,),
            in_specs=[pl.BlockSpec((SC_WINDOW, width), lambda i: (i, 0)),
                      pl.BlockSpec((TOP_K, SC_WINDOW), lambda i: (0, i))],
            out_specs=[],
            core_axis_name=("core", "subcore"),
            dimension_semantics=(pltpu.PARALLEL,),
        )(src_hbm, dest_hbm)

    return scatter(src, dest)


def _experts_kernel(d_ff, n_in, n_out, meta_ref, *refs):
    x_refs, refs = refs[:n_in], refs[n_in:]
    weights = [refs[4 * h:4 * h + 4] for h in range(EXPERT_STEP_BLOCKS)]
    y_refs = refs[4 * EXPERT_STEP_BLOCKS:]
    j = pl.program_id(0)
    used = meta_ref[META_USED_BLOCKS, 0]
    block0 = j * EXPERT_STEP_BLOCKS
    bm = EXPERT_ROWS

    @pl.when(block0 < used)
    def _():
        x = _unpack_bf16_pairs(jnp.concatenate([r[...] for r in x_refs], axis=1))
        row = lax.broadcasted_iota(I32, (bm, x.shape[1]), 0)
        ys = []
        for h, (w_gu_ref, b_gu_ref, w_down_ref, b_down_ref) in enumerate(weights):
            xh = jnp.where(row < meta_ref[META_VALID_ROWS, block0 + h], x[h * bm:(h + 1) * bm], 0.0)
            gu = jnp.dot(xh.astype(BF16), w_gu_ref[0], preferred_element_type=F32) + b_gu_ref[0]
            gate = jnp.minimum(gu[:, :d_ff], SWIGLU_LIMIT)
            up = jnp.clip(gu[:, d_ff:], -SWIGLU_LIMIT, SWIGLU_LIMIT)
            act = (up + 1.0) * (gate * _sigmoid(SWIGLU_ALPHA * gate))
            ys.append(jnp.dot(act.astype(BF16), w_down_ref[0], preferred_element_type=F32)
                      + b_down_ref[0])
        words = _pack_bf16_pairs(jnp.concatenate(ys, axis=0))
        for c, y_ref in enumerate(y_refs):
            y_ref[...] = words[:, c * ROW_CHUNK:(c + 1) * ROW_CHUNK]

    @pl.when(block0 >= used)
    def _():
        for y_ref in y_refs:
            y_ref[...] = jnp.zeros_like(y_ref)


def _experts(meta, slot_chunks, w_gu, b_gu, w_down, b_down):
    assert w_gu.dtype == BF16 and w_down.dtype == BF16
    n_in = len(slot_chunks)
    n_slots = slot_chunks[0].shape[0]
    n_experts, d, two_ff = w_gu.shape
    n_out = d // 2 // ROW_CHUNK
    d_ff = two_ff // 2
    step_rows = EXPERT_ROWS * EXPERT_STEP_BLOCKS
    assert n_slots % step_rows == 0

    def last_used_step(m):
        return (m[META_USED_BLOCKS, 0] - 1) // EXPERT_STEP_BLOCKS

    weight_specs = []
    for h in range(EXPERT_STEP_BLOCKS):
        by_expert = lambda j, m, h=h: (m[META_BLOCK_EXPERT, j * EXPERT_STEP_BLOCKS + h], 0, 0)
        weight_specs += [pl.BlockSpec((1, d, two_ff), by_expert),
                         pl.BlockSpec((1, 1, two_ff), by_expert),
                         pl.BlockSpec((1, d_ff, d), by_expert), pl.BlockSpec((1, 1, d), by_expert)]
    grid_spec = pltpu.PrefetchScalarGridSpec(
        num_scalar_prefetch=1,
        grid=(n_slots // step_rows,),
        in_specs=[pl.BlockSpec((step_rows, ROW_CHUNK),
                               lambda j, m: (jnp.minimum(j, last_used_step(m)), 0))] * n_in
                 + weight_specs,
        out_specs=[pl.BlockSpec((step_rows, ROW_CHUNK), lambda j, m: (j, 0))] * n_out,
    )
    return pl.pallas_call(
        functools.partial(_experts_kernel, d_ff, n_in, n_out),
        grid_spec=grid_spec,
        out_shape=[jax.ShapeDtypeStruct((n_slots, ROW_CHUNK), jnp.uint32)] * n_out,
        compiler_params=_params("arbitrary"),
        name="experts",
    )(meta, *slot_chunks, *([w_gu, b_gu, w_down, b_down] * EXPERT_STEP_BLOCKS))


def _sc_gather_rows(table, idx_row):
    n_idx = idx_row.shape[1]
    width = table.shape[1]
    assert n_idx % SC_WINDOW == 0

    @functools.partial(pl.kernel, out_type=jax.ShapeDtypeStruct((n_idx, width), table.dtype),
                       mesh=_sc_mesh(), name="sc_gather_rows")
    def gather(table_hbm, idx_hbm, out_hbm):
        def body(idx_vmem, out_vmem):
            pltpu.sync_copy(table_hbm.at[idx_vmem.at[0]], out_vmem)

        pltpu.emit_pipeline(
            body,
            grid=(n_idx // SC_WINDOW,),
            in_specs=[pl.BlockSpec((1, SC_WINDOW), lambda i: (0, i))],
            out_specs=[pl.BlockSpec((SC_WINDOW, width), lambda i: (i, 0))],
            core_axis_name=("core", "subcore"),
            dimension_semantics=(pltpu.PARALLEL,),
        )(idx_hbm, out_hbm)

    return gather(table, idx_row)


def _reduce_kernel(n_chunks, *refs):
    y_refs = refs[:n_chunks]
    x1_ref, gates_ref, g_ref = refs[n_chunks:n_chunks + 3]
    out_ref = refs[-1]
    gates = gates_ref[...]
    acc = x1_ref[...]
    for c in range(TOP_K):
        y = _unpack_bf16_pairs(jnp.concatenate([r[c] for r in y_refs], axis=1))
        acc = acc + y * gates[:, c:c + 1]
    out_ref[...] = _rms_rows(acc, g_ref[...])


def _reduce(token0, n_total, out_so_far, y_chunks, x1, gates_t, final_g):
    n, d = x1.shape
    tb = REDUCE_TOKENS
    n_chunks = len(y_chunks)
    assert token0 % tb == 0
    in_specs = ([pl.BlockSpec((TOP_K, tb, ROW_CHUNK), lambda i: (0, i, 0))] * n_chunks
                + [pl.BlockSpec((tb, d), lambda i: (i, 0)),
                   pl.BlockSpec((tb, TOP_K), lambda i: (i, 0)),
                   pl.BlockSpec((1, d), lambda i: (0, 0))])
    args = [y.reshape(TOP_K, n, ROW_CHUNK) for y in y_chunks] + [x1, gates_t, final_g]
    aliases = {}
    if out_so_far is not None:
        in_specs.append(pl.BlockSpec(memory_space=pl.ANY))
        args.append(out_so_far)
        aliases = {len(args) - 1: 0}
    return pl.pallas_call(
        functools.partial(_reduce_kernel, n_chunks),
        grid=(n // tb,),
        in_specs=in_specs,
        out_specs=pl.BlockSpec((tb, d), lambda i: (i + token0 // tb, 0)),
        out_shape=jax.ShapeDtypeStruct((n_total, d), F32),
        input_output_aliases=aliases,
        compiler_params=_params("parallel"),
        name="reduce",
    )(*args)


def _group_matrix(width):
    a = lax.broadcasted_iota(I32, (width, width), 0) // HEAD_DIM
    b = lax.broadcasted_iota(I32, (width, width), 1) // HEAD_DIM
    g = (a == b).astype(BF16)
    return jnp.concatenate([g, g], axis=0)


def _row(vec):
    return vec.reshape(1, -1).astype(F32)


def kernel(x, w_in, conv_w, conv_norm_g, rwkv_mu, w0, w_up, a0, a_up, g_up, k_k, k_a, r_k,
           gn_w, gn_b, w_out, norm_mix_g, norm_ffn_g, w_router, b_router, w_gu, b_gu, w_down,
           b_down, norm_final_g):
    n_batch, seq, d = x.shape
    n = n_batch * seq
    depth = w_in.shape[0]
    cw = conv_w.shape[2]
    rw = w0.shape[1]
    n_experts = w_router.shape[2]
    decay_lora = w_up.shape[1]
    aaa_lora = a_up.shape[1]
    assert decay_lora + aaa_lora == V7X_LANES
    assert depth == 1
    assert n % MOE_PARTS == 0
    n_part = n // MOE_PARTS
    n_slots = n_part * TOP_K + n_experts * EXPERT_ROWS
    assert n_slots % (EXPERT_ROWS * EXPERT_STEP_BLOCKS) == 0
    n_blocks = n_slots // EXPERT_ROWS
    n_blocks_padded = -(-n_blocks // V7X_LANES) * V7X_LANES
    seg2 = _group_matrix(rw)

    x2d = x.reshape(n, d)
    for l in range(depth):
        w_up_pad = jnp.concatenate([w_up[l], jnp.zeros((aaa_lora, rw), F32)], axis=0).astype(BF16)
        a_up_pad = jnp.concatenate([jnp.zeros((decay_lora, rw), F32), a_up[l]], axis=0).astype(BF16)
        (yconv, r, lw, k, v, kk, b, gate, bonus) = _mix_prep(
            x2d, seq, _row(norm_mix_g[l]), w_in[l].astype(BF16), conv_w[l].astype(F32),
            _row(conv_norm_g[l]), _row(rwkv_mu[l]), _row(w0[l]), w_up_pad, _row(a0[l]), a_up_pad,
            g_up[l].astype(BF16), _row(k_k[l]), _row(k_a[l]), _row(r_k[l]), seg2)
        shape3 = (n_batch, seq, rw)
        o, (w_gu_b, w_down_b) = _wkv_chunk(
            r.reshape(shape3), lw.reshape(shape3), k.reshape(shape3), v.reshape(shape3),
            kk.reshape(shape3), b.reshape(shape3),
            side_f32=(w_gu[l].astype(F32), w_down[l].astype(F32)))
        o = o.reshape(n, rw)
        w_out_b = w_out[l].astype(BF16)
        wr_t = w_router[l].T.astype(F32)
        wr_hi = wr_t.astype(BF16)
        wr_lo = (wr_t - wr_hi.astype(F32)).astype(BF16)
        expert_args = (w_gu_b, b_gu[l].reshape(n_experts, 1, -1).astype(F32),
                       w_down_b, b_down[l].reshape(n_experts, 1, -1).astype(F32))
        parts = []
        for token0 in range(0, n, n_part):
            x1, h2_chunks, top_e, gates = _post_mix(
                token0, n_part, o, bonus, gate, yconv, x2d, _row(gn_w[l]), _row(gn_b[l]),
                w_out_b[:cw], w_out_b[cw:], _row(norm_ffn_g[l]), wr_hi, wr_lo,
                b_router[l].reshape(n_experts, 1).astype(F32), seg2)
            dest, meta = _route(top_e, n_experts, n_blocks_padded)
            slot_chunks = [_sc_scatter_rows(h, dest, n_slots) for h in h2_chunks]
            parts.append((token0, x1, gates, dest, meta, slot_chunks))
        out = None
        for token0, x1, gates, dest, meta, slot_chunks in parts:
            y_chunks = _experts(meta, slot_chunks, *expert_args)
            idx_row = dest.reshape(1, TOP_K * n_part)
            out = _reduce(token0, n, out, [_sc_gather_rows(y, idx_row) for y in y_chunks], x1,
                          gates.T, _row(norm_final_g))
        x2d = out
    return x2d.reshape(n_batch, seq, d)
```

```python
import functools

import jax
import jax.numpy as jnp
from jax import lax
from jax.experimental import pallas as pl
from jax.experimental.pallas import tpu as pltpu
from jax.experimental.pallas import tpu_sc as plsc

F32 = jnp.float32
BF16 = jnp.bfloat16
I32 = jnp.int32

HEAD_DIM = 64
TOP_K = 4
NORM_EPS = 1e-5
GN_EPS = HEAD_DIM * 1e-5
SWIGLU_LIMIT = 7.0
SWIGLU_ALPHA = 1.702

V7X_LANES = 128
V7X_SUBLANES = 8
V7X_VMEM_LIMIT_BYTES = 56 * 1024 * 1024

PREP_ROWS = 512
POST_ROWS = 1024
SCAN_FRAMES = 64
ROUTE_TOKENS = 1024
REDUCE_TOKENS = 1024
MOE_PARTS = 1
SC_WINDOW = 128
ROW_CHUNK = 256
META_BLOCK_EXPERT = 0
META_USED_BLOCKS = 1
META_VALID_ROWS = 2
EXPERT_ROWS = 512

def _params(*semantics):
    return pltpu.CompilerParams(dimension_semantics=semantics,
                                vmem_limit_bytes=V7X_VMEM_LIMIT_BYTES)


def _split_bf16(x):
    hi = x.astype(BF16)
    lo = (x - hi.astype(F32)).astype(BF16)
    return hi, lo


def _seg_sum(x, seg2):
    hi, lo = _split_bf16(x)
    return jnp.dot(jnp.concatenate([hi, lo], axis=1), seg2, preferred_element_type=F32)


def _pack_bf16_pairs(x):
    bits = pltpu.bitcast(x.astype(BF16).astype(F32), jnp.uint32)
    half = x.shape[1] // 2
    return bits[:, half:] | (bits[:, :half] >> 16)


def _unpack_bf16_pairs(words):
    return jnp.concatenate([pltpu.bitcast(words << 16, F32),
                            pltpu.bitcast(words & jnp.uint32(0xFFFF0000), F32)], axis=1)


def _rms_rows(x, g):
    return x * lax.rsqrt(jnp.mean(x * x, axis=-1, keepdims=True) + NORM_EPS) * g


def _sigmoid(x):
    return 1.0 / (1.0 + jnp.exp(-x))


def _mix_prep_kernel(blocks_per_seq, cw, rw,
                     xprev_ref, x_ref, g_ref, w_in_ref, conv_w_ref, conv_g_ref, mu_ref,
                     w0_ref, w_up_ref, a0_ref, a_up_ref, g_up_ref, k_k_ref, k_a_ref, r_k_ref,
                     seg2_ref,
                     yconv_ref, r_ref, w_ref, k_ref, v_ref, kk_ref, b_ref, gate_ref, bonus_ref,
                     p_scr, z_scr):
    tm = x_ref.shape[0]
    halo = xprev_ref.shape[0]
    first = (pl.program_id(0) % blocks_per_seq) == 0
    xp = xprev_ref[...] * jnp.where(first, 0.0, 1.0)
    xa = jnp.concatenate([xp, x_ref[...]], axis=0)
    h = _rms_rows(xa, g_ref[...])
    p_scr[...] = jnp.dot(h.astype(BF16), w_in_ref[...], preferred_element_type=F32)
    seg2 = seg2_ref[...]

    z_scr[...] = p_scr[:, 2 * cw:3 * cw] * p_scr[:, 0:cw]
    conv = (conv_w_ref[0:1, :] * z_scr[halo - 2:halo - 2 + tm, :]
            + conv_w_ref[1:2, :] * z_scr[halo - 1:halo - 1 + tm, :]
            + conv_w_ref[2:3, :] * z_scr[halo:halo + tm, :])
    y = p_scr[halo:halo + tm, cw:2 * cw] * conv
    ms = _seg_sum(y * y, seg2) * (1.0 / HEAD_DIM)
    yconv_ref[...] = (y * lax.rsqrt(ms + NORM_EPS) * conv_g_ref[...]).astype(yconv_ref.dtype)

    c0 = 3 * cw
    cur = p_scr[halo:halo + tm, c0:]
    prev = p_scr[halo - 1:halo - 1 + tm, c0:]
    q = cur + (prev - cur) * mu_ref[...]
    r = q[:, 0:rw]
    k = q[:, rw:2 * rw]
    v = q[:, 2 * rw:3 * rw]
    lora_wa = q[:, 3 * rw:3 * rw + V7X_LANES]
    lora_g = q[:, 3 * rw + V7X_LANES:]
    w_lin = w0_ref[...] + jnp.dot(jnp.tanh(lora_wa).astype(BF16), w_up_ref[...],
                                  preferred_element_type=F32)
    neg = -w_lin
    softplus = jnp.maximum(neg, 0.0) + jnp.log(1.0 + jnp.exp(-jnp.abs(neg)))
    log_decay = -jnp.exp(-softplus - 0.5)
    a = _sigmoid(a0_ref[...] + jnp.dot(lora_wa.astype(BF16), a_up_ref[...],
                                       preferred_element_type=F32))
    gate = jnp.dot(_sigmoid(lora_g).astype(BF16), g_up_ref[...], preferred_element_type=F32)
    kk = k * k_k_ref[...]
    kk = kk / jnp.maximum(jnp.sqrt(_seg_sum(kk * kk, seg2)), 1e-12)
    k_mod = k * (1.0 + (a - 1.0) * k_a_ref[...])
    bonus = _seg_sum(r * k_mod * r_k_ref[...], seg2) * v
    r_ref[...] = r
    w_ref[...] = log_decay
    k_ref[...] = k_mod
    v_ref[...] = v
    kk_ref[...] = kk
    b_ref[...] = kk * a
    gate_ref[...] = gate
    bonus_ref[...] = bonus


def _mix_prep(x2d, seq, norm_g, w_in, conv_w, conv_g, mu, w0, w_up, a0, a_up, g_up, k_k, k_a,
              r_k, seg2):
    n, d = x2d.shape
    tm = PREP_ROWS
    halo = V7X_SUBLANES
    cw = conv_w.shape[1]
    rw = w0.shape[1]
    in_cols = w_in.shape[1]
    assert seq % tm == 0 and n % tm == 0
    full = lambda a: pl.BlockSpec(a.shape, lambda i: (0,) * a.ndim)
    row_spec = lambda c: pl.BlockSpec((tm, c), lambda i: (i, 0))
    consts = (norm_g, w_in, conv_w, conv_g, mu, w0, w_up, a0, a_up, g_up, k_k, k_a, r_k, seg2)
    outs = [jax.ShapeDtypeStruct((n, cw), BF16)] + [jax.ShapeDtypeStruct((n, rw), F32)] * 8
    return pl.pallas_call(
        functools.partial(_mix_prep_kernel, seq // tm, cw, rw),
        grid=(n // tm,),
        in_specs=[pl.BlockSpec((halo, d), lambda i: (jnp.maximum(i * (tm // halo) - 1, 0), 0)),
                  row_spec(d)] + [full(c) for c in consts],
        out_specs=[row_spec(cw)] + [row_spec(rw)] * 8,
        out_shape=outs,
        scratch_shapes=[pltpu.VMEM((halo + tm, in_cols), F32), pltpu.VMEM((halo + tm, cw), F32)],
        compiler_params=_params("parallel"),
        name="mix_prep",
    )(x2d, x2d, *consts)


def _bdot(a, b):
    return jnp.dot(a.astype(BF16), b.astype(BF16), preferred_element_type=F32)


def _bdot_nt(a, b):
    return lax.dot_general(a.astype(BF16), b.astype(BF16), (((1,), (1,)), ((), ())),
                           preferred_element_type=F32)


def _wkv_chunk_kernel(n_batch, n_pairs, n_side,
                      r_ref, lw_ref, k_ref, v_ref, kk_ref, b_ref, *refs):
    frames = r_ref.shape[1]
    lanes = V7X_LANES
    hd = HEAD_DIM
    assert frames == hd
    side_in, o_ref, side_out, state_ref = (refs[:n_side], refs[n_side],
                                           refs[n_side + 1:2 * n_side + 1], refs[-1])
    for src, dst in zip(side_in, side_out):
        dst[...] = src[...].astype(dst.dtype)

    @pl.when(pl.program_id(0) == 0)
    def _():
        state_ref[...] = jnp.zeros_like(state_ref)

    row = lax.broadcasted_iota(I32, (lanes, lanes), 0)
    lane = lax.broadcasted_iota(I32, (lanes, lanes), 1)
    same_head = (row // hd) == (lane // hd)
    rt = row % hd
    ls = lane % hd
    strict_same = same_head & (rt > ls)
    strict_cross = jnp.logical_not(same_head) & (rt > ls)
    incl_same = same_head & (rt >= ls)
    incl_cross = jnp.logical_not(same_head) & (rt >= ls)
    eye = row == lane
    level_masks = []
    m = 1
    while m < hd:
        level_masks.append(same_head & ((rt // (2 * m)) == (ls // (2 * m)))
                           & (((rt // m) % 2) == 1) & (((ls // m) % 2) == 0))
        m *= 2
    left = lax.broadcasted_iota(I32, (frames, lanes), 1) < hd
    tri = (lax.broadcasted_iota(I32, (frames, frames), 0)
           >= lax.broadcasted_iota(I32, (frames, frames), 1)).astype(BF16)

    def diag_blocks(x):
        return jnp.concatenate([jnp.where(left, x, 0.0), jnp.where(left, 0.0, x)], axis=0)

    def cross_blocks(x):
        return jnp.concatenate([jnp.where(left, 0.0, x), jnp.where(left, x, 0.0)], axis=0)

    def batch_prep(bi):
        lw = lw_ref[bi]
        hi = lw.astype(BF16)
        rem = lw - hi.astype(F32)
        mid = rem.astype(BF16)
        lo = (rem - mid.astype(F32)).astype(BF16)
        cs = (jnp.dot(tri, hi, preferred_element_type=F32)
              + jnp.dot(tri, mid, preferred_element_type=F32)
              + jnp.dot(tri, lo, preferred_element_type=F32))
        cs_end = cs[frames - 1:frames, :]
        e_neg = jnp.exp(-cs)
        e_end = jnp.exp(cs_end - cs)
        b_in = b_ref[bi]
        k_in = k_ref[bi]
        return dict(a_t=-kk_ref[bi] * jnp.exp(cs - lw), b_t=b_in * e_neg, k_t=k_in * e_neg,
                    r_t=r_ref[bi] * jnp.exp(cs), b_h=b_in * e_end, k_h=k_in * e_end,
                    v=v_ref[bi], g_end=jnp.exp(cs_end))

    group = []
    for bi in range(n_batch):
        prep = batch_prep(bi)
        for pi in range(n_pairs):
            sl = slice(pi * lanes, (pi + 1) * lanes)
            group.append((bi, sl, bi * n_pairs + pi, {n: x[:, sl] for n, x in prep.items()}))
    bf = lambda x: x.astype(BF16)
    a_bd = [bf(diag_blocks(p["a_t"])) for _, _, _, p in group]
    r_bd = [bf(diag_blocks(p["r_t"])) for _, _, _, p in group]
    d0 = [_bdot_nt(jnp.concatenate([a[:hd], r[:hd]], axis=0),
                   bf(jnp.concatenate([p["b_t"], p["k_t"]], axis=0)))
          for a, r, (_, _, _, p) in zip(a_bd, r_bd, group)]
    d1 = [_bdot_nt(jnp.concatenate([a[hd:], r[hd:]], axis=0),
                   bf(jnp.concatenate([p["k_t"], p["b_t"]], axis=0)))
          for a, r, (_, _, _, p) in zip(a_bd, r_bd, group)]
    a_rows = [jnp.concatenate([x[:hd], y[:hd]], axis=0) for x, y in zip(d0, d1)]
    m_rows = [jnp.concatenate([x[hd:], y[hd:]], axis=0) for x, y in zip(d0, d1)]
    a_ab = [bf(jnp.where(strict_same, x, 0.0)) for x in a_rows]
    a_ak = [bf(jnp.where(strict_cross, x, 0.0)) for x in a_rows]
    m_rb = [bf(jnp.where(incl_same, x, 0.0)) for x in m_rows]
    m_rk = [bf(jnp.where(incl_cross, x, 0.0)) for x in m_rows]
    t_inv = [jnp.where(eye, 1.0, jnp.where(level_masks[0], x, 0.0)) for x in a_rows]
    for mask in level_masks[1:]:
        t_bf = [bf(t) for t in t_inv]
        half = [jnp.where(mask, _bdot(t, x), 0.0) for t, x in zip(t_bf, a_ab)]
        t_inv = [t + _bdot(h, tb) for t, h, tb in zip(t_inv, half, t_bf)]
    t_bf = [bf(t) for t in t_inv]
    v_x = [bf(cross_blocks(p["v"])) for _, _, _, p in group]
    akv = [_bdot(x, v) for x, v in zip(a_ak, v_x)]
    wu = [bf(_bdot(t, jnp.concatenate([a, bf(x)], axis=1)))
          for t, a, x in zip(t_bf, a_bd, akv)]
    bh_t = [bf(diag_blocks(p["b_h"]).T) for _, _, _, p in group]
    kh_t = [bf(cross_blocks(p["k_h"]).T) for _, _, _, p in group]
    pw_rw = [_bdot(jnp.concatenate([bt, m], axis=0), x[:, :lanes])
             for bt, m, x in zip(bh_t, m_rb, wu)]
    q_o = [_bdot(jnp.concatenate([jnp.concatenate([bt, kt], axis=1),
                                  jnp.concatenate([mb, mk], axis=1)], axis=0),
                 jnp.concatenate([x[:, lanes:], v], axis=0))
           for bt, kt, mb, mk, x, v in zip(bh_t, kh_t, m_rb, m_rk, wu, v_x)]
    res = [_bdot(jnp.concatenate([diag_blocks(p["r_t"]) + pr[lanes:],
                                  pr[:lanes] + jnp.where(eye, p["g_end"], 0.0)], axis=0),
                 state_ref[idx])
           for pr, (_, _, idx, p) in zip(pw_rw, group)]
    for x, q, (bi, sl, idx, _) in zip(res, q_o, group):
        o_bd = x[:lanes] + q[lanes:]
        state_ref[idx] = x[lanes:] + q[:lanes]
        o_ref[bi, :, sl] = jnp.where(left, o_bd[:hd], o_bd[hd:])


def _wkv_chunk(r, lw, k, v, kk, b, side_f32=()):
    n_batch, seq, rw = r.shape
    n_pairs = rw // V7X_LANES
    frames = SCAN_FRAMES
    assert seq % frames == 0
    n_steps = seq // frames
    spec = pl.BlockSpec((n_batch, frames, rw), lambda c: (0, c, 0))
    sliced = [a.reshape(n_steps, -1, a.shape[-1]) for a in side_f32]
    side_specs = [pl.BlockSpec((1,) + a.shape[1:], lambda c: (c, 0, 0)) for a in sliced]
    outs = pl.pallas_call(
        functools.partial(_wkv_chunk_kernel, n_batch, n_pairs, len(sliced)),
        grid=(n_steps,),
        in_specs=[spec] * 6 + side_specs,
        out_specs=[spec] + side_specs,
        out_shape=[jax.ShapeDtypeStruct((n_batch, seq, rw), F32)]
                  + [jax.ShapeDtypeStruct(a.shape, BF16) for a in sliced],
        scratch_shapes=[pltpu.VMEM((n_batch * n_pairs, V7X_LANES, V7X_LANES), F32)],
        compiler_params=_params("arbitrary"),
        name="wkv_chunk",
    )(r, lw, k, v, kk, b, *sliced)
    return outs[0], [y.reshape(a.shape) for y, a in zip(outs[1:], side_f32)]


def _post_mix_kernel(n_experts,
                     o_ref, bonus_ref, gate_ref, yconv_ref, x_ref, gn_w_ref, gn_b_ref,
                     w_out_c_ref, w_out_r_ref, ffn_g_ref, wr_hi_ref, wr_lo_ref, b_router_ref,
                     seg2_ref,
                     x1_ref, *outs):
    h2_refs, (top_e_ref, gates_ref) = outs[:-2], outs[-2:]
    seg2 = seg2_ref[...]
    o = o_ref[...]
    mean = _seg_sum(o, seg2) * (1.0 / HEAD_DIM)
    cen = o - mean
    var = _seg_sum(cen * cen, seg2) * (1.0 / HEAD_DIM)
    o = cen * lax.rsqrt(var + GN_EPS) * gn_w_ref[...] + gn_b_ref[...]
    y_rwkv = (o + bonus_ref[...]) * gate_ref[...]
    x1 = (x_ref[...]
          + jnp.dot(yconv_ref[...], w_out_c_ref[...], preferred_element_type=F32)
          + jnp.dot(y_rwkv.astype(BF16), w_out_r_ref[...], preferred_element_type=F32))
    x1_ref[...] = x1
    h2 = _rms_rows(x1, ffn_g_ref[...])
    words = _pack_bf16_pairs(h2)
    for c, h2_ref in enumerate(h2_refs):
        h2_ref[...] = words[:, c * ROW_CHUNK:(c + 1) * ROW_CHUNK]

    h_hi, h_lo = _split_bf16(h2)
    nt = (((1,), (1,)), ((), ()))
    logits = (lax.dot_general(wr_hi_ref[...], h_hi, nt, preferred_element_type=F32)
              + lax.dot_general(wr_hi_ref[...], h_lo, nt, preferred_element_type=F32)
              + lax.dot_general(wr_lo_ref[...], h_hi, nt, preferred_element_type=F32)
              + b_router_ref[...])
    e_id = lax.broadcasted_iota(I32, logits.shape, 0).astype(F32)
    work = logits
    tops, ids = [], []
    for _ in range(TOP_K):
        m = jnp.max(work, axis=0, keepdims=True)
        sel = jnp.min(jnp.where(work == m, e_id, float(n_experts)), axis=0, keepdims=True)
        tops.append(m)
        ids.append(sel)
        work = jnp.where(e_id == sel, -jnp.inf, work)
    ex = [jnp.exp(t - tops[0]) for t in tops]
    denom = ex[0] + ex[1] + ex[2] + ex[3]
    top_e_ref[...] = jnp.concatenate(ids, axis=0).astype(I32)
    gates_ref[...] = jnp.concatenate([e / denom for e in ex], axis=0)


def _post_mix(token0, n, o, bonus, gate, yconv, x2d, gn_w, gn_b, w_out_c, w_out_r, ffn_g, wr_hi,
              wr_lo, b_router, seg2):
    d = x2d.shape[1]
    rw = o.shape[1]
    cw = yconv.shape[1]
    n_experts = wr_hi.shape[0]
    tm = POST_ROWS
    assert token0 % tm == 0 and n % tm == 0
    full = lambda a: pl.BlockSpec(a.shape, lambda i: (0,) * a.ndim)
    in_spec = lambda c: pl.BlockSpec((tm, c), lambda i: (i + token0 // tm, 0))
    row_spec = lambda c: pl.BlockSpec((tm, c), lambda i: (i, 0))
    col_spec = pl.BlockSpec((TOP_K, tm), lambda i: (0, i))
    consts = (gn_w, gn_b, w_out_c, w_out_r, ffn_g, wr_hi, wr_lo, b_router, seg2)
    n_chunks = d // 2 // ROW_CHUNK
    outs = pl.pallas_call(
        functools.partial(_post_mix_kernel, n_experts),
        grid=(n // tm,),
        in_specs=[in_spec(rw), in_spec(rw), in_spec(rw), in_spec(cw), in_spec(d)]
                 + [full(c) for c in consts],
        out_specs=[row_spec(d)] + [row_spec(ROW_CHUNK)] * n_chunks + [col_spec, col_spec],
        out_shape=[jax.ShapeDtypeStruct((n, d), F32)]
                  + [jax.ShapeDtypeStruct((n, ROW_CHUNK), jnp.uint32)] * n_chunks
                  + [jax.ShapeDtypeStruct((TOP_K, n), I32), jax.ShapeDtypeStruct((TOP_K, n), F32)],
        compiler_params=_params("parallel"),
        name="post_mix",
    )(o, bonus, gate, yconv, x2d, *consts)
    return outs[0], outs[1:1 + n_chunks], outs[-2], outs[-1]


def _route_kernel(n_experts, block_rows,
                  top_e_ref, tri_ref, dest_ref, meta_ref,
                  count_ref, start_ref, carry_ref):
    phase = pl.program_id(0)
    j = pl.program_id(1)
    tb = top_e_ref.shape[1]
    e_id = lax.broadcasted_iota(I32, (n_experts, tb), 0)
    top_e = top_e_ref[...]
    onehot = jnp.zeros((n_experts, tb), F32)
    for c in range(TOP_K):
        onehot = onehot + jnp.where(top_e[c:c + 1, :] == e_id, 1.0, 0.0)
    block_count = jnp.sum(onehot, axis=1, keepdims=True)

    @pl.when((phase == 0) & (j == 0))
    def _():
        count_ref[...] = jnp.zeros_like(count_ref)

    @pl.when(phase == 0)
    def _():
        count_ref[...] += block_count

    @pl.when((phase == 1) & (j == 0))
    def _():
        counts = count_ref[...]
        padded = jnp.ceil(counts * (1.0 / block_rows)) * block_rows
        sub = lax.broadcasted_iota(I32, (n_experts, n_experts), 0)
        lane = lax.broadcasted_iota(I32, (n_experts, n_experts), 1)
        padded_row = jnp.sum(jnp.where(sub == lane, padded, 0.0), axis=0, keepdims=True)
        start = jnp.sum(jnp.where(lane < sub, padded_row, 0.0), axis=1, keepdims=True)
        start_ref[...] = start
        carry_ref[...] = jnp.zeros_like(carry_ref)
        end = start + padded
        nb = meta_ref.shape[1]
        slot0 = (lax.broadcasted_iota(I32, (n_experts, nb), 1) * block_rows).astype(F32)
        block_e = jnp.sum(jnp.where(end <= slot0, 1.0, 0.0), axis=0, keepdims=True)
        block_e = jnp.minimum(block_e, n_experts - 1.0)
        used = jnp.max(end, axis=0, keepdims=True) * (1.0 / block_rows)
        in_region = (start <= slot0) & (slot0 < end)
        valid = jnp.sum(jnp.where(in_region, jnp.minimum(start + counts - slot0, block_rows), 0.0),
                        axis=0, keepdims=True)
        row = lax.broadcasted_iota(I32, meta_ref.shape, 0)
        meta_ref[...] = jnp.where(row == META_BLOCK_EXPERT, block_e,
                                  jnp.where(row == META_VALID_ROWS, valid, used)).astype(I32)

    @pl.when(phase == 1)
    def _():
        incl = jnp.dot(onehot.astype(BF16), tri_ref[...], preferred_element_type=F32)
        base = incl - onehot + carry_ref[...] + start_ref[...]
        rows = [jnp.sum(jnp.where(top_e[c:c + 1, :] == e_id, base, 0.0), axis=0, keepdims=True)
                for c in range(TOP_K)]
        dest_ref[...] = jnp.concatenate(rows, axis=0).astype(I32)
        carry_ref[...] += block_count


def _route(top_e, n_experts, n_blocks_padded):
    n = top_e.shape[1]
    tb = ROUTE_TOKENS
    assert n % tb == 0
    tri = (lax.broadcasted_iota(I32, (tb, tb), 0) <= lax.broadcasted_iota(I32, (tb, tb), 1)
           ).astype(BF16)
    return pl.pallas_call(
        functools.partial(_route_kernel, n_experts, EXPERT_ROWS),
        grid=(2, n // tb),
        in_specs=[pl.BlockSpec((TOP_K, tb), lambda ph, j: (0, j)),
                  pl.BlockSpec((tb, tb), lambda ph, j: (0, 0))],
        out_specs=[pl.BlockSpec((TOP_K, tb), lambda ph, j: (0, j * ph)),
                   pl.BlockSpec((V7X_SUBLANES, n_blocks_padded), lambda ph, j: (0, 0))],
        out_shape=[jax.ShapeDtypeStruct((TOP_K, n), I32),
                   jax.ShapeDtypeStruct((V7X_SUBLANES, n_blocks_padded), I32)],
        scratch_shapes=[pltpu.VMEM((n_experts, 1), F32)] * 3,
        compiler_params=_params("arbitrary", "arbitrary"),
        name="route",
    )(top_e, tri)


def _sc_mesh():
    return plsc.VectorSubcoreMesh(core_axis_name="core", subcore_axis_name="subcore")


def _sc_scatter_rows(src, dest, n_slots):
    n, width = src.shape
    assert n % SC_WINDOW == 0

    @functools.partial(pl.kernel, out_type=jax.ShapeDtypeStruct((n_slots, width), src.dtype),
                       mesh=_sc_mesh(), name="sc_scatter_rows")
    def scatter(src_hbm, dest_hbm, out_hbm):
        def body(src_vmem, dest_vmem):
            for c in range(TOP_K):
                pltpu.sync_copy(src_vmem, out_hbm.at[dest_vmem.at[c]])

        pltpu.emit_pipeline(
            body,
            grid=(n // SC_WINDOW,),
            in_specs=[pl.BlockSpec((SC_WINDOW, width), lambda i: (i, 0)),
                      pl.BlockSpec((TOP_K, SC_WINDOW), lambda i: (0, i))],
            out_specs=[],
            core_axis_name=("core", "subcore"),
            dimension_semantics=(pltpu.PARALLEL,),
        )(src_hbm, dest_hbm)

    return scatter(src, dest)


def _experts_kernel(d_ff, n_in, n_out, meta_ref, *refs):
    x_refs, refs = refs[:n_in], refs[n_in:]
    w_gu_mxu, b_gu_ref, w_down_mxu, b_down_ref = refs[:4]
    y_refs = refs[4:]
    j = pl.program_id(0)
    used = meta_ref[META_USED_BLOCKS, 0]

    @pl.when(j < used)
    def _():
        x = _unpack_bf16_pairs(jnp.concatenate([r[...] for r in x_refs], axis=1))
        row = lax.broadcasted_iota(I32, x.shape, 0)
        x = jnp.where(row < meta_ref[META_VALID_ROWS, j], x, 0.0)
        gu = jnp.dot(x.astype(BF16), w_gu_mxu[0], preferred_element_type=F32) + b_gu_ref[0]
        gate = jnp.minimum(gu[:, :d_ff], SWIGLU_LIMIT)
        up = jnp.clip(gu[:, d_ff:], -SWIGLU_LIMIT, SWIGLU_LIMIT)
        act = (up + 1.0) * (gate * _sigmoid(SWIGLU_ALPHA * gate))
        y = jnp.dot(act.astype(BF16), w_down_mxu[0], preferred_element_type=F32) + b_down_ref[0]
        words = _pack_bf16_pairs(y)
        for c, y_ref in enumerate(y_refs):
            y_ref[...] = words[:, c * ROW_CHUNK:(c + 1) * ROW_CHUNK]

    @pl.when(j >= used)
    def _():
        for y_ref in y_refs:
            y_ref[...] = jnp.zeros_like(y_ref)


def _experts(meta, slot_chunks, w_gu, b_gu, w_down, b_down):
    assert w_gu.dtype == BF16 and w_down.dtype == BF16
    n_in = len(slot_chunks)
    n_slots = slot_chunks[0].shape[0]
    n_experts, d, two_ff = w_gu.shape
    n_out = d // 2 // ROW_CHUNK
    d_ff = two_ff // 2
    bm = EXPERT_ROWS
    by_expert = lambda j, m: (m[META_BLOCK_EXPERT, j], 0, 0)
    grid_spec = pltpu.PrefetchScalarGridSpec(
        num_scalar_prefetch=1,
        grid=(n_slots // bm,),
        in_specs=[pl.BlockSpec((bm, ROW_CHUNK),
                               lambda j, m: (jnp.minimum(j, m[META_USED_BLOCKS, 0] - 1), 0))
                  ] * n_in
                 + [pl.BlockSpec((1, d, two_ff), by_expert), pl.BlockSpec((1, 1, two_ff), by_expert),
                    pl.BlockSpec((1, d_ff, d), by_expert), pl.BlockSpec((1, 1, d), by_expert)],
        out_specs=[pl.BlockSpec((bm, ROW_CHUNK), lambda j, m: (j, 0))] * n_out,
    )
    return pl.pallas_call(
        functools.partial(_experts_kernel, d_ff, n_in, n_out),
        grid_spec=grid_spec,
        out_shape=[jax.ShapeDtypeStruct((n_slots, ROW_CHUNK), jnp.uint32)] * n_out,
        compiler_params=_params("arbitrary"),
        name="experts",
    )(meta, *slot_chunks, w_gu, b_gu, w_down, b_down)


def _sc_gather_rows(table, idx_row):
    n_idx = idx_row.shape[1]
    width = table.shape[1]
    assert n_idx % SC_WINDOW == 0

    @functools.partial(pl.kernel, out_type=jax.ShapeDtypeStruct((n_idx, width), table.dtype),
                       mesh=_sc_mesh(), name="sc_gather_rows")
    def gather(table_hbm, idx_hbm, out_hbm):
        def body(idx_vmem, out_vmem):
            pltpu.sync_copy(table_hbm.at[idx_vmem.at[0]], out_vmem)

        pltpu.emit_pipeline(
            body,
            grid=(n_idx // SC_WINDOW,),
            in_specs=[pl.BlockSpec((1, SC_WINDOW), lambda i: (0, i))],
            out_specs=[pl.BlockSpec((SC_WINDOW, width), lambda i: (i, 0))],
            core_axis_name=("core", "subcore"),
            dimension_semantics=(pltpu.PARALLEL,),
        )(idx_hbm, out_hbm)

    return gather(table, idx_row)


def _reduce_kernel(n_chunks, *refs):
    y_refs = refs[:n_chunks]
    x1_ref, gates_ref, g_ref = refs[n_chunks:n_chunks + 3]
    out_ref = refs[-1]
    gates = gates_ref[...]
    acc = x1_ref[...]
    for c in range(TOP_K):
        y = _unpack_bf16_pairs(jnp.concatenate([r[c] for r in y_refs], axis=1))
        acc = acc + y * gates[:, c:c + 1]
    out_ref[...] = _rms_rows(acc, g_ref[...])


def _reduce(token0, n_total, out_so_far, y_chunks, x1, gates_t, final_g):
    n, d = x1.shape
    tb = REDUCE_TOKENS
    n_chunks = len(y_chunks)
    assert token0 % tb == 0
    in_specs = ([pl.BlockSpec((TOP_K, tb, ROW_CHUNK), lambda i: (0, i, 0))] * n_chunks
                + [pl.BlockSpec((tb, d), lambda i: (i, 0)),
                   pl.BlockSpec((tb, TOP_K), lambda i: (i, 0)),
                   pl.BlockSpec((1, d), lambda i: (0, 0))])
    args = [y.reshape(TOP_K, n, ROW_CHUNK) for y in y_chunks] + [x1, gates_t, final_g]
    aliases = {}
    if out_so_far is not None:
        in_specs.append(pl.BlockSpec(memory_space=pl.ANY))
        args.append(out_so_far)
        aliases = {len(args) - 1: 0}
    return pl.pallas_call(
        functools.partial(_reduce_kernel, n_chunks),
        grid=(n // tb,),
        in_specs=in_specs,
        out_specs=pl.BlockSpec((tb, d), lambda i: (i + token0 // tb, 0)),
        out_shape=jax.ShapeDtypeStruct((n_total, d), F32),
        input_output_aliases=aliases,
        compiler_params=_params("parallel"),
        name="reduce",
    )(*args)


def _group_matrix(width):
    a = lax.broadcasted_iota(I32, (width, width), 0) // HEAD_DIM
    b = lax.broadcasted_iota(I32, (width, width), 1) // HEAD_DIM
    g = (a == b).astype(BF16)
    return jnp.concatenate([g, g], axis=0)


def _row(vec):
    return vec.reshape(1, -1).astype(F32)


def kernel(x, w_in, conv_w, conv_norm_g, rwkv_mu, w0, w_up, a0, a_up, g_up, k_k, k_a, r_k,
           gn_w, gn_b, w_out, norm_mix_g, norm_ffn_g, w_router, b_router, w_gu, b_gu, w_down,
           b_down, norm_final_g):
    n_batch, seq, d = x.shape
    n = n_batch * seq
    depth = w_in.shape[0]
    cw = conv_w.shape[2]
    rw = w0.shape[1]
    n_experts = w_router.shape[2]
    decay_lora = w_up.shape[1]
    aaa_lora = a_up.shape[1]
    assert decay_lora + aaa_lora == V7X_LANES
    assert depth == 1
    assert n % MOE_PARTS == 0
    n_part = n // MOE_PARTS
    n_slots = n_part * TOP_K + n_experts * EXPERT_ROWS
    n_blocks = n_slots // EXPERT_ROWS
    n_blocks_padded = -(-n_blocks // V7X_LANES) * V7X_LANES
    seg2 = _group_matrix(rw)

    x2d = x.reshape(n, d)
    for l in range(depth):
        w_up_pad = jnp.concatenate([w_up[l], jnp.zeros((aaa_lora, rw), F32)], axis=0).astype(BF16)
        a_up_pad = jnp.concatenate([jnp.zeros((decay_lora, rw), F32), a_up[l]], axis=0).astype(BF16)
        (yconv, r, lw, k, v, kk, b, gate, bonus) = _mix_prep(
            x2d, seq, _row(norm_mix_g[l]), w_in[l].astype(BF16), conv_w[l].astype(F32),
            _row(conv_norm_g[l]), _row(rwkv_mu[l]), _row(w0[l]), w_up_pad, _row(a0[l]), a_up_pad,
            g_up[l].astype(BF16), _row(k_k[l]), _row(k_a[l]), _row(r_k[l]), seg2)
        shape3 = (n_batch, seq, rw)
        o, (w_gu_b, w_down_b) = _wkv_chunk(
            r.reshape(shape3), lw.reshape(shape3), k.reshape(shape3), v.reshape(shape3),
            kk.reshape(shape3), b.reshape(shape3),
            side_f32=(w_gu[l].astype(F32), w_down[l].astype(F32)))
        o = o.reshape(n, rw)
        w_out_b = w_out[l].astype(BF16)
        wr_t = w_router[l].T.astype(F32)
        wr_hi = wr_t.astype(BF16)
        wr_lo = (wr_t - wr_hi.astype(F32)).astype(BF16)
        expert_args = (w_gu_b, b_gu[l].reshape(n_experts, 1, -1).astype(F32),
                       w_down_b, b_down[l].reshape(n_experts, 1, -1).astype(F32))
        parts = []
        for token0 in range(0, n, n_part):
            x1, h2_chunks, top_e, gates = _post_mix(
                token0, n_part, o, bonus, gate, yconv, x2d, _row(gn_w[l]), _row(gn_b[l]),
                w_out_b[:cw], w_out_b[cw:], _row(norm_ffn_g[l]), wr_hi, wr_lo,
                b_router[l].reshape(n_experts, 1).astype(F32), seg2)
            dest, meta = _route(top_e, n_experts, n_blocks_padded)
            slot_chunks = [_sc_scatter_rows(h, dest, n_slots) for h in h2_chunks]
            parts.append((token0, x1, gates, dest, meta, slot_chunks))
        out = None
        for token0, x1, gates, dest, meta, slot_chunks in parts:
            y_chunks = _experts(meta, slot_chunks, *expert_args)
            idx_row = dest.reshape(1, TOP_K * n_part)
            out = _reduce(token0, n, out, [_sc_gather_rows(y, idx_row) for y in y_chunks], x1,
                          gates.T, _row(norm_final_g))
        x2d = out
    return x2d.reshape(n_batch, seq, d)
```

```python
import functools

import jax
import jax.numpy as jnp
from jax import lax
from jax.experimental import pallas as pl
from jax.experimental.pallas import tpu as pltpu
from jax.experimental.pallas import tpu_sc as plsc

F32 = jnp.float32
BF16 = jnp.bfloat16
I32 = jnp.int32

HEAD_DIM = 64
TOP_K = 4
NORM_EPS = 1e-5
GN_EPS = HEAD_DIM * 1e-5
SWIGLU_LIMIT = 7.0
SWIGLU_ALPHA = 1.702

V7X_LANES = 128
V7X_SUBLANES = 8
V7X_VMEM_LIMIT_BYTES = 56 * 1024 * 1024

PREP_ROWS = 512
POST_ROWS = 1024
SCAN_FRAMES = 64
ROUTE_TOKENS = 1024
REDUCE_TOKENS = 1024
MOE_PARTS = 2
SC_WINDOW = 128
ROW_CHUNK = 256
META_BLOCK_EXPERT = 0
META_USED_BLOCKS = 1
META_VALID_ROWS = 2
EXPERT_ROWS = 512

def _params(*semantics):
    return pltpu.CompilerParams(dimension_semantics=semantics,
                                vmem_limit_bytes=V7X_VMEM_LIMIT_BYTES)


def _split_bf16(x):
    hi = x.astype(BF16)
    lo = (x - hi.astype(F32)).astype(BF16)
    return hi, lo


def _seg_sum(x, seg2):
    hi, lo = _split_bf16(x)
    return jnp.dot(jnp.concatenate([hi, lo], axis=1), seg2, preferred_element_type=F32)


def _seg_sum_nonneg(x, seg2):
    return jnp.dot(x.astype(BF16), seg2[:x.shape[1]], preferred_element_type=F32)


def _pack_bf16_pairs(x):
    bits = pltpu.bitcast(x.astype(BF16).astype(F32), jnp.uint32)
    half = x.shape[1] // 2
    return bits[:, half:] | (bits[:, :half] >> 16)


def _unpack_bf16_pairs(words):
    return jnp.concatenate([pltpu.bitcast(words << 16, F32),
                            pltpu.bitcast(words & jnp.uint32(0xFFFF0000), F32)], axis=1)


def _rms_rows(x, g):
    return x * lax.rsqrt(jnp.mean(x * x, axis=-1, keepdims=True) + NORM_EPS) * g


def _sigmoid(x):
    return 1.0 / (1.0 + jnp.exp(-x))


def _mix_prep_kernel(blocks_per_seq, cw, rw,
                     xprev_ref, x_ref, g_ref, w_in_ref, conv_w_ref, conv_g_ref, mu_ref,
                     w0_ref, w_up_ref, a0_ref, a_up_ref, g_up_ref, k_k_ref, k_a_ref, r_k_ref,
                     seg2_ref,
                     yconv_ref, r_ref, w_ref, k_ref, v_ref, kk_ref, b_ref, gate_ref, bonus_ref,
                     p_scr, z_scr):
    tm = x_ref.shape[0]
    halo = xprev_ref.shape[0]
    first = (pl.program_id(0) % blocks_per_seq) == 0
    xp = xprev_ref[...] * jnp.where(first, 0.0, 1.0)
    xa = jnp.concatenate([xp, x_ref[...]], axis=0)
    h = _rms_rows(xa, g_ref[...])
    p_scr[...] = jnp.dot(h.astype(BF16), w_in_ref[...], preferred_element_type=F32)
    seg2 = seg2_ref[...]

    z_scr[...] = p_scr[:, 2 * cw:3 * cw] * p_scr[:, 0:cw]
    conv = (conv_w_ref[0:1, :] * z_scr[halo - 2:halo - 2 + tm, :]
            + conv_w_ref[1:2, :] * z_scr[halo - 1:halo - 1 + tm, :]
            + conv_w_ref[2:3, :] * z_scr[halo:halo + tm, :])
    y = p_scr[halo:halo + tm, cw:2 * cw] * conv
    ms = _seg_sum_nonneg(y * y, seg2) * (1.0 / HEAD_DIM)
    yconv_ref[...] = (y * lax.rsqrt(ms + NORM_EPS) * conv_g_ref[...]).astype(yconv_ref.dtype)

    c0 = 3 * cw
    cur = p_scr[halo:halo + tm, c0:]
    prev = p_scr[halo - 1:halo - 1 + tm, c0:]
    q = cur + (prev - cur) * mu_ref[...]
    r = q[:, 0:rw]
    k = q[:, rw:2 * rw]
    v = q[:, 2 * rw:3 * rw]
    lora_wa = q[:, 3 * rw:3 * rw + V7X_LANES]
    lora_g = q[:, 3 * rw + V7X_LANES:]
    w_lin = w0_ref[...] + jnp.dot(jnp.tanh(lora_wa).astype(BF16), w_up_ref[...],
                                  preferred_element_type=F32)
    neg = -w_lin
    softplus = jnp.maximum(neg, 0.0) + jnp.log(1.0 + jnp.exp(-jnp.abs(neg)))
    log_decay = -jnp.exp(-softplus - 0.5)
    a = _sigmoid(a0_ref[...] + jnp.dot(lora_wa.astype(BF16), a_up_ref[...],
                                       preferred_element_type=F32))
    gate = jnp.dot(_sigmoid(lora_g).astype(BF16), g_up_ref[...], preferred_element_type=F32)
    kk = k * k_k_ref[...]
    kk = kk / jnp.maximum(jnp.sqrt(_seg_sum_nonneg(kk * kk, seg2)), 1e-12)
    k_mod = k * (1.0 + (a - 1.0) * k_a_ref[...])
    bonus = _seg_sum(r * k_mod * r_k_ref[...], seg2) * v
    r_ref[...] = r
    w_ref[...] = log_decay
    k_ref[...] = k_mod
    v_ref[...] = v
    kk_ref[...] = kk
    b_ref[...] = kk * a
    gate_ref[...] = gate
    bonus_ref[...] = bonus


def _mix_prep(x2d, seq, norm_g, w_in, conv_w, conv_g, mu, w0, w_up, a0, a_up, g_up, k_k, k_a,
              r_k, seg2):
    n, d = x2d.shape
    tm = PREP_ROWS
    halo = V7X_SUBLANES
    cw = conv_w.shape[1]
    rw = w0.shape[1]
    in_cols = w_in.shape[1]
    assert seq % tm == 0 and n % tm == 0
    full = lambda a: pl.BlockSpec(a.shape, lambda i: (0,) * a.ndim)
    row_spec = lambda c: pl.BlockSpec((tm, c), lambda i: (i, 0))
    consts = (norm_g, w_in, conv_w, conv_g, mu, w0, w_up, a0, a_up, g_up, k_k, k_a, r_k, seg2)
    outs = [jax.ShapeDtypeStruct((n, cw), BF16)] + [jax.ShapeDtypeStruct((n, rw), F32)] * 8
    return pl.pallas_call(
        functools.partial(_mix_prep_kernel, seq // tm, cw, rw),
        grid=(n // tm,),
        in_specs=[pl.BlockSpec((halo, d), lambda i: (jnp.maximum(i * (tm // halo) - 1, 0), 0)),
                  row_spec(d)] + [full(c) for c in consts],
        out_specs=[row_spec(cw)] + [row_spec(rw)] * 8,
        out_shape=outs,
        scratch_shapes=[pltpu.VMEM((halo + tm, in_cols), F32), pltpu.VMEM((halo + tm, cw), F32)],
        compiler_params=_params("parallel"),
        name="mix_prep",
    )(x2d, x2d, *consts)


def _bdot(a, b):
    return jnp.dot(a.astype(BF16), b.astype(BF16), preferred_element_type=F32)


def _bdot_nt(a, b):
    return lax.dot_general(a.astype(BF16), b.astype(BF16), (((1,), (1,)), ((), ())),
                           preferred_element_type=F32)


def _wkv_chunk_kernel(n_batch, n_pairs, n_side,
                      r_ref, lw_ref, k_ref, v_ref, kk_ref, b_ref, *refs):
    frames = r_ref.shape[1]
    lanes = V7X_LANES
    hd = HEAD_DIM
    assert frames == hd
    side_in, o_ref, side_out, state_ref = (refs[:n_side], refs[n_side],
                                           refs[n_side + 1:2 * n_side + 1], refs[-1])
    for src, dst in zip(side_in, side_out):
        dst[...] = src[...].astype(dst.dtype)

    @pl.when(pl.program_id(0) == 0)
    def _():
        state_ref[...] = jnp.zeros_like(state_ref)

    row = lax.broadcasted_iota(I32, (lanes, lanes), 0)
    lane = lax.broadcasted_iota(I32, (lanes, lanes), 1)
    same_head = (row // hd) == (lane // hd)
    rt = row % hd
    ls = lane % hd
    strict_same = same_head & (rt > ls)
    strict_cross = jnp.logical_not(same_head) & (rt > ls)
    incl_same = same_head & (rt >= ls)
    incl_cross = jnp.logical_not(same_head) & (rt >= ls)
    eye = row == lane
    level_masks = []
    m = 1
    while m < hd:
        level_masks.append(same_head & ((rt // (2 * m)) == (ls // (2 * m)))
                           & (((rt // m) % 2) == 1) & (((ls // m) % 2) == 0))
        m *= 2
    left = lax.broadcasted_iota(I32, (frames, lanes), 1) < hd
    tri = (lax.broadcasted_iota(I32, (frames, frames), 0)
           >= lax.broadcasted_iota(I32, (frames, frames), 1)).astype(BF16)

    def diag_blocks(x):
        return jnp.concatenate([jnp.where(left, x, 0.0), jnp.where(left, 0.0, x)], axis=0)

    def cross_blocks(x):
        return jnp.concatenate([jnp.where(left, 0.0, x), jnp.where(left, x, 0.0)], axis=0)

    def batch_prep(bi):
        lw = lw_ref[bi]
        hi = lw.astype(BF16)
        rem = lw - hi.astype(F32)
        mid = rem.astype(BF16)
        lo = (rem - mid.astype(F32)).astype(BF16)
        cs = (jnp.dot(tri, hi, preferred_element_type=F32)
              + jnp.dot(tri, mid, preferred_element_type=F32)
              + jnp.dot(tri, lo, preferred_element_type=F32))
        cs_end = cs[frames - 1:frames, :]
        e_neg = jnp.exp(-cs)
        e_end = jnp.exp(cs_end - cs)
        b_in = b_ref[bi]
        k_in = k_ref[bi]
        return dict(a_t=-kk_ref[bi] * jnp.exp(cs - lw), b_t=b_in * e_neg, k_t=k_in * e_neg,
                    r_t=r_ref[bi] * jnp.exp(cs), b_h=b_in * e_end, k_h=k_in * e_end,
                    v=v_ref[bi], g_end=jnp.exp(cs_end))

    group = []
    for bi in range(n_batch):
        prep = batch_prep(bi)
        for pi in range(n_pairs):
            sl = slice(pi * lanes, (pi + 1) * lanes)
            group.append((bi, sl, bi * n_pairs + pi, {n: x[:, sl] for n, x in prep.items()}))
    bf = lambda x: x.astype(BF16)
    a_bd = [bf(diag_blocks(p["a_t"])) for _, _, _, p in group]
    r_bd = [bf(diag_blocks(p["r_t"])) for _, _, _, p in group]
    d0 = [_bdot_nt(jnp.concatenate([a[:hd], r[:hd]], axis=0),
                   bf(jnp.concatenate([p["b_t"], p["k_t"]], axis=0)))
          for a, r, (_, _, _, p) in zip(a_bd, r_bd, group)]
    d1 = [_bdot_nt(jnp.concatenate([a[hd:], r[hd:]], axis=0),
                   bf(jnp.concatenate([p["k_t"], p["b_t"]], axis=0)))
          for a, r, (_, _, _, p) in zip(a_bd, r_bd, group)]
    a_rows = [jnp.concatenate([x[:hd], y[:hd]], axis=0) for x, y in zip(d0, d1)]
    m_rows = [jnp.concatenate([x[hd:], y[hd:]], axis=0) for x, y in zip(d0, d1)]
    a_ab = [bf(jnp.where(strict_same, x, 0.0)) for x in a_rows]
    a_ak = [bf(jnp.where(strict_cross, x, 0.0)) for x in a_rows]
    m_rb = [bf(jnp.where(incl_same, x, 0.0)) for x in m_rows]
    m_rk = [bf(jnp.where(incl_cross, x, 0.0)) for x in m_rows]
    t_inv = [jnp.where(eye, 1.0, jnp.where(level_masks[0], x, 0.0)) for x in a_rows]
    for mask in level_masks[1:]:
        t_bf = [bf(t) for t in t_inv]
        half = [jnp.where(mask, _bdot(t, x), 0.0) for t, x in zip(t_bf, a_ab)]
        t_inv = [t + _bdot(h, tb) for t, h, tb in zip(t_inv, half, t_bf)]
    t_bf = [bf(t) for t in t_inv]
    v_x = [bf(cross_blocks(p["v"])) for _, _, _, p in group]
    akv = [_bdot(x, v) for x, v in zip(a_ak, v_x)]
    wu = [bf(_bdot(t, jnp.concatenate([a, bf(x)], axis=1)))
          for t, a, x in zip(t_bf, a_bd, akv)]
    bh_t = [bf(diag_blocks(p["b_h"]).T) for _, _, _, p in group]
    kh_t = [bf(cross_blocks(p["k_h"]).T) for _, _, _, p in group]
    pw_rw = [_bdot(jnp.concatenate([bt, m], axis=0), x[:, :lanes])
             for bt, m, x in zip(bh_t, m_rb, wu)]
    q_o = [_bdot(jnp.concatenate([jnp.concatenate([bt, kt], axis=1),
                                  jnp.concatenate([mb, mk], axis=1)], axis=0),
                 jnp.concatenate([x[:, lanes:], v], axis=0))
           for bt, kt, mb, mk, x, v in zip(bh_t, kh_t, m_rb, m_rk, wu, v_x)]
    res = [_bdot(jnp.concatenate([diag_blocks(p["r_t"]) + pr[lanes:],
                                  pr[:lanes] + jnp.where(eye, p["g_end"], 0.0)], axis=0),
                 state_ref[idx])
           for pr, (_, _, idx, p) in zip(pw_rw, group)]
    for x, q, (bi, sl, idx, _) in zip(res, q_o, group):
        o_bd = x[:lanes] + q[lanes:]
        state_ref[idx] = x[lanes:] + q[:lanes]
        o_ref[bi, :, sl] = jnp.where(left, o_bd[:hd], o_bd[hd:])


def _wkv_chunk(r, lw, k, v, kk, b, side_f32=()):
    n_batch, seq, rw = r.shape
    n_pairs = rw // V7X_LANES
    frames = SCAN_FRAMES
    assert seq % frames == 0
    n_steps = seq // frames
    spec = pl.BlockSpec((n_batch, frames, rw), lambda c: (0, c, 0))
    sliced = [a.reshape(n_steps, -1, a.shape[-1]) for a in side_f32]
    side_specs = [pl.BlockSpec((1,) + a.shape[1:], lambda c: (c, 0, 0)) for a in sliced]
    outs = pl.pallas_call(
        functools.partial(_wkv_chunk_kernel, n_batch, n_pairs, len(sliced)),
        grid=(n_steps,),
        in_specs=[spec] * 6 + side_specs,
        out_specs=[spec] + side_specs,
        out_shape=[jax.ShapeDtypeStruct((n_batch, seq, rw), F32)]
                  + [jax.ShapeDtypeStruct(a.shape, BF16) for a in sliced],
        scratch_shapes=[pltpu.VMEM((n_batch * n_pairs, V7X_LANES, V7X_LANES), F32)],
        compiler_params=_params("arbitrary"),
        name="wkv_chunk",
    )(r, lw, k, v, kk, b, *sliced)
    return outs[0], [y.reshape(a.shape) for y, a in zip(outs[1:], side_f32)]


def _post_mix_kernel(n_experts,
                     o_ref, bonus_ref, gate_ref, yconv_ref, x_ref, gn_w_ref, gn_b_ref,
                     w_out_c_ref, w_out_r_ref, ffn_g_ref, wr_hi_ref, wr_lo_ref, b_router_ref,
                     seg2_ref,
                     x1_ref, *outs):
    h2_refs, (top_e_ref, gates_ref) = outs[:-2], outs[-2:]
    seg2 = seg2_ref[...]
    o = o_ref[...]
    mean = _seg_sum(o, seg2) * (1.0 / HEAD_DIM)
    cen = o - mean
    var = _seg_sum_nonneg(cen * cen, seg2) * (1.0 / HEAD_DIM)
    o = cen * lax.rsqrt(var + GN_EPS) * gn_w_ref[...] + gn_b_ref[...]
    y_rwkv = (o + bonus_ref[...]) * gate_ref[...]
    x1 = (x_ref[...]
          + jnp.dot(yconv_ref[...], w_out_c_ref[...], preferred_element_type=F32)
          + jnp.dot(y_rwkv.astype(BF16), w_out_r_ref[...], preferred_element_type=F32))
    x1_ref[...] = x1
    h2 = _rms_rows(x1, ffn_g_ref[...])
    words = _pack_bf16_pairs(h2)
    for c, h2_ref in enumerate(h2_refs):
        h2_ref[...] = words[:, c * ROW_CHUNK:(c + 1) * ROW_CHUNK]

    h_hi, h_lo = _split_bf16(h2)
    nt = (((1,), (1,)), ((), ()))
    logits = (lax.dot_general(wr_hi_ref[...], h_hi, nt, preferred_element_type=F32)
              + lax.dot_general(wr_hi_ref[...], h_lo, nt, preferred_element_type=F32)
              + lax.dot_general(wr_lo_ref[...], h_hi, nt, preferred_element_type=F32)
              + b_router_ref[...])
    e_id = lax.broadcasted_iota(I32, logits.shape, 0).astype(F32)
    work = logits
    tops, ids = [], []
    for _ in range(TOP_K):
        m = jnp.max(work, axis=0, keepdims=True)
        sel = jnp.min(jnp.where(work == m, e_id, float(n_experts)), axis=0, keepdims=True)
        tops.append(m)
        ids.append(sel)
        work = jnp.where(e_id == sel, -jnp.inf, work)
    ex = [jnp.exp(t - tops[0]) for t in tops]
    denom = ex[0] + ex[1] + ex[2] + ex[3]
    top_e_ref[...] = jnp.concatenate(ids, axis=0).astype(I32)
    gates_ref[...] = jnp.concatenate([e / denom for e in ex], axis=0)


def _post_mix(token0, n, o, bonus, gate, yconv, x2d, gn_w, gn_b, w_out_c, w_out_r, ffn_g, wr_hi,
              wr_lo, b_router, seg2):
    d = x2d.shape[1]
    rw = o.shape[1]
    cw = yconv.shape[1]
    n_experts = wr_hi.shape[0]
    tm = POST_ROWS
    assert token0 % tm == 0 and n % tm == 0
    full = lambda a: pl.BlockSpec(a.shape, lambda i: (0,) * a.ndim)
    in_spec = lambda c: pl.BlockSpec((tm, c), lambda i: (i + token0 // tm, 0))
    row_spec = lambda c: pl.BlockSpec((tm, c), lambda i: (i, 0))
    col_spec = pl.BlockSpec((TOP_K, tm), lambda i: (0, i))
    consts = (gn_w, gn_b, w_out_c, w_out_r, ffn_g, wr_hi, wr_lo, b_router, seg2)
    n_chunks = d // 2 // ROW_CHUNK
    outs = pl.pallas_call(
        functools.partial(_post_mix_kernel, n_experts),
        grid=(n // tm,),
        in_specs=[in_spec(rw), in_spec(rw), in_spec(rw), in_spec(cw), in_spec(d)]
                 + [full(c) for c in consts],
        out_specs=[row_spec(d)] + [row_spec(ROW_CHUNK)] * n_chunks + [col_spec, col_spec],
        out_shape=[jax.ShapeDtypeStruct((n, d), F32)]
                  + [jax.ShapeDtypeStruct((n, ROW_CHUNK), jnp.uint32)] * n_chunks
                  + [jax.ShapeDtypeStruct((TOP_K, n), I32), jax.ShapeDtypeStruct((TOP_K, n), F32)],
        compiler_params=_params("parallel"),
        name="post_mix",
    )(o, bonus, gate, yconv, x2d, *consts)
    return outs[0], outs[1:1 + n_chunks], outs[-2], outs[-1]


def _route_kernel(n_experts, block_rows,
                  top_e_ref, tri_ref, dest_ref, meta_ref,
                  count_ref, start_ref, carry_ref):
    phase = pl.program_id(0)
    j = pl.program_id(1)
    tb = top_e_ref.shape[1]
    e_id = lax.broadcasted_iota(I32, (n_experts, tb), 0)
    top_e = top_e_ref[...]
    onehot = jnp.zeros((n_experts, tb), F32)
    for c in range(TOP_K):
        onehot = onehot + jnp.where(top_e[c:c + 1, :] == e_id, 1.0, 0.0)
    block_count = jnp.sum(onehot, axis=1, keepdims=True)

    @pl.when((phase == 0) & (j == 0))
    def _():
        count_ref[...] = jnp.zeros_like(count_ref)

    @pl.when(phase == 0)
    def _():
        count_ref[...] += block_count

    @pl.when((phase == 1) & (j == 0))
    def _():
        counts = count_ref[...]
        padded = jnp.ceil(counts * (1.0 / block_rows)) * block_rows
        sub = lax.broadcasted_iota(I32, (n_experts, n_experts), 0)
        lane = lax.broadcasted_iota(I32, (n_experts, n_experts), 1)
        padded_row = jnp.sum(jnp.where(sub == lane, padded, 0.0), axis=0, keepdims=True)
        start = jnp.sum(jnp.where(lane < sub, padded_row, 0.0), axis=1, keepdims=True)
        start_ref[...] = start
        carry_ref[...] = jnp.zeros_like(carry_ref)
        end = start + padded
        nb = meta_ref.shape[1]
        slot0 = (lax.broadcasted_iota(I32, (n_experts, nb), 1) * block_rows).astype(F32)
        block_e = jnp.sum(jnp.where(end <= slot0, 1.0, 0.0), axis=0, keepdims=True)
        block_e = jnp.minimum(block_e, n_experts - 1.0)
        used = jnp.max(end, axis=0, keepdims=True) * (1.0 / block_rows)
        in_region = (start <= slot0) & (slot0 < end)
        valid = jnp.sum(jnp.where(in_region, jnp.minimum(start + counts - slot0, block_rows), 0.0),
                        axis=0, keepdims=True)
        row = lax.broadcasted_iota(I32, meta_ref.shape, 0)
        meta_ref[...] = jnp.where(row == META_BLOCK_EXPERT, block_e,
                                  jnp.where(row == META_VALID_ROWS, valid, used)).astype(I32)

    @pl.when(phase == 1)
    def _():
        incl = jnp.dot(onehot.astype(BF16), tri_ref[...], preferred_element_type=F32)
        base = incl - onehot + carry_ref[...] + start_ref[...]
        rows = [jnp.sum(jnp.where(top_e[c:c + 1, :] == e_id, base, 0.0), axis=0, keepdims=True)
                for c in range(TOP_K)]
        dest_ref[...] = jnp.concatenate(rows, axis=0).astype(I32)
        carry_ref[...] += block_count


def _route(top_e, n_experts, n_blocks_padded):
    n = top_e.shape[1]
    tb = ROUTE_TOKENS
    assert n % tb == 0
    tri = (lax.broadcasted_iota(I32, (tb, tb), 0) <= lax.broadcasted_iota(I32, (tb, tb), 1)
           ).astype(BF16)
    return pl.pallas_call(
        functools.partial(_route_kernel, n_experts, EXPERT_ROWS),
        grid=(2, n // tb),
        in_specs=[pl.BlockSpec((TOP_K, tb), lambda ph, j: (0, j)),
                  pl.BlockSpec((tb, tb), lambda ph, j: (0, 0))],
        out_specs=[pl.BlockSpec((TOP_K, tb), lambda ph, j: (0, j * ph)),
                   pl.BlockSpec((V7X_SUBLANES, n_blocks_padded), lambda ph, j: (0, 0))],
        out_shape=[jax.ShapeDtypeStruct((TOP_K, n), I32),
                   jax.ShapeDtypeStruct((V7X_SUBLANES, n_blocks_padded), I32)],
        scratch_shapes=[pltpu.VMEM((n_experts, 1), F32)] * 3,
        compiler_params=_params("arbitrary", "arbitrary"),
        name="route",
    )(top_e, tri)


def _sc_mesh():
    return plsc.VectorSubcoreMesh(core_axis_name="core", subcore_axis_name="subcore")


def _sc_scatter_rows(src, dest, n_slots):
    n, width = src.shape
    assert n % SC_WINDOW == 0

    @functools.partial(pl.kernel, out_type=jax.ShapeDtypeStruct((n_slots, width), src.dtype),
                       mesh=_sc_mesh(), name="sc_scatter_rows")
    def scatter(src_hbm, dest_hbm, out_hbm):
        def body(src_vmem, dest_vmem):
            for c in range(TOP_K):
                pltpu.sync_copy(src_vmem, out_hbm.at[dest_vmem.at[c]])

        pltpu.emit_pipeline(
            body,
            grid=(n // SC_WINDOW,),
            in_specs=[pl.BlockSpec((SC_WINDOW, width), lambda i: (i, 0)),
                      pl.BlockSpec((TOP_K, SC_WINDOW), lambda i: (0, i))],
            out_specs=[],
            core_axis_name=("core", "subcore"),
            dimension_semantics=(pltpu.PARALLEL,),
        )(src_hbm, dest_hbm)

    return scatter(src, dest)


def _experts_kernel(d_ff, n_in, n_out, meta_ref, *refs):
    x_refs, refs = refs[:n_in], refs[n_in:]
    w_gu_mxu, b_gu_ref, w_down_mxu, b_down_ref = refs[:4]
    y_refs = refs[4:]
    j = pl.program_id(0)
    used = meta_ref[META_USED_BLOCKS, 0]

    @pl.when(j < used)
    def _():
        x = _unpack_bf16_pairs(jnp.concatenate([r[...] for r in x_refs], axis=1))
        row = lax.broadcasted_iota(I32, x.shape, 0)
        x = jnp.where(row < meta_ref[META_VALID_ROWS, j], x, 0.0)
        gu = jnp.dot(x.astype(BF16), w_gu_mxu[0], preferred_element_type=F32) + b_gu_ref[0]
        gate = jnp.minimum(gu[:, :d_ff], SWIGLU_LIMIT)
        up = jnp.clip(gu[:, d_ff:], -SWIGLU_LIMIT, SWIGLU_LIMIT)
        act = (up + 1.0) * (gate * _sigmoid(SWIGLU_ALPHA * gate))
        y = jnp.dot(act.astype(BF16), w_down_mxu[0], preferred_element_type=F32) + b_down_ref[0]
        words = _pack_bf16_pairs(y)
        for c, y_ref in enumerate(y_refs):
            y_ref[...] = words[:, c * ROW_CHUNK:(c + 1) * ROW_CHUNK]

    @pl.when(j >= used)
    def _():
        for y_ref in y_refs:
            y_ref[...] = jnp.zeros_like(y_ref)


def _experts(meta, slot_chunks, w_gu, b_gu, w_down, b_down):
    assert w_gu.dtype == BF16 and w_down.dtype == BF16
    n_in = len(slot_chunks)
    n_slots = slot_chunks[0].shape[0]
    n_experts, d, two_ff = w_gu.shape
    n_out = d // 2 // ROW_CHUNK
    d_ff = two_ff // 2
    bm = EXPERT_ROWS
    by_expert = lambda j, m: (m[META_BLOCK_EXPERT, j], 0, 0)
    grid_spec = pltpu.PrefetchScalarGridSpec(
        num_scalar_prefetch=1,
        grid=(n_slots // bm,),
        in_specs=[pl.BlockSpec((bm, ROW_CHUNK),
                               lambda j, m: (jnp.minimum(j, m[META_USED_BLOCKS, 0] - 1), 0))
                  ] * n_in
                 + [pl.BlockSpec((1, d, two_ff), by_expert), pl.BlockSpec((1, 1, two_ff), by_expert),
                    pl.BlockSpec((1, d_ff, d), by_expert), pl.BlockSpec((1, 1, d), by_expert)],
        out_specs=[pl.BlockSpec((bm, ROW_CHUNK), lambda j, m: (j, 0))] * n_out,
    )
    return pl.pallas_call(
        functools.partial(_experts_kernel, d_ff, n_in, n_out),
        grid_spec=grid_spec,
        out_shape=[jax.ShapeDtypeStruct((n_slots, ROW_CHUNK), jnp.uint32)] * n_out,
        compiler_params=_params("arbitrary"),
        name="experts",
    )(meta, *slot_chunks, w_gu, b_gu, w_down, b_down)


def _sc_gather_rows(table, idx_row):
    n_idx = idx_row.shape[1]
    width = table.shape[1]
    assert n_idx % SC_WINDOW == 0

    @functools.partial(pl.kernel, out_type=jax.ShapeDtypeStruct((n_idx, width), table.dtype),
                       mesh=_sc_mesh(), name="sc_gather_rows")
    def gather(table_hbm, idx_hbm, out_hbm):
        def body(idx_vmem, out_vmem):
            pltpu.sync_copy(table_hbm.at[idx_vmem.at[0]], out_vmem)

        pltpu.emit_pipeline(
            body,
            grid=(n_idx // SC_WINDOW,),
            in_specs=[pl.BlockSpec((1, SC_WINDOW), lambda i: (0, i))],
            out_specs=[pl.BlockSpec((SC_WINDOW, width), lambda i: (i, 0))],
            core_axis_name=("core", "subcore"),
            dimension_semantics=(pltpu.PARALLEL,),
        )(idx_hbm, out_hbm)

    return gather(table, idx_row)


def _reduce_kernel(n_chunks, *refs):
    y_refs = refs[:n_chunks]
    x1_ref, gates_ref, g_ref = refs[n_chunks:n_chunks + 3]
    out_ref = refs[-1]
    gates = gates_ref[...]
    acc = x1_ref[...]
    for c in range(TOP_K):
        y = _unpack_bf16_pairs(jnp.concatenate([r[c] for r in y_refs], axis=1))
        acc = acc + y * gates[:, c:c + 1]
    out_ref[...] = _rms_rows(acc, g_ref[...])


def _reduce(token0, n_total, out_so_far, y_chunks, x1, gates_t, final_g):
    n, d = x1.shape
    tb = REDUCE_TOKENS
    n_chunks = len(y_chunks)
    assert token0 % tb == 0
    in_specs = ([pl.BlockSpec((TOP_K, tb, ROW_CHUNK), lambda i: (0, i, 0))] * n_chunks
                + [pl.BlockSpec((tb, d), lambda i: (i, 0)),
                   pl.BlockSpec((tb, TOP_K), lambda i: (i, 0)),
                   pl.BlockSpec((1, d), lambda i: (0, 0))])
    args = [y.reshape(TOP_K, n, ROW_CHUNK) for y in y_chunks] + [x1, gates_t, final_g]
    aliases = {}
    if out_so_far is not None:
        in_specs.append(pl.BlockSpec(memory_space=pl.ANY))
        args.append(out_so_far)
        aliases = {len(args) - 1: 0}
    return pl.pallas_call(
        functools.partial(_reduce_kernel, n_chunks),
        grid=(n // tb,),
        in_specs=in_specs,
        out_specs=pl.BlockSpec((tb, d), lambda i: (i + token0 // tb, 0)),
        out_shape=jax.ShapeDtypeStruct((n_total, d), F32),
        input_output_aliases=aliases,
        compiler_params=_params("parallel"),
        name="reduce",
    )(*args)


def _group_matrix(width):
    a = lax.broadcasted_iota(I32, (width, width), 0) // HEAD_DIM
    b = lax.broadcasted_iota(I32, (width, width), 1) // HEAD_DIM
    g = (a == b).astype(BF16)
    return jnp.concatenate([g, g], axis=0)


def _row(vec):
    return vec.reshape(1, -1).astype(F32)


def kernel(x, w_in, conv_w, conv_norm_g, rwkv_mu, w0, w_up, a0, a_up, g_up, k_k, k_a, r_k,
           gn_w, gn_b, w_out, norm_mix_g, norm_ffn_g, w_router, b_router, w_gu, b_gu, w_down,
           b_down, norm_final_g):
    n_batch, seq, d = x.shape
    n = n_batch * seq
    depth = w_in.shape[0]
    cw = conv_w.shape[2]
    rw = w0.shape[1]
    n_experts = w_router.shape[2]
    decay_lora = w_up.shape[1]
    aaa_lora = a_up.shape[1]
    assert decay_lora + aaa_lora == V7X_LANES
    assert depth == 1
    assert n % MOE_PARTS == 0
    n_part = n // MOE_PARTS
    n_slots = n_part * TOP_K + n_experts * EXPERT_ROWS
    n_blocks = n_slots // EXPERT_ROWS
    n_blocks_padded = -(-n_blocks // V7X_LANES) * V7X_LANES
    seg2 = _group_matrix(rw)

    x2d = x.reshape(n, d)
    for l in range(depth):
        w_up_pad = jnp.concatenate([w_up[l], jnp.zeros((aaa_lora, rw), F32)], axis=0).astype(BF16)
        a_up_pad = jnp.concatenate([jnp.zeros((decay_lora, rw), F32), a_up[l]], axis=0).astype(BF16)
        (yconv, r, lw, k, v, kk, b, gate, bonus) = _mix_prep(
            x2d, seq, _row(norm_mix_g[l]), w_in[l].astype(BF16), conv_w[l].astype(F32),
            _row(conv_norm_g[l]), _row(rwkv_mu[l]), _row(w0[l]), w_up_pad, _row(a0[l]), a_up_pad,
            g_up[l].astype(BF16), _row(k_k[l]), _row(k_a[l]), _row(r_k[l]), seg2)
        shape3 = (n_batch, seq, rw)
        o, (w_gu_b, w_down_b) = _wkv_chunk(
            r.reshape(shape3), lw.reshape(shape3), k.reshape(shape3), v.reshape(shape3),
            kk.reshape(shape3), b.reshape(shape3),
            side_f32=(w_gu[l].astype(F32), w_down[l].astype(F32)))
        o = o.reshape(n, rw)
        w_out_b = w_out[l].astype(BF16)
        wr_t = w_router[l].T.astype(F32)
        wr_hi = wr_t.astype(BF16)
        wr_lo = (wr_t - wr_hi.astype(F32)).astype(BF16)
        expert_args = (w_gu_b, b_gu[l].reshape(n_experts, 1, -1).astype(F32),
                       w_down_b, b_down[l].reshape(n_experts, 1, -1).astype(F32))
        parts = []
        for token0 in range(0, n, n_part):
            x1, h2_chunks, top_e, gates = _post_mix(
                token0, n_part, o, bonus, gate, yconv, x2d, _row(gn_w[l]), _row(gn_b[l]),
                w_out_b[:cw], w_out_b[cw:], _row(norm_ffn_g[l]), wr_hi, wr_lo,
                b_router[l].reshape(n_experts, 1).astype(F32), seg2)
            dest, meta = _route(top_e, n_experts, n_blocks_padded)
            slot_chunks = [_sc_scatter_rows(h, dest, n_slots) for h in h2_chunks]
            parts.append((token0, x1, gates, dest, meta, slot_chunks))
        out = None
        for token0, x1, gates, dest, meta, slot_chunks in parts:
            y_chunks = _experts(meta, slot_chunks, *expert_args)
            idx_row = dest.reshape(1, TOP_K * n_part)
            out = _reduce(token0, n, out, [_sc_gather_rows(y, idx_row) for y in y_chunks], x1,
                          gates.T, _row(norm_final_g))
        x2d = out
    return x2d.reshape(n_batch, seq, d)
```

```python
import functools

import jax
import jax.numpy as jnp
from jax import lax
from jax.experimental import pallas as pl
from jax.experimental.pallas import tpu as pltpu
from jax.experimental.pallas import tpu_sc as plsc

F32 = jnp.float32
BF16 = jnp.bfloat16
I32 = jnp.int32

HEAD_DIM = 64
TOP_K = 4
NORM_EPS = 1e-5
GN_EPS = HEAD_DIM * 1e-5
SWIGLU_LIMIT = 7.0
SWIGLU_ALPHA = 1.702

V7X_LANES = 128
V7X_SUBLANES = 8
V7X_VMEM_LIMIT_BYTES = 56 * 1024 * 1024

PREP_ROWS = 512
POST_ROWS = 1024
SCAN_FRAMES = 64
ROUTE_TOKENS = 1024
REDUCE_TOKENS = 1024
MOE_PARTS = 2
SC_WINDOW = 128
ROW_CHUNK = 256
META_BLOCK_EXPERT = 0
META_USED_BLOCKS = 1
META_VALID_ROWS = 2
EXPERT_ROWS = 512

def _params(*semantics):
    return pltpu.CompilerParams(dimension_semantics=semantics,
                                vmem_limit_bytes=V7X_VMEM_LIMIT_BYTES)


def _split_bf16(x):
    hi = x.astype(BF16)
    lo = (x - hi.astype(F32)).astype(BF16)
    return hi, lo


def _seg_sum(x, seg2):
    hi, lo = _split_bf16(x)
    return jnp.dot(jnp.concatenate([hi, lo], axis=1), seg2, preferred_element_type=F32)


def _seg_sum_nonneg(x, seg2):
    return jnp.dot(x.astype(BF16), seg2[:x.shape[1]], preferred_element_type=F32)


def _pack_bf16_pairs(x):
    bits = pltpu.bitcast(x.astype(BF16).astype(F32), jnp.uint32)
    half = x.shape[1] // 2
    return bits[:, half:] | (bits[:, :half] >> 16)


def _unpack_bf16_pairs(words):
    return jnp.concatenate([pltpu.bitcast(words << 16, F32),
                            pltpu.bitcast(words & jnp.uint32(0xFFFF0000), F32)], axis=1)


def _rms_rows(x, g):
    return x * lax.rsqrt(jnp.mean(x * x, axis=-1, keepdims=True) + NORM_EPS) * g


def _sigmoid(x):
    return 1.0 / (1.0 + jnp.exp(-x))


def _mix_prep_kernel(blocks_per_seq, cw, rw,
                     xprev_ref, x_ref, g_ref, w_in_ref, conv_w_ref, conv_g_ref, mu_ref,
                     w0_ref, w_up_ref, a0_ref, a_up_ref, g_up_ref, k_k_ref, k_a_ref, r_k_ref,
                     seg2_ref,
                     yconv_ref, r_ref, w_ref, k_ref, v_ref, kk_ref, b_ref, gate_ref, bonus_ref,
                     p_scr, z_scr):
    tm = x_ref.shape[0]
    halo = xprev_ref.shape[0]
    first = (pl.program_id(0) % blocks_per_seq) == 0
    xp = xprev_ref[...] * jnp.where(first, 0.0, 1.0)
    xa = jnp.concatenate([xp, x_ref[...]], axis=0)
    h = _rms_rows(xa, g_ref[...])
    p_scr[...] = jnp.dot(h.astype(BF16), w_in_ref[...], preferred_element_type=F32)
    seg2 = seg2_ref[...]

    z_scr[...] = p_scr[:, 2 * cw:3 * cw] * p_scr[:, 0:cw]
    conv = (conv_w_ref[0:1, :] * z_scr[halo - 2:halo - 2 + tm, :]
            + conv_w_ref[1:2, :] * z_scr[halo - 1:halo - 1 + tm, :]
            + conv_w_ref[2:3, :] * z_scr[halo:halo + tm, :])
    y = p_scr[halo:halo + tm, cw:2 * cw] * conv
    ms = _seg_sum_nonneg(y * y, seg2) * (1.0 / HEAD_DIM)
    yconv_ref[...] = (y * lax.rsqrt(ms + NORM_EPS) * conv_g_ref[...]).astype(yconv_ref.dtype)

    c0 = 3 * cw
    cur = p_scr[halo:halo + tm, c0:]
    prev = p_scr[halo - 1:halo - 1 + tm, c0:]
    q = cur + (prev - cur) * mu_ref[...]
    r = q[:, 0:rw]
    k = q[:, rw:2 * rw]
    v = q[:, 2 * rw:3 * rw]
    lora_wa = q[:, 3 * rw:3 * rw + V7X_LANES]
    lora_g = q[:, 3 * rw + V7X_LANES:]
    w_lin = w0_ref[...] + jnp.dot(jnp.tanh(lora_wa).astype(BF16), w_up_ref[...],
                                  preferred_element_type=F32)
    neg = -w_lin
    softplus = jnp.maximum(neg, 0.0) + jnp.log(1.0 + jnp.exp(-jnp.abs(neg)))
    log_decay = -jnp.exp(-softplus - 0.5)
    a = _sigmoid(a0_ref[...] + jnp.dot(lora_wa.astype(BF16), a_up_ref[...],
                                       preferred_element_type=F32))
    gate = jnp.dot(_sigmoid(lora_g).astype(BF16), g_up_ref[...], preferred_element_type=F32)
    kk = k * k_k_ref[...]
    kk = kk / jnp.maximum(jnp.sqrt(_seg_sum_nonneg(kk * kk, seg2)), 1e-12)
    k_mod = k * (1.0 + (a - 1.0) * k_a_ref[...])
    bonus = _seg_sum(r * k_mod * r_k_ref[...], seg2) * v
    r_ref[...] = r
    w_ref[...] = log_decay
    k_ref[...] = k_mod
    v_ref[...] = v
    kk_ref[...] = kk
    b_ref[...] = kk * a
    gate_ref[...] = gate
    bonus_ref[...] = bonus


def _mix_prep(x2d, seq, norm_g, w_in, conv_w, conv_g, mu, w0, w_up, a0, a_up, g_up, k_k, k_a,
              r_k, seg2):
    n, d = x2d.shape
    tm = PREP_ROWS
    halo = V7X_SUBLANES
    cw = conv_w.shape[1]
    rw = w0.shape[1]
    in_cols = w_in.shape[1]
    assert seq % tm == 0 and n % tm == 0
    full = lambda a: pl.BlockSpec(a.shape, lambda i: (0,) * a.ndim)
    row_spec = lambda c: pl.BlockSpec((tm, c), lambda i: (i, 0))
    consts = (norm_g, w_in, conv_w, conv_g, mu, w0, w_up, a0, a_up, g_up, k_k, k_a, r_k, seg2)
    outs = [jax.ShapeDtypeStruct((n, cw), BF16)] + [jax.ShapeDtypeStruct((n, rw), F32)] * 8
    return pl.pallas_call(
        functools.partial(_mix_prep_kernel, seq // tm, cw, rw),
        grid=(n // tm,),
        in_specs=[pl.BlockSpec((halo, d), lambda i: (jnp.maximum(i * (tm // halo) - 1, 0), 0)),
                  row_spec(d)] + [full(c) for c in consts],
        out_specs=[row_spec(cw)] + [row_spec(rw)] * 8,
        out_shape=outs,
        scratch_shapes=[pltpu.VMEM((halo + tm, in_cols), F32), pltpu.VMEM((halo + tm, cw), F32)],
        compiler_params=_params("parallel"),
        name="mix_prep",
    )(x2d, x2d, *consts)


def _bdot(a, b):
    return jnp.dot(a.astype(BF16), b.astype(BF16), preferred_element_type=F32)


def _bdot_nt(a, b):
    return lax.dot_general(a.astype(BF16), b.astype(BF16), (((1,), (1,)), ((), ())),
                           preferred_element_type=F32)


def _wkv_chunk_kernel(n_batch, n_pairs, n_side,
                      r_ref, lw_ref, k_ref, v_ref, kk_ref, b_ref, *refs):
    frames = r_ref.shape[1]
    lanes = V7X_LANES
    hd = HEAD_DIM
    assert frames == hd
    side_in, o_ref, side_out, state_ref = (refs[:n_side], refs[n_side],
                                           refs[n_side + 1:2 * n_side + 1], refs[-1])
    for src, dst in zip(side_in, side_out):
        dst[...] = src[...].astype(dst.dtype)

    @pl.when(pl.program_id(0) == 0)
    def _():
        state_ref[...] = jnp.zeros_like(state_ref)

    row = lax.broadcasted_iota(I32, (lanes, lanes), 0)
    lane = lax.broadcasted_iota(I32, (lanes, lanes), 1)
    same_head = (row // hd) == (lane // hd)
    rt = row % hd
    ls = lane % hd
    strict_same = same_head & (rt > ls)
    strict_cross = jnp.logical_not(same_head) & (rt > ls)
    incl_same = same_head & (rt >= ls)
    incl_cross = jnp.logical_not(same_head) & (rt >= ls)
    eye = row == lane
    level_masks = []
    m = 1
    while m < hd:
        level_masks.append(same_head & ((rt // (2 * m)) == (ls // (2 * m)))
                           & (((rt // m) % 2) == 1) & (((ls // m) % 2) == 0))
        m *= 2
    left = lax.broadcasted_iota(I32, (frames, lanes), 1) < hd
    tri = (lax.broadcasted_iota(I32, (frames, frames), 0)
           >= lax.broadcasted_iota(I32, (frames, frames), 1)).astype(BF16)

    def diag_blocks(x):
        return jnp.concatenate([jnp.where(left, x, 0.0), jnp.where(left, 0.0, x)], axis=0)

    def cross_blocks(x):
        return jnp.concatenate([jnp.where(left, 0.0, x), jnp.where(left, x, 0.0)], axis=0)

    def batch_prep(bi):
        lw = lw_ref[bi]
        hi = lw.astype(BF16)
        rem = lw - hi.astype(F32)
        mid = rem.astype(BF16)
        lo = (rem - mid.astype(F32)).astype(BF16)
        cs = (jnp.dot(tri, hi, preferred_element_type=F32)
              + jnp.dot(tri, mid, preferred_element_type=F32)
              + jnp.dot(tri, lo, preferred_element_type=F32))
        cs_end = cs[frames - 1:frames, :]
        e_neg = jnp.exp(-cs)
        e_end = jnp.exp(cs_end - cs)
        b_in = b_ref[bi]
        k_in = k_ref[bi]
        return dict(a_t=-kk_ref[bi] * jnp.exp(cs - lw), b_t=b_in * e_neg, k_t=k_in * e_neg,
                    r_t=r_ref[bi] * jnp.exp(cs), b_h=b_in * e_end, k_h=k_in * e_end,
                    v=v_ref[bi], g_end=jnp.exp(cs_end))

    group = []
    for bi in range(n_batch):
        prep = batch_prep(bi)
        for pi in range(n_pairs):
            sl = slice(pi * lanes, (pi + 1) * lanes)
            group.append((bi, sl, bi * n_pairs + pi, {n: x[:, sl] for n, x in prep.items()}))
    bf = lambda x: x.astype(BF16)
    a_bd = [bf(diag_blocks(p["a_t"])) for _, _, _, p in group]
    r_bd = [bf(diag_blocks(p["r_t"])) for _, _, _, p in group]
    d0 = [_bdot_nt(jnp.concatenate([a[:hd], r[:hd]], axis=0),
                   bf(jnp.concatenate([p["b_t"], p["k_t"]], axis=0)))
          for a, r, (_, _, _, p) in zip(a_bd, r_bd, group)]
    d1 = [_bdot_nt(jnp.concatenate([a[hd:], r[hd:]], axis=0),
                   bf(jnp.concatenate([p["k_t"], p["b_t"]], axis=0)))
          for a, r, (_, _, _, p) in zip(a_bd, r_bd, group)]
    a_rows = [jnp.concatenate([x[:hd], y[:hd]], axis=0) for x, y in zip(d0, d1)]
    m_rows = [jnp.concatenate([x[hd:], y[hd:]], axis=0) for x, y in zip(d0, d1)]
    a_ab = [bf(jnp.where(strict_same, x, 0.0)) for x in a_rows]
    a_ak = [bf(jnp.where(strict_cross, x, 0.0)) for x in a_rows]
    m_rb = [bf(jnp.where(incl_same, x, 0.0)) for x in m_rows]
    m_rk = [bf(jnp.where(incl_cross, x, 0.0)) for x in m_rows]
    t_inv = [jnp.where(eye, 1.0, jnp.where(level_masks[0], x, 0.0)) for x in a_rows]
    for mask in level_masks[1:]:
        t_bf = [bf(t) for t in t_inv]
        half = [jnp.where(mask, _bdot(t, x), 0.0) for t, x in zip(t_bf, a_ab)]
        t_inv = [t + _bdot(h, tb) for t, h, tb in zip(t_inv, half, t_bf)]
    t_bf = [bf(t) for t in t_inv]
    v_x = [bf(cross_blocks(p["v"])) for _, _, _, p in group]
    akv = [_bdot(x, v) for x, v in zip(a_ak, v_x)]
    wu = [bf(_bdot(t, jnp.concatenate([a, bf(x)], axis=1)))
          for t, a, x in zip(t_bf, a_bd, akv)]
    bh_t = [bf(diag_blocks(p["b_h"]).T) for _, _, _, p in group]
    kh_t = [bf(cross_blocks(p["k_h"]).T) for _, _, _, p in group]
    pw_rw = [_bdot(jnp.concatenate([bt, m], axis=0), x[:, :lanes])
             for bt, m, x in zip(bh_t, m_rb, wu)]
    q_o = [_bdot(jnp.concatenate([jnp.concatenate([bt, kt], axis=1),
                                  jnp.concatenate([mb, mk], axis=1)], axis=0),
                 jnp.concatenate([x[:, lanes:], v], axis=0))
           for bt, kt, mb, mk, x, v in zip(bh_t, kh_t, m_rb, m_rk, wu, v_x)]
    res = [_bdot(jnp.concatenate([diag_blocks(p["r_t"]) + pr[lanes:],
                                  pr[:lanes] + jnp.where(eye, p["g_end"], 0.0)], axis=0),
                 state_ref[idx])
           for pr, (_, _, idx, p) in zip(pw_rw, group)]
    for x, q, (bi, sl, idx, _) in zip(res, q_o, group):
        o_bd = x[:lanes] + q[lanes:]
        state_ref[idx] = x[lanes:] + q[:lanes]
        o_ref[bi, :, sl] = jnp.where(left, o_bd[:hd], o_bd[hd:])


def _wkv_chunk(r, lw, k, v, kk, b, side_f32=()):
    n_batch, seq, rw = r.shape
    n_pairs = rw // V7X_LANES
    frames = SCAN_FRAMES
    assert seq % frames == 0
    n_steps = seq // frames
    spec = pl.BlockSpec((n_batch, frames, rw), lambda c: (0, c, 0))
    sliced = [a.reshape(n_steps, -1, a.shape[-1]) for a in side_f32]
    side_specs = [pl.BlockSpec((1,) + a.shape[1:], lambda c: (c, 0, 0)) for a in sliced]
    outs = pl.pallas_call(
        functools.partial(_wkv_chunk_kernel, n_batch, n_pairs, len(sliced)),
        grid=(n_steps,),
        in_specs=[spec] * 6 + side_specs,
        out_specs=[spec] + side_specs,
        out_shape=[jax.ShapeDtypeStruct((n_batch, seq, rw), F32)]
                  + [jax.ShapeDtypeStruct(a.shape, BF16) for a in sliced],
        scratch_shapes=[pltpu.VMEM((n_batch * n_pairs, V7X_LANES, V7X_LANES), F32)],
        compiler_params=_params("arbitrary"),
        name="wkv_chunk",
    )(r, lw, k, v, kk, b, *sliced)
    return outs[0], [y.reshape(a.shape) for y, a in zip(outs[1:], side_f32)]


def _post_mix_kernel(n_experts,
                     o_ref, bonus_ref, gate_ref, yconv_ref, x_ref, gn_w_ref, gn_b_ref,
                     w_out_c_ref, w_out_r_ref, ffn_g_ref, wr_hi_ref, wr_lo_ref, b_router_ref,
                     seg2_ref,
                     x1_ref, *outs):
    h2_refs, (top_e_ref, gates_ref) = outs[:-2], outs[-2:]
    seg2 = seg2_ref[...]
    o = o_ref[...]
    mean = _seg_sum(o, seg2) * (1.0 / HEAD_DIM)
    cen = o - mean
    var = _seg_sum_nonneg(cen * cen, seg2) * (1.0 / HEAD_DIM)
    o = cen * lax.rsqrt(var + GN_EPS) * gn_w_ref[...] + gn_b_ref[...]
    y_rwkv = (o + bonus_ref[...]) * gate_ref[...]
    x1 = (x_ref[...]
          + jnp.dot(yconv_ref[...], w_out_c_ref[...], preferred_element_type=F32)
          + jnp.dot(y_rwkv.astype(BF16), w_out_r_ref[...], preferred_element_type=F32))
    x1_ref[...] = x1
    h2 = _rms_rows(x1, ffn_g_ref[...])
    words = _pack_bf16_pairs(h2)
    for c, h2_ref in enumerate(h2_refs):
        h2_ref[...] = words[:, c * ROW_CHUNK:(c + 1) * ROW_CHUNK]

    h_hi, h_lo = _split_bf16(h2)
    nt = (((1,), (1,)), ((), ()))
    logits = (lax.dot_general(wr_hi_ref[...], h_hi, nt, preferred_element_type=F32)
              + lax.dot_general(wr_hi_ref[...], h_lo, nt, preferred_element_type=F32)
              + lax.dot_general(wr_lo_ref[...], h_hi, nt, preferred_element_type=F32)
              + b_router_ref[...])
    e_id = lax.broadcasted_iota(I32, logits.shape, 0).astype(F32)
    work = logits
    tops, ids = [], []
    for _ in range(TOP_K):
        m = jnp.max(work, axis=0, keepdims=True)
        sel = jnp.min(jnp.where(work == m, e_id, float(n_experts)), axis=0, keepdims=True)
        tops.append(m)
        ids.append(sel)
        work = jnp.where(e_id == sel, -jnp.inf, work)
    ex = [jnp.exp(t - tops[0]) for t in tops]
    denom = ex[0] + ex[1] + ex[2] + ex[3]
    top_e_ref[...] = jnp.concatenate(ids, axis=0).astype(I32)
    gates_ref[...] = jnp.concatenate([e / denom for e in ex], axis=0)


def _post_mix(token0, n, o, bonus, gate, yconv, x2d, gn_w, gn_b, w_out_c, w_out_r, ffn_g, wr_hi,
              wr_lo, b_router, seg2):
    d = x2d.shape[1]
    rw = o.shape[1]
    cw = yconv.shape[1]
    n_experts = wr_hi.shape[0]
    tm = POST_ROWS
    assert token0 % tm == 0 and n % tm == 0
    full = lambda a: pl.BlockSpec(a.shape, lambda i: (0,) * a.ndim)
    in_spec = lambda c: pl.BlockSpec((tm, c), lambda i: (i + token0 // tm, 0))
    row_spec = lambda c: pl.BlockSpec((tm, c), lambda i: (i, 0))
    col_spec = pl.BlockSpec((TOP_K, tm), lambda i: (0, i))
    consts = (gn_w, gn_b, w_out_c, w_out_r, ffn_g, wr_hi, wr_lo, b_router, seg2)
    n_chunks = d // 2 // ROW_CHUNK
    outs = pl.pallas_call(
        functools.partial(_post_mix_kernel, n_experts),
        grid=(n // tm,),
        in_specs=[in_spec(rw), in_spec(rw), in_spec(rw), in_spec(cw), in_spec(d)]
                 + [full(c) for c in consts],
        out_specs=[row_spec(d)] + [row_spec(ROW_CHUNK)] * n_chunks + [col_spec, col_spec],
        out_shape=[jax.ShapeDtypeStruct((n, d), F32)]
                  + [jax.ShapeDtypeStruct((n, ROW_CHUNK), jnp.uint32)] * n_chunks
                  + [jax.ShapeDtypeStruct((TOP_K, n), I32), jax.ShapeDtypeStruct((TOP_K, n), F32)],
        compiler_params=_params("parallel"),
        name="post_mix",
    )(o, bonus, gate, yconv, x2d, *consts)
    return outs[0], outs[1:1 + n_chunks], outs[-2], outs[-1]


def _route_kernel(n_experts, block_rows,
                  top_e_ref, tri_ref, dest_ref, meta_ref,
                  count_ref, start_ref, carry_ref):
    phase = pl.program_id(0)
    j = pl.program_id(1)
    tb = top_e_ref.shape[1]
    e_id = lax.broadcasted_iota(I32, (n_experts, tb), 0)
    top_e = top_e_ref[...]
    onehot = jnp.zeros((n_experts, tb), F32)
    for c in range(TOP_K):
        onehot = onehot + jnp.where(top_e[c:c + 1, :] == e_id, 1.0, 0.0)
    block_count = jnp.sum(onehot, axis=1, keepdims=True)

    @pl.when((phase == 0) & (j == 0))
    def _():
        count_ref[...] = jnp.zeros_like(count_ref)

    @pl.when(phase == 0)
    def _():
        count_ref[...] += block_count

    @pl.when((phase == 1) & (j == 0))
    def _():
        counts = count_ref[...]
        padded = jnp.ceil(counts * (1.0 / block_rows)) * block_rows
        sub = lax.broadcasted_iota(I32, (n_experts, n_experts), 0)
        lane = lax.broadcasted_iota(I32, (n_experts, n_experts), 1)
        padded_row = jnp.sum(jnp.where(sub == lane, padded, 0.0), axis=0, keepdims=True)
        start = jnp.sum(jnp.where(lane < sub, padded_row, 0.0), axis=1, keepdims=True)
        start_ref[...] = start
        carry_ref[...] = jnp.zeros_like(carry_ref)
        end = start + padded
        nb = meta_ref.shape[1]
        slot0 = (lax.broadcasted_iota(I32, (n_experts, nb), 1) * block_rows).astype(F32)
        block_e = jnp.sum(jnp.where(end <= slot0, 1.0, 0.0), axis=0, keepdims=True)
        block_e = jnp.minimum(block_e, n_experts - 1.0)
        used = jnp.max(end, axis=0, keepdims=True) * (1.0 / block_rows)
        in_region = (start <= slot0) & (slot0 < end)
        valid = jnp.sum(jnp.where(in_region, jnp.minimum(start + counts - slot0, block_rows), 0.0),
                        axis=0, keepdims=True)
        row = lax.broadcasted_iota(I32, meta_ref.shape, 0)
        meta_ref[...] = jnp.where(row == META_BLOCK_EXPERT, block_e,
                                  jnp.where(row == META_VALID_ROWS, valid, used)).astype(I32)

    @pl.when(phase == 1)
    def _():
        incl = jnp.dot(onehot.astype(BF16), tri_ref[...], preferred_element_type=F32)
        base = incl - onehot + carry_ref[...] + start_ref[...]
        rows = [jnp.sum(jnp.where(top_e[c:c + 1, :] == e_id, base, 0.0), axis=0, keepdims=True)
                for c in range(TOP_K)]
        dest_ref[...] = jnp.concatenate(rows, axis=0).astype(I32)
        carry_ref[...] += block_count


def _route(top_e, n_experts, n_blocks_padded):
    n = top_e.shape[1]
    tb = ROUTE_TOKENS
    assert n % tb == 0
    tri = (lax.broadcasted_iota(I32, (tb, tb), 0) <= lax.broadcasted_iota(I32, (tb, tb), 1)
           ).astype(BF16)
    return pl.pallas_call(
        functools.partial(_route_kernel, n_experts, EXPERT_ROWS),
        grid=(2, n // tb),
        in_specs=[pl.BlockSpec((TOP_K, tb), lambda ph, j: (0, j)),
                  pl.BlockSpec((tb, tb), lambda ph, j: (0, 0))],
        out_specs=[pl.BlockSpec((TOP_K, tb), lambda ph, j: (0, j * ph)),
                   pl.BlockSpec((V7X_SUBLANES, n_blocks_padded), lambda ph, j: (0, 0))],
        out_shape=[jax.ShapeDtypeStruct((TOP_K, n), I32),
                   jax.ShapeDtypeStruct((V7X_SUBLANES, n_blocks_padded), I32)],
        scratch_shapes=[pltpu.VMEM((n_experts, 1), F32)] * 3,
        compiler_params=_params("arbitrary", "arbitrary"),
        name="route",
    )(top_e, tri)


def _sc_mesh():
    return plsc.VectorSubcoreMesh(core_axis_name="core", subcore_axis_name="subcore")


def _sc_scatter_rows(src, dest, n_slots):
    n, width = src.shape
    assert n % SC_WINDOW == 0

    @functools.partial(pl.kernel, out_type=jax.ShapeDtypeStruct((n_slots, width), src.dtype),
                       mesh=_sc_mesh(), name="sc_scatter_rows")
    def scatter(src_hbm, dest_hbm, out_hbm):
        def body(src_vmem, dest_vmem):
            for c in range(TOP_K):
                pltpu.sync_copy(src_vmem, out_hbm.at[dest_vmem.at[c]])

        pltpu.emit_pipeline(
            body,
            grid=(n // SC_WINDOW,),
            in_specs=[pl.BlockSpec((SC_WINDOW, width), lambda i: (i, 0)),
                      pl.BlockSpec((TOP_K, SC_WINDOW), lambda i: (0, i))],
            out_specs=[],
            core_axis_name=("core", "subcore"),
            dimension_semantics=(pltpu.PARALLEL,),
        )(src_hbm, dest_hbm)

    return scatter(src, dest)


def _experts_kernel(d_ff, n_in, n_out, meta_ref, *refs):
    x_refs, refs = refs[:n_in], refs[n_in:]
    w_gu_hbm, b_gu_ref, w_down_hbm, b_down_ref = refs[:4]
    y_refs = refs[4:4 + n_out]
    w_gu_buf, w_down_buf, sems, slot_ref = refs[4 + n_out:]
    j = pl.program_id(0)
    last = pl.num_programs(0) - 1
    used = meta_ref[META_USED_BLOCKS, 0]

    def block_expert(jj):
        return meta_ref[META_BLOCK_EXPERT, jnp.clip(jj, 0, last)]

    def weight_copies(e, slot):
        return (pltpu.make_async_copy(w_gu_hbm.at[e], w_gu_buf.at[slot], sems.at[0, slot]),
                pltpu.make_async_copy(w_down_hbm.at[e], w_down_buf.at[slot], sems.at[1, slot]))

    @pl.when(j < used)
    def _():
        e = block_expert(j)

        @pl.when(j == 0)
        def _():
            slot_ref[0] = 1
            for copy in weight_copies(e, 0):
                copy.start()

        @pl.when((j == 0) | (block_expert(j - 1) != e))
        def _():
            slot = 1 - slot_ref[0]
            slot_ref[0] = slot
            nxt = lax.while_loop(lambda jj: (jj < used) & (block_expert(jj) == e), lambda jj: jj + 1, j + 1)

            @pl.when(nxt < used)
            def _():
                for copy in weight_copies(block_expert(nxt), 1 - slot):
                    copy.start()

            for copy in weight_copies(e, slot):
                copy.wait()

        slot = slot_ref[0]
        w_gu_mxu, w_down_mxu = w_gu_buf.at[slot], w_down_buf.at[slot]
        x = _unpack_bf16_pairs(jnp.concatenate([r[...] for r in x_refs], axis=1))
        row = lax.broadcasted_iota(I32, x.shape, 0)
        x = jnp.where(row < meta_ref[META_VALID_ROWS, j], x, 0.0)
        gu = jnp.dot(x.astype(BF16), w_gu_mxu[...], preferred_element_type=F32) + b_gu_ref[0]
        gate = jnp.minimum(gu[:, :d_ff], SWIGLU_LIMIT)
        up = jnp.clip(gu[:, d_ff:], -SWIGLU_LIMIT, SWIGLU_LIMIT)
        act = (up + 1.0) * (gate * _sigmoid(SWIGLU_ALPHA * gate))
        y = jnp.dot(act.astype(BF16), w_down_mxu[...], preferred_element_type=F32) + b_down_ref[0]
        words = _pack_bf16_pairs(y)
        for c, y_ref in enumerate(y_refs):
            y_ref[...] = words[:, c * ROW_CHUNK:(c + 1) * ROW_CHUNK]

    @pl.when(j >= used)
    def _():
        for y_ref in y_refs:
            y_ref[...] = jnp.zeros_like(y_ref)


def _experts(meta, slot_chunks, w_gu, b_gu, w_down, b_down):
    assert w_gu.dtype == BF16 and w_down.dtype == BF16
    n_in = len(slot_chunks)
    n_slots = slot_chunks[0].shape[0]
    n_experts, d, two_ff = w_gu.shape
    n_out = d // 2 // ROW_CHUNK
    d_ff = two_ff // 2
    bm = EXPERT_ROWS
    by_expert = lambda j, m: (m[META_BLOCK_EXPERT, j], 0, 0)
    grid_spec = pltpu.PrefetchScalarGridSpec(
        num_scalar_prefetch=1,
        grid=(n_slots // bm,),
        in_specs=[pl.BlockSpec((bm, ROW_CHUNK),
                               lambda j, m: (jnp.minimum(j, m[META_USED_BLOCKS, 0] - 1), 0))
                  ] * n_in
                 + [pl.BlockSpec(memory_space=pl.ANY), pl.BlockSpec((1, 1, two_ff), by_expert),
                    pl.BlockSpec(memory_space=pl.ANY), pl.BlockSpec((1, 1, d), by_expert)],
        out_specs=[pl.BlockSpec((bm, ROW_CHUNK), lambda j, m: (j, 0))] * n_out,
        scratch_shapes=[pltpu.VMEM((2, d, two_ff), BF16), pltpu.VMEM((2, d_ff, d), BF16),
                        pltpu.SemaphoreType.DMA((2, 2)), pltpu.SMEM((1,), I32)],
    )
    return pl.pallas_call(
        functools.partial(_experts_kernel, d_ff, n_in, n_out),
        grid_spec=grid_spec,
        out_shape=[jax.ShapeDtypeStruct((n_slots, ROW_CHUNK), jnp.uint32)] * n_out,
        compiler_params=_params("arbitrary"),
        name="experts",
    )(meta, *slot_chunks, w_gu, b_gu, w_down, b_down)


def _sc_gather_rows(table, idx_row):
    n_idx = idx_row.shape[1]
    width = table.shape[1]
    assert n_idx % SC_WINDOW == 0

    @functools.partial(pl.kernel, out_type=jax.ShapeDtypeStruct((n_idx, width), table.dtype),
                       mesh=_sc_mesh(), name="sc_gather_rows")
    def gather(table_hbm, idx_hbm, out_hbm):
        def body(idx_vmem, out_vmem):
            pltpu.sync_copy(table_hbm.at[idx_vmem.at[0]], out_vmem)

        pltpu.emit_pipeline(
            body,
            grid=(n_idx // SC_WINDOW,),
            in_specs=[pl.BlockSpec((1, SC_WINDOW), lambda i: (0, i))],
            out_specs=[pl.BlockSpec((SC_WINDOW, width), lambda i: (i, 0))],
            core_axis_name=("core", "subcore"),
            dimension_semantics=(pltpu.PARALLEL,),
        )(idx_hbm, out_hbm)

    return gather(table, idx_row)


def _reduce_kernel(n_chunks, *refs):
    y_refs = refs[:n_chunks]
    x1_ref, gates_ref, g_ref = refs[n_chunks:n_chunks + 3]
    out_ref = refs[-1]
    gates = gates_ref[...]
    acc = x1_ref[...]
    for c in range(TOP_K):
        y = _unpack_bf16_pairs(jnp.concatenate([r[c] for r in y_refs], axis=1))
        acc = acc + y * gates[:, c:c + 1]
    out_ref[...] = _rms_rows(acc, g_ref[...])


def _reduce(token0, n_total, out_so_far, y_chunks, x1, gates_t, final_g):
    n, d = x1.shape
    tb = REDUCE_TOKENS
    n_chunks = len(y_chunks)
    assert token0 % tb == 0
    in_specs = ([pl.BlockSpec((TOP_K, tb, ROW_CHUNK), lambda i: (0, i, 0))] * n_chunks
                + [pl.BlockSpec((tb, d), lambda i: (i, 0)),
                   pl.BlockSpec((tb, TOP_K), lambda i: (i, 0)),
                   pl.BlockSpec((1, d), lambda i: (0, 0))])
    args = [y.reshape(TOP_K, n, ROW_CHUNK) for y in y_chunks] + [x1, gates_t, final_g]
    aliases = {}
    if out_so_far is not None:
        in_specs.append(pl.BlockSpec(memory_space=pl.ANY))
        args.append(out_so_far)
        aliases = {len(args) - 1: 0}
    return pl.pallas_call(
        functools.partial(_reduce_kernel, n_chunks),
        grid=(n // tb,),
        in_specs=in_specs,
        out_specs=pl.BlockSpec((tb, d), lambda i: (i + token0 // tb, 0)),
        out_shape=jax.ShapeDtypeStruct((n_total, d), F32),
        input_output_aliases=aliases,
        compiler_params=_params("parallel"),
        name="reduce",
    )(*args)


def _group_matrix(width):
    a = lax.broadcasted_iota(I32, (width, width), 0) // HEAD_DIM
    b = lax.broadcasted_iota(I32, (width, width), 1) // HEAD_DIM
    g = (a == b).astype(BF16)
    return jnp.concatenate([g, g], axis=0)


def _row(vec):
    return vec.reshape(1, -1).astype(F32)


def kernel(x, w_in, conv_w, conv_norm_g, rwkv_mu, w0, w_up, a0, a_up, g_up, k_k, k_a, r_k,
           gn_w, gn_b, w_out, norm_mix_g, norm_ffn_g, w_router, b_router, w_gu, b_gu, w_down,
           b_down, norm_final_g):
    n_batch, seq, d = x.shape
    n = n_batch * seq
    depth = w_in.shape[0]
    cw = conv_w.shape[2]
    rw = w0.shape[1]
    n_experts = w_router.shape[2]
    decay_lora = w_up.shape[1]
    aaa_lora = a_up.shape[1]
    assert decay_lora + aaa_lora == V7X_LANES
    assert depth == 1
    assert n % MOE_PARTS == 0
    n_part = n // MOE_PARTS
    n_slots = n_part * TOP_K + n_experts * EXPERT_ROWS
    n_blocks = n_slots // EXPERT_ROWS
    n_blocks_padded = -(-n_blocks // V7X_LANES) * V7X_LANES
    seg2 = _group_matrix(rw)

    x2d = x.reshape(n, d)
    for l in range(depth):
        w_up_pad = jnp.concatenate([w_up[l], jnp.zeros((aaa_lora, rw), F32)], axis=0).astype(BF16)
        a_up_pad = jnp.concatenate([jnp.zeros((decay_lora, rw), F32), a_up[l]], axis=0).astype(BF16)
        (yconv, r, lw, k, v, kk, b, gate, bonus) = _mix_prep(
            x2d, seq, _row(norm_mix_g[l]), w_in[l].astype(BF16), conv_w[l].astype(F32),
            _row(conv_norm_g[l]), _row(rwkv_mu[l]), _row(w0[l]), w_up_pad, _row(a0[l]), a_up_pad,
            g_up[l].astype(BF16), _row(k_k[l]), _row(k_a[l]), _row(r_k[l]), seg2)
        shape3 = (n_batch, seq, rw)
        o, (w_gu_b, w_down_b) = _wkv_chunk(
            r.reshape(shape3), lw.reshape(shape3), k.reshape(shape3), v.reshape(shape3),
            kk.reshape(shape3), b.reshape(shape3),
            side_f32=(w_gu[l].astype(F32), w_down[l].astype(F32)))
        o = o.reshape(n, rw)
        w_out_b = w_out[l].astype(BF16)
        wr_t = w_router[l].T.astype(F32)
        wr_hi = wr_t.astype(BF16)
        wr_lo = (wr_t - wr_hi.astype(F32)).astype(BF16)
        expert_args = (w_gu_b, b_gu[l].reshape(n_experts, 1, -1).astype(F32),
                       w_down_b, b_down[l].reshape(n_experts, 1, -1).astype(F32))
        parts = []
        for token0 in range(0, n, n_part):
            x1, h2_chunks, top_e, gates = _post_mix(
                token0, n_part, o, bonus, gate, yconv, x2d, _row(gn_w[l]), _row(gn_b[l]),
                w_out_b[:cw], w_out_b[cw:], _row(norm_ffn_g[l]), wr_hi, wr_lo,
                b_router[l].reshape(n_experts, 1).astype(F32), seg2)
            dest, meta = _route(top_e, n_experts, n_blocks_padded)
            slot_chunks = [_sc_scatter_rows(h, dest, n_slots) for h in h2_chunks]
            parts.append((token0, x1, gates, dest, meta, slot_chunks))
        out = None
        for token0, x1, gates, dest, meta, slot_chunks in parts:
            y_chunks = _experts(meta, slot_chunks, *expert_args)
            idx_row = dest.reshape(1, TOP_K * n_part)
            out = _reduce(token0, n, out, [_sc_gather_rows(y, idx_row) for y in y_chunks], x1,
                          gates.T, _row(norm_final_g))
        x2d = out
    return x2d.reshape(n_batch, seq, d)
```

```python
import functools

import jax
import jax.numpy as jnp
from jax import lax
from jax.experimental import pallas as pl
from jax.experimental.pallas import tpu as pltpu
from jax.experimental.pallas import tpu_sc as plsc

F32 = jnp.float32
BF16 = jnp.bfloat16
I32 = jnp.int32

HEAD_DIM = 64
TOP_K = 4
NORM_EPS = 1e-5
GN_EPS = HEAD_DIM * 1e-5
SWIGLU_LIMIT = 7.0
SWIGLU_ALPHA = 1.702

V7X_LANES = 128
V7X_SUBLANES = 8
V7X_VMEM_LIMIT_BYTES = 56 * 1024 * 1024

PREP_ROWS = 512
POST_ROWS = 1024
SCAN_FRAMES = 64
ROUTE_TOKENS = 1024
REDUCE_TOKENS = 1024
MOE_PARTS = 2
SC_WINDOW = 128
ROW_CHUNK = 256
META_BLOCK_EXPERT = 0
META_USED_BLOCKS = 1
META_VALID_ROWS = 2
EXPERT_ROWS = 512

def _params(*semantics):
    return pltpu.CompilerParams(dimension_semantics=semantics,
                                vmem_limit_bytes=V7X_VMEM_LIMIT_BYTES)


def _split_bf16(x):
    hi = x.astype(BF16)
    lo = (x - hi.astype(F32)).astype(BF16)
    return hi, lo


def _seg_sum(x, seg2):
    hi, lo = _split_bf16(x)
    return jnp.dot(jnp.concatenate([hi, lo], axis=1), seg2, preferred_element_type=F32)


def _seg_sum_nonneg(x, seg2):
    return jnp.dot(x.astype(BF16), seg2[:x.shape[1]], preferred_element_type=F32)


def _pack_bf16_pairs(x):
    bits = pltpu.bitcast(x.astype(BF16).astype(F32), jnp.uint32)
    half = x.shape[1] // 2
    return bits[:, half:] | (bits[:, :half] >> 16)


def _unpack_bf16_pairs(words):
    return jnp.concatenate([pltpu.bitcast(words << 16, F32),
                            pltpu.bitcast(words & jnp.uint32(0xFFFF0000), F32)], axis=1)


def _rms_rows(x, g):
    return x * lax.rsqrt(jnp.mean(x * x, axis=-1, keepdims=True) + NORM_EPS) * g


def _sigmoid(x):
    return 1.0 / (1.0 + jnp.exp(-x))


def _mix_prep_kernel(blocks_per_seq, cw, rw,
                     xprev_ref, x_ref, g_ref, w_in_ref, conv_w_ref, conv_g_ref, mu_ref,
                     w0_ref, w_up_ref, a0_ref, a_up_ref, g_up_ref, k_k_ref, k_a_ref, r_k_ref,
                     seg2_ref,
                     yconv_ref, r_ref, w_ref, k_ref, v_ref, kk_ref, b_ref, gate_ref, bonus_ref,
                     p_scr, z_scr):
    tm = x_ref.shape[0]
    halo = xprev_ref.shape[0]
    first = (pl.program_id(0) % blocks_per_seq) == 0
    xp = xprev_ref[...] * jnp.where(first, 0.0, 1.0)
    xa = jnp.concatenate([xp, x_ref[...]], axis=0)
    h = _rms_rows(xa, g_ref[...])
    p_scr[...] = jnp.dot(h.astype(BF16), w_in_ref[...], preferred_element_type=F32)
    seg2 = seg2_ref[...]

    z_scr[...] = p_scr[:, 2 * cw:3 * cw] * p_scr[:, 0:cw]
    conv = (conv_w_ref[0:1, :] * z_scr[halo - 2:halo - 2 + tm, :]
            + conv_w_ref[1:2, :] * z_scr[halo - 1:halo - 1 + tm, :]
            + conv_w_ref[2:3, :] * z_scr[halo:halo + tm, :])
    y = p_scr[halo:halo + tm, cw:2 * cw] * conv
    ms = _seg_sum_nonneg(y * y, seg2) * (1.0 / HEAD_DIM)
    yconv_ref[...] = (y * lax.rsqrt(ms + NORM_EPS) * conv_g_ref[...]).astype(yconv_ref.dtype)

    c0 = 3 * cw
    cur = p_scr[halo:halo + tm, c0:]
    prev = p_scr[halo - 1:halo - 1 + tm, c0:]
    q = cur + (prev - cur) * mu_ref[...]
    r = q[:, 0:rw]
    k = q[:, rw:2 * rw]
    v = q[:, 2 * rw:3 * rw]
    lora_wa = q[:, 3 * rw:3 * rw + V7X_LANES]
    lora_g = q[:, 3 * rw + V7X_LANES:]
    w_lin = w0_ref[...] + jnp.dot(jnp.tanh(lora_wa).astype(BF16), w_up_ref[...],
                                  preferred_element_type=F32)
    neg = -w_lin
    softplus = jnp.maximum(neg, 0.0) + jnp.log(1.0 + jnp.exp(-jnp.abs(neg)))
    log_decay = -jnp.exp(-softplus - 0.5)
    a = _sigmoid(a0_ref[...] + jnp.dot(lora_wa.astype(BF16), a_up_ref[...],
                                       preferred_element_type=F32))
    gate = jnp.dot(_sigmoid(lora_g).astype(BF16), g_up_ref[...], preferred_element_type=F32)
    kk = k * k_k_ref[...]
    kk = kk / jnp.maximum(jnp.sqrt(_seg_sum_nonneg(kk * kk, seg2)), 1e-12)
    k_mod = k * (1.0 + (a - 1.0) * k_a_ref[...])
    bonus = _seg_sum(r * k_mod * r_k_ref[...], seg2) * v
    r_ref[...] = r
    w_ref[...] = log_decay
    k_ref[...] = k_mod
    v_ref[...] = v
    kk_ref[...] = kk
    b_ref[...] = kk * a
    gate_ref[...] = gate
    bonus_ref[...] = bonus


def _mix_prep(x2d, seq, norm_g, w_in, conv_w, conv_g, mu, w0, w_up, a0, a_up, g_up, k_k, k_a,
              r_k, seg2):
    n, d = x2d.shape
    tm = PREP_ROWS
    halo = V7X_SUBLANES
    cw = conv_w.shape[1]
    rw = w0.shape[1]
    in_cols = w_in.shape[1]
    assert seq % tm == 0 and n % tm == 0
    full = lambda a: pl.BlockSpec(a.shape, lambda i: (0,) * a.ndim)
    row_spec = lambda c: pl.BlockSpec((tm, c), lambda i: (i, 0))
    consts = (norm_g, w_in, conv_w, conv_g, mu, w0, w_up, a0, a_up, g_up, k_k, k_a, r_k, seg2)
    outs = [jax.ShapeDtypeStruct((n, cw), BF16)] + [jax.ShapeDtypeStruct((n, rw), F32)] * 8
    return pl.pallas_call(
        functools.partial(_mix_prep_kernel, seq // tm, cw, rw),
        grid=(n // tm,),
        in_specs=[pl.BlockSpec((halo, d), lambda i: (jnp.maximum(i * (tm // halo) - 1, 0), 0)),
                  row_spec(d)] + [full(c) for c in consts],
        out_specs=[row_spec(cw)] + [row_spec(rw)] * 8,
        out_shape=outs,
        scratch_shapes=[pltpu.VMEM((halo + tm, in_cols), F32), pltpu.VMEM((halo + tm, cw), F32)],
        compiler_params=_params("parallel"),
        name="mix_prep",
    )(x2d, x2d, *consts)


def _bdot(a, b):
    return jnp.dot(a.astype(BF16), b.astype(BF16), preferred_element_type=F32)


def _bdot_nt(a, b):
    return lax.dot_general(a.astype(BF16), b.astype(BF16), (((1,), (1,)), ((), ())),
                           preferred_element_type=F32)


def _wkv_chunk_kernel(n_batch, n_pairs, n_side,
                      r_ref, lw_ref, k_ref, v_ref, kk_ref, b_ref, *refs):
    frames = r_ref.shape[1]
    lanes = V7X_LANES
    hd = HEAD_DIM
    assert frames == hd
    side_in, o_ref, side_out, state_ref = (refs[:n_side], refs[n_side],
                                           refs[n_side + 1:2 * n_side + 1], refs[-1])
    for src, dst in zip(side_in, side_out):
        dst[...] = src[...].astype(dst.dtype)

    @pl.when(pl.program_id(0) == 0)
    def _():
        state_ref[...] = jnp.zeros_like(state_ref)

    row = lax.broadcasted_iota(I32, (lanes, lanes), 0)
    lane = lax.broadcasted_iota(I32, (lanes, lanes), 1)
    same_head = (row // hd) == (lane // hd)
    rt = row % hd
    ls = lane % hd
    strict_same = same_head & (rt > ls)
    strict_cross = jnp.logical_not(same_head) & (rt > ls)
    incl_same = same_head & (rt >= ls)
    incl_cross = jnp.logical_not(same_head) & (rt >= ls)
    eye = row == lane
    level_masks = []
    m = 1
    while m < hd:
        level_masks.append(same_head & ((rt // (2 * m)) == (ls // (2 * m)))
                           & (((rt // m) % 2) == 1) & (((ls // m) % 2) == 0))
        m *= 2
    left = lax.broadcasted_iota(I32, (frames, lanes), 1) < hd
    tri = (lax.broadcasted_iota(I32, (frames, frames), 0)
           >= lax.broadcasted_iota(I32, (frames, frames), 1)).astype(BF16)

    def diag_blocks(x):
        return jnp.concatenate([jnp.where(left, x, 0.0), jnp.where(left, 0.0, x)], axis=0)

    def cross_blocks(x):
        return jnp.concatenate([jnp.where(left, 0.0, x), jnp.where(left, x, 0.0)], axis=0)

    def batch_prep(bi):
        lw = lw_ref[bi]
        hi = lw.astype(BF16)
        rem = lw - hi.astype(F32)
        mid = rem.astype(BF16)
        lo = (rem - mid.astype(F32)).astype(BF16)
        cs = (jnp.dot(tri, hi, preferred_element_type=F32)
              + jnp.dot(tri, mid, preferred_element_type=F32)
              + jnp.dot(tri, lo, preferred_element_type=F32))
        cs_end = cs[frames - 1:frames, :]
        e_neg = jnp.exp(-cs)
        e_end = jnp.exp(cs_end - cs)
        b_in = b_ref[bi]
        k_in = k_ref[bi]
        return dict(a_t=-kk_ref[bi] * jnp.exp(cs - lw), b_t=b_in * e_neg, k_t=k_in * e_neg,
                    r_t=r_ref[bi] * jnp.exp(cs), b_h=b_in * e_end, k_h=k_in * e_end,
                    v=v_ref[bi], g_end=jnp.exp(cs_end))

    group = []
    for bi in range(n_batch):
        prep = batch_prep(bi)
        for pi in range(n_pairs):
            sl = slice(pi * lanes, (pi + 1) * lanes)
            group.append((bi, sl, bi * n_pairs + pi, {n: x[:, sl] for n, x in prep.items()}))
    bf = lambda x: x.astype(BF16)
    a_bd = [bf(diag_blocks(p["a_t"])) for _, _, _, p in group]
    r_bd = [bf(diag_blocks(p["r_t"])) for _, _, _, p in group]
    d0 = [_bdot_nt(jnp.concatenate([a[:hd], r[:hd]], axis=0),
                   bf(jnp.concatenate([p["b_t"], p["k_t"]], axis=0)))
          for a, r, (_, _, _, p) in zip(a_bd, r_bd, group)]
    d1 = [_bdot_nt(jnp.concatenate([a[hd:], r[hd:]], axis=0),
                   bf(jnp.concatenate([p["k_t"], p["b_t"]], axis=0)))
          for a, r, (_, _, _, p) in zip(a_bd, r_bd, group)]
    a_rows = [jnp.concatenate([x[:hd], y[:hd]], axis=0) for x, y in zip(d0, d1)]
    m_rows = [jnp.concatenate([x[hd:], y[hd:]], axis=0) for x, y in zip(d0, d1)]
    a_ab = [bf(jnp.where(strict_same, x, 0.0)) for x in a_rows]
    a_ak = [bf(jnp.where(strict_cross, x, 0.0)) for x in a_rows]
    m_rb = [bf(jnp.where(incl_same, x, 0.0)) for x in m_rows]
    m_rk = [bf(jnp.where(incl_cross, x, 0.0)) for x in m_rows]
    t_inv = [jnp.where(eye, 1.0, jnp.where(level_masks[0], x, 0.0)) for x in a_rows]
    for mask in level_masks[1:]:
        t_bf = [bf(t) for t in t_inv]
        half = [jnp.where(mask, _bdot(t, x), 0.0) for t, x in zip(t_bf, a_ab)]
        t_inv = [t + _bdot(h, tb) for t, h, tb in zip(t_inv, half, t_bf)]
    t_bf = [bf(t) for t in t_inv]
    v_x = [bf(cross_blocks(p["v"])) for _, _, _, p in group]
    akv = [_bdot(x, v) for x, v in zip(a_ak, v_x)]
    wu = [bf(_bdot(t, jnp.concatenate([a, bf(x)], axis=1)))
          for t, a, x in zip(t_bf, a_bd, akv)]
    bh_t = [bf(diag_blocks(p["b_h"]).T) for _, _, _, p in group]
    kh_t = [bf(cross_blocks(p["k_h"]).T) for _, _, _, p in group]
    pw_rw = [_bdot(jnp.concatenate([bt, m], axis=0), x[:, :lanes])
             for bt, m, x in zip(bh_t, m_rb, wu)]
    q_o = [_bdot(jnp.concatenate([jnp.concatenate([bt, kt], axis=1),
                                  jnp.concatenate([mb, mk], axis=1)], axis=0),
                 jnp.concatenate([x[:, lanes:], v], axis=0))
           for bt, kt, mb, mk, x, v in zip(bh_t, kh_t, m_rb, m_rk, wu, v_x)]
    res = [_bdot(jnp.concatenate([diag_blocks(p["r_t"]) + pr[lanes:],
                                  pr[:lanes] + jnp.where(eye, p["g_end"], 0.0)], axis=0),
                 state_ref[idx])
           for pr, (_, _, idx, p) in zip(pw_rw, group)]
    for x, q, (bi, sl, idx, _) in zip(res, q_o, group):
        o_bd = x[:lanes] + q[lanes:]
        state_ref[idx] = x[lanes:] + q[:lanes]
        o_ref[bi, :, sl] = jnp.where(left, o_bd[:hd], o_bd[hd:])


def _wkv_chunk(r, lw, k, v, kk, b, side_f32=()):
    n_batch, seq, rw = r.shape
    n_pairs = rw // V7X_LANES
    frames = SCAN_FRAMES
    assert seq % frames == 0
    n_steps = seq // frames
    spec = pl.BlockSpec((n_batch, frames, rw), lambda c: (0, c, 0))
    sliced = [a.reshape(n_steps, -1, a.shape[-1]) for a in side_f32]
    side_specs = [pl.BlockSpec((1,) + a.shape[1:], lambda c: (c, 0, 0)) for a in sliced]
    outs = pl.pallas_call(
        functools.partial(_wkv_chunk_kernel, n_batch, n_pairs, len(sliced)),
        grid=(n_steps,),
        in_specs=[spec] * 6 + side_specs,
        out_specs=[spec] + side_specs,
        out_shape=[jax.ShapeDtypeStruct((n_batch, seq, rw), F32)]
                  + [jax.ShapeDtypeStruct(a.shape, BF16) for a in sliced],
        scratch_shapes=[pltpu.VMEM((n_batch * n_pairs, V7X_LANES, V7X_LANES), F32)],
        compiler_params=_params("arbitrary"),
        name="wkv_chunk",
    )(r, lw, k, v, kk, b, *sliced)
    return outs[0], [y.reshape(a.shape) for y, a in zip(outs[1:], side_f32)]


def _post_mix_kernel(n_experts,
                     o_ref, bonus_ref, gate_ref, yconv_ref, x_ref, gn_w_ref, gn_b_ref,
                     w_out_c_ref, w_out_r_ref, ffn_g_ref, wr_hi_ref, wr_lo_ref, b_router_ref,
                     seg2_ref,
                     x1_ref, *outs):
    h2_refs, (top_e_ref, gates_ref) = outs[:-2], outs[-2:]
    seg2 = seg2_ref[...]
    o = o_ref[...]
    mean = _seg_sum(o, seg2) * (1.0 / HEAD_DIM)
    cen = o - mean
    var = _seg_sum_nonneg(cen * cen, seg2) * (1.0 / HEAD_DIM)
    o = cen * lax.rsqrt(var + GN_EPS) * gn_w_ref[...] + gn_b_ref[...]
    y_rwkv = (o + bonus_ref[...]) * gate_ref[...]
    x1 = (x_ref[...]
          + jnp.dot(yconv_ref[...], w_out_c_ref[...], preferred_element_type=F32)
          + jnp.dot(y_rwkv.astype(BF16), w_out_r_ref[...], preferred_element_type=F32))
    x1_ref[...] = x1
    h2 = _rms_rows(x1, ffn_g_ref[...])
    words = _pack_bf16_pairs(h2)
    for c, h2_ref in enumerate(h2_refs):
        h2_ref[...] = words[:, c * ROW_CHUNK:(c + 1) * ROW_CHUNK]

    h_hi, h_lo = _split_bf16(h2)
    nt = (((1,), (1,)), ((), ()))
    logits = (lax.dot_general(wr_hi_ref[...], h_hi, nt, preferred_element_type=F32)
              + lax.dot_general(wr_hi_ref[...], h_lo, nt, preferred_element_type=F32)
              + lax.dot_general(wr_lo_ref[...], h_hi, nt, preferred_element_type=F32)
              + b_router_ref[...])
    e_id = lax.broadcasted_iota(I32, logits.shape, 0).astype(F32)
    work = logits
    tops, ids = [], []
    for _ in range(TOP_K):
        m = jnp.max(work, axis=0, keepdims=True)
        sel = jnp.min(jnp.where(work == m, e_id, float(n_experts)), axis=0, keepdims=True)
        tops.append(m)
        ids.append(sel)
        work = jnp.where(e_id == sel, -jnp.inf, work)
    ex = [jnp.exp(t - tops[0]) for t in tops]
    denom = ex[0] + ex[1] + ex[2] + ex[3]
    top_e_ref[...] = jnp.concatenate(ids, axis=0).astype(I32)
    gates_ref[...] = jnp.concatenate([e / denom for e in ex], axis=0)


def _post_mix(token0, n, o, bonus, gate, yconv, x2d, gn_w, gn_b, w_out_c, w_out_r, ffn_g, wr_hi,
              wr_lo, b_router, seg2):
    d = x2d.shape[1]
    rw = o.shape[1]
    cw = yconv.shape[1]
    n_experts = wr_hi.shape[0]
    tm = POST_ROWS
    assert token0 % tm == 0 and n % tm == 0
    full = lambda a: pl.BlockSpec(a.shape, lambda i: (0,) * a.ndim)
    in_spec = lambda c: pl.BlockSpec((tm, c), lambda i: (i + token0 // tm, 0))
    row_spec = lambda c: pl.BlockSpec((tm, c), lambda i: (i, 0))
    col_spec = pl.BlockSpec((TOP_K, tm), lambda i: (0, i))
    consts = (gn_w, gn_b, w_out_c, w_out_r, ffn_g, wr_hi, wr_lo, b_router, seg2)
    n_chunks = d // 2 // ROW_CHUNK
    outs = pl.pallas_call(
        functools.partial(_post_mix_kernel, n_experts),
        grid=(n // tm,),
        in_specs=[in_spec(rw), in_spec(rw), in_spec(rw), in_spec(cw), in_spec(d)]
                 + [full(c) for c in consts],
        out_specs=[row_spec(d)] + [row_spec(ROW_CHUNK)] * n_chunks + [col_spec, col_spec],
        out_shape=[jax.ShapeDtypeStruct((n, d), F32)]
                  + [jax.ShapeDtypeStruct((n, ROW_CHUNK), jnp.uint32)] * n_chunks
                  + [jax.ShapeDtypeStruct((TOP_K, n), I32), jax.ShapeDtypeStruct((TOP_K, n), F32)],
        compiler_params=_params("parallel"),
        name="post_mix",
    )(o, bonus, gate, yconv, x2d, *consts)
    return outs[0], outs[1:1 + n_chunks], outs[-2], outs[-1]


def _route_kernel(n_experts, block_rows,
                  top_e_ref, tri_ref, dest_ref, meta_ref,
                  count_ref, start_ref, carry_ref):
    phase = pl.program_id(0)
    j = pl.program_id(1)
    tb = top_e_ref.shape[1]
    e_id = lax.broadcasted_iota(I32, (n_experts, tb), 0)
    top_e = top_e_ref[...]
    onehot = jnp.zeros((n_experts, tb), F32)
    for c in range(TOP_K):
        onehot = onehot + jnp.where(top_e[c:c + 1, :] == e_id, 1.0, 0.0)
    block_count = jnp.sum(onehot, axis=1, keepdims=True)

    @pl.when((phase == 0) & (j == 0))
    def _():
        count_ref[...] = jnp.zeros_like(count_ref)

    @pl.when(phase == 0)
    def _():
        count_ref[...] += block_count

    @pl.when((phase == 1) & (j == 0))
    def _():
        counts = count_ref[...]
        padded = jnp.ceil(counts * (1.0 / block_rows)) * block_rows
        sub = lax.broadcasted_iota(I32, (n_experts, n_experts), 0)
        lane = lax.broadcasted_iota(I32, (n_experts, n_experts), 1)
        padded_row = jnp.sum(jnp.where(sub == lane, padded, 0.0), axis=0, keepdims=True)
        start = jnp.sum(jnp.where(lane < sub, padded_row, 0.0), axis=1, keepdims=True)
        start_ref[...] = start
        carry_ref[...] = jnp.zeros_like(carry_ref)
        end = start + padded
        nb = meta_ref.shape[1]
        slot0 = (lax.broadcasted_iota(I32, (n_experts, nb), 1) * block_rows).astype(F32)
        block_e = jnp.sum(jnp.where(end <= slot0, 1.0, 0.0), axis=0, keepdims=True)
        block_e = jnp.minimum(block_e, n_experts - 1.0)
        used = jnp.max(end, axis=0, keepdims=True) * (1.0 / block_rows)
        in_region = (start <= slot0) & (slot0 < end)
        valid = jnp.sum(jnp.where(in_region, jnp.minimum(start + counts - slot0, block_rows), 0.0),
                        axis=0, keepdims=True)
        row = lax.broadcasted_iota(I32, meta_ref.shape, 0)
        meta_ref[...] = jnp.where(row == META_BLOCK_EXPERT, block_e,
                                  jnp.where(row == META_VALID_ROWS, valid, used)).astype(I32)

    @pl.when(phase == 1)
    def _():
        incl = jnp.dot(onehot.astype(BF16), tri_ref[...], preferred_element_type=F32)
        base = incl - onehot + carry_ref[...] + start_ref[...]
        rows = [jnp.sum(jnp.where(top_e[c:c + 1, :] == e_id, base, 0.0), axis=0, keepdims=True)
                for c in range(TOP_K)]
        dest_ref[...] = jnp.concatenate(rows, axis=0).astype(I32)
        carry_ref[...] += block_count


def _route(top_e, n_experts, n_blocks_padded):
    n = top_e.shape[1]
    tb = ROUTE_TOKENS
    assert n % tb == 0
    tri = (lax.broadcasted_iota(I32, (tb, tb), 0) <= lax.broadcasted_iota(I32, (tb, tb), 1)
           ).astype(BF16)
    return pl.pallas_call(
        functools.partial(_route_kernel, n_experts, EXPERT_ROWS),
        grid=(2, n // tb),
        in_specs=[pl.BlockSpec((TOP_K, tb), lambda ph, j: (0, j)),
                  pl.BlockSpec((tb, tb), lambda ph, j: (0, 0))],
        out_specs=[pl.BlockSpec((TOP_K, tb), lambda ph, j: (0, j * ph)),
                   pl.BlockSpec((V7X_SUBLANES, n_blocks_padded), lambda ph, j: (0, 0))],
        out_shape=[jax.ShapeDtypeStruct((TOP_K, n), I32),
                   jax.ShapeDtypeStruct((V7X_SUBLANES, n_blocks_padded), I32)],
        scratch_shapes=[pltpu.VMEM((n_experts, 1), F32)] * 3,
        compiler_params=_params("arbitrary", "arbitrary"),
        name="route",
    )(top_e, tri)


def _sc_mesh():
    return plsc.VectorSubcoreMesh(core_axis_name="core", subcore_axis_name="subcore")


def _sc_scatter_rows(src, dest, n_slots):
    n, width = src.shape
    assert n % SC_WINDOW == 0

    @functools.partial(pl.kernel, out_type=jax.ShapeDtypeStruct((n_slots, width), src.dtype),
                       mesh=_sc_mesh(), name="sc_scatter_rows")
    def scatter(src_hbm, dest_hbm, out_hbm):
        def body(src_vmem, dest_vmem):
            for c in range(TOP_K):
                pltpu.sync_copy(src_vmem, out_hbm.at[dest_vmem.at[c]])

        pltpu.emit_pipeline(
            body,
            grid=(n // SC_WINDOW,),
            in_specs=[pl.BlockSpec((SC_WINDOW, width), lambda i: (i, 0)),
                      pl.BlockSpec((TOP_K, SC_WINDOW), lambda i: (0, i))],
            out_specs=[],
            core_axis_name=("core", "subcore"),
            dimension_semantics=(pltpu.PARALLEL,),
        )(src_hbm, dest_hbm)

    return scatter(src, dest)


def _experts_kernel(d_ff, n_in, n_out, meta_ref, *refs):
    x_refs, refs = refs[:n_in], refs[n_in:]
    w_gu_hbm, b_gu_ref, w_down_hbm, b_down_ref = refs[:4]
    y_refs = refs[4:4 + n_out]
    w_gu_buf, w_down_buf, sems, slot_ref = refs[4 + n_out:]
    j = pl.program_id(0)
    last = pl.num_programs(0) - 1
    used = meta_ref[META_USED_BLOCKS, 0]

    def block_expert(jj):
        return meta_ref[META_BLOCK_EXPERT, jnp.clip(jj, 0, last)]

    def weight_copies(e, slot):
        return (pltpu.make_async_copy(w_gu_hbm.at[e], w_gu_buf.at[slot], sems.at[0, slot]),
                pltpu.make_async_copy(w_down_hbm.at[e], w_down_buf.at[slot], sems.at[1, slot]))

    @pl.when(j < used)
    def _():
        e = block_expert(j)

        @pl.when(j == 0)
        def _():
            slot_ref[0] = 1
            for copy in weight_copies(e, 0):
                copy.start()

        @pl.when((j == 0) | (block_expert(j - 1) != e))
        def _():
            slot = 1 - slot_ref[0]
            slot_ref[0] = slot
            nxt = lax.while_loop(lambda jj: (jj < used) & (block_expert(jj) == e), lambda jj: jj + 1, j + 1)

            @pl.when(nxt < used)
            def _():
                for copy in weight_copies(block_expert(nxt), 1 - slot):
                    copy.start()

            for copy in weight_copies(e, slot):
                copy.wait()

        slot = slot_ref[0]
        w_gu_mxu, w_down_mxu = w_gu_buf.at[slot], w_down_buf.at[slot]
        x = _unpack_bf16_pairs(jnp.concatenate([r[...] for r in x_refs], axis=1))
        row = lax.broadcasted_iota(I32, x.shape, 0)
        x = jnp.where(row < meta_ref[META_VALID_ROWS, j], x, 0.0)
        gu = jnp.dot(x.astype(BF16), w_gu_mxu[...], preferred_element_type=F32) + b_gu_ref[0]
        gate = jnp.minimum(gu[:, :d_ff], SWIGLU_LIMIT)
        up = jnp.clip(gu[:, d_ff:], -SWIGLU_LIMIT, SWIGLU_LIMIT)
        act = (up + 1.0) * (gate * _sigmoid(SWIGLU_ALPHA * gate))
        y = jnp.dot(act.astype(BF16), w_down_mxu[...], preferred_element_type=F32) + b_down_ref[0]
        words = _pack_bf16_pairs(y)
        for c, y_ref in enumerate(y_refs):
            y_ref[...] = words[:, c * ROW_CHUNK:(c + 1) * ROW_CHUNK]

    @pl.when(j >= used)
    def _():
        for y_ref in y_refs:
            y_ref[...] = jnp.zeros_like(y_ref)


def _experts(meta, slot_chunks, w_gu, b_gu, w_down, b_down):
    assert w_gu.dtype == BF16 and w_down.dtype == BF16
    n_in = len(slot_chunks)
    n_slots = slot_chunks[0].shape[0]
    n_experts, d, two_ff = w_gu.shape
    n_out = d // 2 // ROW_CHUNK
    d_ff = two_ff // 2
    bm = EXPERT_ROWS
    by_expert = lambda j, m: (m[META_BLOCK_EXPERT, j], 0, 0)
    grid_spec = pltpu.PrefetchScalarGridSpec(
        num_scalar_prefetch=1,
        grid=(n_slots // bm,),
        in_specs=[pl.BlockSpec((bm, ROW_CHUNK),
                               lambda j, m: (jnp.minimum(j, m[META_USED_BLOCKS, 0] - 1), 0))
                  ] * n_in
                 + [pl.BlockSpec(memory_space=pl.ANY), pl.BlockSpec((1, 1, two_ff), by_expert),
                    pl.BlockSpec(memory_space=pl.ANY), pl.BlockSpec((1, 1, d), by_expert)],
        out_specs=[pl.BlockSpec((bm, ROW_CHUNK), lambda j, m: (j, 0))] * n_out,
        scratch_shapes=[pltpu.VMEM((2, d, two_ff), BF16), pltpu.VMEM((2, d_ff, d), BF16),
                        pltpu.SemaphoreType.DMA((2, 2)), pltpu.SMEM((1,), I32)],
    )
    return pl.pallas_call(
        functools.partial(_experts_kernel, d_ff, n_in, n_out),
        grid_spec=grid_spec,
        out_shape=[jax.ShapeDtypeStruct((n_slots, ROW_CHUNK), jnp.uint32)] * n_out,
        compiler_params=_params("arbitrary"),
        name="experts",
    )(meta, *slot_chunks, w_gu, b_gu, w_down, b_down)


def _sc_gather_rows(table, idx_row):
    n_idx = idx_row.shape[1]
    width = table.shape[1]
    assert n_idx % SC_WINDOW == 0

    @functools.partial(pl.kernel, out_type=jax.ShapeDtypeStruct((n_idx, width), table.dtype),
                       mesh=_sc_mesh(), name="sc_gather_rows")
    def gather(table_hbm, idx_hbm, out_hbm):
        def body(idx_vmem, out_vmem):
            pltpu.sync_copy(table_hbm.at[idx_vmem.at[0]], out_vmem)

        pltpu.emit_pipeline(
            body,
            grid=(n_idx // SC_WINDOW,),
            in_specs=[pl.BlockSpec((1, SC_WINDOW), lambda i: (0, i))],
            out_specs=[pl.BlockSpec((SC_WINDOW, width), lambda i: (i, 0))],
            core_axis_name=("core", "subcore"),
            dimension_semantics=(pltpu.PARALLEL,),
        )(idx_hbm, out_hbm)

    return gather(table, idx_row)


def _reduce_kernel(n_chunks, *refs):
    y_refs = refs[:n_chunks]
    x1_ref, gates_ref, g_ref = refs[n_chunks:n_chunks + 3]
    out_ref = refs[-1]
    g_lanes = gates_ref[...]
    row = lax.broadcasted_iota(I32, (V7X_LANES, g_lanes.shape[1]), 0)
    g_rows = jnp.zeros(row.shape, F32)
    for c in range(TOP_K):
        g_rows = jnp.where(row == c, g_lanes[c:c + 1], g_rows)
    gates = g_rows.T
    acc = x1_ref[...]
    for c in range(TOP_K):
        y = _unpack_bf16_pairs(jnp.concatenate([r[c] for r in y_refs], axis=1))
        acc = acc + y * gates[:, c:c + 1]
    out_ref[...] = _rms_rows(acc, g_ref[...])


def _reduce(token0, n_total, out_so_far, y_chunks, x1, gates, final_g):
    n, d = x1.shape
    tb = REDUCE_TOKENS
    n_chunks = len(y_chunks)
    assert token0 % tb == 0
    in_specs = ([pl.BlockSpec((TOP_K, tb, ROW_CHUNK), lambda i: (0, i, 0))] * n_chunks
                + [pl.BlockSpec((tb, d), lambda i: (i, 0)),
                   pl.BlockSpec((TOP_K, tb), lambda i: (0, i)),
                   pl.BlockSpec((1, d), lambda i: (0, 0))])
    args = [y.reshape(TOP_K, n, ROW_CHUNK) for y in y_chunks] + [x1, gates, final_g]
    aliases = {}
    if out_so_far is not None:
        in_specs.append(pl.BlockSpec(memory_space=pl.ANY))
        args.append(out_so_far)
        aliases = {len(args) - 1: 0}
    return pl.pallas_call(
        functools.partial(_reduce_kernel, n_chunks),
        grid=(n // tb,),
        in_specs=in_specs,
        out_specs=pl.BlockSpec((tb, d), lambda i: (i + token0 // tb, 0)),
        out_shape=jax.ShapeDtypeStruct((n_total, d), F32),
        input_output_aliases=aliases,
        compiler_params=_params("parallel"),
        name="reduce",
    )(*args)


def _group_matrix(width):
    a = lax.broadcasted_iota(I32, (width, width), 0) // HEAD_DIM
    b = lax.broadcasted_iota(I32, (width, width), 1) // HEAD_DIM
    g = (a == b).astype(BF16)
    return jnp.concatenate([g, g], axis=0)


def _row(vec):
    return vec.reshape(1, -1).astype(F32)


def kernel(x, w_in, conv_w, conv_norm_g, rwkv_mu, w0, w_up, a0, a_up, g_up, k_k, k_a, r_k,
           gn_w, gn_b, w_out, norm_mix_g, norm_ffn_g, w_router, b_router, w_gu, b_gu, w_down,
           b_down, norm_final_g):
    n_batch, seq, d = x.shape
    n = n_batch * seq
    depth = w_in.shape[0]
    cw = conv_w.shape[2]
    rw = w0.shape[1]
    n_experts = w_router.shape[2]
    decay_lora = w_up.shape[1]
    aaa_lora = a_up.shape[1]
    assert decay_lora + aaa_lora == V7X_LANES
    assert depth == 1
    assert n % MOE_PARTS == 0
    n_part = n // MOE_PARTS
    n_slots = n_part * TOP_K + n_experts * EXPERT_ROWS
    n_blocks = n_slots // EXPERT_ROWS
    n_blocks_padded = -(-n_blocks // V7X_LANES) * V7X_LANES
    seg2 = _group_matrix(rw)

    x2d = x.reshape(n, d)
    for l in range(depth):
        w_up_pad = jnp.concatenate([w_up[l], jnp.zeros((aaa_lora, rw), F32)], axis=0).astype(BF16)
        a_up_pad = jnp.concatenate([jnp.zeros((decay_lora, rw), F32), a_up[l]], axis=0).astype(BF16)
        (yconv, r, lw, k, v, kk, b, gate, bonus) = _mix_prep(
            x2d, seq, _row(norm_mix_g[l]), w_in[l].astype(BF16), conv_w[l].astype(F32),
            _row(conv_norm_g[l]), _row(rwkv_mu[l]), _row(w0[l]), w_up_pad, _row(a0[l]), a_up_pad,
            g_up[l].astype(BF16), _row(k_k[l]), _row(k_a[l]), _row(r_k[l]), seg2)
        shape3 = (n_batch, seq, rw)
        o, (w_gu_b, w_down_b) = _wkv_chunk(
            r.reshape(shape3), lw.reshape(shape3), k.reshape(shape3), v.reshape(shape3),
            kk.reshape(shape3), b.reshape(shape3),
            side_f32=(w_gu[l].astype(F32), w_down[l].astype(F32)))
        o = o.reshape(n, rw)
        w_out_b = w_out[l].astype(BF16)
        wr_t = w_router[l].T.astype(F32)
        wr_hi = wr_t.astype(BF16)
        wr_lo = (wr_t - wr_hi.astype(F32)).astype(BF16)
        expert_args = (w_gu_b, b_gu[l].reshape(n_experts, 1, -1).astype(F32),
                       w_down_b, b_down[l].reshape(n_experts, 1, -1).astype(F32))
        parts = []
        for token0 in range(0, n, n_part):
            x1, h2_chunks, top_e, gates = _post_mix(
                token0, n_part, o, bonus, gate, yconv, x2d, _row(gn_w[l]), _row(gn_b[l]),
                w_out_b[:cw], w_out_b[cw:], _row(norm_ffn_g[l]), wr_hi, wr_lo,
                b_router[l].reshape(n_experts, 1).astype(F32), seg2)
            dest, meta = _route(top_e, n_experts, n_blocks_padded)
            slot_chunks = [_sc_scatter_rows(h, dest, n_slots) for h in h2_chunks]
            parts.append((token0, x1, gates, dest, meta, slot_chunks))
        out = None
        for token0, x1, gates, dest, meta, slot_chunks in parts:
            y_chunks = _experts(meta, slot_chunks, *expert_args)
            idx_row = dest.reshape(1, TOP_K * n_part)
            out = _reduce(token0, n, out, [_sc_gather_rows(y, idx_row) for y in y_chunks], x1,
                          gates, _row(norm_final_g))
        x2d = out
    return x2d.reshape(n_batch, seq, d)
```
